```python
import jax, jax.numpy as jnp
from jax import lax
import numpy as np

D_MODEL = 1024
BATCH = 8
SEQ = 8192
DEPTH = 4

CHUNK = 64
WINDOW = 128
WIN_CHUNKS = WINDOW // CHUNK
BAND = (WIN_CHUNKS + 1) * CHUNK
HEAD_DIM = 64
ATTN_WIDTH = D_MODEL // 2
N_Q_HEADS = ATTN_WIDTH // HEAD_DIM
N_KV_HEADS = N_Q_HEADS // 4
KV_WIDTH = N_KV_HEADS * HEAD_DIM
LRU_WIDTH = D_MODEL
LRU_BLOCKS = 16
LRU_BLOCK = LRU_WIDTH // LRU_BLOCKS
CONV_WIDTH = 4
LRU_C = 8.0
N_MEM = 256
MEM_HEADS = 4
MEM_WIDTH = D_MODEL // 2
MEM_HEAD_DIM = MEM_WIDTH // MEM_HEADS
N_BRANCH = 3
D_FF = 11 * D_MODEL // 4
EPS = 1e-6
IN_WIDTH = ATTN_WIDTH + 2 * KV_WIDTH + 2 * LRU_WIDTH + MEM_WIDTH + N_BRANCH * D_MODEL

kernel_name = 'hybrid_swa_rglru_memory_macaron'


def rmsnorm(x, g):
    x32 = x.astype(jnp.float32)
    y = x32 * lax.rsqrt(jnp.mean(x32 * x32, axis=-1, keepdims=True) + EPS)
    return (y * g.astype(jnp.float32)).astype(x.dtype)


def swiglu(h, w_gate, w_up, w_down):
    return (jax.nn.silu(h @ w_gate) * (h @ w_up)) @ w_down


def alibi_slopes(n):
    return jnp.asarray(np.array([2.0 ** (-8.0 * (i + 1) / n) for i in range(n)], dtype=np.float32))


def window_attention(q, k, v, sinks):
    B, S = q.shape[0], q.shape[1]
    nc = S // CHUNK
    G = N_Q_HEADS // N_KV_HEADS
    pad = WIN_CHUNKS * CHUNK
    kp = jnp.pad(k, ((0, 0), (pad, 0), (0, 0), (0, 0))).reshape(B, nc + WIN_CHUNKS, CHUNK, N_KV_HEADS, HEAD_DIM)
    vp = jnp.pad(v, ((0, 0), (pad, 0), (0, 0), (0, 0))).reshape(B, nc + WIN_CHUNKS, CHUNK, N_KV_HEADS, HEAD_DIM)
    kb = jnp.concatenate([kp[:, j:j + nc] for j in range(WIN_CHUNKS + 1)], axis=2)
    vb = jnp.concatenate([vp[:, j:j + nc] for j in range(WIN_CHUNKS + 1)], axis=2)
    qb = q.reshape(B, nc, CHUNK, N_KV_HEADS, G, HEAD_DIM)
    s = jnp.einsum('bcqkgd,bcskd->bckgqs', qb, kb).astype(jnp.float32) * (HEAD_DIM ** -0.5)
    qi = jnp.arange(CHUNK) + pad
    kj = jnp.arange(BAND)
    dist = jnp.abs(qi[:, None] - kj[None, :]).astype(jnp.float32)
    slopes = alibi_slopes(N_Q_HEADS).reshape(N_KV_HEADS, G)
    s = s - slopes[:, :, None, None] * dist
    key_chunk = jnp.arange(nc)[:, None] - WIN_CHUNKS + kj[None, :] // CHUNK
    valid = key_chunk >= 0
    s = jnp.where(valid[None, :, None, None, None, :], s, -jnp.inf)
    sink = sinks.astype(jnp.float32).reshape(N_KV_HEADS, G)[None, None, :, :, None, None]
    m = jnp.maximum(jnp.max(s, axis=-1, keepdims=True), sink)
    p = jnp.exp(s - m)
    p = p / (jnp.sum(p, axis=-1, keepdims=True) + jnp.exp(sink - m))
    o = jnp.einsum('bckgqs,bcskd->bcqkgd', p.astype(v.dtype), vb)
    return o.reshape(B, S, ATTN_WIDTH)


def rglru_branch(xr, yr, conv_w, conv_b, wa, ba, wx, bx, lam):
    B, S = xr.shape[0], xr.shape[1]
    xc = lax.conv_general_dilated(xr, conv_w[:, None, :].astype(xr.dtype), window_strides=(1,),
                                  padding=[(CONV_WIDTH - 1, 0)],
                                  dimension_numbers=('NWC', 'WIO', 'NWC'),
                                  feature_group_count=LRU_WIDTH) + conv_b
    xb = xc.reshape(B, S, LRU_BLOCKS, LRU_BLOCK)
    r = jax.nn.sigmoid((jnp.einsum('bshi,hij->bshj', xb, wa).reshape(B, S, LRU_WIDTH) + ba).astype(jnp.float32))
    gi = jax.nn.sigmoid((jnp.einsum('bshi,hij->bshj', xb, wx).reshape(B, S, LRU_WIDTH) + bx).astype(jnp.float32))
    log_a = -LRU_C * r * jax.nn.softplus(-lam.astype(jnp.float32))
    a = jnp.exp(log_a)
    mult = jnp.sqrt(jnp.maximum(1.0 - jnp.exp(2.0 * log_a), 0.0))
    mult = jnp.where(jnp.arange(S)[None, :, None] == 0, 1.0, mult)
    b = mult * gi * xc.astype(jnp.float32)

    def combine(left, right):
        al, bl = left
        ar, br = right
        return al * ar, ar * bl + br

    _, h = lax.associative_scan(combine, (a, b), axis=1)
    return h.astype(xr.dtype) * jax.nn.gelu(yr)


def memory_attention(cq, mem_n, w_mem_kv):
    B, S = cq.shape[0], cq.shape[1]
    M = mem_n.shape[1]
    kv = mem_n @ w_mem_kv
    k = kv[..., :MEM_WIDTH].reshape(B, M, MEM_HEADS, MEM_HEAD_DIM)
    v = kv[..., MEM_WIDTH:].reshape(B, M, MEM_HEADS, MEM_HEAD_DIM)
    q = cq.reshape(B, S, MEM_HEADS, MEM_HEAD_DIM)
    s = jnp.einsum('bshd,bmhd->bhsm', q, k).astype(jnp.float32) * (MEM_HEAD_DIM ** -0.5)
    p = jax.nn.softmax(s, axis=-1).astype(v.dtype)
    return jnp.einsum('bhsm,bmhd->bshd', p, v).reshape(B, S, MEM_WIDTH)


def _fwd_setup_inputs(seed: int = 0) -> dict:
    key = jax.random.key(seed)
    ks = jax.random.split(key, 32)
    f32 = jnp.float32

    def nrm(k, shape, fan_in):
        return jax.random.normal(k, shape, f32) * (fan_in ** -0.5)

    def gain(k, shape):
        return 1.0 + 0.05 * jax.random.normal(k, shape, f32)

    u = jax.random.uniform(ks[20], (DEPTH, LRU_WIDTH), f32, 0.9, 0.999)
    a0 = u ** (1.0 / LRU_C)
    lam = jnp.log(a0) - jnp.log1p(-a0)
    return {
        'x': jax.random.normal(ks[0], (BATCH, SEQ, D_MODEL), f32),
        'mem': jax.random.normal(ks[1], (BATCH, N_MEM, D_MODEL), f32),
        'ffn1_norm': gain(ks[2], (DEPTH, D_MODEL)),
        'ffn1_w_gate': nrm(ks[3], (DEPTH, D_MODEL, D_FF), D_MODEL),
        'ffn1_w_up': nrm(ks[4], (DEPTH, D_MODEL, D_FF), D_MODEL),
        'ffn1_w_down': nrm(ks[5], (DEPTH, D_FF, D_MODEL), D_FF),
        'mix_norm': gain(ks[6], (DEPTH, D_MODEL)),
        'w_in': nrm(ks[7], (DEPTH, D_MODEL, IN_WIDTH), D_MODEL),
        'gate_bias': 0.1 * jax.random.normal(ks[8], (DEPTH, N_BRANCH * D_MODEL), f32),
        'attn_sinks': 0.5 * jax.random.normal(ks[9], (DEPTH, N_Q_HEADS), f32),
        'w_attn_out': nrm(ks[10], (DEPTH, ATTN_WIDTH, D_MODEL), ATTN_WIDTH),
        'conv_w': nrm(ks[11], (DEPTH, CONV_WIDTH, LRU_WIDTH), CONV_WIDTH),
        'conv_b': 0.02 * jax.random.normal(ks[12], (DEPTH, LRU_WIDTH), f32),
        'lru_wa': nrm(ks[13], (DEPTH, LRU_BLOCKS, LRU_BLOCK, LRU_BLOCK), LRU_BLOCK),
        'lru_ba': 0.1 * jax.random.normal(ks[14], (DEPTH, LRU_WIDTH), f32),
        'lru_wx': nrm(ks[15], (DEPTH, LRU_BLOCKS, LRU_BLOCK, LRU_BLOCK), LRU_BLOCK),
        'lru_bx': 0.1 * jax.random.normal(ks[16], (DEPTH, LRU_WIDTH), f32),
        'lru_lambda': lam,
        'w_lru_out': nrm(ks[17], (DEPTH, LRU_WIDTH, D_MODEL), LRU_WIDTH),
        'mem_norm': gain(ks[18], (DEPTH, D_MODEL)),
        'w_mem_kv': nrm(ks[19], (DEPTH, D_MODEL, 2 * MEM_WIDTH), D_MODEL),
        'w_mem_out': nrm(ks[21], (DEPTH, MEM_WIDTH, D_MODEL), MEM_WIDTH),
        'w_out': nrm(ks[22], (DEPTH, D_MODEL, D_MODEL), D_MODEL),
        'ffn2_norm': gain(ks[23], (DEPTH, D_MODEL)),
        'ffn2_w_gate': nrm(ks[24], (DEPTH, D_MODEL, D_FF), D_MODEL),
        'ffn2_w_up': nrm(ks[25], (DEPTH, D_MODEL, D_FF), D_MODEL),
        'ffn2_w_down': nrm(ks[26], (DEPTH, D_FF, D_MODEL), D_FF),
        'final_norm': gain(ks[27], (D_MODEL,)),
    }


def _fwd_reference(x, mem, ffn1_norm, ffn1_w_gate, ffn1_w_up, ffn1_w_down, mix_norm, w_in, gate_bias,
              attn_sinks, w_attn_out, conv_w, conv_b, lru_wa, lru_ba, lru_wx, lru_bx, lru_lambda,
              w_lru_out, mem_norm, w_mem_kv, w_mem_out, w_out, ffn2_norm, ffn2_w_gate, ffn2_w_up,
              ffn2_w_down, final_norm):
    B, S = x.shape[0], x.shape[1]
    o1 = ATTN_WIDTH
    o2 = o1 + KV_WIDTH
    o3 = o2 + KV_WIDTH
    o4 = o3 + LRU_WIDTH
    o5 = o4 + LRU_WIDTH
    o6 = o5 + MEM_WIDTH
    for l in range(DEPTH):
        x = x + 0.5 * swiglu(rmsnorm(x, ffn1_norm[l]), ffn1_w_gate[l], ffn1_w_up[l], ffn1_w_down[l])
        h = rmsnorm(x, mix_norm[l])
        proj = h @ w_in[l]
        q = proj[..., :o1].reshape(B, S, N_Q_HEADS, HEAD_DIM)
        k = proj[..., o1:o2].reshape(B, S, N_KV_HEADS, HEAD_DIM)
        v = proj[..., o2:o3].reshape(B, S, N_KV_HEADS, HEAD_DIM)
        xr = proj[..., o3:o4]
        yr = proj[..., o4:o5]
        cq = proj[..., o5:o6]
        g = jax.nn.sigmoid((proj[..., o6:] + gate_bias[l]).astype(jnp.float32)).astype(x.dtype)
        g = g.reshape(B, S, N_BRANCH, D_MODEL)
        br_attn = window_attention(q, k, v, attn_sinks[l]) @ w_attn_out[l]
        br_lru = rglru_branch(xr, yr, conv_w[l], conv_b[l], lru_wa[l], lru_ba[l], lru_wx[l], lru_bx[l],
                              lru_lambda[l]) @ w_lru_out[l]
        br_mem = memory_attention(cq, rmsnorm(mem, mem_norm[l]), w_mem_kv[l]) @ w_mem_out[l]
        merged = g[:, :, 0] * br_attn + g[:, :, 1] * br_lru + g[:, :, 2] * br_mem
        x = x + merged @ w_out[l]
        x = x + 0.5 * swiglu(rmsnorm(x, ffn2_norm[l]), ffn2_w_gate[l], ffn2_w_up[l], ffn2_w_down[l])
    return rmsnorm(x, final_norm)


import jax as _jax
import jax.numpy as _jnp

TWIN_FORMAT = 'train_step'
FWD_PARAMS = ['x', 'mem', 'ffn1_norm', 'ffn1_w_gate', 'ffn1_w_up', 'ffn1_w_down', 'mix_norm', 'w_in', 'gate_bias', 'attn_sinks', 'w_attn_out', 'conv_w', 'conv_b', 'lru_wa', 'lru_ba', 'lru_wx', 'lru_bx', 'lru_lambda', 'w_lru_out', 'mem_norm', 'w_mem_kv', 'w_mem_out', 'w_out', 'ffn2_norm', 'ffn2_w_gate', 'ffn2_w_up', 'ffn2_w_down', 'final_norm']
TWIN_WEIGHTS = ['ffn1_norm', 'ffn1_w_gate', 'ffn1_w_up', 'ffn1_w_down', 'mix_norm', 'w_in', 'gate_bias', 'attn_sinks', 'w_attn_out', 'conv_w', 'conv_b', 'lru_wa', 'lru_ba', 'lru_wx', 'lru_bx', 'lru_lambda', 'w_lru_out', 'mem_norm', 'w_mem_kv', 'w_mem_out', 'w_out', 'ffn2_norm', 'ffn2_w_gate', 'ffn2_w_up', 'ffn2_w_down', 'final_norm']
TWIN_DIFF_INPUT = 'x'
TWIN_INPUTS = ['x', 'mem', 'ffn1_norm', 'ffn1_w_gate', 'ffn1_w_up', 'ffn1_w_down', 'mix_norm', 'w_in', 'gate_bias', 'attn_sinks', 'w_attn_out', 'conv_w', 'conv_b', 'lru_wa', 'lru_ba', 'lru_wx', 'lru_bx', 'lru_lambda', 'w_lru_out', 'mem_norm', 'w_mem_kv', 'w_mem_out', 'w_out', 'ffn2_norm', 'ffn2_w_gate', 'ffn2_w_up', 'ffn2_w_down', 'final_norm', 'loss_target', 'm_ffn1_norm', 'm_ffn1_w_gate', 'm_ffn1_w_up', 'm_ffn1_w_down', 'm_mix_norm', 'm_w_in', 'm_gate_bias', 'm_attn_sinks', 'm_w_attn_out', 'm_conv_w', 'm_conv_b', 'm_lru_wa', 'm_lru_ba', 'm_lru_wx', 'm_lru_bx', 'm_lru_lambda', 'm_w_lru_out', 'm_mem_norm', 'm_w_mem_kv', 'm_w_mem_out', 'm_w_out', 'm_ffn2_norm', 'm_ffn2_w_gate', 'm_ffn2_w_up', 'm_ffn2_w_down', 'm_final_norm', 'v_ffn1_norm', 'v_ffn1_w_gate', 'v_ffn1_w_up', 'v_ffn1_w_down', 'v_mix_norm', 'v_w_in', 'v_gate_bias', 'v_attn_sinks', 'v_w_attn_out', 'v_conv_w', 'v_conv_b', 'v_lru_wa', 'v_lru_ba', 'v_lru_wx', 'v_lru_bx', 'v_lru_lambda', 'v_w_lru_out', 'v_mem_norm', 'v_w_mem_kv', 'v_w_mem_out', 'v_w_out', 'v_ffn2_norm', 'v_ffn2_w_gate', 'v_ffn2_w_up', 'v_ffn2_w_down', 'v_final_norm']
TWIN_OUTPUTS = ['loss', 'grad_x', 'grad_ffn1_norm', 'grad_ffn1_w_gate', 'grad_ffn1_w_up', 'grad_ffn1_w_down', 'grad_mix_norm', 'grad_w_in', 'grad_gate_bias', 'grad_attn_sinks', 'grad_w_attn_out', 'grad_conv_w', 'grad_conv_b', 'grad_lru_wa', 'grad_lru_ba', 'grad_lru_wx', 'grad_lru_bx', 'grad_lru_lambda', 'grad_w_lru_out', 'grad_mem_norm', 'grad_w_mem_kv', 'grad_w_mem_out', 'grad_w_out', 'grad_ffn2_norm', 'grad_ffn2_w_gate', 'grad_ffn2_w_up', 'grad_ffn2_w_down', 'grad_final_norm', 'delta_ffn1_norm', 'delta_ffn1_w_gate', 'delta_ffn1_w_up', 'delta_ffn1_w_down', 'delta_mix_norm', 'delta_w_in', 'delta_gate_bias', 'delta_attn_sinks', 'delta_w_attn_out', 'delta_conv_w', 'delta_conv_b', 'delta_lru_wa', 'delta_lru_ba', 'delta_lru_wx', 'delta_lru_bx', 'delta_lru_lambda', 'delta_w_lru_out', 'delta_mem_norm', 'delta_w_mem_kv', 'delta_w_mem_out', 'delta_w_out', 'delta_ffn2_norm', 'delta_ffn2_w_gate', 'delta_ffn2_w_up', 'delta_ffn2_w_down', 'delta_final_norm', 'new_m_ffn1_norm', 'new_m_ffn1_w_gate', 'new_m_ffn1_w_up', 'new_m_ffn1_w_down', 'new_m_mix_norm', 'new_m_w_in', 'new_m_gate_bias', 'new_m_attn_sinks', 'new_m_w_attn_out', 'new_m_conv_w', 'new_m_conv_b', 'new_m_lru_wa', 'new_m_lru_ba', 'new_m_lru_wx', 'new_m_lru_bx', 'new_m_lru_lambda', 'new_m_w_lru_out', 'new_m_mem_norm', 'new_m_w_mem_kv', 'new_m_w_mem_out', 'new_m_w_out', 'new_m_ffn2_norm', 'new_m_ffn2_w_gate', 'new_m_ffn2_w_up', 'new_m_ffn2_w_down', 'new_m_final_norm', 'new_v_ffn1_norm', 'new_v_ffn1_w_gate', 'new_v_ffn1_w_up', 'new_v_ffn1_w_down', 'new_v_mix_norm', 'new_v_w_in', 'new_v_gate_bias', 'new_v_attn_sinks', 'new_v_w_attn_out', 'new_v_conv_w', 'new_v_conv_b', 'new_v_lru_wa', 'new_v_lru_ba', 'new_v_lru_wx', 'new_v_lru_bx', 'new_v_lru_lambda', 'new_v_w_lru_out', 'new_v_mem_norm', 'new_v_w_mem_kv', 'new_v_w_mem_out', 'new_v_w_out', 'new_v_ffn2_norm', 'new_v_ffn2_w_gate', 'new_v_ffn2_w_up', 'new_v_ffn2_w_down', 'new_v_final_norm']
TWIN_LEAF_KINDS = {'loss': 'loss', 'grad_x': 'grad_x', 'grad_ffn1_norm': 'grad_w', 'grad_ffn1_w_gate': 'grad_w', 'grad_ffn1_w_up': 'grad_w', 'grad_ffn1_w_down': 'grad_w', 'grad_mix_norm': 'grad_w', 'grad_w_in': 'grad_w', 'grad_gate_bias': 'grad_w', 'grad_attn_sinks': 'grad_w', 'grad_w_attn_out': 'grad_w', 'grad_conv_w': 'grad_w', 'grad_conv_b': 'grad_w', 'grad_lru_wa': 'grad_w', 'grad_lru_ba': 'grad_w', 'grad_lru_wx': 'grad_w', 'grad_lru_bx': 'grad_w', 'grad_lru_lambda': 'grad_w', 'grad_w_lru_out': 'grad_w', 'grad_mem_norm': 'grad_w', 'grad_w_mem_kv': 'grad_w', 'grad_w_mem_out': 'grad_w', 'grad_w_out': 'grad_w', 'grad_ffn2_norm': 'grad_w', 'grad_ffn2_w_gate': 'grad_w', 'grad_ffn2_w_up': 'grad_w', 'grad_ffn2_w_down': 'grad_w', 'grad_final_norm': 'grad_w', 'delta_ffn1_norm': 'delta_w', 'delta_ffn1_w_gate': 'delta_w', 'delta_ffn1_w_up': 'delta_w', 'delta_ffn1_w_down': 'delta_w', 'delta_mix_norm': 'delta_w', 'delta_w_in': 'delta_w', 'delta_gate_bias': 'delta_w', 'delta_attn_sinks': 'delta_w', 'delta_w_attn_out': 'delta_w', 'delta_conv_w': 'delta_w', 'delta_conv_b': 'delta_w', 'delta_lru_wa': 'delta_w', 'delta_lru_ba': 'delta_w', 'delta_lru_wx': 'delta_w', 'delta_lru_bx': 'delta_w', 'delta_lru_lambda': 'delta_w', 'delta_w_lru_out': 'delta_w', 'delta_mem_norm': 'delta_w', 'delta_w_mem_kv': 'delta_w', 'delta_w_mem_out': 'delta_w', 'delta_w_out': 'delta_w', 'delta_ffn2_norm': 'delta_w', 'delta_ffn2_w_gate': 'delta_w', 'delta_ffn2_w_up': 'delta_w', 'delta_ffn2_w_down': 'delta_w', 'delta_final_norm': 'delta_w', 'new_m_ffn1_norm': 'new_m', 'new_m_ffn1_w_gate': 'new_m', 'new_m_ffn1_w_up': 'new_m', 'new_m_ffn1_w_down': 'new_m', 'new_m_mix_norm': 'new_m', 'new_m_w_in': 'new_m', 'new_m_gate_bias': 'new_m', 'new_m_attn_sinks': 'new_m', 'new_m_w_attn_out': 'new_m', 'new_m_conv_w': 'new_m', 'new_m_conv_b': 'new_m', 'new_m_lru_wa': 'new_m', 'new_m_lru_ba': 'new_m', 'new_m_lru_wx': 'new_m', 'new_m_lru_bx': 'new_m', 'new_m_lru_lambda': 'new_m', 'new_m_w_lru_out': 'new_m', 'new_m_mem_norm': 'new_m', 'new_m_w_mem_kv': 'new_m', 'new_m_w_mem_out': 'new_m', 'new_m_w_out': 'new_m', 'new_m_ffn2_norm': 'new_m', 'new_m_ffn2_w_gate': 'new_m', 'new_m_ffn2_w_up': 'new_m', 'new_m_ffn2_w_down': 'new_m', 'new_m_final_norm': 'new_m', 'new_v_ffn1_norm': 'new_v', 'new_v_ffn1_w_gate': 'new_v', 'new_v_ffn1_w_up': 'new_v', 'new_v_ffn1_w_down': 'new_v', 'new_v_mix_norm': 'new_v', 'new_v_w_in': 'new_v', 'new_v_gate_bias': 'new_v', 'new_v_attn_sinks': 'new_v', 'new_v_w_attn_out': 'new_v', 'new_v_conv_w': 'new_v', 'new_v_conv_b': 'new_v', 'new_v_lru_wa': 'new_v', 'new_v_lru_ba': 'new_v', 'new_v_lru_wx': 'new_v', 'new_v_lru_bx': 'new_v', 'new_v_lru_lambda': 'new_v', 'new_v_w_lru_out': 'new_v', 'new_v_mem_norm': 'new_v', 'new_v_w_mem_kv': 'new_v', 'new_v_w_mem_out': 'new_v', 'new_v_w_out': 'new_v', 'new_v_ffn2_norm': 'new_v', 'new_v_ffn2_w_gate': 'new_v', 'new_v_ffn2_w_up': 'new_v', 'new_v_ffn2_w_down': 'new_v', 'new_v_final_norm': 'new_v'}


def _forward(args):
    return _fwd_reference(*[args[k] for k in FWD_PARAMS])


def _output_shape():
    def fwd():
        inp = _fwd_setup_inputs(0)
        return _fwd_reference(*[inp[k] for k in FWD_PARAMS])
    out = _jax.eval_shape(fwd)
    return out.shape, out.dtype

N_MICROBATCH = 1
ADAM_LR = 0.001
ADAM_B1 = 0.9
ADAM_B2 = 0.999
ADAM_EPS = 1e-08
ADAM_WD = 0.01
ADAM_STEP = 10
PER_EXAMPLE_BATCH_AXIS = {'x': 0, 'mem': 0, 'loss_target': 0}
SHARED_INPUTS = []
_WEIGHT_DTYPES = {'ffn1_norm': _jnp.float32, 'ffn1_w_gate': _jnp.float32, 'ffn1_w_up': _jnp.float32, 'ffn1_w_down': _jnp.float32, 'mix_norm': _jnp.float32, 'w_in': _jnp.float32, 'gate_bias': _jnp.float32, 'attn_sinks': _jnp.float32, 'w_attn_out': _jnp.float32, 'conv_w': _jnp.float32, 'conv_b': _jnp.float32, 'lru_wa': _jnp.float32, 'lru_ba': _jnp.float32, 'lru_wx': _jnp.float32, 'lru_bx': _jnp.float32, 'lru_lambda': _jnp.float32, 'w_lru_out': _jnp.float32, 'mem_norm': _jnp.float32, 'w_mem_kv': _jnp.float32, 'w_mem_out': _jnp.float32, 'w_out': _jnp.float32, 'ffn2_norm': _jnp.float32, 'ffn2_w_gate': _jnp.float32, 'ffn2_w_up': _jnp.float32, 'ffn2_w_down': _jnp.float32, 'final_norm': _jnp.float32}
MOMENT_SCALE = {'ffn1_norm': 1.144763e-01, 'ffn1_w_gate': 4.845921e-02, 'ffn1_w_up': 4.716311e-02, 'ffn1_w_down': 7.817572e-02, 'mix_norm': 1.357971e-01, 'w_in': 5.418306e-02, 'gate_bias': 2.689736e-02, 'attn_sinks': 4.670020e-02, 'w_attn_out': 3.877598e-02, 'conv_w': 1.018400e-01, 'conv_b': 1.076798e+00, 'lru_wa': 3.546168e-02, 'lru_ba': 2.685920e-02, 'lru_wx': 6.392046e-02, 'lru_bx': 3.712666e-02, 'lru_lambda': 5.101217e-02, 'w_lru_out': 1.048107e-01, 'mem_norm': 2.660062e-02, 'w_mem_kv': 2.689409e-02, 'w_mem_out': 2.348943e-02, 'w_out': 1.122946e-01, 'ffn2_norm': 9.860834e-02, 'ffn2_w_gate': 4.275919e-02, 'ffn2_w_up': 4.146350e-02, 'ffn2_w_down': 6.869102e-02, 'final_norm': 6.404143e+01}


def _to_microbatches(a, axis):
    t = _jnp.moveaxis(a, axis, 0)
    t = t.reshape((N_MICROBATCH, t.shape[0] // N_MICROBATCH) + t.shape[1:])
    return _jnp.moveaxis(t, 1, axis + 1)


def setup_inputs(seed: int = 0) -> dict:
    inp = _fwd_setup_inputs(seed)
    key = _jax.random.fold_in(_jax.random.key(seed), 7919)
    shape, _ = _output_shape()
    out = dict(inp)
    out["loss_target"] = _jax.random.normal(_jax.random.fold_in(key, 0), shape, _jnp.float32)
    for i, name in enumerate(TWIN_WEIGHTS):
        w = inp[name].astype(_jnp.float32)
        if MOMENT_SCALE is None:
            s = _jnp.sqrt(_jnp.mean(_jnp.square(w)) + 1e-30)
        else:
            s = MOMENT_SCALE[name]
        km, kv = _jax.random.split(_jax.random.fold_in(key, i + 1))
        out[name] = w
        out["m_" + name] = s * _jax.random.normal(km, w.shape, _jnp.float32)
        out["v_" + name] = (s * s) * _jax.random.uniform(kv, w.shape, _jnp.float32, 0.5, 1.5)
    if N_MICROBATCH > 1:
        for name, axis in PER_EXAMPLE_BATCH_AXIS.items():
            out[name] = _to_microbatches(out[name], axis)
    return {'x': out['x'], 'mem': out['mem'], 'ffn1_norm': out['ffn1_norm'], 'ffn1_w_gate': out['ffn1_w_gate'], 'ffn1_w_up': out['ffn1_w_up'], 'ffn1_w_down': out['ffn1_w_down'], 'mix_norm': out['mix_norm'], 'w_in': out['w_in'], 'gate_bias': out['gate_bias'], 'attn_sinks': out['attn_sinks'], 'w_attn_out': out['w_attn_out'], 'conv_w': out['conv_w'], 'conv_b': out['conv_b'], 'lru_wa': out['lru_wa'], 'lru_ba': out['lru_ba'], 'lru_wx': out['lru_wx'], 'lru_bx': out['lru_bx'], 'lru_lambda': out['lru_lambda'], 'w_lru_out': out['w_lru_out'], 'mem_norm': out['mem_norm'], 'w_mem_kv': out['w_mem_kv'], 'w_mem_out': out['w_mem_out'], 'w_out': out['w_out'], 'ffn2_norm': out['ffn2_norm'], 'ffn2_w_gate': out['ffn2_w_gate'], 'ffn2_w_up': out['ffn2_w_up'], 'ffn2_w_down': out['ffn2_w_down'], 'final_norm': out['final_norm'], 'loss_target': out['loss_target'], 'm_ffn1_norm': out['m_ffn1_norm'], 'm_ffn1_w_gate': out['m_ffn1_w_gate'], 'm_ffn1_w_up': out['m_ffn1_w_up'], 'm_ffn1_w_down': out['m_ffn1_w_down'], 'm_mix_norm': out['m_mix_norm'], 'm_w_in': out['m_w_in'], 'm_gate_bias': out['m_gate_bias'], 'm_attn_sinks': out['m_attn_sinks'], 'm_w_attn_out': out['m_w_attn_out'], 'm_conv_w': out['m_conv_w'], 'm_conv_b': out['m_conv_b'], 'm_lru_wa': out['m_lru_wa'], 'm_lru_ba': out['m_lru_ba'], 'm_lru_wx': out['m_lru_wx'], 'm_lru_bx': out['m_lru_bx'], 'm_lru_lambda': out['m_lru_lambda'], 'm_w_lru_out': out['m_w_lru_out'], 'm_mem_norm': out['m_mem_norm'], 'm_w_mem_kv': out['m_w_mem_kv'], 'm_w_mem_out': out['m_w_mem_out'], 'm_w_out': out['m_w_out'], 'm_ffn2_norm': out['m_ffn2_norm'], 'm_ffn2_w_gate': out['m_ffn2_w_gate'], 'm_ffn2_w_up': out['m_ffn2_w_up'], 'm_ffn2_w_down': out['m_ffn2_w_down'], 'm_final_norm': out['m_final_norm'], 'v_ffn1_norm': out['v_ffn1_norm'], 'v_ffn1_w_gate': out['v_ffn1_w_gate'], 'v_ffn1_w_up': out['v_ffn1_w_up'], 'v_ffn1_w_down': out['v_ffn1_w_down'], 'v_mix_norm': out['v_mix_norm'], 'v_w_in': out['v_w_in'], 'v_gate_bias': out['v_gate_bias'], 'v_attn_sinks': out['v_attn_sinks'], 'v_w_attn_out': out['v_w_attn_out'], 'v_conv_w': out['v_conv_w'], 'v_conv_b': out['v_conv_b'], 'v_lru_wa': out['v_lru_wa'], 'v_lru_ba': out['v_lru_ba'], 'v_lru_wx': out['v_lru_wx'], 'v_lru_bx': out['v_lru_bx'], 'v_lru_lambda': out['v_lru_lambda'], 'v_w_lru_out': out['v_w_lru_out'], 'v_mem_norm': out['v_mem_norm'], 'v_w_mem_kv': out['v_w_mem_kv'], 'v_w_mem_out': out['v_w_mem_out'], 'v_w_out': out['v_w_out'], 'v_ffn2_norm': out['v_ffn2_norm'], 'v_ffn2_w_gate': out['v_ffn2_w_gate'], 'v_ffn2_w_up': out['v_ffn2_w_up'], 'v_ffn2_w_down': out['v_ffn2_w_down'], 'v_final_norm': out['v_final_norm']}


def _loss(weights, diff, rest, loss_target):
    with _jax.named_scope("forward"):
        args = {**rest, TWIN_DIFF_INPUT: diff, **{k: w.astype(_WEIGHT_DTYPES[k]) for k, w in weights.items()}}
        y = _forward(args)
    with _jax.named_scope("loss_head"):
        err = _jnp.square(y.astype(_jnp.float32) - loss_target)
        return 0.5 * _jnp.sum(_jnp.mean(err, axis=-1)) if err.ndim else 0.5 * err


def _adamw(w, g, m, v):
    m = ADAM_B1 * m + (1.0 - ADAM_B1) * g
    v = ADAM_B2 * v + (1.0 - ADAM_B2) * _jnp.square(g)
    m_hat = m / (1.0 - ADAM_B1 ** ADAM_STEP)
    v_hat = v / (1.0 - ADAM_B2 ** ADAM_STEP)
    delta = -ADAM_LR * (m_hat / (_jnp.sqrt(v_hat) + ADAM_EPS) + ADAM_WD * w)
    return delta, m, v


def reference(x, mem, ffn1_norm, ffn1_w_gate, ffn1_w_up, ffn1_w_down, mix_norm, w_in, gate_bias, attn_sinks, w_attn_out, conv_w, conv_b, lru_wa, lru_ba, lru_wx, lru_bx, lru_lambda, w_lru_out, mem_norm, w_mem_kv, w_mem_out, w_out, ffn2_norm, ffn2_w_gate, ffn2_w_up, ffn2_w_down, final_norm, loss_target, m_ffn1_norm, m_ffn1_w_gate, m_ffn1_w_up, m_ffn1_w_down, m_mix_norm, m_w_in, m_gate_bias, m_attn_sinks, m_w_attn_out, m_conv_w, m_conv_b, m_lru_wa, m_lru_ba, m_lru_wx, m_lru_bx, m_lru_lambda, m_w_lru_out, m_mem_norm, m_w_mem_kv, m_w_mem_out, m_w_out, m_ffn2_norm, m_ffn2_w_gate, m_ffn2_w_up, m_ffn2_w_down, m_final_norm, v_ffn1_norm, v_ffn1_w_gate, v_ffn1_w_up, v_ffn1_w_down, v_mix_norm, v_w_in, v_gate_bias, v_attn_sinks, v_w_attn_out, v_conv_w, v_conv_b, v_lru_wa, v_lru_ba, v_lru_wx, v_lru_bx, v_lru_lambda, v_w_lru_out, v_mem_norm, v_w_mem_kv, v_w_mem_out, v_w_out, v_ffn2_norm, v_ffn2_w_gate, v_ffn2_w_up, v_ffn2_w_down, v_final_norm):
    given = dict(x=x, mem=mem, ffn1_norm=ffn1_norm, ffn1_w_gate=ffn1_w_gate, ffn1_w_up=ffn1_w_up, ffn1_w_down=ffn1_w_down, mix_norm=mix_norm, w_in=w_in, gate_bias=gate_bias, attn_sinks=attn_sinks, w_attn_out=w_attn_out, conv_w=conv_w, conv_b=conv_b, lru_wa=lru_wa, lru_ba=lru_ba, lru_wx=lru_wx, lru_bx=lru_bx, lru_lambda=lru_lambda, w_lru_out=w_lru_out, mem_norm=mem_norm, w_mem_kv=w_mem_kv, w_mem_out=w_mem_out, w_out=w_out, ffn2_norm=ffn2_norm, ffn2_w_gate=ffn2_w_gate, ffn2_w_up=ffn2_w_up, ffn2_w_down=ffn2_w_down, final_norm=final_norm, loss_target=loss_target, m_ffn1_norm=m_ffn1_norm, m_ffn1_w_gate=m_ffn1_w_gate, m_ffn1_w_up=m_ffn1_w_up, m_ffn1_w_down=m_ffn1_w_down, m_mix_norm=m_mix_norm, m_w_in=m_w_in, m_gate_bias=m_gate_bias, m_attn_sinks=m_attn_sinks, m_w_attn_out=m_w_attn_out, m_conv_w=m_conv_w, m_conv_b=m_conv_b, m_lru_wa=m_lru_wa, m_lru_ba=m_lru_ba, m_lru_wx=m_lru_wx, m_lru_bx=m_lru_bx, m_lru_lambda=m_lru_lambda, m_w_lru_out=m_w_lru_out, m_mem_norm=m_mem_norm, m_w_mem_kv=m_w_mem_kv, m_w_mem_out=m_w_mem_out, m_w_out=m_w_out, m_ffn2_norm=m_ffn2_norm, m_ffn2_w_gate=m_ffn2_w_gate, m_ffn2_w_up=m_ffn2_w_up, m_ffn2_w_down=m_ffn2_w_down, m_final_norm=m_final_norm, v_ffn1_norm=v_ffn1_norm, v_ffn1_w_gate=v_ffn1_w_gate, v_ffn1_w_up=v_ffn1_w_up, v_ffn1_w_down=v_ffn1_w_down, v_mix_norm=v_mix_norm, v_w_in=v_w_in, v_gate_bias=v_gate_bias, v_attn_sinks=v_attn_sinks, v_w_attn_out=v_w_attn_out, v_conv_w=v_conv_w, v_conv_b=v_conv_b, v_lru_wa=v_lru_wa, v_lru_ba=v_lru_ba, v_lru_wx=v_lru_wx, v_lru_bx=v_lru_bx, v_lru_lambda=v_lru_lambda, v_w_lru_out=v_w_lru_out, v_mem_norm=v_mem_norm, v_w_mem_kv=v_w_mem_kv, v_w_mem_out=v_w_mem_out, v_w_out=v_w_out, v_ffn2_norm=v_ffn2_norm, v_ffn2_w_gate=v_ffn2_w_gate, v_ffn2_w_up=v_ffn2_w_up, v_ffn2_w_down=v_ffn2_w_down, v_final_norm=v_final_norm)
    weights = {n: given[n] for n in TWIN_WEIGHTS}
    shared = {n: given[n] for n in SHARED_INPUTS}
    per_example = {n: given[n] for n in ['x', 'mem']}
    grad_fn = _jax.value_and_grad(_loss, argnums=(0, 1))

    def one_microbatch(ex, loss_target):
        ex = dict(ex)
        diff = ex.pop(TWIN_DIFF_INPUT)
        return grad_fn(weights, diff, {**shared, **ex}, loss_target)

    if N_MICROBATCH == 1:
        loss, (grad_w, grad_x) = one_microbatch(per_example, given["loss_target"])
    else:
        def body(carry, xs):
            loss_sum, grad_sum = carry
            l_k, (gw_k, gx_k) = one_microbatch(xs[0], xs[1])
            with _jax.named_scope("update"):
                return (loss_sum + l_k, _jax.tree.map(_jnp.add, grad_sum, gw_k)), gx_k

        init = (_jnp.zeros((), _jnp.float32), _jax.tree.map(_jnp.zeros_like, weights))
        (loss, grad_w), grad_x = _jax.lax.scan(body, init, (per_example, given["loss_target"]))
    with _jax.named_scope("update"):
        delta_w, new_m, new_v = {}, {}, {}
        for n in TWIN_WEIGHTS:
            delta_w[n], new_m[n], new_v[n] = _adamw(weights[n], grad_w[n], given["m_" + n], given["v_" + n])
    return (loss, grad_x, *[grad_w[n] for n in TWIN_WEIGHTS], *[delta_w[n] for n in TWIN_WEIGHTS],
            *[new_m[n] for n in TWIN_WEIGHTS], *[new_v[n] for n in TWIN_WEIGHTS])
```

```python
import functools

import jax
import jax.numpy as jnp
import numpy as np
from jax import lax
from jax.experimental import pallas as pl
from jax.experimental.pallas import tpu as pltpu

F32 = jnp.float32
MXU_DTYPE = jnp.bfloat16

DEPTH = 4
D_MODEL = 1024
CHUNK = 64
HALO = 2 * CHUNK
N_Q_HEADS = 8
HEAD_DIM = 64
ATTN_WIDTH = 512
KV_WIDTH = 128
LRU_WIDTH = 1024
LRU_TILE = 128
CONV_WIDTH = 4
LRU_C = 8.0
MEM_HEADS = 4
MEM_HEAD_DIM = 128
MEM_WIDTH = 512
D_FF = 2816
EPS = 1e-6
ADAM_LR = 0.001
ADAM_B1 = 0.9
ADAM_B2 = 0.999
ADAM_EPS = 1e-08
ADAM_WD = 0.01
ADAM_STEP = 10

N_CHIPS = 4
N_DEVICES = 8
SUBLANES = 8
LANES = 128
VMEM_LIMIT_BYTES = 56 * 1024 * 1024
NEG_BIG = -1e30
MESH = pl.DeviceIdType.MESH

ALIBI_SLOPES = tuple(float(2.0 ** (-8.0 * (i + 1) / N_Q_HEADS)) for i in range(N_Q_HEADS))


def _params(*semantics):
    return pltpu.CompilerParams(dimension_semantics=semantics, vmem_limit_bytes=VMEM_LIMIT_BYTES)


def _tile(n, pref, unit=SUBLANES):
    if n <= pref:
        return n
    for t in range(pref - pref % unit, 0, -unit):
        if n % t == 0:
            return t
    return n


def _dot(a, b, ca, cb):
    return lax.dot_general(a, b, (((ca,), (cb,)), ((), ())), preferred_element_type=F32)


def _sigmoid(x):
    return jax.nn.sigmoid(x)


def rms_fwd(x, g):
    T, D = x.shape
    tm = _tile(T, 512)

    def body(x_ref, g_ref, h_ref):
        xv = x_ref[...]
        r = lax.rsqrt(jnp.mean(xv * xv, axis=-1, keepdims=True) + EPS)
        h_ref[...] = (xv * r * g_ref[...]).astype(h_ref.dtype)

    return pl.pallas_call(
        body, name="rms_fwd", grid=(T // tm,),
        in_specs=[pl.BlockSpec((tm, D), lambda i: (i, 0)), pl.BlockSpec((1, D), lambda i: (0, 0))],
        out_specs=pl.BlockSpec((tm, D), lambda i: (i, 0)),
        out_shape=jax.ShapeDtypeStruct((T, D), MXU_DTYPE),
        compiler_params=_params("parallel"),
    )(x, g)


def rms_bwd(x, g, dh, dy):
    T, D = x.shape
    tm = _tile(T, 512)

    def body(x_ref, g_ref, dh_ref, dy_ref, dx_ref, dg_ref):
        i = pl.program_id(0)
        xv = x_ref[...]
        r = lax.rsqrt(jnp.mean(xv * xv, axis=-1, keepdims=True) + EPS)
        xh = xv * r
        dhv = dh_ref[...]
        gd = dhv * g_ref[...]
        dx_ref[...] = dy_ref[...] + r * (gd - xh * jnp.mean(gd * xh, axis=-1, keepdims=True))
        part = jnp.sum(dhv * xh, axis=0, keepdims=True)

        @pl.when(i == 0)
        def _():
            dg_ref[...] = part

        @pl.when(i > 0)
        def _():
            dg_ref[...] += part

    row = pl.BlockSpec((tm, D), lambda i: (i, 0))
    vec = pl.BlockSpec((1, D), lambda i: (0, 0))
    return pl.pallas_call(
        body, name="rms_bwd", grid=(T // tm,),
        in_specs=[row, vec, row, row], out_specs=[row, vec],
        out_shape=[jax.ShapeDtypeStruct((T, D), F32), jax.ShapeDtypeStruct((1, D), F32)],
        compiler_params=_params("arbitrary"),
    )(x, g, dh, dy)


def mm(a_list, b_list, *, nt, out_dtype, res=None, scale=1.0, tm=256, tn=1024, name="mm"):
    n_pairs = len(a_list)
    T = a_list[0].shape[0]
    N = b_list[0].shape[0] if nt else b_list[0].shape[1]
    tm = _tile(T, tm)
    tn = _tile(N, tn, LANES)
    has_res = res is not None

    def body(*refs):
        a_refs = refs[:n_pairs]
        b_refs = refs[n_pairs:2 * n_pairs]
        o_ref = refs[-1]
        acc = None
        for a_ref, b_ref in zip(a_refs, b_refs):
            d = _dot(a_ref[...].astype(MXU_DTYPE), b_ref[...], 1, 1 if nt else 0)
            acc = d if acc is None else acc + d
        if has_res:
            acc = refs[2 * n_pairs][...] + scale * acc
        elif scale != 1.0:
            acc = scale * acc
        o_ref[...] = acc.astype(o_ref.dtype)

    in_specs = [pl.BlockSpec((tm, a.shape[1]), lambda j, i: (i, 0)) for a in a_list]
    if nt:
        in_specs += [pl.BlockSpec((tn, b.shape[1]), lambda j, i: (j, 0)) for b in b_list]
    else:
        in_specs += [pl.BlockSpec((b.shape[0], tn), lambda j, i: (0, j)) for b in b_list]
    out_spec = pl.BlockSpec((tm, tn), lambda j, i: (i, j))
    operands = list(a_list) + list(b_list)
    if has_res:
        in_specs.append(out_spec)
        operands.append(res)
    return pl.pallas_call(
        body, name=name, grid=(N // tn, T // tm), in_specs=in_specs, out_specs=out_spec,
        out_shape=jax.ShapeDtypeStruct((T, N), out_dtype),
        compiler_params=_params("parallel", "parallel"),
    )(*operands)


def mm_tn(a, b, *, out_dtype, b_scale=1.0, tk=1408, tn=1408, tt=1024, name="mm_tn"):
    T, K = a.shape
    N = b.shape[1]
    tk = _tile(K, tk, LANES)
    tn = _tile(N, tn, LANES)
    tt = _tile(T, tt)
    steps = T // tt

    def body(a_ref, b_ref, o_ref, acc_ref):
        t = pl.program_id(2)
        bv = b_ref[...]
        if b_scale != 1.0:
            bv = b_scale * bv
        d = _dot(a_ref[...].astype(MXU_DTYPE), bv.astype(MXU_DTYPE), 0, 0)

        @pl.when(t == 0)
        def _():
            acc_ref[...] = d

        @pl.when(t > 0)
        def _():
            acc_ref[...] += d

        @pl.when(t == steps - 1)
        def _():
            o_ref[...] = acc_ref[...].astype(o_ref.dtype)

    return pl.pallas_call(
        body, name=name, grid=(K // tk, N // tn, steps),
        in_specs=[pl.BlockSpec((tt, tk), lambda p, q, t: (t, p)), pl.BlockSpec((tt, tn), lambda p, q, t: (t, q))],
        out_specs=pl.BlockSpec((tk, tn), lambda p, q, t: (p, q)),
        out_shape=jax.ShapeDtypeStruct((K, N), out_dtype),
        scratch_shapes=[pltpu.VMEM((tk, tn), F32)],
        compiler_params=_params("parallel", "parallel", "arbitrary"),
    )(a, b)


def ffn_up(h, w_gate, w_up):
    T, D = h.shape
    F = w_gate.shape[1]
    tm = _tile(T, 256)
    tn = _tile(F, 1408, LANES)

    def body(h_ref, wg_ref, wu_ref, g_ref, u_ref, a_ref):
        hv = h_ref[...]
        G = _dot(hv, wg_ref[...], 1, 0)
        U = _dot(hv, wu_ref[...], 1, 0)
        g_ref[...] = G
        u_ref[...] = U
        a_ref[...] = (G * _sigmoid(G) * U).astype(a_ref.dtype)

    w_spec = pl.BlockSpec((D, tn), lambda j, i: (0, j))
    o_spec = pl.BlockSpec((tm, tn), lambda j, i: (i, j))
    return pl.pallas_call(
        body, name="ffn_up", grid=(F // tn, T // tm),
        in_specs=[pl.BlockSpec((tm, D), lambda j, i: (i, 0)), w_spec, w_spec],
        out_specs=[o_spec, o_spec, o_spec],
        out_shape=[jax.ShapeDtypeStruct((T, F), F32), jax.ShapeDtypeStruct((T, F), F32),
                   jax.ShapeDtypeStruct((T, F), MXU_DTYPE)],
        compiler_params=_params("parallel", "parallel"),
    )(h, w_gate, w_up)


def ffn_bwd_act(dy, w_down, G, U):
    T, D = dy.shape
    F = w_down.shape[0]
    tm = _tile(T, 256)
    tn = _tile(F, 1408, LANES)

    def body(dy_ref, wd_ref, g_ref, u_ref, dg_ref, du_ref):
        d_out = (0.5 * dy_ref[...]).astype(MXU_DTYPE)
        dA = _dot(d_out, wd_ref[...], 1, 1)
        Gv = g_ref[...]
        s = _sigmoid(Gv)
        du_ref[...] = (dA * (Gv * s)).astype(du_ref.dtype)
        dg_ref[...] = (dA * u_ref[...] * (s * (1.0 + Gv * (1.0 - s)))).astype(dg_ref.dtype)

    o_spec = pl.BlockSpec((tm, tn), lambda j, i: (i, j))
    return pl.pallas_call(
        body, name="ffn_bwd_act", grid=(F // tn, T // tm),
        in_specs=[pl.BlockSpec((tm, D), lambda j, i: (i, 0)), pl.BlockSpec((tn, D), lambda j, i: (j, 0)),
                  o_spec, o_spec],
        out_specs=[o_spec, o_spec],
        out_shape=[jax.ShapeDtypeStruct((T, F), MXU_DTYPE), jax.ShapeDtypeStruct((T, F), MXU_DTYPE)],
        compiler_params=_params("parallel", "parallel"),
    )(dy, w_down, G, U)


ATTN_ROWS = 256


def _head_variants(kv128):
    lane = lax.broadcasted_iota(jnp.int32, kv128.shape, 1)
    low = lane < HEAD_DIM
    rolled = pltpu.roll(kv128, HEAD_DIM, 1)
    out = {}
    for j in (0, 1):
        for pos in (0, 1):
            src = kv128 if j == pos else rolled
            out[j, pos] = jnp.where(low if pos == 0 else jnp.logical_not(low), src, 0.0).astype(MXU_DTYPE)
    return out


def _fold_variant(d128, j, pos):
    lane = lax.broadcasted_iota(jnp.int32, d128.shape, 1)
    low = lane < HEAD_DIM
    kept = jnp.where(low if pos == 0 else jnp.logical_not(low), d128, 0.0)
    return kept if j == pos else pltpu.roll(kept, HEAD_DIM, 1)


def _attn_mask(block, rows, keys):
    r = lax.broadcasted_iota(jnp.int32, (rows, keys), 0)
    c = lax.broadcasted_iota(jnp.int32, (rows, keys), 1)
    q_chunk = r // CHUNK
    k_chunk = c // CHUNK - HALO // CHUNK
    first_chunk = jnp.where(block > 0, -(HALO // CHUNK), 0)
    valid = (k_chunk <= q_chunk) & (k_chunk >= q_chunk - HALO // CHUNK) & (k_chunk >= first_chunk)
    dist = jnp.abs(r + HALO - c).astype(F32)
    return valid, dist


def _attn_probs(q128, k_var, head, sink, valid, dist):
    s = _dot(q128, k_var, 1, 1) * (HEAD_DIM ** -0.5) - ALIBI_SLOPES[head] * dist
    s = jnp.where(valid, s, NEG_BIG)
    mx = jnp.maximum(jnp.max(s, axis=-1, keepdims=True), sink)
    p = jnp.exp(s - mx)
    p_sink = jnp.exp(sink - mx)
    inv = 1.0 / (jnp.sum(p, axis=-1, keepdims=True) + p_sink)
    return p * inv, p_sink * inv


def _attn_specs(T, tq, order):
    ratio = tq // HALO
    q_spec = pl.BlockSpec((tq, ATTN_WIDTH), lambda i: (order(i), 0))
    cur = lambda col: pl.BlockSpec((tq, KV_WIDTH), lambda i: (order(i), col))
    prev = lambda col: pl.BlockSpec((HALO, KV_WIDTH), lambda i: (jnp.maximum(order(i) * ratio - 1, 0), col))
    k_col, v_col = ATTN_WIDTH // KV_WIDTH, ATTN_WIDTH // KV_WIDTH + 1
    return [pl.BlockSpec(memory_space=pltpu.SMEM), q_spec, prev(k_col), cur(k_col), prev(v_col), cur(v_col)]


def attn_fwd(qkv, sinks):
    T = qkv.shape[0]
    tq = _tile(T, ATTN_ROWS)
    keys = tq + HALO

    def body(sink_ref, q_ref, kp_ref, kc_ref, vp_ref, vc_ref, o_ref):
        block = pl.program_id(0)
        k_var = _head_variants(jnp.concatenate([kp_ref[...], kc_ref[...]], axis=0))
        v_var = _head_variants(jnp.concatenate([vp_ref[...], vc_ref[...]], axis=0))
        valid, dist = _attn_mask(block, tq, keys)
        for pair in range(N_Q_HEADS // 2):
            q128 = q_ref[:, pair * LANES:(pair + 1) * LANES].astype(MXU_DTYPE)
            acc = None
            for pos in (0, 1):
                head = 2 * pair + pos
                j = head // 4
                p, _ = _attn_probs(q128, k_var[j, pos], head, sink_ref[head], valid, dist)
                d = _dot(p.astype(MXU_DTYPE), v_var[j, pos], 1, 0)
                acc = d if acc is None else acc + d
            o_ref[:, pair * LANES:(pair + 1) * LANES] = acc.astype(o_ref.dtype)

    return pl.pallas_call(
        body, name="attn_fwd", grid=(T // tq,),
        in_specs=_attn_specs(T, tq, lambda i: i),
        out_specs=pl.BlockSpec((tq, ATTN_WIDTH), lambda i: (i, 0)),
        out_shape=jax.ShapeDtypeStruct((T, ATTN_WIDTH), MXU_DTYPE),
        compiler_params=_params("parallel"),
    )(sinks, qkv, qkv, qkv, qkv, qkv)


def attn_bwd(qkv, sinks, d_out):
    T = qkv.shape[0]
    tq = _tile(T, ATTN_ROWS)
    keys = tq + HALO
    n_blocks = T // tq
    order = lambda i: n_blocks - 1 - i

    def body(sink_ref, q_ref, kp_ref, kc_ref, vp_ref, vc_ref, do_ref, dqkv_ref, dsink_ref, carry_ref):
        step = pl.program_id(0)
        block = order(step)
        k_var = _head_variants(jnp.concatenate([kp_ref[...], kc_ref[...]], axis=0))
        v_var = _head_variants(jnp.concatenate([vp_ref[...], vc_ref[...]], axis=0))
        valid, dist = _attn_mask(block, tq, keys)

        @pl.when(step == 0)
        def _():
            carry_ref[...] = jnp.zeros_like(carry_ref)
            dsink_ref[...] = jnp.zeros_like(dsink_ref)

        dk = jnp.zeros((keys, KV_WIDTH), F32)
        dv = jnp.zeros((keys, KV_WIDTH), F32)
        for pair in range(N_Q_HEADS // 2):
            q128 = q_ref[:, pair * LANES:(pair + 1) * LANES].astype(MXU_DTYPE)
            do128 = do_ref[:, pair * LANES:(pair + 1) * LANES].astype(MXU_DTYPE)
            dq128 = None
            for pos in (0, 1):
                head = 2 * pair + pos
                j = head // 4
                p, p_sink = _attn_probs(q128, k_var[j, pos], head, sink_ref[head], valid, dist)
                dp = _dot(do128, v_var[j, pos], 1, 1)
                delta = jnp.sum(p * dp, axis=-1, keepdims=True)
                ds = (p * (dp - delta) * (HEAD_DIM ** -0.5)).astype(MXU_DTYPE)
                d = _dot(ds, k_var[j, pos], 1, 0)
                dq128 = d if dq128 is None else dq128 + d
                dk = dk + _fold_variant(_dot(ds, q128, 0, 0), j, pos)
                dv = dv + _fold_variant(_dot(p.astype(MXU_DTYPE), do128, 0, 0), j, pos)
                dsink_ref[pl.ds(head, 1), :] += jnp.broadcast_to(
                    -jnp.sum(p_sink * delta, axis=0, keepdims=True), (1, LANES))
            dqkv_ref[:, pair * LANES:(pair + 1) * LANES] = dq128.astype(dqkv_ref.dtype)
        carried = jnp.concatenate([jnp.zeros((tq - HALO, 2 * KV_WIDTH), F32), carry_ref[...]], axis=0)
        dkv = jnp.concatenate([dk, dv], axis=1)
        dqkv_ref[:, ATTN_WIDTH:] = (dkv[HALO:] + carried).astype(dqkv_ref.dtype)
        carry_ref[...] = dkv[:HALO]

    out_rows = pl.BlockSpec((tq, ATTN_WIDTH + 2 * KV_WIDTH), lambda i: (order(i), 0))
    dqkv, dsink = pl.pallas_call(
        body, name="attn_bwd", grid=(n_blocks,),
        in_specs=_attn_specs(T, tq, order) + [pl.BlockSpec((tq, ATTN_WIDTH), lambda i: (order(i), 0))],
        out_specs=[out_rows, pl.BlockSpec((N_Q_HEADS, LANES), lambda i: (0, 0))],
        out_shape=[jax.ShapeDtypeStruct((T, ATTN_WIDTH + 2 * KV_WIDTH), MXU_DTYPE),
                   jax.ShapeDtypeStruct((N_Q_HEADS, LANES), F32)],
        scratch_shapes=[pltpu.VMEM((HALO, 2 * KV_WIDTH), F32)],
        compiler_params=_params("arbitrary"),
    )(sinks, qkv, qkv, qkv, qkv, qkv, d_out)
    return dqkv, dsink[:, 0]


def _mem_probs(q, k):
    s = _dot(q, k, 1, 1) * (MEM_HEAD_DIM ** -0.5)
    p = jnp.exp(s - jnp.max(s, axis=-1, keepdims=True))
    return p / jnp.sum(p, axis=-1, keepdims=True)


def mem_attn_fwd(cq, kv):
    T = cq.shape[0]
    M = kv.shape[0]
    tq = _tile(T, 512)

    def body(q_ref, kv_ref, o_ref):
        for h in range(MEM_HEADS):
            lo, hi = h * MEM_HEAD_DIM, (h + 1) * MEM_HEAD_DIM
            p = _mem_probs(q_ref[:, lo:hi].astype(MXU_DTYPE), kv_ref[:, lo:hi].astype(MXU_DTYPE))
            v = kv_ref[:, MEM_WIDTH + lo:MEM_WIDTH + hi].astype(MXU_DTYPE)
            o_ref[:, lo:hi] = _dot(p.astype(MXU_DTYPE), v, 1, 0).astype(o_ref.dtype)

    return pl.pallas_call(
        body, name="mem_attn_fwd", grid=(T // tq,),
        in_specs=[pl.BlockSpec((tq, MEM_WIDTH), lambda i: (i, 0)), pl.BlockSpec((M, 2 * MEM_WIDTH), lambda i: (0, 0))],
        out_specs=pl.BlockSpec((tq, MEM_WIDTH), lambda i: (i, 0)),
        out_shape=jax.ShapeDtypeStruct((T, MEM_WIDTH), MXU_DTYPE),
        compiler_params=_params("parallel"),
    )(cq, kv)


def mem_attn_bwd(cq, kv, d_out):
    T = cq.shape[0]
    M = kv.shape[0]
    tq = _tile(T, 512)

    def body(q_ref, kv_ref, do_ref, dq_ref, dkv_ref):
        @pl.when(pl.program_id(0) == 0)
        def _():
            dkv_ref[...] = jnp.zeros_like(dkv_ref)

        for h in range(MEM_HEADS):
            lo, hi = h * MEM_HEAD_DIM, (h + 1) * MEM_HEAD_DIM
            q = q_ref[:, lo:hi].astype(MXU_DTYPE)
            k = kv_ref[:, lo:hi].astype(MXU_DTYPE)
            v = kv_ref[:, MEM_WIDTH + lo:MEM_WIDTH + hi].astype(MXU_DTYPE)
            do = do_ref[:, lo:hi].astype(MXU_DTYPE)
            p = _mem_probs(q, k)
            dp = _dot(do, v, 1, 1)
            ds = (p * (dp - jnp.sum(p * dp, axis=-1, keepdims=True)) * (MEM_HEAD_DIM ** -0.5)).astype(MXU_DTYPE)
            dq_ref[:, lo:hi] = _dot(ds, k, 1, 0).astype(dq_ref.dtype)
            dkv_ref[:, lo:hi] += _dot(ds, q, 0, 0)
            dkv_ref[:, MEM_WIDTH + lo:MEM_WIDTH + hi] += _dot(p.astype(MXU_DTYPE), do, 0, 0)

    rows = pl.BlockSpec((tq, MEM_WIDTH), lambda i: (i, 0))
    whole = pl.BlockSpec((M, 2 * MEM_WIDTH), lambda i: (0, 0))
    return pl.pallas_call(
        body, name="mem_attn_bwd", grid=(T // tq,),
        in_specs=[rows, whole, rows], out_specs=[rows, whole],
        out_shape=[jax.ShapeDtypeStruct((T, MEM_WIDTH), MXU_DTYPE), jax.ShapeDtypeStruct((M, 2 * MEM_WIDTH), F32)],
        compiler_params=_params("arbitrary"),
    )(cq, kv, d_out)


LRU_ROWS = 256


def _shift_down(ext, k, rows):
    return pltpu.roll(ext, k, 0)[SUBLANES:SUBLANES + rows] if k else ext[SUBLANES:SUBLANES + rows]


def _shift_up(ext, k, rows):
    return pltpu.roll(ext, rows + SUBLANES - k, 0)[:rows] if k else ext[:rows]


def _conv(x_ext, conv_w_ref, conv_b_ref, rows):
    xc = conv_b_ref[...] + conv_w_ref[CONV_WIDTH - 1:CONV_WIDTH, :] * _shift_down(x_ext, 0, rows)
    for j in range(CONV_WIDTH - 1):
        xc = xc + conv_w_ref[j:j + 1, :] * _shift_down(x_ext, CONV_WIDTH - 1 - j, rows)
    return xc


def _block_matmul(x, w_ref, cb):
    return jnp.concatenate(
        [_dot(x[:, c * LANES:(c + 1) * LANES].astype(MXU_DTYPE), w_ref[c], 1, cb)
         for c in range(LRU_WIDTH // LRU_TILE)], axis=1)


def _lru_gates(block, xc, wa_ref, ba_ref, wx_ref, bx_ref, sp_ref):
    rows = xc.shape[0]
    ra = _sigmoid(_block_matmul(xc, wa_ref, 0) + ba_ref[...])
    gi = _sigmoid(_block_matmul(xc, wx_ref, 0) + bx_ref[...])
    log_a = -LRU_C * ra * sp_ref[...]
    a = jnp.exp(log_a)
    one_minus = 1.0 - jnp.exp(2.0 * log_a)
    mult = jnp.sqrt(jnp.maximum(one_minus, 0.0))
    row = lax.broadcasted_iota(jnp.int32, (rows, 1), 0)
    first = row == jnp.where(block == 0, 0, -1)
    mult = jnp.where(first, 1.0, mult)
    return ra, gi, a, one_minus, mult, first


def _gelu(x):
    inner = np.sqrt(2.0 / np.pi).astype(np.float32) * (x + 0.044715 * (x * x * x))
    return 0.5 * x * (1.0 + jnp.tanh(inner))


def _gelu_grad(x):
    c = np.sqrt(2.0 / np.pi).astype(np.float32)
    t = jnp.tanh(c * (x + 0.044715 * (x * x * x)))
    return 0.5 * (1.0 + t) + 0.5 * x * (1.0 - t * t) * c * (1.0 + 3.0 * 0.044715 * (x * x))


def _x_ext(block, prev_ref, cur_ref):
    prev = jnp.where(block > 0, prev_ref[...], 0.0)
    return jnp.concatenate([prev, cur_ref[...]], axis=0)


def _lru_in_specs(tb, order, n_rows):
    ratio = tb // SUBLANES
    rows = pl.BlockSpec((tb, LRU_WIDTH), lambda i: (order(i), 0))
    prev8 = pl.BlockSpec((SUBLANES, LRU_WIDTH), lambda i: (jnp.maximum(order(i) * ratio - 1, 0), 0))
    vec = pl.BlockSpec((1, LRU_WIDTH), lambda i: (0, 0))
    conv_w = pl.BlockSpec((CONV_WIDTH, LRU_WIDTH), lambda i: (0, 0))
    tiles = pl.BlockSpec((LRU_WIDTH // LRU_TILE, LRU_TILE, LRU_TILE), lambda i: (0, 0, 0))
    return rows, prev8, vec, conv_w, tiles


def lru_fwd(xr, yr, conv_w, conv_b, wa_t, ba, wx_t, bx, sp):
    T = xr.shape[0]
    tb = _tile(T, LRU_ROWS)

    def body(xp_ref, x_ref, y_ref, cw_ref, cb_ref, wa_ref, ba_ref, wx_ref, bx_ref, sp_ref, h_ref, r_ref, carry_ref):
        block = pl.program_id(0)
        xc = _conv(_x_ext(block, xp_ref, x_ref), cw_ref, cb_ref, tb)
        _, gi, a, _, mult, _ = _lru_gates(block, xc, wa_ref, ba_ref, wx_ref, bx_ref, sp_ref)
        b = mult * gi * xc
        row = lax.broadcasted_iota(jnp.int32, (tb, 1), 0)
        shift = 1
        while shift < tb:
            keep = row >= shift
            b = b + a * jnp.where(keep, pltpu.roll(b, shift, 0), 0.0)
            a = a * jnp.where(keep, pltpu.roll(a, shift, 0), 1.0)
            shift *= 2

        @pl.when(block == 0)
        def _():
            carry_ref[...] = jnp.zeros_like(carry_ref)

        h = b + a * carry_ref[...]
        carry_ref[...] = h[tb - 1:tb, :]
        h_ref[...] = h
        r_ref[...] = (h * _gelu(y_ref[...])).astype(r_ref.dtype)

    rows, prev8, vec, cw, tiles = _lru_in_specs(tb, lambda i: i, T)
    return pl.pallas_call(
        body, name="lru_fwd", grid=(T // tb,),
        in_specs=[prev8, rows, rows, cw, vec, tiles, vec, tiles, vec, vec],
        out_specs=[rows, rows],
        out_shape=[jax.ShapeDtypeStruct((T, LRU_WIDTH), F32), jax.ShapeDtypeStruct((T, LRU_WIDTH), MXU_DTYPE)],
        scratch_shapes=[pltpu.VMEM((1, LRU_WIDTH), F32)],
        compiler_params=_params("arbitrary"),
    )(xr, xr, yr, conv_w, conv_b, wa_t, ba, wx_t, bx, sp)


def lru_bwd(xr, yr, h, d_r, conv_w, conv_b, wa_t, ba, wx_t, bx, sp):
    T = xr.shape[0]
    tb = _tile(T, LRU_ROWS)
    n_blocks = T // tb
    order = lambda i: n_blocks - 1 - i
    n_tiles = LRU_WIDTH // LRU_TILE

    def body(xp_ref, x_ref, y_ref, hp_ref, h_ref, dr_ref, cw_ref, cb_ref, wa_ref, ba_ref, wx_ref, bx_ref, sp_ref,
             dxc_ref, dy_ref, dwa_ref, dwx_ref, dba_ref, dbx_ref, dsp_ref, dcb_ref, lam_ref, a_next_ref):
        step = pl.program_id(0)
        block = order(step)

        @pl.when(step == 0)
        def _():
            lam_ref[...] = jnp.zeros_like(lam_ref)
            a_next_ref[...] = jnp.zeros_like(a_next_ref)
            for ref in (dwa_ref, dwx_ref, dba_ref, dbx_ref, dsp_ref, dcb_ref):
                ref[...] = jnp.zeros_like(ref)

        xc = _conv(_x_ext(block, xp_ref, x_ref), cw_ref, cb_ref, tb)
        ra, gi, a, one_minus, mult, first = _lru_gates(block, xc, wa_ref, ba_ref, wx_ref, bx_ref, sp_ref)
        yv = y_ref[...]
        hv = h_ref[...]
        drv = dr_ref[...].astype(F32)
        dy_ref[...] = (drv * hv * _gelu_grad(yv)).astype(dy_ref.dtype)

        row = lax.broadcasted_iota(jnp.int32, (tb, 1), 0)
        coef = jnp.where(row == tb - 1, a_next_ref[...], pltpu.roll(a, tb - 1, 0))
        lam = drv * _gelu(yv)
        shift = 1
        while shift < tb:
            keep = row < tb - shift
            lam = lam + coef * jnp.where(keep, pltpu.roll(lam, tb - shift, 0), 0.0)
            coef = coef * jnp.where(keep, pltpu.roll(coef, tb - shift, 0), 1.0)
            shift *= 2
        lam = lam + coef * lam_ref[...]
        lam_ref[...] = lam[0:1, :]
        a_next_ref[...] = a[0:1, :]

        h_prev = _shift_down(jnp.concatenate([jnp.where(block > 0, hp_ref[...], 0.0), hv], axis=0), 1, tb)
        d_a = lam * h_prev
        d_mult = jnp.where(first, 0.0, lam * gi * xc)
        d_gi = lam * mult * xc
        d_xc = lam * mult * gi
        safe = one_minus > 0.0
        d_log_a = d_a * a + d_mult * jnp.where(safe, -(1.0 - one_minus) * lax.rsqrt(jnp.where(safe, one_minus, 1.0)), 0.0)
        d_za = d_log_a * (-LRU_C * sp_ref[...]) * ra * (1.0 - ra)
        d_zx = d_gi * gi * (1.0 - gi)
        dsp_ref[...] += jnp.sum(d_log_a * (-LRU_C * ra), axis=0, keepdims=True)
        dba_ref[...] += jnp.sum(d_za, axis=0, keepdims=True)
        dbx_ref[...] += jnp.sum(d_zx, axis=0, keepdims=True)
        d_xc = d_xc + _block_matmul(d_za, wa_ref, 1) + _block_matmul(d_zx, wx_ref, 1)
        dcb_ref[...] += jnp.sum(d_xc, axis=0, keepdims=True)
        dxc_ref[...] = d_xc
        xc_m = xc.astype(MXU_DTYPE)
        for c in range(n_tiles):
            lanes = slice(c * LANES, (c + 1) * LANES)
            dwa_ref[c] += _dot(xc_m[:, lanes], d_za[:, lanes].astype(MXU_DTYPE), 0, 0)
            dwx_ref[c] += _dot(xc_m[:, lanes], d_zx[:, lanes].astype(MXU_DTYPE), 0, 0)

    rows, prev8, vec, cw, tiles = _lru_in_specs(tb, order, T)
    return pl.pallas_call(
        body, name="lru_bwd", grid=(n_blocks,),
        in_specs=[prev8, rows, rows, prev8, rows, rows, cw, vec, tiles, vec, tiles, vec, vec],
        out_specs=[rows, rows, tiles, tiles, vec, vec, vec, vec],
        out_shape=[jax.ShapeDtypeStruct((T, LRU_WIDTH), F32), jax.ShapeDtypeStruct((T, LRU_WIDTH), MXU_DTYPE),
                   jax.ShapeDtypeStruct((n_tiles, LRU_TILE, LRU_TILE), F32),
                   jax.ShapeDtypeStruct((n_tiles, LRU_TILE, LRU_TILE), F32)]
        + [jax.ShapeDtypeStruct((1, LRU_WIDTH), F32)] * 4,
        scratch_shapes=[pltpu.VMEM((1, LRU_WIDTH), F32), pltpu.VMEM((1, LRU_WIDTH), F32)],
        compiler_params=_params("arbitrary"),
    )(xr, xr, yr, h, h, d_r, conv_w, conv_b, wa_t, ba, wx_t, bx, sp)


def conv_bwd(xr, d_xc, conv_w):
    T = xr.shape[0]
    tb = _tile(T, LRU_ROWS)
    n_blocks = T // tb
    ratio = tb // SUBLANES

    def body(xp_ref, x_ref, d_ref, dn_ref, cw_ref, dx_ref, dcw_ref):
        block = pl.program_id(0)

        @pl.when(block == 0)
        def _():
            dcw_ref[...] = jnp.zeros_like(dcw_ref)

        x_ext = _x_ext(block, xp_ref, x_ref)
        dv = d_ref[...]
        d_ext = jnp.concatenate([dv, jnp.where(block < n_blocks - 1, dn_ref[...], 0.0)], axis=0)
        dx = None
        for j in range(CONV_WIDTH):
            k = CONV_WIDTH - 1 - j
            term = cw_ref[j:j + 1, :] * _shift_up(d_ext, k, tb)
            dx = term if dx is None else dx + term
            dcw_ref[j:j + 1, :] += jnp.sum(dv * _shift_down(x_ext, k, tb), axis=0, keepdims=True)
        dx_ref[...] = dx.astype(dx_ref.dtype)

    rows = pl.BlockSpec((tb, LRU_WIDTH), lambda i: (i, 0))
    prev8 = pl.BlockSpec((SUBLANES, LRU_WIDTH), lambda i: (jnp.maximum(i * ratio - 1, 0), 0))
    next8 = pl.BlockSpec((SUBLANES, LRU_WIDTH), lambda i: (jnp.minimum((i + 1) * ratio, n_blocks * ratio - 1), 0))
    cw = pl.BlockSpec((CONV_WIDTH, LRU_WIDTH), lambda i: (0, 0))
    return pl.pallas_call(
        body, name="conv_bwd", grid=(n_blocks,),
        in_specs=[prev8, rows, rows, next8, cw], out_specs=[rows, cw],
        out_shape=[jax.ShapeDtypeStruct((T, LRU_WIDTH), MXU_DTYPE), jax.ShapeDtypeStruct((CONV_WIDTH, LRU_WIDTH), F32)],
        compiler_params=_params("arbitrary"),
    )(xr, xr, d_xc, d_xc, conv_w)


def merge_fwd(gates, bias, branches):
    T = gates.shape[0]
    tm = _tile(T, 512)
    D = D_MODEL

    def body(g_ref, b_ref, a_ref, r_ref, c_ref, o_ref):
        acc = None
        for k, br in enumerate((a_ref, r_ref, c_ref)):
            term = _sigmoid(g_ref[:, k * D:(k + 1) * D] + b_ref[:, k * D:(k + 1) * D]) * br[...]
            acc = term if acc is None else acc + term
        o_ref[...] = acc.astype(o_ref.dtype)

    rows = pl.BlockSpec((tm, D), lambda i: (i, 0))
    return pl.pallas_call(
        body, name="merge_fwd", grid=(T // tm,),
        in_specs=[pl.BlockSpec((tm, 3 * D), lambda i: (i, 0)), pl.BlockSpec((1, 3 * D), lambda i: (0, 0)), rows, rows, rows],
        out_specs=rows, out_shape=jax.ShapeDtypeStruct((T, D), MXU_DTYPE),
        compiler_params=_params("parallel"),
    )(gates, bias, *branches)


def merge_bwd(d_merged, gates, bias, branches):
    T = gates.shape[0]
    tm = _tile(T, 512)
    D = D_MODEL

    def body(dm_ref, g_ref, b_ref, a_ref, r_ref, c_ref, da_ref, dr_ref, dc_ref, dg_ref, db_ref):
        @pl.when(pl.program_id(0) == 0)
        def _():
            db_ref[...] = jnp.zeros_like(db_ref)

        dm = dm_ref[...]
        for k, (br, dbr) in enumerate(((a_ref, da_ref), (r_ref, dr_ref), (c_ref, dc_ref))):
            cols = slice(k * D, (k + 1) * D)
            s = _sigmoid(g_ref[:, cols] + b_ref[:, cols])
            dbr[...] = (dm * s).astype(dbr.dtype)
            d_pre = dm * br[...] * s * (1.0 - s)
            dg_ref[:, cols] = d_pre.astype(dg_ref.dtype)
            db_ref[:, cols] += jnp.sum(d_pre, axis=0, keepdims=True)

    rows = pl.BlockSpec((tm, D), lambda i: (i, 0))
    wide = pl.BlockSpec((tm, 3 * D), lambda i: (i, 0))
    vec = pl.BlockSpec((1, 3 * D), lambda i: (0, 0))
    act = jax.ShapeDtypeStruct((T, D), MXU_DTYPE)
    return pl.pallas_call(
        body, name="merge_bwd", grid=(T // tm,),
        in_specs=[rows, wide, vec, rows, rows, rows], out_specs=[rows, rows, rows, wide, vec],
        out_shape=[act, act, act, jax.ShapeDtypeStruct((T, 3 * D), MXU_DTYPE), jax.ShapeDtypeStruct((1, 3 * D), F32)],
        compiler_params=_params("arbitrary"),
    )(d_merged, gates, bias, *branches)


def loss_head(x, g, target):
    T, D = x.shape
    tm = _tile(T, 512)

    def body(x_ref, g_ref, t_ref, loss_ref, dx_ref, dg_ref):
        @pl.when(pl.program_id(0) == 0)
        def _():
            loss_ref[...] = jnp.zeros_like(loss_ref)
            dg_ref[...] = jnp.zeros_like(dg_ref)

        xv = x_ref[...]
        gv = g_ref[...]
        r = lax.rsqrt(jnp.mean(xv * xv, axis=-1, keepdims=True) + EPS)
        xh = xv * r
        err = xh * gv - t_ref[...]
        per_row = jnp.mean(err * err, axis=-1, keepdims=True)
        loss_ref[...] += jnp.broadcast_to(0.5 * jnp.sum(per_row, axis=0, keepdims=True), loss_ref.shape)
        dyv = err * (1.0 / D)
        dg_ref[...] += jnp.sum(dyv * xh, axis=0, keepdims=True)
        gd = dyv * gv
        dx_ref[...] = r * (gd - xh * jnp.mean(gd * xh, axis=-1, keepdims=True))

    rows = pl.BlockSpec((tm, D), lambda i: (i, 0))
    vec = pl.BlockSpec((1, D), lambda i: (0, 0))
    return pl.pallas_call(
        body, name="loss_head", grid=(T // tm,),
        in_specs=[rows, vec, rows], out_specs=[pl.BlockSpec((1, LANES), lambda i: (0, 0)), rows, vec],
        out_shape=[jax.ShapeDtypeStruct((1, LANES), F32), jax.ShapeDtypeStruct((T, D), F32),
                   jax.ShapeDtypeStruct((1, D), F32)],
        compiler_params=_params("arbitrary"),
    )(x, g, target)


def _adamw_math(w, g, m, v):
    m = ADAM_B1 * m + (1.0 - ADAM_B1) * g
    v = ADAM_B2 * v + (1.0 - ADAM_B2) * (g * g)
    m_hat = m / (1.0 - ADAM_B1 ** ADAM_STEP)
    v_hat = v / (1.0 - ADAM_B2 ** ADAM_STEP)
    delta = -ADAM_LR * (m_hat / (jnp.sqrt(v_hat) + ADAM_EPS) + ADAM_WD * w)
    return delta, m, v


def _as_rows(a):
    return a.reshape(-1, a.shape[-1])


def sum_parts(parts):
    n, R, C = parts.shape
    tr = _tile(R, 512)

    def body(p_ref, o_ref):
        acc = p_ref[0].astype(F32)
        for k in range(1, n):
            acc = acc + p_ref[k].astype(F32)
        o_ref[...] = acc

    return pl.pallas_call(
        body, name="sum_parts", grid=(R // tr,),
        in_specs=[pl.BlockSpec((n, tr, C), lambda i: (0, i, 0))], out_specs=pl.BlockSpec((tr, C), lambda i: (i, 0)),
        out_shape=jax.ShapeDtypeStruct((R, C), F32), compiler_params=_params("parallel"),
    )(parts)


def adamw_sum(w, parts, m, v):
    n, R, C = parts.shape
    tr = _tile(R, 256)

    def body(w_ref, p_ref, m_ref, v_ref, g_ref, d_ref, nm_ref, nv_ref):
        g = p_ref[0]
        for k in range(1, n):
            g = g + p_ref[k]
        delta, nm, nv = _adamw_math(w_ref[...], g, m_ref[...], v_ref[...])
        g_ref[...] = g
        d_ref[...] = delta
        nm_ref[...] = nm
        nv_ref[...] = nv

    rows = pl.BlockSpec((tr, C), lambda i: (i, 0))
    shape = jax.ShapeDtypeStruct((R, C), F32)
    return pl.pallas_call(
        body, name="adamw_sum", grid=(R // tr,),
        in_specs=[rows, pl.BlockSpec((n, tr, C), lambda i: (0, i, 0)), rows, rows], out_specs=[rows] * 4,
        out_shape=[shape] * 4, compiler_params=_params("parallel"),
    )(w, parts, m, v)


def _place():
    return lax.axis_index("x"), lax.axis_index("y"), lax.axis_index("c")


def chip_exchange(arrays, *, scatter, name):
    n = len(arrays)

    def body(*refs):
        ins, outs = refs[:n], refs[n:2 * n]
        send_sems, recv_sems, local_sems = refs[2 * n:]
        x, y, c = _place()
        me = 2 * x + y
        chips = [(1 - x, y), (x, 1 - y), (1 - x, 1 - y)]

        def copy(k, d):
            px, py = chips[d]
            peer = 2 * px + py
            return pltpu.make_async_remote_copy(
                src_ref=ins[k].at[peer] if scatter else ins[k], dst_ref=outs[k].at[me],
                send_sem=send_sems.at[3 * k + d], recv_sem=recv_sems.at[3 * k + d],
                device_id=(px, py, c), device_id_type=MESH)

        def arrival(k, d):
            px, py = chips[d]
            peer = 2 * px + py
            return pltpu.make_async_remote_copy(
                src_ref=ins[k].at[me] if scatter else ins[k], dst_ref=outs[k].at[peer],
                send_sem=send_sems.at[3 * k + d], recv_sem=recv_sems.at[3 * k + d],
                device_id=(px, py, c), device_id_type=MESH)

        local = [pltpu.make_async_copy(ins[k].at[me] if scatter else ins[k], outs[k].at[me], local_sems.at[k])
                 for k in range(n)]
        for k in range(n):
            for d in range(3):
                copy(k, d).start()
            local[k].start()
        for k in range(n):
            for d in range(3):
                arrival(k, d).wait()
            local[k].wait()

    any_spec = pl.BlockSpec(memory_space=pl.ANY)
    out_shape = [jax.ShapeDtypeStruct(a.shape if scatter else (N_CHIPS,) + a.shape, a.dtype) for a in arrays]
    return pl.pallas_call(
        body, name=name, in_specs=[any_spec] * n, out_specs=[any_spec] * n, out_shape=out_shape,
        scratch_shapes=[pltpu.SemaphoreType.DMA((3 * n,)), pltpu.SemaphoreType.DMA((3 * n,)),
                        pltpu.SemaphoreType.DMA((n,))],
    )(*arrays)


def sibling_exchange(arrays, *, name):
    n = len(arrays)

    def body(*refs):
        ins, outs = refs[:n], refs[n:2 * n]
        send_sems, recv_sems = refs[2 * n:]
        x, y, c = _place()
        copies = [pltpu.make_async_remote_copy(
            src_ref=ins[k], dst_ref=outs[k], send_sem=send_sems.at[k], recv_sem=recv_sems.at[k],
            device_id=(x, y, 1 - c), device_id_type=MESH) for k in range(n)]
        for cp in copies:
            cp.start()
        for cp in copies:
            cp.wait()

    any_spec = pl.BlockSpec(memory_space=pl.ANY)
    return pl.pallas_call(
        body, name=name, in_specs=[any_spec] * n, out_specs=[any_spec] * n,
        out_shape=[jax.ShapeDtypeStruct(a.shape, a.dtype) for a in arrays],
        scratch_shapes=[pltpu.SemaphoreType.DMA((n,)), pltpu.SemaphoreType.DMA((n,))],
    )(*arrays)


def all_exchange(a, *, name):
    def body(in_ref, out_ref, send_sems, recv_sems, local_sem):
        x, y, c = _place()
        me = 4 * x + 2 * y + c
        flips = [(fx, fy, fc) for fx in (0, 1) for fy in (0, 1) for fc in (0, 1)][1:]

        def peer_of(d):
            fx, fy, fc = flips[d]
            return (x ^ fx, y ^ fy, c ^ fc)

        def copy(d, arriving):
            px, py, pc = peer_of(d)
            slot = (4 * px + 2 * py + pc) if arriving else me
            return pltpu.make_async_remote_copy(
                src_ref=in_ref, dst_ref=out_ref.at[slot], send_sem=send_sems.at[d], recv_sem=recv_sems.at[d],
                device_id=(px, py, pc), device_id_type=MESH)

        local = pltpu.make_async_copy(in_ref, out_ref.at[me], local_sem)
        local.start()
        for d in range(N_DEVICES - 1):
            copy(d, False).start()
        for d in range(N_DEVICES - 1):
            copy(d, True).wait()
        local.wait()

    any_spec = pl.BlockSpec(memory_space=pl.ANY)
    return pl.pallas_call(
        body, name=name, in_specs=[any_spec], out_specs=any_spec,
        out_shape=jax.ShapeDtypeStruct((N_DEVICES,) + a.shape, a.dtype),
        scratch_shapes=[pltpu.SemaphoreType.DMA((N_DEVICES - 1,)), pltpu.SemaphoreType.DMA((N_DEVICES - 1,)),
                        pltpu.SemaphoreType.DMA(())],
    )(a)


BIG = ("ffn1_w_gate", "ffn1_w_up", "ffn1_w_down", "w_in", "w_attn_out", "w_lru_out", "w_mem_kv", "w_mem_out",
       "w_out", "ffn2_w_gate", "ffn2_w_up", "ffn2_w_down")
COLUMN_SHARDED = ("ffn1_w_gate", "ffn1_w_up", "w_in", "w_attn_out", "w_mem_out", "ffn2_w_gate", "ffn2_w_up", "conv_w")
SMALL = ("ffn1_norm", "mix_norm", "gate_bias", "attn_sinks", "conv_w", "conv_b", "lru_wa", "lru_ba", "lru_wx", "lru_bx",
         "lru_lambda", "mem_norm", "ffn2_norm", "final_norm")
WEIGHTS = ("ffn1_norm", "ffn1_w_gate", "ffn1_w_up", "ffn1_w_down", "mix_norm", "w_in", "gate_bias", "attn_sinks",
           "w_attn_out", "conv_w", "conv_b", "lru_wa", "lru_ba", "lru_wx", "lru_bx", "lru_lambda", "w_lru_out", "mem_norm",
           "w_mem_kv", "w_mem_out", "w_out", "ffn2_norm", "ffn2_w_gate", "ffn2_w_up", "ffn2_w_down", "final_norm")
SEGMENTS = (("qkv", 0, 768), ("xr", 768, 1792), ("yr", 1792, 2816), ("cq", 2816, 3328), ("gates", 3328, 6400))
PACK_ROW = 1024


def _join_shards(gathered, layer, name):
    axis = -1 if name in COLUMN_SHARDED else 0
    return jnp.concatenate([gathered[s, layer] for s in range(N_CHIPS)], axis=axis)


def _split_shards(full, name):
    axis = full.ndim - 1 if name in COLUMN_SHARDED else 0
    return jnp.stack(jnp.split(full, N_CHIPS, axis=axis))


def _lane_tiles(w):
    z = jnp.zeros((LRU_WIDTH // LRU_TILE, HEAD_DIM, HEAD_DIM), w.dtype)
    top = jnp.concatenate([w[0::2], z], axis=2)
    bottom = jnp.concatenate([z, w[1::2]], axis=2)
    return jnp.concatenate([top, bottom], axis=1)


def _from_lane_tiles(t):
    pairs = jnp.stack([t[:, :HEAD_DIM, :HEAD_DIM], t[:, HEAD_DIM:, HEAD_DIM:]], axis=1)
    return pairs.reshape(2 * t.shape[0], HEAD_DIM, HEAD_DIM)


def _pack(arrays):
    flat = jnp.concatenate([a.reshape(-1).astype(F32) for a in arrays])
    pad = (-flat.shape[0]) % (SUBLANES * PACK_ROW)
    return jnp.pad(flat, (0, pad)).reshape(-1, PACK_ROW)


def _unpack(packed, shapes):
    flat = packed.reshape(-1)
    out, at = [], 0
    for shape in shapes:
        size = int(np.prod(shape))
        out.append(flat[at:at + size].reshape(shape))
        at += size
    return out


def _ffn_forward(x, norm, w_gate, w_up, w_down):
    h = rms_fwd(x, norm)
    G, U, A = ffn_up(h, w_gate, w_up)
    y = mm([A], [w_down], nt=False, out_dtype=F32, res=x, scale=0.5, name="ffn_down")
    return y, (x, h, G, U, A)


def _ffn_backward(dy, saved, norm, w_gate, w_up, w_down):
    x, h, G, U, A = saved
    dG, dU = ffn_bwd_act(dy, w_down, G, U)
    d_down = mm_tn(A, dy, out_dtype=MXU_DTYPE, b_scale=0.5, name="ffn_dw_down")
    d_gate = mm_tn(h, dG, out_dtype=MXU_DTYPE, name="ffn_dw_up")
    d_up = mm_tn(h, dU, out_dtype=MXU_DTYPE, name="ffn_dw_up")
    dh = mm([dG, dU], [w_gate, w_up], nt=True, out_dtype=F32, name="ffn_dh")
    dx, d_norm = rms_bwd(x, norm, dh, dy)
    return dx, d_norm, d_gate, d_up, d_down


def kernel(x, mem, ffn1_norm, ffn1_w_gate, ffn1_w_up, ffn1_w_down, mix_norm, w_in, gate_bias, attn_sinks, w_attn_out, conv_w, conv_b, lru_wa, lru_ba, lru_wx, lru_bx, lru_lambda, w_lru_out, mem_norm, w_mem_kv, w_mem_out, w_out, ffn2_norm, ffn2_w_gate, ffn2_w_up, ffn2_w_down, final_norm, loss_target, m_ffn1_norm, m_ffn1_w_gate, m_ffn1_w_up, m_ffn1_w_down, m_mix_norm, m_w_in, m_gate_bias, m_attn_sinks, m_w_attn_out, m_conv_w, m_conv_b, m_lru_wa, m_lru_ba, m_lru_wx, m_lru_bx, m_lru_lambda, m_w_lru_out, m_mem_norm, m_w_mem_kv, m_w_mem_out, m_w_out, m_ffn2_norm, m_ffn2_w_gate, m_ffn2_w_up, m_ffn2_w_down, m_final_norm, v_ffn1_norm, v_ffn1_w_gate, v_ffn1_w_up, v_ffn1_w_down, v_mix_norm, v_w_in, v_gate_bias, v_attn_sinks, v_w_attn_out, v_conv_w, v_conv_b, v_lru_wa, v_lru_ba, v_lru_wx, v_lru_bx, v_lru_lambda, v_w_lru_out, v_mem_norm, v_w_mem_kv, v_w_mem_out, v_w_out, v_ffn2_norm, v_ffn2_w_gate, v_ffn2_w_up, v_ffn2_w_down, v_final_norm):
    args = dict(locals())
    W = {n: args[n] for n in WEIGHTS}
    M = {n: args["m_" + n] for n in WEIGHTS}
    V = {n: args["v_" + n] for n in WEIGHTS}
    chip = 2 * lax.axis_index("x") + lax.axis_index("y")
    x0 = x[0]
    mem0 = mem[0]
    target = loss_target[0]

    shards = [W[n].astype(MXU_DTYPE) for n in BIG] + [W["conv_w"]]
    gathered = dict(zip(BIG + ("conv_w",), chip_exchange(shards, scatter=False, name="gather_weights")))
    full = [{n: _join_shards(gathered[n], l, n) for n in gathered} for l in range(DEPTH)]

    def row(v):
        return v.reshape(1, -1)

    saved = []
    xs = x0
    for l in range(DEPTH):
        P = full[l]
        xs, ffn1 = _ffn_forward(xs, row(W["ffn1_norm"][l]), P["ffn1_w_gate"], P["ffn1_w_up"], P["ffn1_w_down"])
        x_mix = xs
        h = rms_fwd(x_mix, row(W["mix_norm"][l]))
        seg = {name: mm([h], [P["w_in"][:, lo:hi]], nt=False, out_dtype=F32, name="proj_" + name)
               for name, lo, hi in SEGMENTS}
        attn = attn_fwd(seg["qkv"], W["attn_sinks"][l])
        wa_t = _lane_tiles(W["lru_wa"][l]).astype(MXU_DTYPE)
        wx_t = _lane_tiles(W["lru_wx"][l]).astype(MXU_DTYPE)
        sp = row(jax.nn.softplus(-W["lru_lambda"][l]))
        lru_consts = (P["conv_w"], row(W["conv_b"][l]), wa_t, row(W["lru_ba"][l]), wx_t, row(W["lru_bx"][l]), sp)
        h_lru, rec = lru_fwd(seg["xr"], seg["yr"], *lru_consts)
        mem_n = rms_fwd(mem0, row(W["mem_norm"][l]))
        kv = mm([mem_n], [P["w_mem_kv"]], nt=False, out_dtype=F32, name="mem_kv")
        cross = mem_attn_fwd(seg["cq"], kv)
        branches = (mm([attn], [P["w_attn_out"]], nt=False, out_dtype=F32, name="branch_attn"),
                    mm([rec], [P["w_lru_out"]], nt=False, out_dtype=F32, name="branch_lru"),
                    mm([cross], [P["w_mem_out"]], nt=False, out_dtype=F32, name="branch_mem"))
        bias = row(W["gate_bias"][l])
        merged = merge_fwd(seg["gates"], bias, branches)
        xs = mm([merged], [P["w_out"]], nt=False, out_dtype=F32, res=x_mix, name="mix_out")
        mix = (x_mix, h, seg, attn, lru_consts, h_lru, rec, mem_n, kv, cross, branches, bias, merged)
        xs, ffn2 = _ffn_forward(xs, row(W["ffn2_norm"][l]), P["ffn2_w_gate"], P["ffn2_w_up"], P["ffn2_w_down"])
        saved.append((ffn1, mix, ffn2))

    loss_lanes, dx, d_final = loss_head(xs, row(W["final_norm"]), target)
    loss = lax.psum(loss_lanes[0, 0], ("x", "y", "c"))

    big_grads = {n: [None] * DEPTH for n in BIG}
    small_grads = {n: [None] * DEPTH for n in SMALL if n != "final_norm"}
    for l in reversed(range(DEPTH)):
        P = full[l]
        ffn1, mix, ffn2 = saved[l]
        dx, g_norm, g_gate, g_up, g_down = _ffn_backward(
            dx, ffn2, row(W["ffn2_norm"][l]), P["ffn2_w_gate"], P["ffn2_w_up"], P["ffn2_w_down"])
        small_grads["ffn2_norm"][l] = g_norm
        big_grads["ffn2_w_gate"][l], big_grads["ffn2_w_up"][l], big_grads["ffn2_w_down"][l] = g_gate, g_up, g_down

        x_mix, h, seg, attn, lru_consts, h_lru, rec, mem_n, kv, cross, branches, bias, merged = mix
        d_merged = mm([dx], [P["w_out"]], nt=True, out_dtype=F32, name="d_merged")
        big_grads["w_out"][l] = mm_tn(merged, dx, out_dtype=MXU_DTYPE, name="dw_out")
        d_attn_br, d_lru_br, d_mem_br, d_gates, d_bias = merge_bwd(d_merged, seg["gates"], bias, branches)
        small_grads["gate_bias"][l] = d_bias
        d_attn = mm([d_attn_br], [P["w_attn_out"]], nt=True, out_dtype=F32, name="d_attn")
        d_rec = mm([d_lru_br], [P["w_lru_out"]], nt=True, out_dtype=F32, name="d_rec")
        d_cross = mm([d_mem_br], [P["w_mem_out"]], nt=True, out_dtype=F32, name="d_cross")
        big_grads["w_attn_out"][l] = mm_tn(attn, d_attn_br, out_dtype=MXU_DTYPE, name="dw_attn_out")
        big_grads["w_lru_out"][l] = mm_tn(rec, d_lru_br, out_dtype=MXU_DTYPE, name="dw_lru_out")
        big_grads["w_mem_out"][l] = mm_tn(cross, d_mem_br, out_dtype=MXU_DTYPE, name="dw_mem_out")

        d_qkv, d_sinks = attn_bwd(seg["qkv"], W["attn_sinks"][l], d_attn)
        small_grads["attn_sinks"][l] = d_sinks

        d_xc, d_yr, d_wa_t, d_wx_t, d_ba, d_bx, d_sp, d_cb = lru_bwd(seg["xr"], seg["yr"], h_lru, d_rec, *lru_consts)
        d_xr, d_cw = conv_bwd(seg["xr"], d_xc, lru_consts[0])
        small_grads["conv_w"][l] = d_cw
        small_grads["conv_b"][l] = d_cb
        small_grads["lru_wa"][l] = _from_lane_tiles(d_wa_t)
        small_grads["lru_wx"][l] = _from_lane_tiles(d_wx_t)
        small_grads["lru_ba"][l] = d_ba
        small_grads["lru_bx"][l] = d_bx
        small_grads["lru_lambda"][l] = -d_sp * _sigmoid(-row(W["lru_lambda"][l]))

        d_cq, d_kv = mem_attn_bwd(seg["cq"], kv, d_cross)
        big_grads["w_mem_kv"][l] = mm_tn(mem_n, d_kv, out_dtype=MXU_DTYPE, name="dw_mem_kv")
        d_mem_n = mm([d_kv], [P["w_mem_kv"]], nt=True, out_dtype=F32, name="d_mem_n")
        _, g_mem_norm = rms_bwd(mem0, row(W["mem_norm"][l]), d_mem_n, jnp.zeros_like(mem0))
        small_grads["mem_norm"][l] = g_mem_norm

        d_seg = {"qkv": d_qkv, "xr": d_xr, "yr": d_yr, "cq": d_cq, "gates": d_gates}
        big_grads["w_in"][l] = jnp.concatenate(
            [mm_tn(h, d_seg[name], out_dtype=MXU_DTYPE, name="dw_in_" + name) for name, _, _ in SEGMENTS], axis=1)
        dh = mm([d_seg[name] for name, _, _ in SEGMENTS], [P["w_in"][:, lo:hi] for _, lo, hi in SEGMENTS],
                nt=True, out_dtype=F32, tm=256, tn=512, name="mix_dh")
        dx, g_norm = rms_bwd(x_mix, row(W["mix_norm"][l]), dh, dx)
        small_grads["mix_norm"][l] = g_norm

        dx, g_norm, g_gate, g_up, g_down = _ffn_backward(
            dx, ffn1, row(W["ffn1_norm"][l]), P["ffn1_w_gate"], P["ffn1_w_up"], P["ffn1_w_down"])
        small_grads["ffn1_norm"][l] = g_norm
        big_grads["ffn1_w_gate"][l], big_grads["ffn1_w_up"][l], big_grads["ffn1_w_down"][l] = g_gate, g_up, g_down

    grad_x = dx[None]

    chunks = [jnp.stack([_split_shards(big_grads[n][l], n) for l in range(DEPTH)], axis=1) for n in BIG]
    arrived = chip_exchange(chunks, scatter=True, name="scatter_grads")
    partial = [sum_parts(a.reshape(N_CHIPS, -1, a.shape[-1])) for a in arrived]
    other = sibling_exchange(partial, name="sibling_grads")
    results = {}
    for n, mine, theirs in zip(BIG, partial, other):
        outs = adamw_sum(_as_rows(W[n]), jnp.stack([mine, theirs]), _as_rows(M[n]), _as_rows(V[n]))
        results[n] = [o.reshape(W[n].shape) for o in outs]

    small_list = [jnp.stack(small_grads[n]) for n in SMALL if n != "final_norm"] + [d_final]
    small_shapes = [(DEPTH,) + W[n].shape[1:] if n != "conv_w" else (DEPTH, CONV_WIDTH, LRU_WIDTH)
                    for n in SMALL if n != "final_norm"] + [W["final_norm"].shape]
    everyone = all_exchange(_pack(small_list), name="exchange_small")

    def own_columns(full_conv):
        return lax.dynamic_slice_in_dim(full_conv, chip * (LRU_WIDTH // N_CHIPS), LRU_WIDTH // N_CHIPS, axis=2)

    def packed_state(state):
        arrays = []
        for n in SMALL:
            a = state[n]
            if n == "conv_w":
                a = lax.dynamic_update_slice_in_dim(jnp.zeros((DEPTH, CONV_WIDTH, LRU_WIDTH), F32), a,
                                                    chip * (LRU_WIDTH // N_CHIPS), axis=2)
            arrays.append(a)
        return _pack(arrays)

    small_outs = adamw_sum(packed_state(W), everyone, packed_state(M), packed_state(V))
    for kind, packed in enumerate(small_outs):
        for n, a in zip(SMALL, _unpack(packed, small_shapes)):
            results.setdefault(n, [None] * 4)[kind] = own_columns(a) if n == "conv_w" else a.reshape(W[n].shape)

    return (loss, grad_x, *[results[n][0] for n in WEIGHTS], *[results[n][1] for n in WEIGHTS],
            *[results[n][2] for n in WEIGHTS], *[results[n][3] for n in WEIGHTS])
```

```python
import functools

import jax
import jax.numpy as jnp
import numpy as np
from jax import lax
from jax.experimental import pallas as pl
from jax.experimental.pallas import tpu as pltpu

F32 = jnp.float32
MXU_DTYPE = jnp.bfloat16

DEPTH = 4
D_MODEL = 1024
CHUNK = 64
HALO = 2 * CHUNK
N_Q_HEADS = 8
HEAD_DIM = 64
ATTN_WIDTH = 512
KV_WIDTH = 128
LRU_WIDTH = 1024
LRU_TILE = 128
CONV_WIDTH = 4
LRU_C = 8.0
MEM_HEADS = 4
MEM_HEAD_DIM = 128
MEM_WIDTH = 512
D_FF = 2816
EPS = 1e-6
ADAM_LR = 0.001
ADAM_B1 = 0.9
ADAM_B2 = 0.999
ADAM_EPS = 1e-08
ADAM_WD = 0.01
ADAM_STEP = 10

N_CHIPS = 4
N_DEVICES = 8
SUBLANES = 8
LANES = 128
VMEM_LIMIT_BYTES = 56 * 1024 * 1024
NEG_BIG = -1e30
MESH = pl.DeviceIdType.MESH

ALIBI_SLOPES = tuple(float(2.0 ** (-8.0 * (i + 1) / N_Q_HEADS)) for i in range(N_Q_HEADS))


def _params(*semantics):
    return pltpu.CompilerParams(dimension_semantics=semantics, vmem_limit_bytes=VMEM_LIMIT_BYTES)


def _tile(n, pref, unit=SUBLANES):
    if n <= pref:
        return n
    for t in range(pref - pref % unit, 0, -unit):
        if n % t == 0:
            return t
    return n


def _dot(a, b, ca, cb):
    return lax.dot_general(a, b, (((ca,), (cb,)), ((), ())), preferred_element_type=F32)


def _sigmoid(x):
    return jax.nn.sigmoid(x)


def rms_fwd(x, g):
    T, D = x.shape
    tm = _tile(T, 512)

    def body(x_ref, g_ref, h_ref):
        xv = x_ref[...]
        r = lax.rsqrt(jnp.mean(xv * xv, axis=-1, keepdims=True) + EPS)
        h_ref[...] = (xv * r * g_ref[...]).astype(h_ref.dtype)

    return pl.pallas_call(
        body, name="rms_fwd", grid=(T // tm,),
        in_specs=[pl.BlockSpec((tm, D), lambda i: (i, 0)), pl.BlockSpec((1, D), lambda i: (0, 0))],
        out_specs=pl.BlockSpec((tm, D), lambda i: (i, 0)),
        out_shape=jax.ShapeDtypeStruct((T, D), MXU_DTYPE),
        compiler_params=_params("parallel"),
    )(x, g)


def rms_bwd(x, g, dh, dy):
    T, D = x.shape
    tm = _tile(T, 512)

    def body(x_ref, g_ref, dh_ref, dy_ref, dx_ref, dg_ref):
        i = pl.program_id(0)
        xv = x_ref[...]
        r = lax.rsqrt(jnp.mean(xv * xv, axis=-1, keepdims=True) + EPS)
        xh = xv * r
        dhv = dh_ref[...]
        gd = dhv * g_ref[...]
        dx_ref[...] = dy_ref[...] + r * (gd - xh * jnp.mean(gd * xh, axis=-1, keepdims=True))
        part = jnp.sum(dhv * xh, axis=0, keepdims=True)

        @pl.when(i == 0)
        def _():
            dg_ref[...] = part

        @pl.when(i > 0)
        def _():
            dg_ref[...] += part

    row = pl.BlockSpec((tm, D), lambda i: (i, 0))
    vec = pl.BlockSpec((1, D), lambda i: (0, 0))
    return pl.pallas_call(
        body, name="rms_bwd", grid=(T // tm,),
        in_specs=[row, vec, row, row], out_specs=[row, vec],
        out_shape=[jax.ShapeDtypeStruct((T, D), F32), jax.ShapeDtypeStruct((1, D), F32)],
        compiler_params=_params("arbitrary"),
    )(x, g, dh, dy)


def mm(a_list, b_list, *, nt, out_dtype, res=None, scale=1.0, tm=256, tn=1024, name="mm"):
    n_pairs = len(a_list)
    T = a_list[0].shape[0]
    N = b_list[0].shape[0] if nt else b_list[0].shape[1]
    tm = _tile(T, tm)
    tn = _tile(N, tn, LANES)
    has_res = res is not None

    def body(*refs):
        a_refs = refs[:n_pairs]
        b_refs = refs[n_pairs:2 * n_pairs]
        o_ref = refs[-1]
        acc = None
        for a_ref, b_ref in zip(a_refs, b_refs):
            d = _dot(a_ref[...].astype(MXU_DTYPE), b_ref[...], 1, 1 if nt else 0)
            acc = d if acc is None else acc + d
        if has_res:
            acc = refs[2 * n_pairs][...] + scale * acc
        elif scale != 1.0:
            acc = scale * acc
        o_ref[...] = acc.astype(o_ref.dtype)

    in_specs = [pl.BlockSpec((tm, a.shape[1]), lambda j, i: (i, 0)) for a in a_list]
    if nt:
        in_specs += [pl.BlockSpec((tn, b.shape[1]), lambda j, i: (j, 0)) for b in b_list]
    else:
        in_specs += [pl.BlockSpec((b.shape[0], tn), lambda j, i: (0, j)) for b in b_list]
    out_spec = pl.BlockSpec((tm, tn), lambda j, i: (i, j))
    operands = list(a_list) + list(b_list)
    if has_res:
        in_specs.append(out_spec)
        operands.append(res)
    return pl.pallas_call(
        body, name=name, grid=(N // tn, T // tm), in_specs=in_specs, out_specs=out_spec,
        out_shape=jax.ShapeDtypeStruct((T, N), out_dtype),
        compiler_params=_params("parallel", "parallel"),
    )(*operands)


def mm_tn(a, b, *, out_dtype, b_scale=1.0, tk=1408, tn=1408, tt=1024, name="mm_tn"):
    T, K = a.shape
    N = b.shape[1]
    tk = _tile(K, tk, LANES)
    tn = _tile(N, tn, LANES)
    tt = _tile(T, tt)
    steps = T // tt

    def body(a_ref, b_ref, o_ref, acc_ref):
        t = pl.program_id(2)
        bv = b_ref[...]
        if b_scale != 1.0:
            bv = b_scale * bv
        d = _dot(a_ref[...].astype(MXU_DTYPE), bv.astype(MXU_DTYPE), 0, 0)

        @pl.when(t == 0)
        def _():
            acc_ref[...] = d

        @pl.when(t > 0)
        def _():
            acc_ref[...] += d

        @pl.when(t == steps - 1)
        def _():
            o_ref[...] = acc_ref[...].astype(o_ref.dtype)

    return pl.pallas_call(
        body, name=name, grid=(K // tk, N // tn, steps),
        in_specs=[pl.BlockSpec((tt, tk), lambda p, q, t: (t, p)), pl.BlockSpec((tt, tn), lambda p, q, t: (t, q))],
        out_specs=pl.BlockSpec((tk, tn), lambda p, q, t: (p, q)),
        out_shape=jax.ShapeDtypeStruct((K, N), out_dtype),
        scratch_shapes=[pltpu.VMEM((tk, tn), F32)],
        compiler_params=_params("parallel", "parallel", "arbitrary"),
    )(a, b)


def ffn_up(h, w_gate, w_up):
    T, D = h.shape
    F = w_gate.shape[1]
    tm = _tile(T, 256)
    tn = _tile(F, 1408, LANES)

    def body(h_ref, wg_ref, wu_ref, g_ref, u_ref, a_ref):
        hv = h_ref[...]
        G = _dot(hv, wg_ref[...], 1, 0)
        U = _dot(hv, wu_ref[...], 1, 0)
        g_ref[...] = G
        u_ref[...] = U
        a_ref[...] = (G * _sigmoid(G) * U).astype(a_ref.dtype)

    w_spec = pl.BlockSpec((D, tn), lambda j, i: (0, j))
    o_spec = pl.BlockSpec((tm, tn), lambda j, i: (i, j))
    return pl.pallas_call(
        body, name="ffn_up", grid=(F // tn, T // tm),
        in_specs=[pl.BlockSpec((tm, D), lambda j, i: (i, 0)), w_spec, w_spec],
        out_specs=[o_spec, o_spec, o_spec],
        out_shape=[jax.ShapeDtypeStruct((T, F), F32), jax.ShapeDtypeStruct((T, F), F32),
                   jax.ShapeDtypeStruct((T, F), MXU_DTYPE)],
        compiler_params=_params("parallel", "parallel"),
    )(h, w_gate, w_up)


def ffn_bwd_act(half, dy, w_down, G, U):
    T, D = dy.shape
    F = w_down.shape[0]
    tm = _tile(T, 256)
    tn = _tile(F, 1408, LANES)

    def body(half_ref, dy_ref, wd_ref, g_ref, u_ref, dg_ref, du_ref):
        d_out = (half_ref[0] * dy_ref[...]).astype(MXU_DTYPE)
        dA = _dot(d_out, wd_ref[...], 1, 1)
        Gv = g_ref[...]
        s = _sigmoid(Gv)
        du_ref[...] = (dA * (Gv * s)).astype(du_ref.dtype)
        dg_ref[...] = (dA * u_ref[...] * (s * (1.0 + Gv * (1.0 - s)))).astype(dg_ref.dtype)

    o_spec = pl.BlockSpec((tm, tn), lambda j, i: (i, j))
    return pl.pallas_call(
        body, name="ffn_bwd_act", grid=(F // tn, T // tm),
        in_specs=[pl.BlockSpec(memory_space=pltpu.SMEM), pl.BlockSpec((tm, D), lambda j, i: (i, 0)),
                  pl.BlockSpec((tn, D), lambda j, i: (j, 0)), o_spec, o_spec],
        out_specs=[o_spec, o_spec],
        out_shape=[jax.ShapeDtypeStruct((T, F), MXU_DTYPE), jax.ShapeDtypeStruct((T, F), MXU_DTYPE)],
        compiler_params=_params("parallel", "parallel"),
    )(half, dy, w_down, G, U)


ATTN_ROWS = 256


def _head_variants(kv128):
    lane = lax.broadcasted_iota(jnp.int32, kv128.shape, 1)
    low = lane < HEAD_DIM
    rolled = pltpu.roll(kv128, HEAD_DIM, 1)
    out = {}
    for j in (0, 1):
        for pos in (0, 1):
            src = kv128 if j == pos else rolled
            out[j, pos] = jnp.where(low if pos == 0 else jnp.logical_not(low), src, 0.0).astype(MXU_DTYPE)
    return out


def _fold_variant(d128, j, pos):
    lane = lax.broadcasted_iota(jnp.int32, d128.shape, 1)
    low = lane < HEAD_DIM
    kept = jnp.where(low if pos == 0 else jnp.logical_not(low), d128, 0.0)
    return kept if j == pos else pltpu.roll(kept, HEAD_DIM, 1)


def _attn_mask(block, rows, keys):
    r = lax.broadcasted_iota(jnp.int32, (rows, keys), 0)
    c = lax.broadcasted_iota(jnp.int32, (rows, keys), 1)
    q_chunk = r // CHUNK
    k_chunk = c // CHUNK - HALO // CHUNK
    first_chunk = jnp.where(block > 0, -(HALO // CHUNK), 0)
    valid = (k_chunk <= q_chunk) & (k_chunk >= q_chunk - HALO // CHUNK) & (k_chunk >= first_chunk)
    dist = jnp.abs(r + HALO - c).astype(F32)
    return valid, dist


def _attn_probs(q128, k_var, head, sink, valid, dist):
    s = _dot(q128, k_var, 1, 1) * (HEAD_DIM ** -0.5) - ALIBI_SLOPES[head] * dist
    s = jnp.where(valid, s, NEG_BIG)
    mx = jnp.maximum(jnp.max(s, axis=-1, keepdims=True), sink)
    p = jnp.exp(s - mx)
    p_sink = jnp.exp(sink - mx)
    inv = 1.0 / (jnp.sum(p, axis=-1, keepdims=True) + p_sink)
    return p * inv, p_sink * inv


def _attn_specs(T, tq, order):
    ratio = tq // HALO
    q_spec = pl.BlockSpec((tq, ATTN_WIDTH), lambda i: (order(i), 0))
    cur = lambda col: pl.BlockSpec((tq, KV_WIDTH), lambda i: (order(i), col))
    prev = lambda col: pl.BlockSpec((HALO, KV_WIDTH), lambda i: (jnp.maximum(order(i) * ratio - 1, 0), col))
    k_col, v_col = ATTN_WIDTH // KV_WIDTH, ATTN_WIDTH // KV_WIDTH + 1
    return [pl.BlockSpec(memory_space=pltpu.SMEM), q_spec, prev(k_col), cur(k_col), prev(v_col), cur(v_col)]


def attn_fwd(qkv, sinks):
    T = qkv.shape[0]
    tq = _tile(T, ATTN_ROWS)
    keys = tq + HALO

    def body(sink_ref, q_ref, kp_ref, kc_ref, vp_ref, vc_ref, o_ref):
        block = pl.program_id(0)
        k_var = _head_variants(jnp.concatenate([kp_ref[...], kc_ref[...]], axis=0))
        v_var = _head_variants(jnp.concatenate([vp_ref[...], vc_ref[...]], axis=0))
        valid, dist = _attn_mask(block, tq, keys)
        for pair in range(N_Q_HEADS // 2):
            q128 = q_ref[:, pair * LANES:(pair + 1) * LANES].astype(MXU_DTYPE)
            acc = None
            for pos in (0, 1):
                head = 2 * pair + pos
                j = head // 4
                p, _ = _attn_probs(q128, k_var[j, pos], head, sink_ref[head], valid, dist)
                d = _dot(p.astype(MXU_DTYPE), v_var[j, pos], 1, 0)
                acc = d if acc is None else acc + d
            o_ref[:, pair * LANES:(pair + 1) * LANES] = acc.astype(o_ref.dtype)

    return pl.pallas_call(
        body, name="attn_fwd", grid=(T // tq,),
        in_specs=_attn_specs(T, tq, lambda i: i),
        out_specs=pl.BlockSpec((tq, ATTN_WIDTH), lambda i: (i, 0)),
        out_shape=jax.ShapeDtypeStruct((T, ATTN_WIDTH), MXU_DTYPE),
        compiler_params=_params("parallel"),
    )(sinks, qkv, qkv, qkv, qkv, qkv)


def attn_bwd(qkv, sinks, d_out):
    T = qkv.shape[0]
    tq = _tile(T, ATTN_ROWS)
    keys = tq + HALO
    n_blocks = T // tq
    order = lambda i: n_blocks - 1 - i

    def body(sink_ref, q_ref, kp_ref, kc_ref, vp_ref, vc_ref, do_ref, dqkv_ref, dsink_ref, carry_ref):
        step = pl.program_id(0)
        block = order(step)
        k_var = _head_variants(jnp.concatenate([kp_ref[...], kc_ref[...]], axis=0))
        v_var = _head_variants(jnp.concatenate([vp_ref[...], vc_ref[...]], axis=0))
        valid, dist = _attn_mask(block, tq, keys)

        @pl.when(step == 0)
        def _():
            carry_ref[...] = jnp.zeros_like(carry_ref)
            dsink_ref[...] = jnp.zeros_like(dsink_ref)

        dk = jnp.zeros((keys, KV_WIDTH), F32)
        dv = jnp.zeros((keys, KV_WIDTH), F32)
        for pair in range(N_Q_HEADS // 2):
            q128 = q_ref[:, pair * LANES:(pair + 1) * LANES].astype(MXU_DTYPE)
            do128 = do_ref[:, pair * LANES:(pair + 1) * LANES].astype(MXU_DTYPE)
            dq128 = None
            for pos in (0, 1):
                head = 2 * pair + pos
                j = head // 4
                p, p_sink = _attn_probs(q128, k_var[j, pos], head, sink_ref[head], valid, dist)
                dp = _dot(do128, v_var[j, pos], 1, 1)
                delta = jnp.sum(p * dp, axis=-1, keepdims=True)
                ds = (p * (dp - delta) * (HEAD_DIM ** -0.5)).astype(MXU_DTYPE)
                d = _dot(ds, k_var[j, pos], 1, 0)
                dq128 = d if dq128 is None else dq128 + d
                dk = dk + _fold_variant(_dot(ds, q128, 0, 0), j, pos)
                dv = dv + _fold_variant(_dot(p.astype(MXU_DTYPE), do128, 0, 0), j, pos)
                dsink_ref[pl.ds(head, 1), :] += jnp.broadcast_to(
                    -jnp.sum(p_sink * delta, axis=0, keepdims=True), (1, LANES))
            dqkv_ref[:, pair * LANES:(pair + 1) * LANES] = dq128.astype(dqkv_ref.dtype)
        carried = jnp.concatenate([jnp.zeros((tq - HALO, 2 * KV_WIDTH), F32), carry_ref[...]], axis=0)
        dkv = jnp.concatenate([dk, dv], axis=1)
        dqkv_ref[:, ATTN_WIDTH:] = (dkv[HALO:] + carried).astype(dqkv_ref.dtype)
        carry_ref[...] = dkv[:HALO]

    out_rows = pl.BlockSpec((tq, ATTN_WIDTH + 2 * KV_WIDTH), lambda i: (order(i), 0))
    dqkv, dsink = pl.pallas_call(
        body, name="attn_bwd", grid=(n_blocks,),
        in_specs=_attn_specs(T, tq, order) + [pl.BlockSpec((tq, ATTN_WIDTH), lambda i: (order(i), 0))],
        out_specs=[out_rows, pl.BlockSpec((N_Q_HEADS, LANES), lambda i: (0, 0))],
        out_shape=[jax.ShapeDtypeStruct((T, ATTN_WIDTH + 2 * KV_WIDTH), MXU_DTYPE),
                   jax.ShapeDtypeStruct((N_Q_HEADS, LANES), F32)],
        scratch_shapes=[pltpu.VMEM((HALO, 2 * KV_WIDTH), F32)],
        compiler_params=_params("arbitrary"),
    )(sinks, qkv, qkv, qkv, qkv, qkv, d_out)
    return dqkv, dsink[:, 0]


def _mem_probs(q, k):
    s = _dot(q, k, 1, 1) * (MEM_HEAD_DIM ** -0.5)
    p = jnp.exp(s - jnp.max(s, axis=-1, keepdims=True))
    return p / jnp.sum(p, axis=-1, keepdims=True)


def mem_attn_fwd(cq, kv):
    T = cq.shape[0]
    M = kv.shape[0]
    tq = _tile(T, 512)

    def body(q_ref, kv_ref, o_ref):
        for h in range(MEM_HEADS):
            lo, hi = h * MEM_HEAD_DIM, (h + 1) * MEM_HEAD_DIM
            p = _mem_probs(q_ref[:, lo:hi].astype(MXU_DTYPE), kv_ref[:, lo:hi].astype(MXU_DTYPE))
            v = kv_ref[:, MEM_WIDTH + lo:MEM_WIDTH + hi].astype(MXU_DTYPE)
            o_ref[:, lo:hi] = _dot(p.astype(MXU_DTYPE), v, 1, 0).astype(o_ref.dtype)

    return pl.pallas_call(
        body, name="mem_attn_fwd", grid=(T // tq,),
        in_specs=[pl.BlockSpec((tq, MEM_WIDTH), lambda i: (i, 0)), pl.BlockSpec((M, 2 * MEM_WIDTH), lambda i: (0, 0))],
        out_specs=pl.BlockSpec((tq, MEM_WIDTH), lambda i: (i, 0)),
        out_shape=jax.ShapeDtypeStruct((T, MEM_WIDTH), MXU_DTYPE),
        compiler_params=_params("parallel"),
    )(cq, kv)


def mem_attn_bwd(cq, kv, d_out):
    T = cq.shape[0]
    M = kv.shape[0]
    tq = _tile(T, 512)

    def body(q_ref, kv_ref, do_ref, dq_ref, dkv_ref):
        @pl.when(pl.program_id(0) == 0)
        def _():
            dkv_ref[...] = jnp.zeros_like(dkv_ref)

        for h in range(MEM_HEADS):
            lo, hi = h * MEM_HEAD_DIM, (h + 1) * MEM_HEAD_DIM
            q = q_ref[:, lo:hi].astype(MXU_DTYPE)
            k = kv_ref[:, lo:hi].astype(MXU_DTYPE)
            v = kv_ref[:, MEM_WIDTH + lo:MEM_WIDTH + hi].astype(MXU_DTYPE)
            do = do_ref[:, lo:hi].astype(MXU_DTYPE)
            p = _mem_probs(q, k)
            dp = _dot(do, v, 1, 1)
            ds = (p * (dp - jnp.sum(p * dp, axis=-1, keepdims=True)) * (MEM_HEAD_DIM ** -0.5)).astype(MXU_DTYPE)
            dq_ref[:, lo:hi] = _dot(ds, k, 1, 0).astype(dq_ref.dtype)
            dkv_ref[:, lo:hi] += _dot(ds, q, 0, 0)
            dkv_ref[:, MEM_WIDTH + lo:MEM_WIDTH + hi] += _dot(p.astype(MXU_DTYPE), do, 0, 0)

    rows = pl.BlockSpec((tq, MEM_WIDTH), lambda i: (i, 0))
    whole = pl.BlockSpec((M, 2 * MEM_WIDTH), lambda i: (0, 0))
    return pl.pallas_call(
        body, name="mem_attn_bwd", grid=(T // tq,),
        in_specs=[rows, whole, rows], out_specs=[rows, whole],
        out_shape=[jax.ShapeDtypeStruct((T, MEM_WIDTH), MXU_DTYPE), jax.ShapeDtypeStruct((M, 2 * MEM_WIDTH), F32)],
        compiler_params=_params("arbitrary"),
    )(cq, kv, d_out)


LRU_ROWS = 256


def _shift_down(ext, k, rows):
    return pltpu.roll(ext, k, 0)[SUBLANES:SUBLANES + rows] if k else ext[SUBLANES:SUBLANES + rows]


def _shift_up(ext, k, rows):
    return pltpu.roll(ext, rows + SUBLANES - k, 0)[:rows] if k else ext[:rows]


def _conv(x_ext, conv_w_ref, conv_b_ref, rows):
    xc = conv_b_ref[...] + conv_w_ref[CONV_WIDTH - 1:CONV_WIDTH, :] * _shift_down(x_ext, 0, rows)
    for j in range(CONV_WIDTH - 1):
        xc = xc + conv_w_ref[j:j + 1, :] * _shift_down(x_ext, CONV_WIDTH - 1 - j, rows)
    return xc


def _block_matmul(x, w_ref, cb):
    return jnp.concatenate(
        [_dot(x[:, c * LANES:(c + 1) * LANES].astype(MXU_DTYPE), w_ref[c], 1, cb)
         for c in range(LRU_WIDTH // LRU_TILE)], axis=1)


def _lru_gates(block, xc, wa_ref, ba_ref, wx_ref, bx_ref, sp_ref):
    rows = xc.shape[0]
    ra = _sigmoid(_block_matmul(xc, wa_ref, 0) + ba_ref[...])
    gi = _sigmoid(_block_matmul(xc, wx_ref, 0) + bx_ref[...])
    log_a = -LRU_C * ra * sp_ref[...]
    a = jnp.exp(log_a)
    one_minus = 1.0 - jnp.exp(2.0 * log_a)
    mult = jnp.sqrt(jnp.maximum(one_minus, 0.0))
    row = lax.broadcasted_iota(jnp.int32, (rows, 1), 0)
    first = row == jnp.where(block == 0, 0, -1)
    mult = jnp.where(first, 1.0, mult)
    return ra, gi, a, one_minus, mult, first


def _gelu(x):
    inner = np.sqrt(2.0 / np.pi).astype(np.float32) * (x + 0.044715 * (x * x * x))
    return 0.5 * x * (1.0 + jnp.tanh(inner))


def _gelu_grad(x):
    c = np.sqrt(2.0 / np.pi).astype(np.float32)
    t = jnp.tanh(c * (x + 0.044715 * (x * x * x)))
    return 0.5 * (1.0 + t) + 0.5 * x * (1.0 - t * t) * c * (1.0 + 3.0 * 0.044715 * (x * x))


def _x_ext(block, prev_ref, cur_ref):
    prev = jnp.where(block > 0, prev_ref[...], 0.0)
    return jnp.concatenate([prev, cur_ref[...]], axis=0)


def _lru_in_specs(tb, order, n_rows):
    ratio = tb // SUBLANES
    rows = pl.BlockSpec((tb, LRU_WIDTH), lambda i: (order(i), 0))
    prev8 = pl.BlockSpec((SUBLANES, LRU_WIDTH), lambda i: (jnp.maximum(order(i) * ratio - 1, 0), 0))
    vec = pl.BlockSpec((1, LRU_WIDTH), lambda i: (0, 0))
    conv_w = pl.BlockSpec((CONV_WIDTH, LRU_WIDTH), lambda i: (0, 0))
    tiles = pl.BlockSpec((LRU_WIDTH // LRU_TILE, LRU_TILE, LRU_TILE), lambda i: (0, 0, 0))
    return rows, prev8, vec, conv_w, tiles


def lru_fwd(xr, yr, conv_w, conv_b, wa_t, ba, wx_t, bx, sp):
    T = xr.shape[0]
    tb = _tile(T, LRU_ROWS)

    def body(xp_ref, x_ref, y_ref, cw_ref, cb_ref, wa_ref, ba_ref, wx_ref, bx_ref, sp_ref, h_ref, r_ref, carry_ref):
        block = pl.program_id(0)
        xc = _conv(_x_ext(block, xp_ref, x_ref), cw_ref, cb_ref, tb)
        _, gi, a, _, mult, _ = _lru_gates(block, xc, wa_ref, ba_ref, wx_ref, bx_ref, sp_ref)
        b = mult * gi * xc
        row = lax.broadcasted_iota(jnp.int32, (tb, 1), 0)
        shift = 1
        while shift < tb:
            keep = row >= shift
            b = b + a * jnp.where(keep, pltpu.roll(b, shift, 0), 0.0)
            a = a * jnp.where(keep, pltpu.roll(a, shift, 0), 1.0)
            shift *= 2

        @pl.when(block == 0)
        def _():
            carry_ref[...] = jnp.zeros_like(carry_ref)

        h = b + a * carry_ref[...]
        carry_ref[...] = h[tb - 1:tb, :]
        h_ref[...] = h
        r_ref[...] = (h * _gelu(y_ref[...])).astype(r_ref.dtype)

    rows, prev8, vec, cw, tiles = _lru_in_specs(tb, lambda i: i, T)
    return pl.pallas_call(
        body, name="lru_fwd", grid=(T // tb,),
        in_specs=[prev8, rows, rows, cw, vec, tiles, vec, tiles, vec, vec],
        out_specs=[rows, rows],
        out_shape=[jax.ShapeDtypeStruct((T, LRU_WIDTH), F32), jax.ShapeDtypeStruct((T, LRU_WIDTH), MXU_DTYPE)],
        scratch_shapes=[pltpu.VMEM((1, LRU_WIDTH), F32)],
        compiler_params=_params("arbitrary"),
    )(xr, xr, yr, conv_w, conv_b, wa_t, ba, wx_t, bx, sp)


def lru_bwd(xr, yr, h, d_r, conv_w, conv_b, wa_t, ba, wx_t, bx, sp):
    T = xr.shape[0]
    tb = _tile(T, LRU_ROWS)
    n_blocks = T // tb
    order = lambda i: n_blocks - 1 - i
    n_tiles = LRU_WIDTH // LRU_TILE

    def body(xp_ref, x_ref, y_ref, hp_ref, h_ref, dr_ref, cw_ref, cb_ref, wa_ref, ba_ref, wx_ref, bx_ref, sp_ref,
             dxc_ref, dy_ref, dwa_ref, dwx_ref, dba_ref, dbx_ref, dsp_ref, dcb_ref, lam_ref, a_next_ref):
        step = pl.program_id(0)
        block = order(step)

        @pl.when(step == 0)
        def _():
            lam_ref[...] = jnp.zeros_like(lam_ref)
            a_next_ref[...] = jnp.zeros_like(a_next_ref)
            for ref in (dwa_ref, dwx_ref, dba_ref, dbx_ref, dsp_ref, dcb_ref):
                ref[...] = jnp.zeros_like(ref)

        xc = _conv(_x_ext(block, xp_ref, x_ref), cw_ref, cb_ref, tb)
        ra, gi, a, one_minus, mult, first = _lru_gates(block, xc, wa_ref, ba_ref, wx_ref, bx_ref, sp_ref)
        yv = y_ref[...]
        hv = h_ref[...]
        drv = dr_ref[...].astype(F32)
        dy_ref[...] = (drv * hv * _gelu_grad(yv)).astype(dy_ref.dtype)

        row = lax.broadcasted_iota(jnp.int32, (tb, 1), 0)
        coef = jnp.where(row == tb - 1, a_next_ref[...], pltpu.roll(a, tb - 1, 0))
        lam = drv * _gelu(yv)
        shift = 1
        while shift < tb:
            keep = row < tb - shift
            lam = lam + coef * jnp.where(keep, pltpu.roll(lam, tb - shift, 0), 0.0)
            coef = coef * jnp.where(keep, pltpu.roll(coef, tb - shift, 0), 1.0)
            shift *= 2
        lam = lam + coef * lam_ref[...]
        lam_ref[...] = lam[0:1, :]
        a_next_ref[...] = a[0:1, :]

        h_prev = _shift_down(jnp.concatenate([jnp.where(block > 0, hp_ref[...], 0.0), hv], axis=0), 1, tb)
        d_a = lam * h_prev
        d_mult = jnp.where(first, 0.0, lam * gi * xc)
        d_gi = lam * mult * xc
        d_xc = lam * mult * gi
        safe = one_minus > 0.0
        d_log_a = d_a * a + d_mult * jnp.where(safe, -(1.0 - one_minus) * lax.rsqrt(jnp.where(safe, one_minus, 1.0)), 0.0)
        d_za = d_log_a * (-LRU_C * sp_ref[...]) * ra * (1.0 - ra)
        d_zx = d_gi * gi * (1.0 - gi)
        dsp_ref[...] += jnp.sum(d_log_a * (-LRU_C * ra), axis=0, keepdims=True)
        dba_ref[...] += jnp.sum(d_za, axis=0, keepdims=True)
        dbx_ref[...] += jnp.sum(d_zx, axis=0, keepdims=True)
        d_xc = d_xc + _block_matmul(d_za, wa_ref, 1) + _block_matmul(d_zx, wx_ref, 1)
        dcb_ref[...] += jnp.sum(d_xc, axis=0, keepdims=True)
        dxc_ref[...] = d_xc
        xc_m = xc.astype(MXU_DTYPE)
        for c in range(n_tiles):
            lanes = slice(c * LANES, (c + 1) * LANES)
            dwa_ref[c] += _dot(xc_m[:, lanes], d_za[:, lanes].astype(MXU_DTYPE), 0, 0)
            dwx_ref[c] += _dot(xc_m[:, lanes], d_zx[:, lanes].astype(MXU_DTYPE), 0, 0)

    rows, prev8, vec, cw, tiles = _lru_in_specs(tb, order, T)
    return pl.pallas_call(
        body, name="lru_bwd", grid=(n_blocks,),
        in_specs=[prev8, rows, rows, prev8, rows, rows, cw, vec, tiles, vec, tiles, vec, vec],
        out_specs=[rows, rows, tiles, tiles, vec, vec, vec, vec],
        out_shape=[jax.ShapeDtypeStruct((T, LRU_WIDTH), F32), jax.ShapeDtypeStruct((T, LRU_WIDTH), MXU_DTYPE),
                   jax.ShapeDtypeStruct((n_tiles, LRU_TILE, LRU_TILE), F32),
                   jax.ShapeDtypeStruct((n_tiles, LRU_TILE, LRU_TILE), F32)]
        + [jax.ShapeDtypeStruct((1, LRU_WIDTH), F32)] * 4,
        scratch_shapes=[pltpu.VMEM((1, LRU_WIDTH), F32), pltpu.VMEM((1, LRU_WIDTH), F32)],
        compiler_params=_params("arbitrary"),
    )(xr, xr, yr, h, h, d_r, conv_w, conv_b, wa_t, ba, wx_t, bx, sp)


def conv_bwd(xr, d_xc, conv_w):
    T = xr.shape[0]
    tb = _tile(T, LRU_ROWS)
    n_blocks = T // tb
    ratio = tb // SUBLANES

    def body(xp_ref, x_ref, d_ref, dn_ref, cw_ref, dx_ref, dcw_ref):
        block = pl.program_id(0)

        @pl.when(block == 0)
        def _():
            dcw_ref[...] = jnp.zeros_like(dcw_ref)

        x_ext = _x_ext(block, xp_ref, x_ref)
        dv = d_ref[...]
        d_ext = jnp.concatenate([dv, jnp.where(block < n_blocks - 1, dn_ref[...], 0.0)], axis=0)
        dx = None
        for j in range(CONV_WIDTH):
            k = CONV_WIDTH - 1 - j
            term = cw_ref[j:j + 1, :] * _shift_up(d_ext, k, tb)
            dx = term if dx is None else dx + term
            dcw_ref[j:j + 1, :] += jnp.sum(dv * _shift_down(x_ext, k, tb), axis=0, keepdims=True)
        dx_ref[...] = dx.astype(dx_ref.dtype)

    rows = pl.BlockSpec((tb, LRU_WIDTH), lambda i: (i, 0))
    prev8 = pl.BlockSpec((SUBLANES, LRU_WIDTH), lambda i: (jnp.maximum(i * ratio - 1, 0), 0))
    next8 = pl.BlockSpec((SUBLANES, LRU_WIDTH), lambda i: (jnp.minimum((i + 1) * ratio, n_blocks * ratio - 1), 0))
    cw = pl.BlockSpec((CONV_WIDTH, LRU_WIDTH), lambda i: (0, 0))
    return pl.pallas_call(
        body, name="conv_bwd", grid=(n_blocks,),
        in_specs=[prev8, rows, rows, next8, cw], out_specs=[rows, cw],
        out_shape=[jax.ShapeDtypeStruct((T, LRU_WIDTH), MXU_DTYPE), jax.ShapeDtypeStruct((CONV_WIDTH, LRU_WIDTH), F32)],
        compiler_params=_params("arbitrary"),
    )(xr, xr, d_xc, d_xc, conv_w)


def merge_fwd(gates, bias, branches):
    T = gates.shape[0]
    tm = _tile(T, 512)
    D = D_MODEL

    def body(g_ref, b_ref, a_ref, r_ref, c_ref, o_ref):
        acc = None
        for k, br in enumerate((a_ref, r_ref, c_ref)):
            term = _sigmoid(g_ref[:, k * D:(k + 1) * D] + b_ref[:, k * D:(k + 1) * D]) * br[...]
            acc = term if acc is None else acc + term
        o_ref[...] = acc.astype(o_ref.dtype)

    rows = pl.BlockSpec((tm, D), lambda i: (i, 0))
    return pl.pallas_call(
        body, name="merge_fwd", grid=(T // tm,),
        in_specs=[pl.BlockSpec((tm, 3 * D), lambda i: (i, 0)), pl.BlockSpec((1, 3 * D), lambda i: (0, 0)), rows, rows, rows],
        out_specs=rows, out_shape=jax.ShapeDtypeStruct((T, D), MXU_DTYPE),
        compiler_params=_params("parallel"),
    )(gates, bias, *branches)


def merge_bwd(d_merged, gates, bias, branches):
    T = gates.shape[0]
    tm = _tile(T, 512)
    D = D_MODEL

    def body(dm_ref, g_ref, b_ref, a_ref, r_ref, c_ref, da_ref, dr_ref, dc_ref, dg_ref, db_ref):
        @pl.when(pl.program_id(0) == 0)
        def _():
            db_ref[...] = jnp.zeros_like(db_ref)

        dm = dm_ref[...]
        for k, (br, dbr) in enumerate(((a_ref, da_ref), (r_ref, dr_ref), (c_ref, dc_ref))):
            cols = slice(k * D, (k + 1) * D)
            s = _sigmoid(g_ref[:, cols] + b_ref[:, cols])
            dbr[...] = (dm * s).astype(dbr.dtype)
            d_pre = dm * br[...] * s * (1.0 - s)
            dg_ref[:, cols] = d_pre.astype(dg_ref.dtype)
            db_ref[:, cols] += jnp.sum(d_pre, axis=0, keepdims=True)

    rows = pl.BlockSpec((tm, D), lambda i: (i, 0))
    wide = pl.BlockSpec((tm, 3 * D), lambda i: (i, 0))
    vec = pl.BlockSpec((1, 3 * D), lambda i: (0, 0))
    act = jax.ShapeDtypeStruct((T, D), MXU_DTYPE)
    return pl.pallas_call(
        body, name="merge_bwd", grid=(T // tm,),
        in_specs=[rows, wide, vec, rows, rows, rows], out_specs=[rows, rows, rows, wide, vec],
        out_shape=[act, act, act, jax.ShapeDtypeStruct((T, 3 * D), MXU_DTYPE), jax.ShapeDtypeStruct((1, 3 * D), F32)],
        compiler_params=_params("arbitrary"),
    )(d_merged, gates, bias, *branches)


def loss_head(x, g, target):
    T, D = x.shape
    tm = _tile(T, 512)

    def body(x_ref, g_ref, t_ref, loss_ref, dx_ref, dg_ref):
        @pl.when(pl.program_id(0) == 0)
        def _():
            loss_ref[...] = jnp.zeros_like(loss_ref)
            dg_ref[...] = jnp.zeros_like(dg_ref)

        xv = x_ref[...]
        gv = g_ref[...]
        r = lax.rsqrt(jnp.mean(xv * xv, axis=-1, keepdims=True) + EPS)
        xh = xv * r
        err = xh * gv - t_ref[...]
        per_row = jnp.mean(err * err, axis=-1, keepdims=True)
        loss_ref[...] += jnp.broadcast_to(0.5 * jnp.sum(per_row, axis=0, keepdims=True), loss_ref.shape)
        dyv = err * (1.0 / D)
        dg_ref[...] += jnp.sum(dyv * xh, axis=0, keepdims=True)
        gd = dyv * gv
        dx_ref[...] = r * (gd - xh * jnp.mean(gd * xh, axis=-1, keepdims=True))

    rows = pl.BlockSpec((tm, D), lambda i: (i, 0))
    vec = pl.BlockSpec((1, D), lambda i: (0, 0))
    return pl.pallas_call(
        body, name="loss_head", grid=(T // tm,),
        in_specs=[rows, vec, rows], out_specs=[pl.BlockSpec((1, LANES), lambda i: (0, 0)), rows, vec],
        out_shape=[jax.ShapeDtypeStruct((1, LANES), F32), jax.ShapeDtypeStruct((T, D), F32),
                   jax.ShapeDtypeStruct((1, D), F32)],
        compiler_params=_params("arbitrary"),
    )(x, g, target)


def _adamw_math(w, g, m, v):
    m = ADAM_B1 * m + (1.0 - ADAM_B1) * g
    v = ADAM_B2 * v + (1.0 - ADAM_B2) * (g * g)
    m_hat = m / (1.0 - ADAM_B1 ** ADAM_STEP)
    v_hat = v / (1.0 - ADAM_B2 ** ADAM_STEP)
    delta = -ADAM_LR * (m_hat / (jnp.sqrt(v_hat) + ADAM_EPS) + ADAM_WD * w)
    return delta, m, v


def _as_rows(a):
    return a.reshape(-1, a.shape[-1])


def sum_parts(layers):
    n, R, C = layers[0].shape
    depth = len(layers)
    tr = _tile(R, 128)
    blocks = R // tr

    def body(*refs):
        o_ref = refs[-1]
        for l in range(depth):
            @pl.when(pl.program_id(0) == l)
            def _(p_ref=refs[l]):
                acc = p_ref[0].astype(F32)
                for k in range(1, n):
                    acc = acc + p_ref[k].astype(F32)
                o_ref[...] = acc

    in_specs = [pl.BlockSpec((n, tr, C), lambda L, i, l=l: (0, jnp.where(L == l, i, 0), 0)) for l in range(depth)]
    return pl.pallas_call(
        body, name="sum_parts", grid=(depth, blocks), in_specs=in_specs,
        out_specs=pl.BlockSpec((tr, C), lambda L, i: (L * blocks + i, 0)),
        out_shape=jax.ShapeDtypeStruct((depth * R, C), F32), compiler_params=_params("arbitrary", "arbitrary"),
    )(*layers)


def adamw_sum(w, parts, m, v):
    n = len(parts)
    R, C = w.shape
    tr = _tile(R, 256)

    def body(*refs):
        w_ref, m_ref, v_ref = refs[:3]
        g_ref, d_ref, nm_ref, nv_ref = refs[3 + n:]
        g = refs[3][...]
        for p_ref in refs[4:3 + n]:
            g = g + p_ref[...]
        delta, nm, nv = _adamw_math(w_ref[...], g, m_ref[...], v_ref[...])
        g_ref[...] = g
        d_ref[...] = delta
        nm_ref[...] = nm
        nv_ref[...] = nv

    rows = pl.BlockSpec((tr, C), lambda i: (i, 0))
    shape = jax.ShapeDtypeStruct((R, C), F32)
    return pl.pallas_call(
        body, name="adamw_sum", grid=(R // tr,), in_specs=[rows] * (3 + n), out_specs=[rows] * 4,
        out_shape=[shape] * 4, compiler_params=_params("parallel"),
    )(w, m, v, *parts)


def _place():
    return lax.axis_index("x"), lax.axis_index("y"), lax.axis_index("c")


HBM_SPEC = pl.BlockSpec(memory_space=pltpu.HBM)
SEM_SPEC = pl.BlockSpec(memory_space=pltpu.SEMAPHORE)
DATAFLOW = pltpu.SideEffectType.DATAFLOW_SIDE_EFFECTING


def _chip_copies(srcs, lands, send_sems, recv_sems, local_sems, scatter):
    x, y, c = _place()
    me = 2 * x + y
    outgoing, arriving, local = [], [], []
    for k, (src, land) in enumerate(zip(srcs, lands)):
        for d, (px, py) in enumerate([(1 - x, y), (x, 1 - y), (1 - x, 1 - y)]):
            peer = 2 * px + py
            pair = dict(send_sem=send_sems.at[3 * k + d], recv_sem=recv_sems.at[3 * k + d],
                        device_id=(px, py, c), device_id_type=MESH)
            outgoing.append(pltpu.make_async_remote_copy(
                src_ref=src.at[peer] if scatter else src, dst_ref=land.at[me], **pair))
            arriving.append(pltpu.make_async_remote_copy(
                src_ref=src.at[me] if scatter else src, dst_ref=land.at[peer], **pair))
        local.append(pltpu.make_async_copy(src.at[me] if scatter else src, land.at[me], local_sems.at[k]))
    return outgoing, arriving, local


def exchange_start(arrays, *, scatter, name):
    n = len(arrays)
    lands = [lax.empty(a.shape if scatter else (N_CHIPS,) + a.shape, a.dtype) for a in arrays]

    def body(*refs):
        srcs, zones = refs[:n], refs[n:2 * n]
        send_sems, recv_sems, local_sems = refs[2 * n:2 * n + 3]
        token = refs[-1]
        outgoing, _, local = _chip_copies(srcs, zones, send_sems, recv_sems, local_sems, scatter)
        for cp in outgoing + local:
            cp.start()
        token[...] = jnp.zeros_like(token)

    hbm = lambda a: pltpu.HBM(a.shape, a.dtype)
    outs = pl.pallas_call(
        body, name=name,
        out_shape=(pltpu.SemaphoreType.DMA((3 * n,)), pltpu.SemaphoreType.DMA((3 * n,)), pltpu.SemaphoreType.DMA((n,)),
                   *[hbm(a) for a in arrays], *[hbm(a) for a in lands], jax.ShapeDtypeStruct((SUBLANES, LANES), F32)),
        in_specs=[HBM_SPEC] * (2 * n),
        out_specs=(SEM_SPEC, SEM_SPEC, SEM_SPEC, *[HBM_SPEC] * (2 * n), pl.BlockSpec(memory_space=pltpu.VMEM)),
        input_output_aliases={i: 3 + i for i in range(2 * n)},
        compiler_params=pltpu.CompilerParams(has_side_effects=DATAFLOW),
    )(*[pltpu.with_memory_space_constraint(a, pltpu.HBM) for a in list(arrays) + lands])
    return outs[:3], outs[3:3 + n], outs[3 + n:3 + 2 * n], outs[-1]


def exchange_wait(started, after, *, scatter, name):
    sems, srcs, lands, _ = started
    n = len(srcs)

    def body(*refs):
        src_refs, zones = refs[:n], refs[n:2 * n]
        send_sems, recv_sems, local_sems = refs[2 * n:2 * n + 3]
        outgoing, arriving, local = _chip_copies(src_refs, zones, send_sems, recv_sems, local_sems, scatter)
        for sent, landed in zip(outgoing, arriving):
            sent.wait_send()
            landed.wait_recv()
        for cp in local:
            cp.wait()

    hbm = lambda a: pltpu.HBM(a.shape, a.dtype)
    outs = pl.pallas_call(
        body, name=name, out_shape=[hbm(a) for a in list(srcs) + list(lands)],
        in_specs=[HBM_SPEC] * (2 * n) + [SEM_SPEC] * 3 + [pl.BlockSpec(memory_space=pl.ANY)],
        out_specs=[HBM_SPEC] * (2 * n), input_output_aliases={i: i for i in range(2 * n)},
        compiler_params=pltpu.CompilerParams(has_side_effects=DATAFLOW),
    )(*srcs, *lands, *sems, after)
    return outs[n:]


def sibling_exchange(arrays, *, name):
    n = len(arrays)

    def body(*refs):
        ins, outs = refs[:n], refs[n:2 * n]
        send_sems, recv_sems = refs[2 * n:]
        x, y, c = _place()
        copies = [pltpu.make_async_remote_copy(
            src_ref=ins[k], dst_ref=outs[k], send_sem=send_sems.at[k], recv_sem=recv_sems.at[k],
            device_id=(x, y, 1 - c), device_id_type=MESH) for k in range(n)]
        for cp in copies:
            cp.start()
        for cp in copies:
            cp.wait()

    any_spec = pl.BlockSpec(memory_space=pl.ANY)
    return pl.pallas_call(
        body, name=name, in_specs=[any_spec] * n, out_specs=[any_spec] * n,
        out_shape=[jax.ShapeDtypeStruct(a.shape, a.dtype) for a in arrays],
        scratch_shapes=[pltpu.SemaphoreType.DMA((n,)), pltpu.SemaphoreType.DMA((n,))],
    )(*arrays)


def all_exchange(a, *, name):
    def body(in_ref, out_ref, send_sems, recv_sems, local_sem):
        x, y, c = _place()
        me = 4 * x + 2 * y + c
        flips = [(fx, fy, fc) for fx in (0, 1) for fy in (0, 1) for fc in (0, 1)][1:]

        def peer_of(d):
            fx, fy, fc = flips[d]
            return (x ^ fx, y ^ fy, c ^ fc)

        def copy(d, arriving):
            px, py, pc = peer_of(d)
            slot = (4 * px + 2 * py + pc) if arriving else me
            return pltpu.make_async_remote_copy(
                src_ref=in_ref, dst_ref=out_ref.at[slot], send_sem=send_sems.at[d], recv_sem=recv_sems.at[d],
                device_id=(px, py, pc), device_id_type=MESH)

        local = pltpu.make_async_copy(in_ref, out_ref.at[me], local_sem)
        local.start()
        for d in range(N_DEVICES - 1):
            copy(d, False).start()
        for d in range(N_DEVICES - 1):
            copy(d, True).wait()
        local.wait()

    any_spec = pl.BlockSpec(memory_space=pl.ANY)
    return pl.pallas_call(
        body, name=name, in_specs=[any_spec], out_specs=any_spec,
        out_shape=jax.ShapeDtypeStruct((N_DEVICES,) + a.shape, a.dtype),
        scratch_shapes=[pltpu.SemaphoreType.DMA((N_DEVICES - 1,)), pltpu.SemaphoreType.DMA((N_DEVICES - 1,)),
                        pltpu.SemaphoreType.DMA(())],
    )(a)


BIG = ("ffn1_w_gate", "ffn1_w_up", "ffn1_w_down", "w_in", "w_attn_out", "w_lru_out", "w_mem_kv", "w_mem_out",
       "w_out", "ffn2_w_gate", "ffn2_w_up", "ffn2_w_down")
COLUMN_SHARDED = ("ffn1_w_gate", "ffn1_w_up", "w_in", "w_attn_out", "w_mem_out", "ffn2_w_gate", "ffn2_w_up", "conv_w")
SMALL = ("ffn1_norm", "mix_norm", "gate_bias", "attn_sinks", "conv_w", "conv_b", "lru_wa", "lru_ba", "lru_wx", "lru_bx",
         "lru_lambda", "mem_norm", "ffn2_norm", "final_norm")
WEIGHTS = ("ffn1_norm", "ffn1_w_gate", "ffn1_w_up", "ffn1_w_down", "mix_norm", "w_in", "gate_bias", "attn_sinks",
           "w_attn_out", "conv_w", "conv_b", "lru_wa", "lru_ba", "lru_wx", "lru_bx", "lru_lambda", "w_lru_out", "mem_norm",
           "w_mem_kv", "w_mem_out", "w_out", "ffn2_norm", "ffn2_w_gate", "ffn2_w_up", "ffn2_w_down", "final_norm")
SEGMENTS = (("qkv", 0, 768), ("xr", 768, 1792), ("yr", 1792, 2816), ("cq", 2816, 3328), ("gates", 3328, 6400))
PACK_ROW = 1024


STAGES = ("ffn1", "mix", "ffn2")
STAGE_BIG = {"ffn1": ("ffn1_w_gate", "ffn1_w_up", "ffn1_w_down"),
             "mix": ("w_in", "w_attn_out", "w_lru_out", "w_mem_kv", "w_mem_out", "w_out"),
             "ffn2": ("ffn2_w_gate", "ffn2_w_up", "ffn2_w_down")}


def _join_shards(gathered, name):
    axis = -1 if name in COLUMN_SHARDED else 0
    return jnp.concatenate([gathered[s] for s in range(N_CHIPS)], axis=axis)


def _split_shards(full, name):
    axis = full.ndim - 1 if name in COLUMN_SHARDED else 0
    return jnp.stack(jnp.split(full, N_CHIPS, axis=axis))


def _lane_tiles(w):
    z = jnp.zeros((LRU_WIDTH // LRU_TILE, HEAD_DIM, HEAD_DIM), w.dtype)
    top = jnp.concatenate([w[0::2], z], axis=2)
    bottom = jnp.concatenate([z, w[1::2]], axis=2)
    return jnp.concatenate([top, bottom], axis=1)


def _from_lane_tiles(t):
    pairs = jnp.stack([t[:, :HEAD_DIM, :HEAD_DIM], t[:, HEAD_DIM:, HEAD_DIM:]], axis=1)
    return pairs.reshape(2 * t.shape[0], HEAD_DIM, HEAD_DIM)


def _pack(arrays):
    flat = jnp.concatenate([a.reshape(-1).astype(F32) for a in arrays])
    pad = (-flat.shape[0]) % (SUBLANES * PACK_ROW)
    return jnp.pad(flat, (0, pad)).reshape(-1, PACK_ROW)


def _unpack(packed, shapes):
    flat = packed.reshape(-1)
    out, at = [], 0
    for shape in shapes:
        size = int(np.prod(shape))
        out.append(flat[at:at + size].reshape(shape))
        at += size
    return out


def _ffn_forward(x, norm, w_gate, w_up, w_down):
    h = rms_fwd(x, norm)
    G, U, A = ffn_up(h, w_gate, w_up)
    y = mm([A], [w_down], nt=False, out_dtype=F32, res=x, scale=0.5, name="ffn_down")
    return y, (x, h, G, U, A)


def _ffn_backward(half, dy, saved, norm, w_gate, w_up, w_down):
    x, h, G, U, A = saved
    dG, dU = ffn_bwd_act(half, dy, w_down, G, U)
    d_down = mm_tn(A, dy, out_dtype=MXU_DTYPE, b_scale=0.5, name="ffn_dw_down")
    d_gate = mm_tn(h, dG, out_dtype=MXU_DTYPE, name="ffn_dw_up")
    d_up = mm_tn(h, dU, out_dtype=MXU_DTYPE, name="ffn_dw_up")
    dh = mm([dG, dU], [w_gate, w_up], nt=True, out_dtype=F32, name="ffn_dh")
    dx, d_norm = rms_bwd(x, norm, dh, dy)
    return dx, d_norm, d_gate, d_up, d_down


def kernel(x, mem, ffn1_norm, ffn1_w_gate, ffn1_w_up, ffn1_w_down, mix_norm, w_in, gate_bias, attn_sinks, w_attn_out, conv_w, conv_b, lru_wa, lru_ba, lru_wx, lru_bx, lru_lambda, w_lru_out, mem_norm, w_mem_kv, w_mem_out, w_out, ffn2_norm, ffn2_w_gate, ffn2_w_up, ffn2_w_down, final_norm, loss_target, m_ffn1_norm, m_ffn1_w_gate, m_ffn1_w_up, m_ffn1_w_down, m_mix_norm, m_w_in, m_gate_bias, m_attn_sinks, m_w_attn_out, m_conv_w, m_conv_b, m_lru_wa, m_lru_ba, m_lru_wx, m_lru_bx, m_lru_lambda, m_w_lru_out, m_mem_norm, m_w_mem_kv, m_w_mem_out, m_w_out, m_ffn2_norm, m_ffn2_w_gate, m_ffn2_w_up, m_ffn2_w_down, m_final_norm, v_ffn1_norm, v_ffn1_w_gate, v_ffn1_w_up, v_ffn1_w_down, v_mix_norm, v_w_in, v_gate_bias, v_attn_sinks, v_w_attn_out, v_conv_w, v_conv_b, v_lru_wa, v_lru_ba, v_lru_wx, v_lru_bx, v_lru_lambda, v_w_lru_out, v_mem_norm, v_w_mem_kv, v_w_mem_out, v_w_out, v_ffn2_norm, v_ffn2_w_gate, v_ffn2_w_up, v_ffn2_w_down, v_final_norm):
    args = dict(locals())
    W = {n: args[n] for n in WEIGHTS}
    M = {n: args["m_" + n] for n in WEIGHTS}
    V = {n: args["v_" + n] for n in WEIGHTS}
    chip = 2 * lax.axis_index("x") + lax.axis_index("y")
    x0 = x[0]
    mem0 = mem[0]
    target = loss_target[0]

    def row(v):
        return v.reshape(1, -1)

    def stage_names(stage):
        return STAGE_BIG[stage] + (("conv_w",) if stage == "mix" else ())

    gathers = {}
    started_all = jnp.zeros((), F32)
    for l in range(DEPTH):
        for stage in STAGES:
            shards = [W[n][l] if n == "conv_w" else W[n][l].astype(MXU_DTYPE) for n in stage_names(stage)]
            gathers[l, stage] = exchange_start(shards, scatter=False, name=f"gather_start_{stage}_{l}")
            started_all = started_all + gathers[l, stage][3][0, 0]

    def weights_of(l, stage, after):
        lands = exchange_wait(gathers[l, stage], after, scatter=False, name=f"gather_wait_{stage}_{l}")
        return {n: _join_shards(g, n) for n, g in zip(stage_names(stage), lands)}

    saved = []
    full = []
    xs = x0
    for l in range(DEPTH):
        P = weights_of(l, "ffn1", xs)
        first_norm = row(W["ffn1_norm"][l]) + started_all if l == 0 else row(W["ffn1_norm"][l])
        xs, ffn1 = _ffn_forward(xs, first_norm, P["ffn1_w_gate"], P["ffn1_w_up"], P["ffn1_w_down"])
        x_mix = xs
        P.update(weights_of(l, "mix", xs))
        h = rms_fwd(x_mix, row(W["mix_norm"][l]))
        seg = {name: mm([h], [P["w_in"][:, lo:hi]], nt=False, out_dtype=F32, name="proj_" + name)
               for name, lo, hi in SEGMENTS}
        attn = attn_fwd(seg["qkv"], W["attn_sinks"][l])
        wa_t = _lane_tiles(W["lru_wa"][l]).astype(MXU_DTYPE)
        wx_t = _lane_tiles(W["lru_wx"][l]).astype(MXU_DTYPE)
        sp = row(jax.nn.softplus(-W["lru_lambda"][l]))
        lru_consts = (P["conv_w"], row(W["conv_b"][l]), wa_t, row(W["lru_ba"][l]), wx_t, row(W["lru_bx"][l]), sp)
        h_lru, rec = lru_fwd(seg["xr"], seg["yr"], *lru_consts)
        mem_n = rms_fwd(mem0, row(W["mem_norm"][l]))
        kv = mm([mem_n], [P["w_mem_kv"]], nt=False, out_dtype=F32, name="mem_kv")
        cross = mem_attn_fwd(seg["cq"], kv)
        branches = (mm([attn], [P["w_attn_out"]], nt=False, out_dtype=F32, name="branch_attn"),
                    mm([rec], [P["w_lru_out"]], nt=False, out_dtype=F32, name="branch_lru"),
                    mm([cross], [P["w_mem_out"]], nt=False, out_dtype=F32, name="branch_mem"))
        bias = row(W["gate_bias"][l])
        merged = merge_fwd(seg["gates"], bias, branches)
        xs = mm([merged], [P["w_out"]], nt=False, out_dtype=F32, res=x_mix, name="mix_out")
        mix = (x_mix, h, seg, attn, lru_consts, h_lru, rec, mem_n, kv, cross, branches, bias, merged)
        P.update(weights_of(l, "ffn2", xs))
        xs, ffn2 = _ffn_forward(xs, row(W["ffn2_norm"][l]), P["ffn2_w_gate"], P["ffn2_w_up"], P["ffn2_w_down"])
        saved.append((ffn1, mix, ffn2))
        full.append(P)

    loss_lanes, dx, d_final = loss_head(xs, row(W["final_norm"]), target)
    loss = lax.psum(loss_lanes[0, 0], ("x", "y", "c"))

    big_grads = {}
    small_grads = {n: [None] * DEPTH for n in SMALL if n != "final_norm"}
    scatters = {}
    token = jnp.zeros((), F32)

    def send_grads(l, stage):
        chunks = [_split_shards(big_grads[n, l], n) for n in STAGE_BIG[stage]]
        scatters[l, stage] = exchange_start(chunks, scatter=True, name=f"scatter_start_{stage}_{l}")
        return scatters[l, stage][3][0, 0]

    for l in reversed(range(DEPTH)):
        P = full[l]
        ffn1, mix, ffn2 = saved[l]
        dx, g_norm, g_gate, g_up, g_down = _ffn_backward(
            (0.5 + token).reshape(1), dx, ffn2, row(W["ffn2_norm"][l]), P["ffn2_w_gate"], P["ffn2_w_up"], P["ffn2_w_down"])
        small_grads["ffn2_norm"][l] = g_norm
        big_grads["ffn2_w_gate", l], big_grads["ffn2_w_up", l], big_grads["ffn2_w_down", l] = g_gate, g_up, g_down
        token = send_grads(l, "ffn2")

        x_mix, h, seg, attn, lru_consts, h_lru, rec, mem_n, kv, cross, branches, bias, merged = mix
        bias = bias + token
        d_merged = mm([dx], [P["w_out"]], nt=True, out_dtype=F32, name="d_merged")
        big_grads["w_out", l] = mm_tn(merged, dx, out_dtype=MXU_DTYPE, name="dw_out")
        d_attn_br, d_lru_br, d_mem_br, d_gates, d_bias = merge_bwd(d_merged, seg["gates"], bias, branches)
        small_grads["gate_bias"][l] = d_bias
        d_attn = mm([d_attn_br], [P["w_attn_out"]], nt=True, out_dtype=F32, name="d_attn")
        d_rec = mm([d_lru_br], [P["w_lru_out"]], nt=True, out_dtype=F32, name="d_rec")
        d_cross = mm([d_mem_br], [P["w_mem_out"]], nt=True, out_dtype=F32, name="d_cross")
        big_grads["w_attn_out", l] = mm_tn(attn, d_attn_br, out_dtype=MXU_DTYPE, name="dw_attn_out")
        big_grads["w_lru_out", l] = mm_tn(rec, d_lru_br, out_dtype=MXU_DTYPE, name="dw_lru_out")
        big_grads["w_mem_out", l] = mm_tn(cross, d_mem_br, out_dtype=MXU_DTYPE, name="dw_mem_out")

        d_qkv, d_sinks = attn_bwd(seg["qkv"], W["attn_sinks"][l], d_attn)
        small_grads["attn_sinks"][l] = d_sinks

        d_xc, d_yr, d_wa_t, d_wx_t, d_ba, d_bx, d_sp, d_cb = lru_bwd(seg["xr"], seg["yr"], h_lru, d_rec, *lru_consts)
        d_xr, d_cw = conv_bwd(seg["xr"], d_xc, lru_consts[0])
        small_grads["conv_w"][l] = d_cw
        small_grads["conv_b"][l] = d_cb
        small_grads["lru_wa"][l] = _from_lane_tiles(d_wa_t)
        small_grads["lru_wx"][l] = _from_lane_tiles(d_wx_t)
        small_grads["lru_ba"][l] = d_ba
        small_grads["lru_bx"][l] = d_bx
        small_grads["lru_lambda"][l] = -d_sp * _sigmoid(-row(W["lru_lambda"][l]))

        d_cq, d_kv = mem_attn_bwd(seg["cq"], kv, d_cross)
        big_grads["w_mem_kv", l] = mm_tn(mem_n, d_kv, out_dtype=MXU_DTYPE, name="dw_mem_kv")
        d_mem_n = mm([d_kv], [P["w_mem_kv"]], nt=True, out_dtype=F32, name="d_mem_n")
        _, g_mem_norm = rms_bwd(mem0, row(W["mem_norm"][l]), d_mem_n, jnp.zeros_like(mem0))
        small_grads["mem_norm"][l] = g_mem_norm

        d_seg = {"qkv": d_qkv, "xr": d_xr, "yr": d_yr, "cq": d_cq, "gates": d_gates}
        big_grads["w_in", l] = jnp.concatenate(
            [mm_tn(h, d_seg[name], out_dtype=MXU_DTYPE, name="dw_in_" + name) for name, _, _ in SEGMENTS], axis=1)
        dh = mm([d_seg[name] for name, _, _ in SEGMENTS], [P["w_in"][:, lo:hi] for _, lo, hi in SEGMENTS],
                nt=True, out_dtype=F32, tm=256, tn=512, name="mix_dh")
        dx, g_norm = rms_bwd(x_mix, row(W["mix_norm"][l]), dh, dx)
        small_grads["mix_norm"][l] = g_norm
        token = send_grads(l, "mix")

        dx, g_norm, g_gate, g_up, g_down = _ffn_backward(
            (0.5 + token).reshape(1), dx, ffn1, row(W["ffn1_norm"][l]), P["ffn1_w_gate"], P["ffn1_w_up"], P["ffn1_w_down"])
        small_grads["ffn1_norm"][l] = g_norm
        big_grads["ffn1_w_gate", l], big_grads["ffn1_w_up", l], big_grads["ffn1_w_down", l] = g_gate, g_up, g_down
        token = send_grads(l, "ffn1")

    grad_x = dx[None]

    arrived = {}
    backward_done = dx[:1, :1] + token
    for l in range(DEPTH):
        for stage in STAGES:
            lands = exchange_wait(scatters[l, stage], backward_done, scatter=True, name=f"scatter_wait_{stage}_{l}")
            for n, a in zip(STAGE_BIG[stage], lands):
                arrived[n, l] = a.reshape(N_CHIPS, -1, a.shape[-1])
    partial = [sum_parts([arrived[n, l] for l in range(DEPTH)]) for n in BIG]
    other = sibling_exchange(partial, name="sibling_grads")
    results = {}
    for n, mine, theirs in zip(BIG, partial, other):
        outs = adamw_sum(_as_rows(W[n]), [mine, theirs], _as_rows(M[n]), _as_rows(V[n]))
        results[n] = [o.reshape(W[n].shape) for o in outs]

    small_list = [jnp.stack(small_grads[n]) for n in SMALL if n != "final_norm"] + [d_final]
    small_shapes = [(DEPTH,) + W[n].shape[1:] if n != "conv_w" else (DEPTH, CONV_WIDTH, LRU_WIDTH)
                    for n in SMALL if n != "final_norm"] + [W["final_norm"].shape]
    everyone = all_exchange(_pack(small_list), name="exchange_small")

    def own_columns(full_conv):
        return lax.dynamic_slice_in_dim(full_conv, chip * (LRU_WIDTH // N_CHIPS), LRU_WIDTH // N_CHIPS, axis=2)

    def packed_state(state):
        arrays = []
        for n in SMALL:
            a = state[n]
            if n == "conv_w":
                a = lax.dynamic_update_slice_in_dim(jnp.zeros((DEPTH, CONV_WIDTH, LRU_WIDTH), F32), a,
                                                    chip * (LRU_WIDTH // N_CHIPS), axis=2)
            arrays.append(a)
        return _pack(arrays)

    small_outs = adamw_sum(packed_state(W), [everyone[s] for s in range(N_DEVICES)], packed_state(M), packed_state(V))
    for kind, packed in enumerate(small_outs):
        for n, a in zip(SMALL, _unpack(packed, small_shapes)):
            results.setdefault(n, [None] * 4)[kind] = own_columns(a) if n == "conv_w" else a.reshape(W[n].shape)

    return (loss, grad_x, *[results[n][0] for n in WEIGHTS], *[results[n][1] for n in WEIGHTS],
            *[results[n][2] for n in WEIGHTS], *[results[n][3] for n in WEIGHTS])
```

```python
import functools

import jax
import jax.numpy as jnp
import numpy as np
from jax import lax
from jax.experimental import pallas as pl
from jax.experimental.pallas import tpu as pltpu

F32 = jnp.float32
MXU_DTYPE = jnp.bfloat16

DEPTH = 4
D_MODEL = 1024
CHUNK = 64
HALO = 2 * CHUNK
N_Q_HEADS = 8
HEAD_DIM = 64
ATTN_WIDTH = 512
KV_WIDTH = 128
LRU_WIDTH = 1024
LRU_TILE = 128
CONV_WIDTH = 4
LRU_C = 8.0
MEM_HEADS = 4
MEM_HEAD_DIM = 128
MEM_WIDTH = 512
D_FF = 2816
EPS = 1e-6
ADAM_LR = 0.001
ADAM_B1 = 0.9
ADAM_B2 = 0.999
ADAM_EPS = 1e-08
ADAM_WD = 0.01
ADAM_STEP = 10

N_CHIPS = 4
N_DEVICES = 8
SUBLANES = 8
LANES = 128
VMEM_LIMIT_BYTES = 56 * 1024 * 1024
NEG_BIG = -1e30
MESH = pl.DeviceIdType.MESH

ALIBI_SLOPES = tuple(float(2.0 ** (-8.0 * (i + 1) / N_Q_HEADS)) for i in range(N_Q_HEADS))


def _params(*semantics):
    return pltpu.CompilerParams(dimension_semantics=semantics, vmem_limit_bytes=VMEM_LIMIT_BYTES)


def _tile(n, pref, unit=SUBLANES):
    if n <= pref:
        return n
    for t in range(pref - pref % unit, 0, -unit):
        if n % t == 0:
            return t
    return n


def _dot(a, b, ca, cb):
    return lax.dot_general(a, b, (((ca,), (cb,)), ((), ())), preferred_element_type=F32)


def _sigmoid(x):
    return jax.nn.sigmoid(x)


def rms_fwd(x, g):
    T, D = x.shape
    tm = _tile(T, 512)

    def body(x_ref, g_ref, h_ref):
        xv = x_ref[...]
        r = lax.rsqrt(jnp.mean(xv * xv, axis=-1, keepdims=True) + EPS)
        h_ref[...] = (xv * r * g_ref[...]).astype(h_ref.dtype)

    return pl.pallas_call(
        body, name="rms_fwd", grid=(T // tm,),
        in_specs=[pl.BlockSpec((tm, D), lambda i: (i, 0)), pl.BlockSpec((1, D), lambda i: (0, 0))],
        out_specs=pl.BlockSpec((tm, D), lambda i: (i, 0)),
        out_shape=jax.ShapeDtypeStruct((T, D), MXU_DTYPE),
        compiler_params=_params("parallel"),
    )(x, g)


def rms_bwd(x, g, dh, dy):
    T, D = x.shape
    tm = _tile(T, 512)

    def body(x_ref, g_ref, dh_ref, dy_ref, dx_ref, dg_ref):
        i = pl.program_id(0)
        xv = x_ref[...]
        r = lax.rsqrt(jnp.mean(xv * xv, axis=-1, keepdims=True) + EPS)
        xh = xv * r
        dhv = dh_ref[...]
        gd = dhv * g_ref[...]
        dx_ref[...] = dy_ref[...] + r * (gd - xh * jnp.mean(gd * xh, axis=-1, keepdims=True))
        part = jnp.sum(dhv * xh, axis=0, keepdims=True)

        @pl.when(i == 0)
        def _():
            dg_ref[...] = part

        @pl.when(i > 0)
        def _():
            dg_ref[...] += part

    row = pl.BlockSpec((tm, D), lambda i: (i, 0))
    vec = pl.BlockSpec((1, D), lambda i: (0, 0))
    return pl.pallas_call(
        body, name="rms_bwd", grid=(T // tm,),
        in_specs=[row, vec, row, row], out_specs=[row, vec],
        out_shape=[jax.ShapeDtypeStruct((T, D), F32), jax.ShapeDtypeStruct((1, D), F32)],
        compiler_params=_params("arbitrary"),
    )(x, g, dh, dy)


def mm(a_list, b_list, *, nt, out_dtype, res=None, scale=1.0, tm=512, tn=1024, name="mm"):
    n_pairs = len(a_list)
    T = a_list[0].shape[0]
    N = b_list[0].shape[0] if nt else b_list[0].shape[1]
    tm = _tile(T, tm)
    tn = _tile(N, tn, LANES)
    has_res = res is not None

    def body(*refs):
        a_refs = refs[:n_pairs]
        b_refs = refs[n_pairs:2 * n_pairs]
        o_ref = refs[-1]
        acc = None
        for a_ref, b_ref in zip(a_refs, b_refs):
            d = _dot(a_ref[...].astype(MXU_DTYPE), b_ref[...], 1, 1 if nt else 0)
            acc = d if acc is None else acc + d
        if has_res:
            acc = refs[2 * n_pairs][...] + scale * acc
        elif scale != 1.0:
            acc = scale * acc
        o_ref[...] = acc.astype(o_ref.dtype)

    in_specs = [pl.BlockSpec((tm, a.shape[1]), lambda j, i: (i, 0)) for a in a_list]
    if nt:
        in_specs += [pl.BlockSpec((tn, b.shape[1]), lambda j, i: (j, 0)) for b in b_list]
    else:
        in_specs += [pl.BlockSpec((b.shape[0], tn), lambda j, i: (0, j)) for b in b_list]
    out_spec = pl.BlockSpec((tm, tn), lambda j, i: (i, j))
    operands = list(a_list) + list(b_list)
    if has_res:
        in_specs.append(out_spec)
        operands.append(res)
    return pl.pallas_call(
        body, name=name, grid=(N // tn, T // tm), in_specs=in_specs, out_specs=out_spec,
        out_shape=jax.ShapeDtypeStruct((T, N), out_dtype),
        compiler_params=_params("parallel", "parallel"),
    )(*operands)


def mm_tn(a, b, *, out_dtype, b_scale=1.0, tk=1408, tn=1408, tt=1024, name="mm_tn"):
    T, K = a.shape
    N = b.shape[1]
    tk = _tile(K, tk, LANES)
    tn = _tile(N, tn, LANES)
    tt = _tile(T, tt)
    steps = T // tt

    def body(a_ref, b_ref, o_ref, acc_ref):
        t = pl.program_id(2)
        bv = b_ref[...]
        if b_scale != 1.0:
            bv = b_scale * bv
        d = _dot(a_ref[...].astype(MXU_DTYPE), bv.astype(MXU_DTYPE), 0, 0)

        @pl.when(t == 0)
        def _():
            acc_ref[...] = d

        @pl.when(t > 0)
        def _():
            acc_ref[...] += d

        @pl.when(t == steps - 1)
        def _():
            o_ref[...] = acc_ref[...].astype(o_ref.dtype)

    return pl.pallas_call(
        body, name=name, grid=(K // tk, N // tn, steps),
        in_specs=[pl.BlockSpec((tt, tk), lambda p, q, t: (t, p)), pl.BlockSpec((tt, tn), lambda p, q, t: (t, q))],
        out_specs=pl.BlockSpec((tk, tn), lambda p, q, t: (p, q)),
        out_shape=jax.ShapeDtypeStruct((K, N), out_dtype),
        scratch_shapes=[pltpu.VMEM((tk, tn), F32)],
        compiler_params=_params("parallel", "parallel", "arbitrary"),
    )(a, b)


def ffn_up(h, w_gate_t, w_up_t):
    T, D = h.shape
    F = w_gate_t.shape[0]
    tm = _tile(T, 512)
    tn = _tile(F, 1408, LANES)

    def body(h_ref, wg_ref, wu_ref, g_ref, u_ref, a_ref):
        hv = h_ref[...]
        G = _dot(hv, wg_ref[...], 1, 1)
        U = _dot(hv, wu_ref[...], 1, 1)
        g_ref[...] = G.astype(g_ref.dtype)
        u_ref[...] = U.astype(u_ref.dtype)
        a_ref[...] = (G * _sigmoid(G) * U).astype(a_ref.dtype)

    w_spec = pl.BlockSpec((tn, D), lambda j, i: (j, 0))
    o_spec = pl.BlockSpec((tm, tn), lambda j, i: (i, j))
    return pl.pallas_call(
        body, name="ffn_up", grid=(F // tn, T // tm),
        in_specs=[pl.BlockSpec((tm, D), lambda j, i: (i, 0)), w_spec, w_spec],
        out_specs=[o_spec, o_spec, o_spec],
        out_shape=[jax.ShapeDtypeStruct((T, F), MXU_DTYPE)] * 3,
        compiler_params=_params("parallel", "parallel"),
    )(h, w_gate_t, w_up_t)


def ffn_bwd_act(half, dy, w_down, G, U):
    T, D = dy.shape
    F = w_down.shape[0]
    tm = _tile(T, 512)
    tn = _tile(F, 1408, LANES)

    def body(half_ref, dy_ref, wd_ref, g_ref, u_ref, dg_ref, du_ref):
        d_out = (half_ref[0] * dy_ref[...]).astype(MXU_DTYPE)
        dA = _dot(d_out, wd_ref[...], 1, 1)
        Gv = g_ref[...].astype(F32)
        s = _sigmoid(Gv)
        du_ref[...] = (dA * (Gv * s)).astype(du_ref.dtype)
        dg_ref[...] = (dA * u_ref[...].astype(F32) * (s * (1.0 + Gv * (1.0 - s)))).astype(dg_ref.dtype)

    o_spec = pl.BlockSpec((tm, tn), lambda j, i: (i, j))
    return pl.pallas_call(
        body, name="ffn_bwd_act", grid=(F // tn, T // tm),
        in_specs=[pl.BlockSpec(memory_space=pltpu.SMEM), pl.BlockSpec((tm, D), lambda j, i: (i, 0)),
                  pl.BlockSpec((tn, D), lambda j, i: (j, 0)), o_spec, o_spec],
        out_specs=[o_spec, o_spec],
        out_shape=[jax.ShapeDtypeStruct((T, F), MXU_DTYPE), jax.ShapeDtypeStruct((T, F), MXU_DTYPE)],
        compiler_params=_params("parallel", "parallel"),
    )(half, dy, w_down, G, U)


ATTN_ROWS = 256


def _head_variants(kv128):
    lane = lax.broadcasted_iota(jnp.int32, kv128.shape, 1)
    low = lane < HEAD_DIM
    rolled = pltpu.roll(kv128, HEAD_DIM, 1)
    out = {}
    for j in (0, 1):
        for pos in (0, 1):
            src = kv128 if j == pos else rolled
            out[j, pos] = jnp.where(low if pos == 0 else jnp.logical_not(low), src, 0.0).astype(MXU_DTYPE)
    return out


def _fold_variant(d128, j, pos):
    lane = lax.broadcasted_iota(jnp.int32, d128.shape, 1)
    low = lane < HEAD_DIM
    kept = jnp.where(low if pos == 0 else jnp.logical_not(low), d128, 0.0)
    return kept if j == pos else pltpu.roll(kept, HEAD_DIM, 1)


def _attn_mask(block, rows, keys):
    r = lax.broadcasted_iota(jnp.int32, (rows, keys), 0)
    c = lax.broadcasted_iota(jnp.int32, (rows, keys), 1)
    q_chunk = r // CHUNK
    k_chunk = c // CHUNK - HALO // CHUNK
    first_chunk = jnp.where(block > 0, -(HALO // CHUNK), 0)
    valid = (k_chunk <= q_chunk) & (k_chunk >= q_chunk - HALO // CHUNK) & (k_chunk >= first_chunk)
    dist = jnp.abs(r + HALO - c).astype(F32)
    return valid, dist


def _attn_probs(q128, k_var, head, sink, valid, dist):
    s = _dot(q128, k_var, 1, 1) * (HEAD_DIM ** -0.5) - ALIBI_SLOPES[head] * dist
    s = jnp.where(valid, s, NEG_BIG)
    mx = jnp.maximum(jnp.max(s, axis=-1, keepdims=True), sink)
    p = jnp.exp(s - mx)
    p_sink = jnp.exp(sink - mx)
    inv = 1.0 / (jnp.sum(p, axis=-1, keepdims=True) + p_sink)
    return p * inv, p_sink * inv


def _attn_specs(T, tq, order):
    ratio = tq // HALO
    q_spec = pl.BlockSpec((tq, ATTN_WIDTH), lambda i: (order(i), 0))
    cur = lambda col: pl.BlockSpec((tq, KV_WIDTH), lambda i: (order(i), col))
    prev = lambda col: pl.BlockSpec((HALO, KV_WIDTH), lambda i: (jnp.maximum(order(i) * ratio - 1, 0), col))
    k_col, v_col = ATTN_WIDTH // KV_WIDTH, ATTN_WIDTH // KV_WIDTH + 1
    return [pl.BlockSpec(memory_space=pltpu.SMEM), q_spec, prev(k_col), cur(k_col), prev(v_col), cur(v_col)]


def attn_fwd(qkv, sinks):
    T = qkv.shape[0]
    tq = _tile(T, ATTN_ROWS)
    keys = tq + HALO

    def body(sink_ref, q_ref, kp_ref, kc_ref, vp_ref, vc_ref, o_ref):
        block = pl.program_id(0)
        k_var = _head_variants(jnp.concatenate([kp_ref[...], kc_ref[...]], axis=0))
        v_var = _head_variants(jnp.concatenate([vp_ref[...], vc_ref[...]], axis=0))
        valid, dist = _attn_mask(block, tq, keys)
        for pair in range(N_Q_HEADS // 2):
            q128 = q_ref[:, pair * LANES:(pair + 1) * LANES].astype(MXU_DTYPE)
            acc = None
            for pos in (0, 1):
                head = 2 * pair + pos
                j = head // 4
                p, _ = _attn_probs(q128, k_var[j, pos], head, sink_ref[head], valid, dist)
                d = _dot(p.astype(MXU_DTYPE), v_var[j, pos], 1, 0)
                acc = d if acc is None else acc + d
            o_ref[:, pair * LANES:(pair + 1) * LANES] = acc.astype(o_ref.dtype)

    return pl.pallas_call(
        body, name="attn_fwd", grid=(T // tq,),
        in_specs=_attn_specs(T, tq, lambda i: i),
        out_specs=pl.BlockSpec((tq, ATTN_WIDTH), lambda i: (i, 0)),
        out_shape=jax.ShapeDtypeStruct((T, ATTN_WIDTH), MXU_DTYPE),
        compiler_params=_params("parallel"),
    )(sinks, qkv, qkv, qkv, qkv, qkv)


def attn_bwd(qkv, sinks, d_out):
    T = qkv.shape[0]
    tq = _tile(T, ATTN_ROWS)
    keys = tq + HALO
    n_blocks = T // tq
    order = lambda i: n_blocks - 1 - i

    def body(sink_ref, q_ref, kp_ref, kc_ref, vp_ref, vc_ref, do_ref, dqkv_ref, dsink_ref, carry_ref):
        step = pl.program_id(0)
        block = order(step)
        k_var = _head_variants(jnp.concatenate([kp_ref[...], kc_ref[...]], axis=0))
        v_var = _head_variants(jnp.concatenate([vp_ref[...], vc_ref[...]], axis=0))
        valid, dist = _attn_mask(block, tq, keys)

        @pl.when(step == 0)
        def _():
            carry_ref[...] = jnp.zeros_like(carry_ref)
            dsink_ref[...] = jnp.zeros_like(dsink_ref)

        dk = jnp.zeros((keys, KV_WIDTH), F32)
        dv = jnp.zeros((keys, KV_WIDTH), F32)
        for pair in range(N_Q_HEADS // 2):
            q128 = q_ref[:, pair * LANES:(pair + 1) * LANES].astype(MXU_DTYPE)
            do128 = do_ref[:, pair * LANES:(pair + 1) * LANES].astype(MXU_DTYPE)
            dq128 = None
            for pos in (0, 1):
                head = 2 * pair + pos
                j = head // 4
                p, p_sink = _attn_probs(q128, k_var[j, pos], head, sink_ref[head], valid, dist)
                dp = _dot(do128, v_var[j, pos], 1, 1)
                delta = jnp.sum(p * dp, axis=-1, keepdims=True)
                ds = (p * (dp - delta) * (HEAD_DIM ** -0.5)).astype(MXU_DTYPE)
                d = _dot(ds, k_var[j, pos], 1, 0)
                dq128 = d if dq128 is None else dq128 + d
                dk = dk + _fold_variant(_dot(ds, q128, 0, 0), j, pos)
                dv = dv + _fold_variant(_dot(p.astype(MXU_DTYPE), do128, 0, 0), j, pos)
                dsink_ref[pl.ds(head, 1), :] += jnp.broadcast_to(
                    -jnp.sum(p_sink * delta, axis=0, keepdims=True), (1, LANES))
            dqkv_ref[:, pair * LANES:(pair + 1) * LANES] = dq128.astype(dqkv_ref.dtype)
        carried = jnp.concatenate([jnp.zeros((tq - HALO, 2 * KV_WIDTH), F32), carry_ref[...]], axis=0)
        dkv = jnp.concatenate([dk, dv], axis=1)
        dqkv_ref[:, ATTN_WIDTH:] = (dkv[HALO:] + carried).astype(dqkv_ref.dtype)
        carry_ref[...] = dkv[:HALO]

    out_rows = pl.BlockSpec((tq, ATTN_WIDTH + 2 * KV_WIDTH), lambda i: (order(i), 0))
    dqkv, dsink = pl.pallas_call(
        body, name="attn_bwd", grid=(n_blocks,),
        in_specs=_attn_specs(T, tq, order) + [pl.BlockSpec((tq, ATTN_WIDTH), lambda i: (order(i), 0))],
        out_specs=[out_rows, pl.BlockSpec((N_Q_HEADS, LANES), lambda i: (0, 0))],
        out_shape=[jax.ShapeDtypeStruct((T, ATTN_WIDTH + 2 * KV_WIDTH), MXU_DTYPE),
                   jax.ShapeDtypeStruct((N_Q_HEADS, LANES), F32)],
        scratch_shapes=[pltpu.VMEM((HALO, 2 * KV_WIDTH), F32)],
        compiler_params=_params("arbitrary"),
    )(sinks, qkv, qkv, qkv, qkv, qkv, d_out)
    return dqkv, dsink[:, 0]


def _mem_probs(q, k):
    s = _dot(q, k, 1, 1) * (MEM_HEAD_DIM ** -0.5)
    p = jnp.exp(s - jnp.max(s, axis=-1, keepdims=True))
    return p / jnp.sum(p, axis=-1, keepdims=True)


def mem_attn_fwd(cq, kv):
    T = cq.shape[0]
    M = kv.shape[0]
    tq = _tile(T, 512)

    def body(q_ref, kv_ref, o_ref):
        for h in range(MEM_HEADS):
            lo, hi = h * MEM_HEAD_DIM, (h + 1) * MEM_HEAD_DIM
            p = _mem_probs(q_ref[:, lo:hi].astype(MXU_DTYPE), kv_ref[:, lo:hi].astype(MXU_DTYPE))
            v = kv_ref[:, MEM_WIDTH + lo:MEM_WIDTH + hi].astype(MXU_DTYPE)
            o_ref[:, lo:hi] = _dot(p.astype(MXU_DTYPE), v, 1, 0).astype(o_ref.dtype)

    return pl.pallas_call(
        body, name="mem_attn_fwd", grid=(T // tq,),
        in_specs=[pl.BlockSpec((tq, MEM_WIDTH), lambda i: (i, 0)), pl.BlockSpec((M, 2 * MEM_WIDTH), lambda i: (0, 0))],
        out_specs=pl.BlockSpec((tq, MEM_WIDTH), lambda i: (i, 0)),
        out_shape=jax.ShapeDtypeStruct((T, MEM_WIDTH), MXU_DTYPE),
        compiler_params=_params("parallel"),
    )(cq, kv)


def mem_attn_bwd(cq, kv, d_out):
    T = cq.shape[0]
    M = kv.shape[0]
    tq = _tile(T, 512)

    def body(q_ref, kv_ref, do_ref, dq_ref, dkv_ref):
        @pl.when(pl.program_id(0) == 0)
        def _():
            dkv_ref[...] = jnp.zeros_like(dkv_ref)

        for h in range(MEM_HEADS):
            lo, hi = h * MEM_HEAD_DIM, (h + 1) * MEM_HEAD_DIM
            q = q_ref[:, lo:hi].astype(MXU_DTYPE)
            k = kv_ref[:, lo:hi].astype(MXU_DTYPE)
            v = kv_ref[:, MEM_WIDTH + lo:MEM_WIDTH + hi].astype(MXU_DTYPE)
            do = do_ref[:, lo:hi].astype(MXU_DTYPE)
            p = _mem_probs(q, k)
            dp = _dot(do, v, 1, 1)
            ds = (p * (dp - jnp.sum(p * dp, axis=-1, keepdims=True)) * (MEM_HEAD_DIM ** -0.5)).astype(MXU_DTYPE)
            dq_ref[:, lo:hi] = _dot(ds, k, 1, 0).astype(dq_ref.dtype)
            dkv_ref[:, lo:hi] += _dot(ds, q, 0, 0)
            dkv_ref[:, MEM_WIDTH + lo:MEM_WIDTH + hi] += _dot(p.astype(MXU_DTYPE), do, 0, 0)

    rows = pl.BlockSpec((tq, MEM_WIDTH), lambda i: (i, 0))
    whole = pl.BlockSpec((M, 2 * MEM_WIDTH), lambda i: (0, 0))
    return pl.pallas_call(
        body, name="mem_attn_bwd", grid=(T // tq,),
        in_specs=[rows, whole, rows], out_specs=[rows, whole],
        out_shape=[jax.ShapeDtypeStruct((T, MEM_WIDTH), MXU_DTYPE), jax.ShapeDtypeStruct((M, 2 * MEM_WIDTH), F32)],
        compiler_params=_params("arbitrary"),
    )(cq, kv, d_out)


LRU_ROWS = 256


def _shift_down(ext, k, rows):
    return pltpu.roll(ext, k, 0)[SUBLANES:SUBLANES + rows] if k else ext[SUBLANES:SUBLANES + rows]


def _shift_up(ext, k, rows):
    return pltpu.roll(ext, rows + SUBLANES - k, 0)[:rows] if k else ext[:rows]


def _conv(x_ext, conv_w_ref, conv_b_ref, rows):
    xc = conv_b_ref[...] + conv_w_ref[CONV_WIDTH - 1:CONV_WIDTH, :] * _shift_down(x_ext, 0, rows)
    for j in range(CONV_WIDTH - 1):
        xc = xc + conv_w_ref[j:j + 1, :] * _shift_down(x_ext, CONV_WIDTH - 1 - j, rows)
    return xc


def _block_matmul(x, w_ref, cb):
    return jnp.concatenate(
        [_dot(x[:, c * LANES:(c + 1) * LANES].astype(MXU_DTYPE), w_ref[c], 1, cb)
         for c in range(LRU_WIDTH // LRU_TILE)], axis=1)


def _lru_gates(block, xc, wa_ref, ba_ref, wx_ref, bx_ref, sp_ref):
    rows = xc.shape[0]
    ra = _sigmoid(_block_matmul(xc, wa_ref, 0) + ba_ref[...])
    gi = _sigmoid(_block_matmul(xc, wx_ref, 0) + bx_ref[...])
    log_a = -LRU_C * ra * sp_ref[...]
    a = jnp.exp(log_a)
    one_minus = 1.0 - jnp.exp(2.0 * log_a)
    mult = jnp.sqrt(jnp.maximum(one_minus, 0.0))
    row = lax.broadcasted_iota(jnp.int32, (rows, 1), 0)
    first = row == jnp.where(block == 0, 0, -1)
    mult = jnp.where(first, 1.0, mult)
    return ra, gi, a, one_minus, mult, first


def _gelu(x):
    inner = np.sqrt(2.0 / np.pi).astype(np.float32) * (x + 0.044715 * (x * x * x))
    return 0.5 * x * (1.0 + jnp.tanh(inner))


def _gelu_grad(x):
    c = np.sqrt(2.0 / np.pi).astype(np.float32)
    t = jnp.tanh(c * (x + 0.044715 * (x * x * x)))
    return 0.5 * (1.0 + t) + 0.5 * x * (1.0 - t * t) * c * (1.0 + 3.0 * 0.044715 * (x * x))


def _x_ext(block, prev_ref, cur_ref):
    prev = jnp.where(block > 0, prev_ref[...], 0.0)
    return jnp.concatenate([prev, cur_ref[...]], axis=0)


def _lru_in_specs(tb, order, n_rows):
    ratio = tb // SUBLANES
    rows = pl.BlockSpec((tb, LRU_WIDTH), lambda i: (order(i), 0))
    prev8 = pl.BlockSpec((SUBLANES, LRU_WIDTH), lambda i: (jnp.maximum(order(i) * ratio - 1, 0), 0))
    vec = pl.BlockSpec((1, LRU_WIDTH), lambda i: (0, 0))
    conv_w = pl.BlockSpec((CONV_WIDTH, LRU_WIDTH), lambda i: (0, 0))
    tiles = pl.BlockSpec((LRU_WIDTH // LRU_TILE, LRU_TILE, LRU_TILE), lambda i: (0, 0, 0))
    return rows, prev8, vec, conv_w, tiles


def lru_fwd(xr, yr, conv_w, conv_b, wa_t, ba, wx_t, bx, sp):
    T = xr.shape[0]
    tb = _tile(T, LRU_ROWS)

    def body(xp_ref, x_ref, y_ref, cw_ref, cb_ref, wa_ref, ba_ref, wx_ref, bx_ref, sp_ref, h_ref, r_ref, carry_ref):
        block = pl.program_id(0)
        xc = _conv(_x_ext(block, xp_ref, x_ref), cw_ref, cb_ref, tb)
        _, gi, a, _, mult, _ = _lru_gates(block, xc, wa_ref, ba_ref, wx_ref, bx_ref, sp_ref)
        b = mult * gi * xc
        row = lax.broadcasted_iota(jnp.int32, (tb, 1), 0)
        shift = 1
        while shift < tb:
            keep = row >= shift
            b = b + a * jnp.where(keep, pltpu.roll(b, shift, 0), 0.0)
            a = a * jnp.where(keep, pltpu.roll(a, shift, 0), 1.0)
            shift *= 2

        @pl.when(block == 0)
        def _():
            carry_ref[...] = jnp.zeros_like(carry_ref)

        h = b + a * carry_ref[...]
        carry_ref[...] = h[tb - 1:tb, :]
        h_ref[...] = h
        r_ref[...] = (h * _gelu(y_ref[...])).astype(r_ref.dtype)

    rows, prev8, vec, cw, tiles = _lru_in_specs(tb, lambda i: i, T)
    return pl.pallas_call(
        body, name="lru_fwd", grid=(T // tb,),
        in_specs=[prev8, rows, rows, cw, vec, tiles, vec, tiles, vec, vec],
        out_specs=[rows, rows],
        out_shape=[jax.ShapeDtypeStruct((T, LRU_WIDTH), F32), jax.ShapeDtypeStruct((T, LRU_WIDTH), MXU_DTYPE)],
        scratch_shapes=[pltpu.VMEM((1, LRU_WIDTH), F32)],
        compiler_params=_params("arbitrary"),
    )(xr, xr, yr, conv_w, conv_b, wa_t, ba, wx_t, bx, sp)


def lru_bwd(xr, yr, h, d_r, conv_w, conv_b, wa_t, ba, wx_t, bx, sp):
    T = xr.shape[0]
    tb = _tile(T, LRU_ROWS)
    n_blocks = T // tb
    order = lambda i: n_blocks - 1 - i
    n_tiles = LRU_WIDTH // LRU_TILE

    def body(xp_ref, x_ref, y_ref, hp_ref, h_ref, dr_ref, cw_ref, cb_ref, wa_ref, ba_ref, wx_ref, bx_ref, sp_ref,
             dxc_ref, dy_ref, dwa_ref, dwx_ref, dba_ref, dbx_ref, dsp_ref, dcb_ref, lam_ref, a_next_ref):
        step = pl.program_id(0)
        block = order(step)

        @pl.when(step == 0)
        def _():
            lam_ref[...] = jnp.zeros_like(lam_ref)
            a_next_ref[...] = jnp.zeros_like(a_next_ref)
            for ref in (dwa_ref, dwx_ref, dba_ref, dbx_ref, dsp_ref, dcb_ref):
                ref[...] = jnp.zeros_like(ref)

        xc = _conv(_x_ext(block, xp_ref, x_ref), cw_ref, cb_ref, tb)
        ra, gi, a, one_minus, mult, first = _lru_gates(block, xc, wa_ref, ba_ref, wx_ref, bx_ref, sp_ref)
        yv = y_ref[...]
        hv = h_ref[...]
        drv = dr_ref[...].astype(F32)
        dy_ref[...] = (drv * hv * _gelu_grad(yv)).astype(dy_ref.dtype)

        row = lax.broadcasted_iota(jnp.int32, (tb, 1), 0)
        coef = jnp.where(row == tb - 1, a_next_ref[...], pltpu.roll(a, tb - 1, 0))
        lam = drv * _gelu(yv)
        shift = 1
        while shift < tb:
            keep = row < tb - shift
            lam = lam + coef * jnp.where(keep, pltpu.roll(lam, tb - shift, 0), 0.0)
            coef = coef * jnp.where(keep, pltpu.roll(coef, tb - shift, 0), 1.0)
            shift *= 2
        lam = lam + coef * lam_ref[...]
        lam_ref[...] = lam[0:1, :]
        a_next_ref[...] = a[0:1, :]

        h_prev = _shift_down(jnp.concatenate([jnp.where(block > 0, hp_ref[...], 0.0), hv], axis=0), 1, tb)
        d_a = lam * h_prev
        d_mult = jnp.where(first, 0.0, lam * gi * xc)
        d_gi = lam * mult * xc
        d_xc = lam * mult * gi
        safe = one_minus > 0.0
        d_log_a = d_a * a + d_mult * jnp.where(safe, -(1.0 - one_minus) * lax.rsqrt(jnp.where(safe, one_minus, 1.0)), 0.0)
        d_za = d_log_a * (-LRU_C * sp_ref[...]) * ra * (1.0 - ra)
        d_zx = d_gi * gi * (1.0 - gi)
        dsp_ref[...] += jnp.sum(d_log_a * (-LRU_C * ra), axis=0, keepdims=True)
        dba_ref[...] += jnp.sum(d_za, axis=0, keepdims=True)
        dbx_ref[...] += jnp.sum(d_zx, axis=0, keepdims=True)
        d_xc = d_xc + _block_matmul(d_za, wa_ref, 1) + _block_matmul(d_zx, wx_ref, 1)
        dcb_ref[...] += jnp.sum(d_xc, axis=0, keepdims=True)
        dxc_ref[...] = d_xc
        xc_m = xc.astype(MXU_DTYPE)
        for c in range(n_tiles):
            lanes = slice(c * LANES, (c + 1) * LANES)
            dwa_ref[c] += _dot(xc_m[:, lanes], d_za[:, lanes].astype(MXU_DTYPE), 0, 0)
            dwx_ref[c] += _dot(xc_m[:, lanes], d_zx[:, lanes].astype(MXU_DTYPE), 0, 0)

    rows, prev8, vec, cw, tiles = _lru_in_specs(tb, order, T)
    return pl.pallas_call(
        body, name="lru_bwd", grid=(n_blocks,),
        in_specs=[prev8, rows, rows, prev8, rows, rows, cw, vec, tiles, vec, tiles, vec, vec],
        out_specs=[rows, rows, tiles, tiles, vec, vec, vec, vec],
        out_shape=[jax.ShapeDtypeStruct((T, LRU_WIDTH), F32), jax.ShapeDtypeStruct((T, LRU_WIDTH), MXU_DTYPE),
                   jax.ShapeDtypeStruct((n_tiles, LRU_TILE, LRU_TILE), F32),
                   jax.ShapeDtypeStruct((n_tiles, LRU_TILE, LRU_TILE), F32)]
        + [jax.ShapeDtypeStruct((1, LRU_WIDTH), F32)] * 4,
        scratch_shapes=[pltpu.VMEM((1, LRU_WIDTH), F32), pltpu.VMEM((1, LRU_WIDTH), F32)],
        compiler_params=_params("arbitrary"),
    )(xr, xr, yr, h, h, d_r, conv_w, conv_b, wa_t, ba, wx_t, bx, sp)


def conv_bwd(xr, d_xc, conv_w):
    T = xr.shape[0]
    tb = _tile(T, LRU_ROWS)
    n_blocks = T // tb
    ratio = tb // SUBLANES

    def body(xp_ref, x_ref, d_ref, dn_ref, cw_ref, dx_ref, dcw_ref):
        block = pl.program_id(0)

        @pl.when(block == 0)
        def _():
            dcw_ref[...] = jnp.zeros_like(dcw_ref)

        x_ext = _x_ext(block, xp_ref, x_ref)
        dv = d_ref[...]
        d_ext = jnp.concatenate([dv, jnp.where(block < n_blocks - 1, dn_ref[...], 0.0)], axis=0)
        dx = None
        for j in range(CONV_WIDTH):
            k = CONV_WIDTH - 1 - j
            term = cw_ref[j:j + 1, :] * _shift_up(d_ext, k, tb)
            dx = term if dx is None else dx + term
            dcw_ref[j:j + 1, :] += jnp.sum(dv * _shift_down(x_ext, k, tb), axis=0, keepdims=True)
        dx_ref[...] = dx.astype(dx_ref.dtype)

    rows = pl.BlockSpec((tb, LRU_WIDTH), lambda i: (i, 0))
    prev8 = pl.BlockSpec((SUBLANES, LRU_WIDTH), lambda i: (jnp.maximum(i * ratio - 1, 0), 0))
    next8 = pl.BlockSpec((SUBLANES, LRU_WIDTH), lambda i: (jnp.minimum((i + 1) * ratio, n_blocks * ratio - 1), 0))
    cw = pl.BlockSpec((CONV_WIDTH, LRU_WIDTH), lambda i: (0, 0))
    return pl.pallas_call(
        body, name="conv_bwd", grid=(n_blocks,),
        in_specs=[prev8, rows, rows, next8, cw], out_specs=[rows, cw],
        out_shape=[jax.ShapeDtypeStruct((T, LRU_WIDTH), MXU_DTYPE), jax.ShapeDtypeStruct((CONV_WIDTH, LRU_WIDTH), F32)],
        compiler_params=_params("arbitrary"),
    )(xr, xr, d_xc, d_xc, conv_w)


def merge_fwd(gates, bias, branches):
    T = gates.shape[0]
    tm = _tile(T, 512)
    D = D_MODEL

    def body(g_ref, b_ref, a_ref, r_ref, c_ref, o_ref):
        acc = None
        for k, br in enumerate((a_ref, r_ref, c_ref)):
            term = _sigmoid(g_ref[:, k * D:(k + 1) * D] + b_ref[:, k * D:(k + 1) * D]) * br[...]
            acc = term if acc is None else acc + term
        o_ref[...] = acc.astype(o_ref.dtype)

    rows = pl.BlockSpec((tm, D), lambda i: (i, 0))
    return pl.pallas_call(
        body, name="merge_fwd", grid=(T // tm,),
        in_specs=[pl.BlockSpec((tm, 3 * D), lambda i: (i, 0)), pl.BlockSpec((1, 3 * D), lambda i: (0, 0)), rows, rows, rows],
        out_specs=rows, out_shape=jax.ShapeDtypeStruct((T, D), MXU_DTYPE),
        compiler_params=_params("parallel"),
    )(gates, bias, *branches)


def merge_bwd(d_merged, gates, bias, branches):
    T = gates.shape[0]
    tm = _tile(T, 512)
    D = D_MODEL

    def body(dm_ref, g_ref, b_ref, a_ref, r_ref, c_ref, da_ref, dr_ref, dc_ref, dg_ref, db_ref):
        @pl.when(pl.program_id(0) == 0)
        def _():
            db_ref[...] = jnp.zeros_like(db_ref)

        dm = dm_ref[...]
        for k, (br, dbr) in enumerate(((a_ref, da_ref), (r_ref, dr_ref), (c_ref, dc_ref))):
            cols = slice(k * D, (k + 1) * D)
            s = _sigmoid(g_ref[:, cols] + b_ref[:, cols])
            dbr[...] = (dm * s).astype(dbr.dtype)
            d_pre = dm * br[...] * s * (1.0 - s)
            dg_ref[:, cols] = d_pre.astype(dg_ref.dtype)
            db_ref[:, cols] += jnp.sum(d_pre, axis=0, keepdims=True)

    rows = pl.BlockSpec((tm, D), lambda i: (i, 0))
    wide = pl.BlockSpec((tm, 3 * D), lambda i: (i, 0))
    vec = pl.BlockSpec((1, 3 * D), lambda i: (0, 0))
    act = jax.ShapeDtypeStruct((T, D), MXU_DTYPE)
    return pl.pallas_call(
        body, name="merge_bwd", grid=(T // tm,),
        in_specs=[rows, wide, vec, rows, rows, rows], out_specs=[rows, rows, rows, wide, vec],
        out_shape=[act, act, act, jax.ShapeDtypeStruct((T, 3 * D), MXU_DTYPE), jax.ShapeDtypeStruct((1, 3 * D), F32)],
        compiler_params=_params("arbitrary"),
    )(d_merged, gates, bias, *branches)


def loss_head(x, g, target):
    T, D = x.shape
    tm = _tile(T, 512)

    def body(x_ref, g_ref, t_ref, loss_ref, dx_ref, dg_ref):
        @pl.when(pl.program_id(0) == 0)
        def _():
            loss_ref[...] = jnp.zeros_like(loss_ref)
            dg_ref[...] = jnp.zeros_like(dg_ref)

        xv = x_ref[...]
        gv = g_ref[...]
        r = lax.rsqrt(jnp.mean(xv * xv, axis=-1, keepdims=True) + EPS)
        xh = xv * r
        err = xh * gv - t_ref[...]
        per_row = jnp.mean(err * err, axis=-1, keepdims=True)
        loss_ref[...] += jnp.broadcast_to(0.5 * jnp.sum(per_row, axis=0, keepdims=True), loss_ref.shape)
        dyv = err * (1.0 / D)
        dg_ref[...] += jnp.sum(dyv * xh, axis=0, keepdims=True)
        gd = dyv * gv
        dx_ref[...] = r * (gd - xh * jnp.mean(gd * xh, axis=-1, keepdims=True))

    rows = pl.BlockSpec((tm, D), lambda i: (i, 0))
    vec = pl.BlockSpec((1, D), lambda i: (0, 0))
    return pl.pallas_call(
        body, name="loss_head", grid=(T // tm,),
        in_specs=[rows, vec, rows], out_specs=[pl.BlockSpec((1, LANES), lambda i: (0, 0)), rows, vec],
        out_shape=[jax.ShapeDtypeStruct((1, LANES), F32), jax.ShapeDtypeStruct((T, D), F32),
                   jax.ShapeDtypeStruct((1, D), F32)],
        compiler_params=_params("arbitrary"),
    )(x, g, target)


def _adamw_math(w, g, m, v):
    m = ADAM_B1 * m + (1.0 - ADAM_B1) * g
    v = ADAM_B2 * v + (1.0 - ADAM_B2) * (g * g)
    m_hat = m / (1.0 - ADAM_B1 ** ADAM_STEP)
    v_hat = v / (1.0 - ADAM_B2 ** ADAM_STEP)
    delta = -ADAM_LR * (m_hat / (jnp.sqrt(v_hat) + ADAM_EPS) + ADAM_WD * w)
    return delta, m, v


def _as_rows(a):
    return a.reshape(-1, a.shape[-1])


def sum_parts(layers):
    n, R, C = layers[0].shape
    depth = len(layers)
    tr = _tile(R, 512 if C <= 1024 else 128)
    blocks = R // tr

    def body(*refs):
        o_ref = refs[-1]
        for l in range(depth):
            @pl.when(pl.program_id(0) == l)
            def _(p_ref=refs[l]):
                acc = p_ref[0].astype(F32)
                for k in range(1, n):
                    acc = acc + p_ref[k].astype(F32)
                o_ref[...] = acc

    in_specs = [pl.BlockSpec((n, tr, C), lambda L, i, l=l: (0, jnp.where(L == l, i, 0), 0)) for l in range(depth)]
    return pl.pallas_call(
        body, name="sum_parts", grid=(depth, blocks), in_specs=in_specs,
        out_specs=pl.BlockSpec((tr, C), lambda L, i: (L * blocks + i, 0)),
        out_shape=jax.ShapeDtypeStruct((depth * R, C), F32), compiler_params=_params("arbitrary", "arbitrary"),
    )(*layers)


def adamw_sum(w, parts, m, v):
    n = len(parts)
    R, C = w.shape
    tr = _tile(R, 512 if n <= 2 and C <= 1024 else 256)

    def body(*refs):
        w_ref, m_ref, v_ref = refs[:3]
        g_ref, d_ref, nm_ref, nv_ref = refs[3 + n:]
        g = refs[3][...]
        for p_ref in refs[4:3 + n]:
            g = g + p_ref[...]
        delta, nm, nv = _adamw_math(w_ref[...], g, m_ref[...], v_ref[...])
        g_ref[...] = g
        d_ref[...] = delta
        nm_ref[...] = nm
        nv_ref[...] = nv

    rows = pl.BlockSpec((tr, C), lambda i: (i, 0))
    shape = jax.ShapeDtypeStruct((R, C), F32)
    return pl.pallas_call(
        body, name="adamw_sum", grid=(R // tr,), in_specs=[rows] * (3 + n), out_specs=[rows] * 4,
        out_shape=[shape] * 4, compiler_params=_params("parallel"),
    )(w, m, v, *parts)


def _place():
    return lax.axis_index("x"), lax.axis_index("y"), lax.axis_index("c")


HBM_SPEC = pl.BlockSpec(memory_space=pltpu.HBM)
SEM_SPEC = pl.BlockSpec(memory_space=pltpu.SEMAPHORE)
DATAFLOW = pltpu.SideEffectType.DATAFLOW_SIDE_EFFECTING


def _chip_copies(srcs, lands, send_sems, recv_sems, local_sems, scatter):
    x, y, c = _place()
    me = 2 * x + y
    outgoing, arriving, local = [], [], []
    for k, (src, land) in enumerate(zip(srcs, lands)):
        for d, (px, py) in enumerate([(1 - x, y), (x, 1 - y), (1 - x, 1 - y)]):
            peer = 2 * px + py
            pair = dict(send_sem=send_sems.at[3 * k + d], recv_sem=recv_sems.at[3 * k + d],
                        device_id=(px, py, c), device_id_type=MESH)
            outgoing.append(pltpu.make_async_remote_copy(
                src_ref=src.at[peer] if scatter else src, dst_ref=land.at[me], **pair))
            arriving.append(pltpu.make_async_remote_copy(
                src_ref=src.at[me] if scatter else src, dst_ref=land.at[peer], **pair))
        local.append(pltpu.make_async_copy(src.at[me] if scatter else src, land.at[me], local_sems.at[k]))
    return outgoing, arriving, local


def exchange_start(arrays, *, scatter, name):
    n = len(arrays)
    lands = [lax.empty(a.shape if scatter else (N_CHIPS,) + a.shape, a.dtype) for a in arrays]

    def body(*refs):
        srcs, zones = refs[:n], refs[n:2 * n]
        send_sems, recv_sems, local_sems = refs[2 * n:2 * n + 3]
        token = refs[-1]
        outgoing, _, local = _chip_copies(srcs, zones, send_sems, recv_sems, local_sems, scatter)
        for cp in outgoing + local:
            cp.start()
        token[...] = jnp.zeros_like(token)

    hbm = lambda a: pltpu.HBM(a.shape, a.dtype)
    outs = pl.pallas_call(
        body, name=name,
        out_shape=(pltpu.SemaphoreType.DMA((3 * n,)), pltpu.SemaphoreType.DMA((3 * n,)), pltpu.SemaphoreType.DMA((n,)),
                   *[hbm(a) for a in arrays], *[hbm(a) for a in lands], jax.ShapeDtypeStruct((SUBLANES, LANES), F32)),
        in_specs=[HBM_SPEC] * (2 * n),
        out_specs=(SEM_SPEC, SEM_SPEC, SEM_SPEC, *[HBM_SPEC] * (2 * n), pl.BlockSpec(memory_space=pltpu.VMEM)),
        input_output_aliases={i: 3 + i for i in range(2 * n)},
        compiler_params=pltpu.CompilerParams(has_side_effects=DATAFLOW),
    )(*[pltpu.with_memory_space_constraint(a, pltpu.HBM) for a in list(arrays) + lands])
    return outs[:3], outs[3:3 + n], outs[3 + n:3 + 2 * n], outs[-1]


def exchange_wait(started, after, *, scatter, name):
    sems, srcs, lands, _ = started
    n = len(srcs)

    def body(*refs):
        src_refs, zones = refs[:n], refs[n:2 * n]
        send_sems, recv_sems, local_sems = refs[2 * n:2 * n + 3]
        outgoing, arriving, local = _chip_copies(src_refs, zones, send_sems, recv_sems, local_sems, scatter)
        for sent, landed in zip(outgoing, arriving):
            sent.wait_send()
            landed.wait_recv()
        for cp in local:
            cp.wait()

    hbm = lambda a: pltpu.HBM(a.shape, a.dtype)
    outs = pl.pallas_call(
        body, name=name, out_shape=[hbm(a) for a in list(srcs) + list(lands)],
        in_specs=[HBM_SPEC] * (2 * n) + [SEM_SPEC] * 3 + [pl.BlockSpec(memory_space=pl.ANY)],
        out_specs=[HBM_SPEC] * (2 * n), input_output_aliases={i: i for i in range(2 * n)},
        compiler_params=pltpu.CompilerParams(has_side_effects=DATAFLOW),
    )(*srcs, *lands, *sems, after)
    return outs[n:]


def sibling_exchange(arrays, *, name):
    n = len(arrays)

    def body(*refs):
        ins, outs = refs[:n], refs[n:2 * n]
        send_sems, recv_sems = refs[2 * n:]
        x, y, c = _place()
        copies = [pltpu.make_async_remote_copy(
            src_ref=ins[k], dst_ref=outs[k], send_sem=send_sems.at[k], recv_sem=recv_sems.at[k],
            device_id=(x, y, 1 - c), device_id_type=MESH) for k in range(n)]
        for cp in copies:
            cp.start()
        for cp in copies:
            cp.wait()

    any_spec = pl.BlockSpec(memory_space=pl.ANY)
    return pl.pallas_call(
        body, name=name, in_specs=[any_spec] * n, out_specs=[any_spec] * n,
        out_shape=[jax.ShapeDtypeStruct(a.shape, a.dtype) for a in arrays],
        scratch_shapes=[pltpu.SemaphoreType.DMA((n,)), pltpu.SemaphoreType.DMA((n,))],
    )(*arrays)


def all_exchange(a, *, name):
    def body(in_ref, out_ref, send_sems, recv_sems, local_sem):
        x, y, c = _place()
        me = 4 * x + 2 * y + c
        flips = [(fx, fy, fc) for fx in (0, 1) for fy in (0, 1) for fc in (0, 1)][1:]

        def peer_of(d):
            fx, fy, fc = flips[d]
            return (x ^ fx, y ^ fy, c ^ fc)

        def copy(d, arriving):
            px, py, pc = peer_of(d)
            slot = (4 * px + 2 * py + pc) if arriving else me
            return pltpu.make_async_remote_copy(
                src_ref=in_ref, dst_ref=out_ref.at[slot], send_sem=send_sems.at[d], recv_sem=recv_sems.at[d],
                device_id=(px, py, pc), device_id_type=MESH)

        local = pltpu.make_async_copy(in_ref, out_ref.at[me], local_sem)
        local.start()
        for d in range(N_DEVICES - 1):
            copy(d, False).start()
        for d in range(N_DEVICES - 1):
            copy(d, True).wait()
        local.wait()

    any_spec = pl.BlockSpec(memory_space=pl.ANY)
    return pl.pallas_call(
        body, name=name, in_specs=[any_spec], out_specs=any_spec,
        out_shape=jax.ShapeDtypeStruct((N_DEVICES,) + a.shape, a.dtype),
        scratch_shapes=[pltpu.SemaphoreType.DMA((N_DEVICES - 1,)), pltpu.SemaphoreType.DMA((N_DEVICES - 1,)),
                        pltpu.SemaphoreType.DMA(())],
    )(a)


BIG = ("ffn1_w_gate", "ffn1_w_up", "ffn1_w_down", "w_in", "w_attn_out", "w_lru_out", "w_mem_kv", "w_mem_out",
       "w_out", "ffn2_w_gate", "ffn2_w_up", "ffn2_w_down")
TRANSPOSED = ("ffn1_w_gate", "ffn1_w_up", "w_in", "w_attn_out", "w_mem_out", "ffn2_w_gate", "ffn2_w_up")
SMALL = ("ffn1_norm", "mix_norm", "gate_bias", "attn_sinks", "conv_w", "conv_b", "lru_wa", "lru_ba", "lru_wx", "lru_bx",
         "lru_lambda", "mem_norm", "ffn2_norm", "final_norm")
WEIGHTS = ("ffn1_norm", "ffn1_w_gate", "ffn1_w_up", "ffn1_w_down", "mix_norm", "w_in", "gate_bias", "attn_sinks",
           "w_attn_out", "conv_w", "conv_b", "lru_wa", "lru_ba", "lru_wx", "lru_bx", "lru_lambda", "w_lru_out", "mem_norm",
           "w_mem_kv", "w_mem_out", "w_out", "ffn2_norm", "ffn2_w_gate", "ffn2_w_up", "ffn2_w_down", "final_norm")
SEGMENTS = (("qkv", 0, 768), ("xr", 768, 1792), ("yr", 1792, 2816), ("cq", 2816, 3328), ("gates", 3328, 6400))
PACK_ROW = 1024


STAGES = ("ffn1", "mix", "ffn2")
STAGE_BIG = {"ffn1": ("ffn1_w_gate", "ffn1_w_up", "ffn1_w_down"),
             "mix": ("w_in", "w_attn_out", "w_lru_out", "w_mem_kv", "w_mem_out", "w_out"),
             "ffn2": ("ffn2_w_gate", "ffn2_w_up", "ffn2_w_down")}


def _row_sharded(state, name):
    return jnp.swapaxes(state[name], 1, 2) if name in TRANSPOSED else state[name]


def _join_shards(gathered, name):
    if name == "conv_w":
        return jnp.concatenate([gathered[s] for s in range(N_CHIPS)], axis=-1)
    return gathered.reshape(-1, gathered.shape[-1])


def _split_shards(full):
    return full.reshape(N_CHIPS, -1, full.shape[-1])


def _lane_tiles(w):
    z = jnp.zeros((LRU_WIDTH // LRU_TILE, HEAD_DIM, HEAD_DIM), w.dtype)
    top = jnp.concatenate([w[0::2], z], axis=2)
    bottom = jnp.concatenate([z, w[1::2]], axis=2)
    return jnp.concatenate([top, bottom], axis=1)


def _from_lane_tiles(t):
    pairs = jnp.stack([t[:, :HEAD_DIM, :HEAD_DIM], t[:, HEAD_DIM:, HEAD_DIM:]], axis=1)
    return pairs.reshape(2 * t.shape[0], HEAD_DIM, HEAD_DIM)


def _pack(arrays):
    flat = jnp.concatenate([a.reshape(-1).astype(F32) for a in arrays])
    pad = (-flat.shape[0]) % (SUBLANES * PACK_ROW)
    return jnp.pad(flat, (0, pad)).reshape(-1, PACK_ROW)


def _unpack(packed, shapes):
    flat = packed.reshape(-1)
    out, at = [], 0
    for shape in shapes:
        size = int(np.prod(shape))
        out.append(flat[at:at + size].reshape(shape))
        at += size
    return out


def _ffn_forward(x, norm, w_gate_t, w_up_t, w_down):
    h = rms_fwd(x, norm)
    G, U, A = ffn_up(h, w_gate_t, w_up_t)
    y = mm([A], [w_down], nt=False, out_dtype=F32, res=x, scale=0.5, name="ffn_down")
    return y, (x, h, G, U, A)


def _ffn_backward(half, dy, saved, norm, w_gate_t, w_up_t, w_down):
    x, h, G, U, A = saved
    dG, dU = ffn_bwd_act(half, dy, w_down, G, U)
    d_down = mm_tn(A, dy, out_dtype=MXU_DTYPE, b_scale=0.5, name="ffn_dw_down")
    d_gate_t = mm_tn(dG, h, out_dtype=MXU_DTYPE, name="ffn_dw_up")
    d_up_t = mm_tn(dU, h, out_dtype=MXU_DTYPE, name="ffn_dw_up")
    dh = mm([dG, dU], [w_gate_t, w_up_t], nt=False, out_dtype=F32, name="ffn_dh")
    dx, d_norm = rms_bwd(x, norm, dh, dy)
    return dx, d_norm, d_gate_t, d_up_t, d_down


def kernel(x, mem, ffn1_norm, ffn1_w_gate, ffn1_w_up, ffn1_w_down, mix_norm, w_in, gate_bias, attn_sinks, w_attn_out, conv_w, conv_b, lru_wa, lru_ba, lru_wx, lru_bx, lru_lambda, w_lru_out, mem_norm, w_mem_kv, w_mem_out, w_out, ffn2_norm, ffn2_w_gate, ffn2_w_up, ffn2_w_down, final_norm, loss_target, m_ffn1_norm, m_ffn1_w_gate, m_ffn1_w_up, m_ffn1_w_down, m_mix_norm, m_w_in, m_gate_bias, m_attn_sinks, m_w_attn_out, m_conv_w, m_conv_b, m_lru_wa, m_lru_ba, m_lru_wx, m_lru_bx, m_lru_lambda, m_w_lru_out, m_mem_norm, m_w_mem_kv, m_w_mem_out, m_w_out, m_ffn2_norm, m_ffn2_w_gate, m_ffn2_w_up, m_ffn2_w_down, m_final_norm, v_ffn1_norm, v_ffn1_w_gate, v_ffn1_w_up, v_ffn1_w_down, v_mix_norm, v_w_in, v_gate_bias, v_attn_sinks, v_w_attn_out, v_conv_w, v_conv_b, v_lru_wa, v_lru_ba, v_lru_wx, v_lru_bx, v_lru_lambda, v_w_lru_out, v_mem_norm, v_w_mem_kv, v_w_mem_out, v_w_out, v_ffn2_norm, v_ffn2_w_gate, v_ffn2_w_up, v_ffn2_w_down, v_final_norm):
    args = dict(locals())
    W = {n: args[n] for n in WEIGHTS}
    M = {n: args["m_" + n] for n in WEIGHTS}
    V = {n: args["v_" + n] for n in WEIGHTS}
    chip = 2 * lax.axis_index("x") + lax.axis_index("y")
    x0 = x[0]
    mem0 = mem[0]
    target = loss_target[0]

    def row(v):
        return v.reshape(1, -1)

    def stage_names(stage):
        return STAGE_BIG[stage] + (("conv_w",) if stage == "mix" else ())

    RW, RM, RV = ({n: _row_sharded(state, n) for n in BIG} for state in (W, M, V))
    gathers = {}
    started_all = jnp.zeros((), F32)
    for l in range(DEPTH):
        for stage in STAGES:
            shards = [W[n][l] if n == "conv_w" else RW[n][l].astype(MXU_DTYPE) for n in stage_names(stage)]
            gathers[l, stage] = exchange_start(shards, scatter=False, name=f"gather_start_{stage}_{l}")
            started_all = started_all + gathers[l, stage][3][0, 0]

    def weights_of(l, stage, after):
        lands = exchange_wait(gathers[l, stage], after, scatter=False, name=f"gather_wait_{stage}_{l}")
        return {n: _join_shards(g, n) for n, g in zip(stage_names(stage), lands)}

    saved = []
    full = []
    xs = x0
    for l in range(DEPTH):
        P = weights_of(l, "ffn1", xs)
        first_norm = row(W["ffn1_norm"][l]) + started_all if l == 0 else row(W["ffn1_norm"][l])
        xs, ffn1 = _ffn_forward(xs, first_norm, P["ffn1_w_gate"], P["ffn1_w_up"], P["ffn1_w_down"])
        x_mix = xs
        P.update(weights_of(l, "mix", xs))
        h = rms_fwd(x_mix, row(W["mix_norm"][l]))
        seg = {name: mm([h], [P["w_in"][lo:hi]], nt=True, out_dtype=F32, tm=1024, name="proj_" + name)
               for name, lo, hi in SEGMENTS}
        attn = attn_fwd(seg["qkv"], W["attn_sinks"][l])
        wa_t = _lane_tiles(W["lru_wa"][l]).astype(MXU_DTYPE)
        wx_t = _lane_tiles(W["lru_wx"][l]).astype(MXU_DTYPE)
        sp = row(jax.nn.softplus(-W["lru_lambda"][l]))
        lru_consts = (P["conv_w"], row(W["conv_b"][l]), wa_t, row(W["lru_ba"][l]), wx_t, row(W["lru_bx"][l]), sp)
        h_lru, rec = lru_fwd(seg["xr"], seg["yr"], *lru_consts)
        mem_n = rms_fwd(mem0, row(W["mem_norm"][l]))
        kv = mm([mem_n], [P["w_mem_kv"]], nt=False, out_dtype=F32, name="mem_kv")
        cross = mem_attn_fwd(seg["cq"], kv)
        branches = (mm([attn], [P["w_attn_out"]], nt=True, out_dtype=F32, tm=1024, name="branch_attn"),
                    mm([rec], [P["w_lru_out"]], nt=False, out_dtype=F32, tm=1024, name="branch_lru"),
                    mm([cross], [P["w_mem_out"]], nt=True, out_dtype=F32, tm=1024, name="branch_mem"))
        bias = row(W["gate_bias"][l])
        merged = merge_fwd(seg["gates"], bias, branches)
        xs = mm([merged], [P["w_out"]], nt=False, out_dtype=F32, res=x_mix, tm=1024, name="mix_out")
        mix = (x_mix, h, seg, attn, lru_consts, h_lru, rec, mem_n, kv, cross, branches, bias, merged)
        P.update(weights_of(l, "ffn2", xs))
        xs, ffn2 = _ffn_forward(xs, row(W["ffn2_norm"][l]), P["ffn2_w_gate"], P["ffn2_w_up"], P["ffn2_w_down"])
        saved.append((ffn1, mix, ffn2))
        full.append(P)

    loss_lanes, dx, d_final = loss_head(xs, row(W["final_norm"]), target)
    loss = lax.psum(loss_lanes[0, 0], ("x", "y", "c"))

    big_grads = {}
    small_grads = {n: [None] * DEPTH for n in SMALL if n != "final_norm"}
    scatters = {}
    token = jnp.zeros((), F32)

    def send_grads(l, stage):
        chunks = [_split_shards(big_grads[n, l]) for n in STAGE_BIG[stage]]
        scatters[l, stage] = exchange_start(chunks, scatter=True, name=f"scatter_start_{stage}_{l}")
        return scatters[l, stage][3][0, 0]

    for l in reversed(range(DEPTH)):
        P = full[l]
        ffn1, mix, ffn2 = saved[l]
        dx, g_norm, g_gate, g_up, g_down = _ffn_backward(
            (0.5 + token).reshape(1), dx, ffn2, row(W["ffn2_norm"][l]), P["ffn2_w_gate"], P["ffn2_w_up"], P["ffn2_w_down"])
        small_grads["ffn2_norm"][l] = g_norm
        big_grads["ffn2_w_gate", l], big_grads["ffn2_w_up", l], big_grads["ffn2_w_down", l] = g_gate, g_up, g_down
        token = send_grads(l, "ffn2")

        x_mix, h, seg, attn, lru_consts, h_lru, rec, mem_n, kv, cross, branches, bias, merged = mix
        bias = bias + token
        d_merged = mm([dx], [P["w_out"]], nt=True, out_dtype=F32, tm=1024, name="d_merged")
        big_grads["w_out", l] = mm_tn(merged, dx, out_dtype=MXU_DTYPE, name="dw_out")
        d_attn_br, d_lru_br, d_mem_br, d_gates, d_bias = merge_bwd(d_merged, seg["gates"], bias, branches)
        small_grads["gate_bias"][l] = d_bias
        d_attn = mm([d_attn_br], [P["w_attn_out"]], nt=False, out_dtype=F32, tm=1024, name="d_attn")
        d_rec = mm([d_lru_br], [P["w_lru_out"]], nt=True, out_dtype=F32, tm=1024, name="d_rec")
        d_cross = mm([d_mem_br], [P["w_mem_out"]], nt=False, out_dtype=F32, tm=1024, name="d_cross")
        big_grads["w_attn_out", l] = mm_tn(d_attn_br, attn, out_dtype=MXU_DTYPE, name="dw_attn_out")
        big_grads["w_lru_out", l] = mm_tn(rec, d_lru_br, out_dtype=MXU_DTYPE, name="dw_lru_out")
        big_grads["w_mem_out", l] = mm_tn(d_mem_br, cross, out_dtype=MXU_DTYPE, name="dw_mem_out")

        d_qkv, d_sinks = attn_bwd(seg["qkv"], W["attn_sinks"][l], d_attn)
        small_grads["attn_sinks"][l] = d_sinks

        d_xc, d_yr, d_wa_t, d_wx_t, d_ba, d_bx, d_sp, d_cb = lru_bwd(seg["xr"], seg["yr"], h_lru, d_rec, *lru_consts)
        d_xr, d_cw = conv_bwd(seg["xr"], d_xc, lru_consts[0])
        small_grads["conv_w"][l] = d_cw
        small_grads["conv_b"][l] = d_cb
        small_grads["lru_wa"][l] = _from_lane_tiles(d_wa_t)
        small_grads["lru_wx"][l] = _from_lane_tiles(d_wx_t)
        small_grads["lru_ba"][l] = d_ba
        small_grads["lru_bx"][l] = d_bx
        small_grads["lru_lambda"][l] = -d_sp * _sigmoid(-row(W["lru_lambda"][l]))

        d_cq, d_kv = mem_attn_bwd(seg["cq"], kv, d_cross)
        big_grads["w_mem_kv", l] = mm_tn(mem_n, d_kv, out_dtype=MXU_DTYPE, name="dw_mem_kv")
        d_mem_n = mm([d_kv], [P["w_mem_kv"]], nt=True, out_dtype=F32, name="d_mem_n")
        _, g_mem_norm = rms_bwd(mem0, row(W["mem_norm"][l]), d_mem_n, jnp.zeros_like(mem0))
        small_grads["mem_norm"][l] = g_mem_norm

        d_seg = {"qkv": d_qkv, "xr": d_xr, "yr": d_yr, "cq": d_cq, "gates": d_gates}
        big_grads["w_in", l] = jnp.concatenate(
            [mm_tn(d_seg[name], h, out_dtype=MXU_DTYPE, name="dw_in_" + name) for name, _, _ in SEGMENTS], axis=0)
        dh = mm([d_seg[name] for name, _, _ in SEGMENTS], [P["w_in"][lo:hi] for _, lo, hi in SEGMENTS],
                nt=False, out_dtype=F32, tn=512, name="mix_dh")
        dx, g_norm = rms_bwd(x_mix, row(W["mix_norm"][l]), dh, dx)
        small_grads["mix_norm"][l] = g_norm
        token = send_grads(l, "mix")

        dx, g_norm, g_gate, g_up, g_down = _ffn_backward(
            (0.5 + token).reshape(1), dx, ffn1, row(W["ffn1_norm"][l]), P["ffn1_w_gate"], P["ffn1_w_up"], P["ffn1_w_down"])
        small_grads["ffn1_norm"][l] = g_norm
        big_grads["ffn1_w_gate", l], big_grads["ffn1_w_up", l], big_grads["ffn1_w_down", l] = g_gate, g_up, g_down
        token = send_grads(l, "ffn1")

    grad_x = dx[None]

    arrived = {}
    backward_done = dx[:1, :1] + token
    for l in range(DEPTH):
        for stage in STAGES:
            lands = exchange_wait(scatters[l, stage], backward_done, scatter=True, name=f"scatter_wait_{stage}_{l}")
            for n, a in zip(STAGE_BIG[stage], lands):
                arrived[n, l] = a.reshape(N_CHIPS, -1, a.shape[-1])
    partial = [sum_parts([arrived[n, l] for l in range(DEPTH)]) for n in BIG]
    other = sibling_exchange(partial, name="sibling_grads")
    results = {}
    for n, mine, theirs in zip(BIG, partial, other):
        outs = adamw_sum(_as_rows(RW[n]), [mine, theirs], _as_rows(RM[n]), _as_rows(RV[n]))
        outs = [o.reshape(RW[n].shape) for o in outs]
        results[n] = [jnp.swapaxes(o, 1, 2) for o in outs] if n in TRANSPOSED else outs

    small_list = [jnp.stack(small_grads[n]) for n in SMALL if n != "final_norm"] + [d_final]
    small_shapes = [(DEPTH,) + W[n].shape[1:] if n != "conv_w" else (DEPTH, CONV_WIDTH, LRU_WIDTH)
                    for n in SMALL if n != "final_norm"] + [W["final_norm"].shape]
    everyone = all_exchange(_pack(small_list), name="exchange_small")

    def own_columns(full_conv):
        return lax.dynamic_slice_in_dim(full_conv, chip * (LRU_WIDTH // N_CHIPS), LRU_WIDTH // N_CHIPS, axis=2)

    def packed_state(state):
        arrays = []
        for n in SMALL:
            a = state[n]
            if n == "conv_w":
                a = lax.dynamic_update_slice_in_dim(jnp.zeros((DEPTH, CONV_WIDTH, LRU_WIDTH), F32), a,
                                                    chip * (LRU_WIDTH // N_CHIPS), axis=2)
            arrays.append(a)
        return _pack(arrays)

    small_outs = adamw_sum(packed_state(W), [everyone[s] for s in range(N_DEVICES)], packed_state(M), packed_state(V))
    for kind, packed in enumerate(small_outs):
        for n, a in zip(SMALL, _unpack(packed, small_shapes)):
            results.setdefault(n, [None] * 4)[kind] = own_columns(a) if n == "conv_w" else a.reshape(W[n].shape)

    return (loss, grad_x, *[results[n][0] for n in WEIGHTS], *[results[n][1] for n in WEIGHTS],
            *[results[n][2] for n in WEIGHTS], *[results[n][3] for n in WEIGHTS])
```

```python
import functools

import jax
import jax.numpy as jnp
import numpy as np
from jax import lax
from jax.experimental import pallas as pl
from jax.experimental.pallas import tpu as pltpu

F32 = jnp.float32
MXU_DTYPE = jnp.bfloat16

DEPTH = 4
D_MODEL = 1024
CHUNK = 64
HALO = 2 * CHUNK
N_Q_HEADS = 8
HEAD_DIM = 64
ATTN_WIDTH = 512
KV_WIDTH = 128
LRU_WIDTH = 1024
LRU_TILE = 128
CONV_WIDTH = 4
LRU_C = 8.0
MEM_HEADS = 4
MEM_HEAD_DIM = 128
MEM_WIDTH = 512
D_FF = 2816
EPS = 1e-6
ADAM_LR = 0.001
ADAM_B1 = 0.9
ADAM_B2 = 0.999
ADAM_EPS = 1e-08
ADAM_WD = 0.01
ADAM_STEP = 10

N_CHIPS = 4
N_DEVICES = 8
SUBLANES = 8
LANES = 128
VMEM_LIMIT_BYTES = 56 * 1024 * 1024
NEG_BIG = -1e30
MESH = pl.DeviceIdType.MESH

ALIBI_SLOPES = tuple(float(2.0 ** (-8.0 * (i + 1) / N_Q_HEADS)) for i in range(N_Q_HEADS))


def _params(*semantics):
    return pltpu.CompilerParams(dimension_semantics=semantics, vmem_limit_bytes=VMEM_LIMIT_BYTES)


def _tile(n, pref, unit=SUBLANES):
    if n <= pref:
        return n
    for t in range(pref - pref % unit, 0, -unit):
        if n % t == 0:
            return t
    return n


def _dot(a, b, ca, cb):
    return lax.dot_general(a, b, (((ca,), (cb,)), ((), ())), preferred_element_type=F32)


def _sigmoid(x):
    return 0.5 * jnp.tanh(0.5 * x) + 0.5


def rms_fwd(x, g):
    T, D = x.shape
    tm = _tile(T, 512)

    def body(x_ref, g_ref, h_ref):
        xv = x_ref[...]
        r = lax.rsqrt(jnp.mean(xv * xv, axis=-1, keepdims=True) + EPS)
        h_ref[...] = (xv * r * g_ref[...]).astype(h_ref.dtype)

    return pl.pallas_call(
        body, name="rms_fwd", grid=(T // tm,),
        in_specs=[pl.BlockSpec((tm, D), lambda i: (i, 0)), pl.BlockSpec((1, D), lambda i: (0, 0))],
        out_specs=pl.BlockSpec((tm, D), lambda i: (i, 0)),
        out_shape=jax.ShapeDtypeStruct((T, D), MXU_DTYPE),
        compiler_params=_params("parallel"),
    )(x, g)


def mm_rms_bwd(a_list, b_list, x, g, dy, *, nt, tm=512, name="mm_rms_bwd"):
    n_pairs = len(a_list)
    T, D = x.shape
    tm = _tile(T, tm)
    has_dy = dy is not None

    def body(*refs):
        a_refs, b_refs = refs[:n_pairs], refs[n_pairs:2 * n_pairs]
        x_ref, g_ref = refs[2 * n_pairs:2 * n_pairs + 2]
        dx_ref, dg_ref = refs[-2:]
        i = pl.program_id(0)
        dh = None
        for a_ref, b_ref in zip(a_refs, b_refs):
            d = _dot(a_ref[...].astype(MXU_DTYPE), b_ref[...], 1, 1 if nt else 0)
            dh = d if dh is None else dh + d
        xv = x_ref[...]
        r = lax.rsqrt(jnp.mean(xv * xv, axis=-1, keepdims=True) + EPS)
        xh = xv * r
        gd = dh * g_ref[...]
        dx = r * (gd - xh * jnp.mean(gd * xh, axis=-1, keepdims=True))
        dx_ref[...] = refs[2 * n_pairs + 2][...] + dx if has_dy else dx
        part = jnp.sum(dh * xh, axis=0, keepdims=True)

        @pl.when(i == 0)
        def _():
            dg_ref[...] = part

        @pl.when(i > 0)
        def _():
            dg_ref[...] += part

    row = pl.BlockSpec((tm, D), lambda i: (i, 0))
    vec = pl.BlockSpec((1, D), lambda i: (0, 0))
    in_specs = [pl.BlockSpec((tm, a.shape[1]), lambda i: (i, 0)) for a in a_list]
    in_specs += [pl.BlockSpec(b.shape, lambda i: (0, 0), pipeline_mode=pl.Buffered(1)) for b in b_list]
    in_specs += [row, vec] + ([row] if has_dy else [])
    operands = list(a_list) + list(b_list) + [x, g] + ([dy] if has_dy else [])
    return pl.pallas_call(
        body, name=name, grid=(T // tm,), in_specs=in_specs, out_specs=[row, vec],
        out_shape=[jax.ShapeDtypeStruct((T, D), F32), jax.ShapeDtypeStruct((1, D), F32)],
        compiler_params=_params("arbitrary"),
    )(*operands)


def mm(a_list, b_list, *, nt, out_dtype, res=None, scale=1.0, tm=512, tn=1024, name="mm"):
    n_pairs = len(a_list)
    T = a_list[0].shape[0]
    N = b_list[0].shape[0] if nt else b_list[0].shape[1]
    tm = _tile(T, tm)
    tn = _tile(N, tn, LANES)
    has_res = res is not None

    def body(*refs):
        a_refs = refs[:n_pairs]
        b_refs = refs[n_pairs:2 * n_pairs]
        o_ref = refs[-1]
        acc = None
        for a_ref, b_ref in zip(a_refs, b_refs):
            d = _dot(a_ref[...].astype(MXU_DTYPE), b_ref[...], 1, 1 if nt else 0)
            acc = d if acc is None else acc + d
        if has_res:
            acc = refs[2 * n_pairs][...] + scale * acc
        elif scale != 1.0:
            acc = scale * acc
        o_ref[...] = acc.astype(o_ref.dtype)

    in_specs = [pl.BlockSpec((tm, a.shape[1]), lambda j, i: (i, 0)) for a in a_list]
    if nt:
        in_specs += [pl.BlockSpec((tn, b.shape[1]), lambda j, i: (j, 0)) for b in b_list]
    else:
        in_specs += [pl.BlockSpec((b.shape[0], tn), lambda j, i: (0, j)) for b in b_list]
    out_spec = pl.BlockSpec((tm, tn), lambda j, i: (i, j))
    operands = list(a_list) + list(b_list)
    if has_res:
        in_specs.append(out_spec)
        operands.append(res)
    return pl.pallas_call(
        body, name=name, grid=(N // tn, T // tm), in_specs=in_specs, out_specs=out_spec,
        out_shape=jax.ShapeDtypeStruct((T, N), out_dtype),
        compiler_params=_params("parallel", "parallel"),
    )(*operands)


def mm_tn(a, b, *, out_dtype, b_scale=1.0, tk=1408, tn=1408, tt=None, name="mm_tn"):
    T, K = a.shape
    N = b.shape[1]
    tk = _tile(K, tk, LANES)
    tn = _tile(N, tn, LANES)
    if tt is None:
        tt = 1024 if b.dtype == F32 else 2048
    tt = _tile(T, tt)
    steps = T // tt

    def body(a_ref, b_ref, o_ref, acc_ref):
        t = pl.program_id(2)
        bv = b_ref[...]
        if b_scale != 1.0:
            bv = b_scale * bv
        d = _dot(a_ref[...].astype(MXU_DTYPE), bv.astype(MXU_DTYPE), 0, 0)

        @pl.when(t == 0)
        def _():
            acc_ref[...] = d

        @pl.when(t > 0)
        def _():
            acc_ref[...] += d

        @pl.when(t == steps - 1)
        def _():
            o_ref[...] = acc_ref[...].astype(o_ref.dtype)

    return pl.pallas_call(
        body, name=name, grid=(K // tk, N // tn, steps),
        in_specs=[pl.BlockSpec((tt, tk), lambda p, q, t: (t, p)), pl.BlockSpec((tt, tn), lambda p, q, t: (t, q))],
        out_specs=pl.BlockSpec((tk, tn), lambda p, q, t: (p, q)),
        out_shape=jax.ShapeDtypeStruct((K, N), out_dtype),
        scratch_shapes=[pltpu.VMEM((tk, tn), F32)],
        compiler_params=_params("parallel", "parallel", "arbitrary"),
    )(a, b)


def ffn_up(h, w_gate_t, w_up_t):
    T, D = h.shape
    F = w_gate_t.shape[0]
    tm = _tile(T, 512)
    tn = _tile(F, 1408, LANES)

    def body(h_ref, wg_ref, wu_ref, g_ref, u_ref, a_ref):
        hv = h_ref[...]
        G = _dot(hv, wg_ref[...], 1, 1)
        U = _dot(hv, wu_ref[...], 1, 1)
        g_ref[...] = G.astype(g_ref.dtype)
        u_ref[...] = U.astype(u_ref.dtype)
        a_ref[...] = (G * _sigmoid(G) * U).astype(a_ref.dtype)

    w_spec = pl.BlockSpec((tn, D), lambda j, i: (j, 0))
    o_spec = pl.BlockSpec((tm, tn), lambda j, i: (i, j))
    return pl.pallas_call(
        body, name="ffn_up", grid=(F // tn, T // tm),
        in_specs=[pl.BlockSpec((tm, D), lambda j, i: (i, 0)), w_spec, w_spec],
        out_specs=[o_spec, o_spec, o_spec],
        out_shape=[jax.ShapeDtypeStruct((T, F), MXU_DTYPE)] * 3,
        compiler_params=_params("parallel", "parallel"),
    )(h, w_gate_t, w_up_t)


def ffn_bwd_act(half, dy, w_down, G, U):
    T, D = dy.shape
    F = w_down.shape[0]
    tm = _tile(T, 512)
    tn = _tile(F, 1408, LANES)

    def body(half_ref, dy_ref, wd_ref, g_ref, u_ref, dg_ref, du_ref):
        d_out = (half_ref[0] * dy_ref[...]).astype(MXU_DTYPE)
        dA = _dot(d_out, wd_ref[...], 1, 1)
        Gv = g_ref[...].astype(F32)
        s = _sigmoid(Gv)
        du_ref[...] = (dA * (Gv * s)).astype(du_ref.dtype)
        dg_ref[...] = (dA * u_ref[...].astype(F32) * (s * (1.0 + Gv * (1.0 - s)))).astype(dg_ref.dtype)

    o_spec = pl.BlockSpec((tm, tn), lambda j, i: (i, j))
    return pl.pallas_call(
        body, name="ffn_bwd_act", grid=(F // tn, T // tm),
        in_specs=[pl.BlockSpec(memory_space=pltpu.SMEM), pl.BlockSpec((tm, D), lambda j, i: (i, 0)),
                  pl.BlockSpec((tn, D), lambda j, i: (j, 0)), o_spec, o_spec],
        out_specs=[o_spec, o_spec],
        out_shape=[jax.ShapeDtypeStruct((T, F), MXU_DTYPE), jax.ShapeDtypeStruct((T, F), MXU_DTYPE)],
        compiler_params=_params("parallel", "parallel"),
    )(half, dy, w_down, G, U)


ATTN_ROWS = 256


def _head_variants(kv128):
    lane = lax.broadcasted_iota(jnp.int32, kv128.shape, 1)
    low = lane < HEAD_DIM
    rolled = pltpu.roll(kv128, HEAD_DIM, 1)
    out = {}
    for j in (0, 1):
        for pos in (0, 1):
            src = kv128 if j == pos else rolled
            out[j, pos] = jnp.where(low if pos == 0 else jnp.logical_not(low), src, 0.0).astype(MXU_DTYPE)
    return out


def _fold_variant(d128, j, pos):
    lane = lax.broadcasted_iota(jnp.int32, d128.shape, 1)
    low = lane < HEAD_DIM
    kept = jnp.where(low if pos == 0 else jnp.logical_not(low), d128, 0.0)
    return kept if j == pos else pltpu.roll(kept, HEAD_DIM, 1)


def _attn_mask(block, rows, keys):
    r = lax.broadcasted_iota(jnp.int32, (rows, keys), 0)
    c = lax.broadcasted_iota(jnp.int32, (rows, keys), 1)
    q_chunk = r // CHUNK
    k_chunk = c // CHUNK - HALO // CHUNK
    first_chunk = jnp.where(block > 0, -(HALO // CHUNK), 0)
    valid = (k_chunk <= q_chunk) & (k_chunk >= q_chunk - HALO // CHUNK) & (k_chunk >= first_chunk)
    dist = jnp.abs(r + HALO - c).astype(F32)
    return valid, dist


def _attn_probs(q128, k_var, head, sink, valid, dist):
    s = _dot(q128, k_var, 1, 1) * (HEAD_DIM ** -0.5) - ALIBI_SLOPES[head] * dist
    s = jnp.where(valid, s, NEG_BIG)
    mx = jnp.maximum(jnp.max(s, axis=-1, keepdims=True), sink)
    p = jnp.exp(s - mx)
    p_sink = jnp.exp(sink - mx)
    inv = 1.0 / (jnp.sum(p, axis=-1, keepdims=True) + p_sink)
    return p * inv, p_sink * inv


def _attn_specs(T, tq, order):
    ratio = tq // HALO
    q_spec = pl.BlockSpec((tq, ATTN_WIDTH), lambda i: (order(i), 0))
    cur = lambda col: pl.BlockSpec((tq, KV_WIDTH), lambda i: (order(i), col))
    prev = lambda col: pl.BlockSpec((HALO, KV_WIDTH), lambda i: (jnp.maximum(order(i) * ratio - 1, 0), col))
    k_col, v_col = ATTN_WIDTH // KV_WIDTH, ATTN_WIDTH // KV_WIDTH + 1
    return [pl.BlockSpec(memory_space=pltpu.SMEM), q_spec, prev(k_col), cur(k_col), prev(v_col), cur(v_col)]


def attn_fwd(qkv, sinks):
    T = qkv.shape[0]
    tq = _tile(T, ATTN_ROWS)
    keys = tq + HALO

    def body(sink_ref, q_ref, kp_ref, kc_ref, vp_ref, vc_ref, o_ref):
        block = pl.program_id(0)
        k_var = _head_variants(jnp.concatenate([kp_ref[...], kc_ref[...]], axis=0))
        v_var = _head_variants(jnp.concatenate([vp_ref[...], vc_ref[...]], axis=0))
        valid, dist = _attn_mask(block, tq, keys)
        for pair in range(N_Q_HEADS // 2):
            q128 = q_ref[:, pair * LANES:(pair + 1) * LANES].astype(MXU_DTYPE)
            acc = None
            for pos in (0, 1):
                head = 2 * pair + pos
                j = head // 4
                p, _ = _attn_probs(q128, k_var[j, pos], head, sink_ref[head], valid, dist)
                d = _dot(p.astype(MXU_DTYPE), v_var[j, pos], 1, 0)
                acc = d if acc is None else acc + d
            o_ref[:, pair * LANES:(pair + 1) * LANES] = acc.astype(o_ref.dtype)

    return pl.pallas_call(
        body, name="attn_fwd", grid=(T // tq,),
        in_specs=_attn_specs(T, tq, lambda i: i),
        out_specs=pl.BlockSpec((tq, ATTN_WIDTH), lambda i: (i, 0)),
        out_shape=jax.ShapeDtypeStruct((T, ATTN_WIDTH), MXU_DTYPE),
        compiler_params=_params("parallel"),
    )(sinks, qkv, qkv, qkv, qkv, qkv)


def attn_bwd(qkv, sinks, d_out):
    T = qkv.shape[0]
    tq = _tile(T, ATTN_ROWS)
    keys = tq + HALO
    n_blocks = T // tq
    order = lambda i: n_blocks - 1 - i

    def body(sink_ref, q_ref, kp_ref, kc_ref, vp_ref, vc_ref, do_ref, dqkv_ref, dsink_ref, carry_ref):
        step = pl.program_id(0)
        block = order(step)
        k_var = _head_variants(jnp.concatenate([kp_ref[...], kc_ref[...]], axis=0))
        v_var = _head_variants(jnp.concatenate([vp_ref[...], vc_ref[...]], axis=0))
        valid, dist = _attn_mask(block, tq, keys)

        @pl.when(step == 0)
        def _():
            carry_ref[...] = jnp.zeros_like(carry_ref)
            dsink_ref[...] = jnp.zeros_like(dsink_ref)

        dk = jnp.zeros((keys, KV_WIDTH), F32)
        dv = jnp.zeros((keys, KV_WIDTH), F32)
        for pair in range(N_Q_HEADS // 2):
            q128 = q_ref[:, pair * LANES:(pair + 1) * LANES].astype(MXU_DTYPE)
            do128 = do_ref[:, pair * LANES:(pair + 1) * LANES].astype(MXU_DTYPE)
            dq128 = None
            for pos in (0, 1):
                head = 2 * pair + pos
                j = head // 4
                p, p_sink = _attn_probs(q128, k_var[j, pos], head, sink_ref[head], valid, dist)
                dp = _dot(do128, v_var[j, pos], 1, 1)
                delta = jnp.sum(p * dp, axis=-1, keepdims=True)
                ds = (p * (dp - delta) * (HEAD_DIM ** -0.5)).astype(MXU_DTYPE)
                d = _dot(ds, k_var[j, pos], 1, 0)
                dq128 = d if dq128 is None else dq128 + d
                dk = dk + _fold_variant(_dot(ds, q128, 0, 0), j, pos)
                dv = dv + _fold_variant(_dot(p.astype(MXU_DTYPE), do128, 0, 0), j, pos)
                dsink_ref[pl.ds(head, 1), :] += jnp.broadcast_to(
                    -jnp.sum(p_sink * delta, axis=0, keepdims=True), (1, LANES))
            dqkv_ref[:, pair * LANES:(pair + 1) * LANES] = dq128.astype(dqkv_ref.dtype)
        carried = jnp.concatenate([jnp.zeros((tq - HALO, 2 * KV_WIDTH), F32), carry_ref[...]], axis=0)
        dkv = jnp.concatenate([dk, dv], axis=1)
        dqkv_ref[:, ATTN_WIDTH:] = (dkv[HALO:] + carried).astype(dqkv_ref.dtype)
        carry_ref[...] = dkv[:HALO]

    out_rows = pl.BlockSpec((tq, ATTN_WIDTH + 2 * KV_WIDTH), lambda i: (order(i), 0))
    dqkv, dsink = pl.pallas_call(
        body, name="attn_bwd", grid=(n_blocks,),
        in_specs=_attn_specs(T, tq, order) + [pl.BlockSpec((tq, ATTN_WIDTH), lambda i: (order(i), 0))],
        out_specs=[out_rows, pl.BlockSpec((N_Q_HEADS, LANES), lambda i: (0, 0))],
        out_shape=[jax.ShapeDtypeStruct((T, ATTN_WIDTH + 2 * KV_WIDTH), MXU_DTYPE),
                   jax.ShapeDtypeStruct((N_Q_HEADS, LANES), F32)],
        scratch_shapes=[pltpu.VMEM((HALO, 2 * KV_WIDTH), F32)],
        compiler_params=_params("arbitrary"),
    )(sinks, qkv, qkv, qkv, qkv, qkv, d_out)
    return dqkv, dsink[:, 0]


def _mem_probs(q, k):
    s = _dot(q, k, 1, 1) * (MEM_HEAD_DIM ** -0.5)
    p = jnp.exp(s - jnp.max(s, axis=-1, keepdims=True))
    return p / jnp.sum(p, axis=-1, keepdims=True)


def mem_attn_fwd(cq, kv):
    T = cq.shape[0]
    M = kv.shape[0]
    tq = _tile(T, 512)

    def body(q_ref, kv_ref, o_ref):
        for h in range(MEM_HEADS):
            lo, hi = h * MEM_HEAD_DIM, (h + 1) * MEM_HEAD_DIM
            p = _mem_probs(q_ref[:, lo:hi].astype(MXU_DTYPE), kv_ref[:, lo:hi].astype(MXU_DTYPE))
            v = kv_ref[:, MEM_WIDTH + lo:MEM_WIDTH + hi].astype(MXU_DTYPE)
            o_ref[:, lo:hi] = _dot(p.astype(MXU_DTYPE), v, 1, 0).astype(o_ref.dtype)

    return pl.pallas_call(
        body, name="mem_attn_fwd", grid=(T // tq,),
        in_specs=[pl.BlockSpec((tq, MEM_WIDTH), lambda i: (i, 0)), pl.BlockSpec((M, 2 * MEM_WIDTH), lambda i: (0, 0))],
        out_specs=pl.BlockSpec((tq, MEM_WIDTH), lambda i: (i, 0)),
        out_shape=jax.ShapeDtypeStruct((T, MEM_WIDTH), MXU_DTYPE),
        compiler_params=_params("parallel"),
    )(cq, kv)


def mem_attn_bwd(cq, kv, d_out):
    T = cq.shape[0]
    M = kv.shape[0]
    tq = _tile(T, 512)

    def body(q_ref, kv_ref, do_ref, dq_ref, dkv_ref):
        @pl.when(pl.program_id(0) == 0)
        def _():
            dkv_ref[...] = jnp.zeros_like(dkv_ref)

        for h in range(MEM_HEADS):
            lo, hi = h * MEM_HEAD_DIM, (h + 1) * MEM_HEAD_DIM
            q = q_ref[:, lo:hi].astype(MXU_DTYPE)
            k = kv_ref[:, lo:hi].astype(MXU_DTYPE)
            v = kv_ref[:, MEM_WIDTH + lo:MEM_WIDTH + hi].astype(MXU_DTYPE)
            do = do_ref[:, lo:hi].astype(MXU_DTYPE)
            p = _mem_probs(q, k)
            dp = _dot(do, v, 1, 1)
            ds = (p * (dp - jnp.sum(p * dp, axis=-1, keepdims=True)) * (MEM_HEAD_DIM ** -0.5)).astype(MXU_DTYPE)
            dq_ref[:, lo:hi] = _dot(ds, k, 1, 0).astype(dq_ref.dtype)
            dkv_ref[:, lo:hi] += _dot(ds, q, 0, 0)
            dkv_ref[:, MEM_WIDTH + lo:MEM_WIDTH + hi] += _dot(p.astype(MXU_DTYPE), do, 0, 0)

    rows = pl.BlockSpec((tq, MEM_WIDTH), lambda i: (i, 0))
    whole = pl.BlockSpec((M, 2 * MEM_WIDTH), lambda i: (0, 0))
    return pl.pallas_call(
        body, name="mem_attn_bwd", grid=(T // tq,),
        in_specs=[rows, whole, rows], out_specs=[rows, whole],
        out_shape=[jax.ShapeDtypeStruct((T, MEM_WIDTH), MXU_DTYPE), jax.ShapeDtypeStruct((M, 2 * MEM_WIDTH), F32)],
        compiler_params=_params("arbitrary"),
    )(cq, kv, d_out)


LRU_ROWS = 256


def _shift_down(ext, k, rows):
    return pltpu.roll(ext, k, 0)[SUBLANES:SUBLANES + rows] if k else ext[SUBLANES:SUBLANES + rows]


def _shift_up(ext, k, rows):
    return pltpu.roll(ext, rows + SUBLANES - k, 0)[:rows] if k else ext[:rows]


def _conv(x_ext, conv_w_ref, conv_b_ref, rows):
    xc = conv_b_ref[...] + conv_w_ref[CONV_WIDTH - 1:CONV_WIDTH, :] * _shift_down(x_ext, 0, rows)
    for j in range(CONV_WIDTH - 1):
        xc = xc + conv_w_ref[j:j + 1, :] * _shift_down(x_ext, CONV_WIDTH - 1 - j, rows)
    return xc


def _block_matmul(x, w_ref, cb):
    return jnp.concatenate(
        [_dot(x[:, c * LANES:(c + 1) * LANES].astype(MXU_DTYPE), w_ref[c], 1, cb)
         for c in range(LRU_WIDTH // LRU_TILE)], axis=1)


def _lru_gates(block, xc, wa_ref, ba_ref, wx_ref, bx_ref, sp_ref):
    rows = xc.shape[0]
    ra = _sigmoid(_block_matmul(xc, wa_ref, 0) + ba_ref[...])
    gi = _sigmoid(_block_matmul(xc, wx_ref, 0) + bx_ref[...])
    log_a = -LRU_C * ra * sp_ref[...]
    a = jnp.exp(log_a)
    one_minus = 1.0 - jnp.exp(2.0 * log_a)
    mult = jnp.sqrt(jnp.maximum(one_minus, 0.0))
    row = lax.broadcasted_iota(jnp.int32, (rows, 1), 0)
    first = row == jnp.where(block == 0, 0, -1)
    mult = jnp.where(first, 1.0, mult)
    return ra, gi, a, one_minus, mult, first


def _gelu(x):
    inner = np.sqrt(2.0 / np.pi).astype(np.float32) * (x + 0.044715 * (x * x * x))
    return 0.5 * x * (1.0 + jnp.tanh(inner))


def _gelu_grad(x):
    c = np.sqrt(2.0 / np.pi).astype(np.float32)
    t = jnp.tanh(c * (x + 0.044715 * (x * x * x)))
    return 0.5 * (1.0 + t) + 0.5 * x * (1.0 - t * t) * c * (1.0 + 3.0 * 0.044715 * (x * x))


def _x_ext(block, prev_ref, cur_ref):
    prev = jnp.where(block > 0, prev_ref[...], 0.0)
    return jnp.concatenate([prev, cur_ref[...]], axis=0)


def _lru_in_specs(tb, order, n_rows):
    ratio = tb // SUBLANES
    rows = pl.BlockSpec((tb, LRU_WIDTH), lambda i: (order(i), 0))
    prev8 = pl.BlockSpec((SUBLANES, LRU_WIDTH), lambda i: (jnp.maximum(order(i) * ratio - 1, 0), 0))
    vec = pl.BlockSpec((1, LRU_WIDTH), lambda i: (0, 0))
    conv_w = pl.BlockSpec((CONV_WIDTH, LRU_WIDTH), lambda i: (0, 0))
    tiles = pl.BlockSpec((LRU_WIDTH // LRU_TILE, LRU_TILE, LRU_TILE), lambda i: (0, 0, 0))
    return rows, prev8, vec, conv_w, tiles


def _linear_scan(a, b, carry, *, reverse):
    rows, width = a.shape
    groups = rows // SUBLANES
    a3 = a.reshape(groups, SUBLANES, width)
    b3 = b.reshape(groups, SUBLANES, width)
    sub = lax.broadcasted_iota(jnp.int32, (1, SUBLANES, 1), 1)
    shift = 1
    while shift < SUBLANES:
        keep = (sub < SUBLANES - shift) if reverse else (sub >= shift)
        amount = SUBLANES - shift if reverse else shift
        b3 = b3 + a3 * jnp.where(keep, pltpu.roll(b3, amount, 1), 0.0)
        a3 = a3 * jnp.where(keep, pltpu.roll(a3, amount, 1), 1.0)
        shift *= 2
    out = [None] * groups
    edge = 0 if reverse else SUBLANES - 1
    for g in (reversed(range(groups)) if reverse else range(groups)):
        out[g] = b3[g] + a3[g] * carry
        carry = out[g][edge:edge + 1, :]
    return jnp.concatenate(out, axis=0), carry


def lru_fwd(xr, yr, conv_w, conv_b, wa_t, ba, wx_t, bx, sp):
    T = xr.shape[0]
    tb = _tile(T, LRU_ROWS)

    def body(xp_ref, x_ref, y_ref, cw_ref, cb_ref, wa_ref, ba_ref, wx_ref, bx_ref, sp_ref, h_ref, r_ref, carry_ref):
        block = pl.program_id(0)
        xc = _conv(_x_ext(block, xp_ref, x_ref), cw_ref, cb_ref, tb)
        _, gi, a, _, mult, _ = _lru_gates(block, xc, wa_ref, ba_ref, wx_ref, bx_ref, sp_ref)
        b = mult * gi * xc

        @pl.when(block == 0)
        def _():
            carry_ref[...] = jnp.zeros_like(carry_ref)

        h, carry_ref[...] = _linear_scan(a, b, carry_ref[...], reverse=False)
        h_ref[...] = h
        r_ref[...] = (h * _gelu(y_ref[...])).astype(r_ref.dtype)

    rows, prev8, vec, cw, tiles = _lru_in_specs(tb, lambda i: i, T)
    return pl.pallas_call(
        body, name="lru_fwd", grid=(T // tb,),
        in_specs=[prev8, rows, rows, cw, vec, tiles, vec, tiles, vec, vec],
        out_specs=[rows, rows],
        out_shape=[jax.ShapeDtypeStruct((T, LRU_WIDTH), F32), jax.ShapeDtypeStruct((T, LRU_WIDTH), MXU_DTYPE)],
        scratch_shapes=[pltpu.VMEM((1, LRU_WIDTH), F32)],
        compiler_params=_params("arbitrary"),
    )(xr, xr, yr, conv_w, conv_b, wa_t, ba, wx_t, bx, sp)


def lru_bwd(xr, yr, h, d_r, conv_w, conv_b, wa_t, ba, wx_t, bx, sp):
    T = xr.shape[0]
    tb = _tile(T, LRU_ROWS)
    n_blocks = T // tb
    order = lambda i: n_blocks - 1 - i
    n_tiles = LRU_WIDTH // LRU_TILE

    def body(xp_ref, x_ref, y_ref, hp_ref, h_ref, dr_ref, cw_ref, cb_ref, wa_ref, ba_ref, wx_ref, bx_ref, sp_ref,
             dxc_ref, dy_ref, dwa_ref, dwx_ref, dba_ref, dbx_ref, dsp_ref, dcb_ref, lam_ref, a_next_ref):
        step = pl.program_id(0)
        block = order(step)

        @pl.when(step == 0)
        def _():
            lam_ref[...] = jnp.zeros_like(lam_ref)
            a_next_ref[...] = jnp.zeros_like(a_next_ref)
            for ref in (dwa_ref, dwx_ref, dba_ref, dbx_ref, dsp_ref, dcb_ref):
                ref[...] = jnp.zeros_like(ref)

        xc = _conv(_x_ext(block, xp_ref, x_ref), cw_ref, cb_ref, tb)
        ra, gi, a, one_minus, mult, first = _lru_gates(block, xc, wa_ref, ba_ref, wx_ref, bx_ref, sp_ref)
        yv = y_ref[...]
        hv = h_ref[...]
        drv = dr_ref[...].astype(F32)
        dy_ref[...] = (drv * hv * _gelu_grad(yv)).astype(dy_ref.dtype)

        row = lax.broadcasted_iota(jnp.int32, (tb, 1), 0)
        coef = jnp.where(row == tb - 1, a_next_ref[...], pltpu.roll(a, tb - 1, 0))
        lam, lam_ref[...] = _linear_scan(coef, drv * _gelu(yv), lam_ref[...], reverse=True)
        a_next_ref[...] = a[0:1, :]

        h_prev = _shift_down(jnp.concatenate([jnp.where(block > 0, hp_ref[...], 0.0), hv], axis=0), 1, tb)
        d_a = lam * h_prev
        d_mult = jnp.where(first, 0.0, lam * gi * xc)
        d_gi = lam * mult * xc
        d_xc = lam * mult * gi
        safe = one_minus > 0.0
        d_log_a = d_a * a + d_mult * jnp.where(safe, -(1.0 - one_minus) * lax.rsqrt(jnp.where(safe, one_minus, 1.0)), 0.0)
        d_za = d_log_a * (-LRU_C * sp_ref[...]) * ra * (1.0 - ra)
        d_zx = d_gi * gi * (1.0 - gi)
        dsp_ref[...] += jnp.sum(d_log_a * (-LRU_C * ra), axis=0, keepdims=True)
        dba_ref[...] += jnp.sum(d_za, axis=0, keepdims=True)
        dbx_ref[...] += jnp.sum(d_zx, axis=0, keepdims=True)
        d_xc = d_xc + _block_matmul(d_za, wa_ref, 1) + _block_matmul(d_zx, wx_ref, 1)
        dcb_ref[...] += jnp.sum(d_xc, axis=0, keepdims=True)
        dxc_ref[...] = d_xc
        xc_m = xc.astype(MXU_DTYPE)
        for c in range(n_tiles):
            lanes = slice(c * LANES, (c + 1) * LANES)
            dwa_ref[c] += _dot(xc_m[:, lanes], d_za[:, lanes].astype(MXU_DTYPE), 0, 0)
            dwx_ref[c] += _dot(xc_m[:, lanes], d_zx[:, lanes].astype(MXU_DTYPE), 0, 0)

    rows, prev8, vec, cw, tiles = _lru_in_specs(tb, order, T)
    return pl.pallas_call(
        body, name="lru_bwd", grid=(n_blocks,),
        in_specs=[prev8, rows, rows, prev8, rows, rows, cw, vec, tiles, vec, tiles, vec, vec],
        out_specs=[rows, rows, tiles, tiles, vec, vec, vec, vec],
        out_shape=[jax.ShapeDtypeStruct((T, LRU_WIDTH), F32), jax.ShapeDtypeStruct((T, LRU_WIDTH), MXU_DTYPE),
                   jax.ShapeDtypeStruct((n_tiles, LRU_TILE, LRU_TILE), F32),
                   jax.ShapeDtypeStruct((n_tiles, LRU_TILE, LRU_TILE), F32)]
        + [jax.ShapeDtypeStruct((1, LRU_WIDTH), F32)] * 4,
        scratch_shapes=[pltpu.VMEM((1, LRU_WIDTH), F32), pltpu.VMEM((1, LRU_WIDTH), F32)],
        compiler_params=_params("arbitrary"),
    )(xr, xr, yr, h, h, d_r, conv_w, conv_b, wa_t, ba, wx_t, bx, sp)


def conv_bwd(xr, d_xc, conv_w):
    T = xr.shape[0]
    tb = _tile(T, LRU_ROWS)
    n_blocks = T // tb
    ratio = tb // SUBLANES

    def body(xp_ref, x_ref, d_ref, dn_ref, cw_ref, dx_ref, dcw_ref):
        block = pl.program_id(0)

        @pl.when(block == 0)
        def _():
            dcw_ref[...] = jnp.zeros_like(dcw_ref)

        x_ext = _x_ext(block, xp_ref, x_ref)
        dv = d_ref[...]
        d_ext = jnp.concatenate([dv, jnp.where(block < n_blocks - 1, dn_ref[...], 0.0)], axis=0)
        dx = None
        for j in range(CONV_WIDTH):
            k = CONV_WIDTH - 1 - j
            term = cw_ref[j:j + 1, :] * _shift_up(d_ext, k, tb)
            dx = term if dx is None else dx + term
            dcw_ref[j:j + 1, :] += jnp.sum(dv * _shift_down(x_ext, k, tb), axis=0, keepdims=True)
        dx_ref[...] = dx.astype(dx_ref.dtype)

    rows = pl.BlockSpec((tb, LRU_WIDTH), lambda i: (i, 0))
    prev8 = pl.BlockSpec((SUBLANES, LRU_WIDTH), lambda i: (jnp.maximum(i * ratio - 1, 0), 0))
    next8 = pl.BlockSpec((SUBLANES, LRU_WIDTH), lambda i: (jnp.minimum((i + 1) * ratio, n_blocks * ratio - 1), 0))
    cw = pl.BlockSpec((CONV_WIDTH, LRU_WIDTH), lambda i: (0, 0))
    return pl.pallas_call(
        body, name="conv_bwd", grid=(n_blocks,),
        in_specs=[prev8, rows, rows, next8, cw], out_specs=[rows, cw],
        out_shape=[jax.ShapeDtypeStruct((T, LRU_WIDTH), MXU_DTYPE), jax.ShapeDtypeStruct((CONV_WIDTH, LRU_WIDTH), F32)],
        compiler_params=_params("arbitrary"),
    )(xr, xr, d_xc, d_xc, conv_w)


def merge_fwd(gates, bias, branches):
    T = gates.shape[0]
    tm = _tile(T, 512)
    D = D_MODEL

    def body(g_ref, b_ref, a_ref, r_ref, c_ref, o_ref):
        acc = None
        for k, br in enumerate((a_ref, r_ref, c_ref)):
            term = _sigmoid(g_ref[:, k * D:(k + 1) * D] + b_ref[:, k * D:(k + 1) * D]) * br[...]
            acc = term if acc is None else acc + term
        o_ref[...] = acc.astype(o_ref.dtype)

    rows = pl.BlockSpec((tm, D), lambda i: (i, 0))
    return pl.pallas_call(
        body, name="merge_fwd", grid=(T // tm,),
        in_specs=[pl.BlockSpec((tm, 3 * D), lambda i: (i, 0)), pl.BlockSpec((1, 3 * D), lambda i: (0, 0)), rows, rows, rows],
        out_specs=rows, out_shape=jax.ShapeDtypeStruct((T, D), MXU_DTYPE),
        compiler_params=_params("parallel"),
    )(gates, bias, *branches)


def merge_bwd(d_merged, gates, bias, branches):
    T = gates.shape[0]
    tm = _tile(T, 512)
    D = D_MODEL

    def body(dm_ref, g_ref, b_ref, a_ref, r_ref, c_ref, da_ref, dr_ref, dc_ref, dg_ref, db_ref):
        @pl.when(pl.program_id(0) == 0)
        def _():
            db_ref[...] = jnp.zeros_like(db_ref)

        dm = dm_ref[...]
        for k, (br, dbr) in enumerate(((a_ref, da_ref), (r_ref, dr_ref), (c_ref, dc_ref))):
            cols = slice(k * D, (k + 1) * D)
            s = _sigmoid(g_ref[:, cols] + b_ref[:, cols])
            dbr[...] = (dm * s).astype(dbr.dtype)
            d_pre = dm * br[...] * s * (1.0 - s)
            dg_ref[:, cols] = d_pre.astype(dg_ref.dtype)
            db_ref[:, cols] += jnp.sum(d_pre, axis=0, keepdims=True)

    rows = pl.BlockSpec((tm, D), lambda i: (i, 0))
    wide = pl.BlockSpec((tm, 3 * D), lambda i: (i, 0))
    vec = pl.BlockSpec((1, 3 * D), lambda i: (0, 0))
    act = jax.ShapeDtypeStruct((T, D), MXU_DTYPE)
    return pl.pallas_call(
        body, name="merge_bwd", grid=(T // tm,),
        in_specs=[rows, wide, vec, rows, rows, rows], out_specs=[rows, rows, rows, wide, vec],
        out_shape=[act, act, act, jax.ShapeDtypeStruct((T, 3 * D), MXU_DTYPE), jax.ShapeDtypeStruct((1, 3 * D), F32)],
        compiler_params=_params("arbitrary"),
    )(d_merged, gates, bias, *branches)


def loss_head(x, g, target):
    T, D = x.shape
    tm = _tile(T, 512)

    def body(x_ref, g_ref, t_ref, loss_ref, dx_ref, dg_ref):
        @pl.when(pl.program_id(0) == 0)
        def _():
            loss_ref[...] = jnp.zeros_like(loss_ref)
            dg_ref[...] = jnp.zeros_like(dg_ref)

        xv = x_ref[...]
        gv = g_ref[...]
        r = lax.rsqrt(jnp.mean(xv * xv, axis=-1, keepdims=True) + EPS)
        xh = xv * r
        err = xh * gv - t_ref[...]
        per_row = jnp.mean(err * err, axis=-1, keepdims=True)
        loss_ref[...] += jnp.broadcast_to(0.5 * jnp.sum(per_row, axis=0, keepdims=True), loss_ref.shape)
        dyv = err * (1.0 / D)
        dg_ref[...] += jnp.sum(dyv * xh, axis=0, keepdims=True)
        gd = dyv * gv
        dx_ref[...] = r * (gd - xh * jnp.mean(gd * xh, axis=-1, keepdims=True))

    rows = pl.BlockSpec((tm, D), lambda i: (i, 0))
    vec = pl.BlockSpec((1, D), lambda i: (0, 0))
    return pl.pallas_call(
        body, name="loss_head", grid=(T // tm,),
        in_specs=[rows, vec, rows], out_specs=[pl.BlockSpec((1, LANES), lambda i: (0, 0)), rows, vec],
        out_shape=[jax.ShapeDtypeStruct((1, LANES), F32), jax.ShapeDtypeStruct((T, D), F32),
                   jax.ShapeDtypeStruct((1, D), F32)],
        compiler_params=_params("arbitrary"),
    )(x, g, target)


def _adamw_math(w, g, m, v):
    m = ADAM_B1 * m + (1.0 - ADAM_B1) * g
    v = ADAM_B2 * v + (1.0 - ADAM_B2) * (g * g)
    m_hat = m / (1.0 - ADAM_B1 ** ADAM_STEP)
    v_hat = v / (1.0 - ADAM_B2 ** ADAM_STEP)
    delta = -ADAM_LR * (m_hat / (jnp.sqrt(v_hat) + ADAM_EPS) + ADAM_WD * w)
    return delta, m, v


def _as_rows(a):
    return a.reshape(-1, a.shape[-1])


def sum_parts(layers):
    n, R, C = layers[0].shape
    depth = len(layers)
    tr = _tile(R, 512 if C <= 1024 else 128)
    blocks = R // tr

    def body(*refs):
        o_ref = refs[-1]
        for l in range(depth):
            @pl.when(pl.program_id(0) == l)
            def _(p_ref=refs[l]):
                acc = p_ref[0].astype(F32)
                for k in range(1, n):
                    acc = acc + p_ref[k].astype(F32)
                o_ref[...] = acc

    in_specs = [pl.BlockSpec((n, tr, C), lambda L, i, l=l: (0, jnp.where(L == l, i, 0), 0)) for l in range(depth)]
    return pl.pallas_call(
        body, name="sum_parts", grid=(depth, blocks), in_specs=in_specs,
        out_specs=pl.BlockSpec((tr, C), lambda L, i: (L * blocks + i, 0)),
        out_shape=jax.ShapeDtypeStruct((depth * R, C), F32), compiler_params=_params("arbitrary", "arbitrary"),
    )(*layers)


def adamw_sum(w, parts, m, v):
    n = len(parts)
    R, C = w.shape
    tr = _tile(R, 512 if n <= 2 and C <= 1024 else 256)

    def body(*refs):
        w_ref, m_ref, v_ref = refs[:3]
        g_ref, d_ref, nm_ref, nv_ref = refs[3 + n:]
        g = refs[3][...]
        for p_ref in refs[4:3 + n]:
            g = g + p_ref[...]
        delta, nm, nv = _adamw_math(w_ref[...], g, m_ref[...], v_ref[...])
        g_ref[...] = g
        d_ref[...] = delta
        nm_ref[...] = nm
        nv_ref[...] = nv

    rows = pl.BlockSpec((tr, C), lambda i: (i, 0))
    shape = jax.ShapeDtypeStruct((R, C), F32)
    return pl.pallas_call(
        body, name="adamw_sum", grid=(R // tr,), in_specs=[rows] * (3 + n), out_specs=[rows] * 4,
        out_shape=[shape] * 4, compiler_params=_params("parallel"),
    )(w, m, v, *parts)


def _place():
    return lax.axis_index("x"), lax.axis_index("y"), lax.axis_index("c")


HBM_SPEC = pl.BlockSpec(memory_space=pltpu.HBM)
SEM_SPEC = pl.BlockSpec(memory_space=pltpu.SEMAPHORE)
DATAFLOW = pltpu.SideEffectType.DATAFLOW_SIDE_EFFECTING


def _chip_copies(srcs, lands, send_sems, recv_sems, local_sems, scatter):
    x, y, c = _place()
    me = 2 * x + y
    outgoing, arriving, local = [], [], []
    for k, (src, land) in enumerate(zip(srcs, lands)):
        for d, (px, py) in enumerate([(1 - x, y), (x, 1 - y), (1 - x, 1 - y)]):
            peer = 2 * px + py
            pair = dict(send_sem=send_sems.at[3 * k + d], recv_sem=recv_sems.at[3 * k + d],
                        device_id=(px, py, c), device_id_type=MESH)
            outgoing.append(pltpu.make_async_remote_copy(
                src_ref=src.at[peer] if scatter else src, dst_ref=land.at[me], **pair))
            arriving.append(pltpu.make_async_remote_copy(
                src_ref=src.at[me] if scatter else src, dst_ref=land.at[peer], **pair))
        local.append(pltpu.make_async_copy(src.at[me] if scatter else src, land.at[me], local_sems.at[k]))
    return outgoing, arriving, local


def exchange_start(arrays, *, scatter, name):
    n = len(arrays)
    lands = [lax.empty(a.shape if scatter else (N_CHIPS,) + a.shape, a.dtype) for a in arrays]

    def body(*refs):
        srcs, zones = refs[:n], refs[n:2 * n]
        send_sems, recv_sems, local_sems = refs[2 * n:2 * n + 3]
        token = refs[-1]
        outgoing, _, local = _chip_copies(srcs, zones, send_sems, recv_sems, local_sems, scatter)
        for cp in outgoing + local:
            cp.start()
        token[...] = jnp.zeros_like(token)

    hbm = lambda a: pltpu.HBM(a.shape, a.dtype)
    outs = pl.pallas_call(
        body, name=name,
        out_shape=(pltpu.SemaphoreType.DMA((3 * n,)), pltpu.SemaphoreType.DMA((3 * n,)), pltpu.SemaphoreType.DMA((n,)),
                   *[hbm(a) for a in arrays], *[hbm(a) for a in lands], jax.ShapeDtypeStruct((SUBLANES, LANES), F32)),
        in_specs=[HBM_SPEC] * (2 * n),
        out_specs=(SEM_SPEC, SEM_SPEC, SEM_SPEC, *[HBM_SPEC] * (2 * n), pl.BlockSpec(memory_space=pltpu.VMEM)),
        input_output_aliases={i: 3 + i for i in range(2 * n)},
        compiler_params=pltpu.CompilerParams(has_side_effects=DATAFLOW),
    )(*[pltpu.with_memory_space_constraint(a, pltpu.HBM) for a in list(arrays) + lands])
    return outs[:3], outs[3:3 + n], outs[3 + n:3 + 2 * n], outs[-1]


def exchange_wait(started, after, *, scatter, name):
    sems, srcs, lands, _ = started
    n = len(srcs)

    def body(*refs):
        src_refs, zones = refs[:n], refs[n:2 * n]
        send_sems, recv_sems, local_sems = refs[2 * n:2 * n + 3]
        outgoing, arriving, local = _chip_copies(src_refs, zones, send_sems, recv_sems, local_sems, scatter)
        for sent, landed in zip(outgoing, arriving):
            sent.wait_send()
            landed.wait_recv()
        for cp in local:
            cp.wait()

    hbm = lambda a: pltpu.HBM(a.shape, a.dtype)
    outs = pl.pallas_call(
        body, name=name, out_shape=[hbm(a) for a in list(srcs) + list(lands)],
        in_specs=[HBM_SPEC] * (2 * n) + [SEM_SPEC] * 3 + [pl.BlockSpec(memory_space=pl.ANY)],
        out_specs=[HBM_SPEC] * (2 * n), input_output_aliases={i: i for i in range(2 * n)},
        compiler_params=pltpu.CompilerParams(has_side_effects=DATAFLOW),
    )(*srcs, *lands, *sems, after)
    return outs[n:]


def sibling_exchange(arrays, *, name):
    n = len(arrays)

    def body(*refs):
        ins, outs = refs[:n], refs[n:2 * n]
        send_sems, recv_sems = refs[2 * n:]
        x, y, c = _place()
        copies = [pltpu.make_async_remote_copy(
            src_ref=ins[k], dst_ref=outs[k], send_sem=send_sems.at[k], recv_sem=recv_sems.at[k],
            device_id=(x, y, 1 - c), device_id_type=MESH) for k in range(n)]
        for cp in copies:
            cp.start()
        for cp in copies:
            cp.wait()

    any_spec = pl.BlockSpec(memory_space=pl.ANY)
    return pl.pallas_call(
        body, name=name, in_specs=[any_spec] * n, out_specs=[any_spec] * n,
        out_shape=[jax.ShapeDtypeStruct(a.shape, a.dtype) for a in arrays],
        scratch_shapes=[pltpu.SemaphoreType.DMA((n,)), pltpu.SemaphoreType.DMA((n,))],
    )(*arrays)


def all_exchange(a, *, name):
    def body(in_ref, out_ref, send_sems, recv_sems, local_sem):
        x, y, c = _place()
        me = 4 * x + 2 * y + c
        flips = [(fx, fy, fc) for fx in (0, 1) for fy in (0, 1) for fc in (0, 1)][1:]

        def peer_of(d):
            fx, fy, fc = flips[d]
            return (x ^ fx, y ^ fy, c ^ fc)

        def copy(d, arriving):
            px, py, pc = peer_of(d)
            slot = (4 * px + 2 * py + pc) if arriving else me
            return pltpu.make_async_remote_copy(
                src_ref=in_ref, dst_ref=out_ref.at[slot], send_sem=send_sems.at[d], recv_sem=recv_sems.at[d],
                device_id=(px, py, pc), device_id_type=MESH)

        local = pltpu.make_async_copy(in_ref, out_ref.at[me], local_sem)
        local.start()
        for d in range(N_DEVICES - 1):
            copy(d, False).start()
        for d in range(N_DEVICES - 1):
            copy(d, True).wait()
        local.wait()

    any_spec = pl.BlockSpec(memory_space=pl.ANY)
    return pl.pallas_call(
        body, name=name, in_specs=[any_spec], out_specs=any_spec,
        out_shape=jax.ShapeDtypeStruct((N_DEVICES,) + a.shape, a.dtype),
        scratch_shapes=[pltpu.SemaphoreType.DMA((N_DEVICES - 1,)), pltpu.SemaphoreType.DMA((N_DEVICES - 1,)),
                        pltpu.SemaphoreType.DMA(())],
    )(a)


BIG = ("ffn1_w_gate", "ffn1_w_up", "ffn1_w_down", "w_in", "w_attn_out", "w_lru_out", "w_mem_kv", "w_mem_out",
       "w_out", "ffn2_w_gate", "ffn2_w_up", "ffn2_w_down")
TRANSPOSED = ("ffn1_w_gate", "ffn1_w_up", "w_in", "w_attn_out", "w_mem_out", "ffn2_w_gate", "ffn2_w_up")
SMALL = ("ffn1_norm", "mix_norm", "gate_bias", "attn_sinks", "conv_w", "conv_b", "lru_wa", "lru_ba", "lru_wx", "lru_bx",
         "lru_lambda", "mem_norm", "ffn2_norm", "final_norm")
WEIGHTS = ("ffn1_norm", "ffn1_w_gate", "ffn1_w_up", "ffn1_w_down", "mix_norm", "w_in", "gate_bias", "attn_sinks",
           "w_attn_out", "conv_w", "conv_b", "lru_wa", "lru_ba", "lru_wx", "lru_bx", "lru_lambda", "w_lru_out", "mem_norm",
           "w_mem_kv", "w_mem_out", "w_out", "ffn2_norm", "ffn2_w_gate", "ffn2_w_up", "ffn2_w_down", "final_norm")
SEGMENTS = (("qkv", 0, 768), ("xr", 768, 1792), ("yr", 1792, 2816), ("cq", 2816, 3328), ("gates", 3328, 6400))
PACK_ROW = 1024


STAGES = ("ffn1", "mix", "ffn2")
STAGE_BIG = {"ffn1": ("ffn1_w_gate", "ffn1_w_up", "ffn1_w_down"),
             "mix": ("w_in", "w_attn_out", "w_lru_out", "w_mem_kv", "w_mem_out", "w_out"),
             "ffn2": ("ffn2_w_gate", "ffn2_w_up", "ffn2_w_down")}


def _row_sharded(state, name):
    return jnp.swapaxes(state[name], 1, 2) if name in TRANSPOSED else state[name]


def _join_shards(gathered, name):
    if name == "conv_w":
        return jnp.concatenate([gathered[s] for s in range(N_CHIPS)], axis=-1)
    return gathered.reshape(-1, gathered.shape[-1])


def _split_shards(full):
    return full.reshape(N_CHIPS, -1, full.shape[-1])


def _lane_tiles(w):
    z = jnp.zeros((LRU_WIDTH // LRU_TILE, HEAD_DIM, HEAD_DIM), w.dtype)
    top = jnp.concatenate([w[0::2], z], axis=2)
    bottom = jnp.concatenate([z, w[1::2]], axis=2)
    return jnp.concatenate([top, bottom], axis=1)


def _from_lane_tiles(t):
    pairs = jnp.stack([t[:, :HEAD_DIM, :HEAD_DIM], t[:, HEAD_DIM:, HEAD_DIM:]], axis=1)
    return pairs.reshape(2 * t.shape[0], HEAD_DIM, HEAD_DIM)


def _pack(arrays):
    flat = jnp.concatenate([a.reshape(-1).astype(F32) for a in arrays])
    pad = (-flat.shape[0]) % (SUBLANES * PACK_ROW)
    return jnp.pad(flat, (0, pad)).reshape(-1, PACK_ROW)


def _unpack(packed, shapes):
    flat = packed.reshape(-1)
    out, at = [], 0
    for shape in shapes:
        size = int(np.prod(shape))
        out.append(flat[at:at + size].reshape(shape))
        at += size
    return out


def _ffn_forward(x, norm, w_gate_t, w_up_t, w_down):
    h = rms_fwd(x, norm)
    G, U, A = ffn_up(h, w_gate_t, w_up_t)
    y = mm([A], [w_down], nt=False, out_dtype=F32, res=x, scale=0.5, name="ffn_down")
    return y, (x, h, G, U, A)


def _ffn_backward(half, dy, saved, norm, w_gate_t, w_up_t, w_down):
    x, h, G, U, A = saved
    dG, dU = ffn_bwd_act(half, dy, w_down, G, U)
    d_down = mm_tn(A, dy, out_dtype=MXU_DTYPE, b_scale=0.5, name="ffn_dw_down")
    d_gate_t = mm_tn(dG, h, out_dtype=MXU_DTYPE, name="ffn_dw_up")
    d_up_t = mm_tn(dU, h, out_dtype=MXU_DTYPE, name="ffn_dw_up")
    dx, d_norm = mm_rms_bwd([dG, dU], [w_gate_t, w_up_t], x, norm, dy, nt=False, name="ffn_dx")
    return dx, d_norm, d_gate_t, d_up_t, d_down


def kernel(x, mem, ffn1_norm, ffn1_w_gate, ffn1_w_up, ffn1_w_down, mix_norm, w_in, gate_bias, attn_sinks, w_attn_out, conv_w, conv_b, lru_wa, lru_ba, lru_wx, lru_bx, lru_lambda, w_lru_out, mem_norm, w_mem_kv, w_mem_out, w_out, ffn2_norm, ffn2_w_gate, ffn2_w_up, ffn2_w_down, final_norm, loss_target, m_ffn1_norm, m_ffn1_w_gate, m_ffn1_w_up, m_ffn1_w_down, m_mix_norm, m_w_in, m_gate_bias, m_attn_sinks, m_w_attn_out, m_conv_w, m_conv_b, m_lru_wa, m_lru_ba, m_lru_wx, m_lru_bx, m_lru_lambda, m_w_lru_out, m_mem_norm, m_w_mem_kv, m_w_mem_out, m_w_out, m_ffn2_norm, m_ffn2_w_gate, m_ffn2_w_up, m_ffn2_w_down, m_final_norm, v_ffn1_norm, v_ffn1_w_gate, v_ffn1_w_up, v_ffn1_w_down, v_mix_norm, v_w_in, v_gate_bias, v_attn_sinks, v_w_attn_out, v_conv_w, v_conv_b, v_lru_wa, v_lru_ba, v_lru_wx, v_lru_bx, v_lru_lambda, v_w_lru_out, v_mem_norm, v_w_mem_kv, v_w_mem_out, v_w_out, v_ffn2_norm, v_ffn2_w_gate, v_ffn2_w_up, v_ffn2_w_down, v_final_norm):
    args = dict(locals())
    W = {n: args[n] for n in WEIGHTS}
    M = {n: args["m_" + n] for n in WEIGHTS}
    V = {n: args["v_" + n] for n in WEIGHTS}
    chip = 2 * lax.axis_index("x") + lax.axis_index("y")
    x0 = x[0]
    mem0 = mem[0]
    target = loss_target[0]

    def row(v):
        return v.reshape(1, -1)

    def stage_names(stage):
        return STAGE_BIG[stage] + (("conv_w",) if stage == "mix" else ())

    RW, RM, RV = ({n: _row_sharded(state, n) for n in BIG} for state in (W, M, V))
    gathers = {}
    started_all = jnp.zeros((), F32)
    for l in range(DEPTH):
        for stage in STAGES:
            shards = [W[n][l] if n == "conv_w" else RW[n][l].astype(MXU_DTYPE) for n in stage_names(stage)]
            gathers[l, stage] = exchange_start(shards, scatter=False, name=f"gather_start_{stage}_{l}")
            started_all = started_all + gathers[l, stage][3][0, 0]

    def weights_of(l, stage, after):
        lands = exchange_wait(gathers[l, stage], after, scatter=False, name=f"gather_wait_{stage}_{l}")
        return {n: _join_shards(g, n) for n, g in zip(stage_names(stage), lands)}

    saved = []
    full = []
    xs = x0
    for l in range(DEPTH):
        P = weights_of(l, "ffn1", xs)
        first_norm = row(W["ffn1_norm"][l]) + started_all if l == 0 else row(W["ffn1_norm"][l])
        xs, ffn1 = _ffn_forward(xs, first_norm, P["ffn1_w_gate"], P["ffn1_w_up"], P["ffn1_w_down"])
        x_mix = xs
        P.update(weights_of(l, "mix", xs))
        h = rms_fwd(x_mix, row(W["mix_norm"][l]))
        seg = {name: mm([h], [P["w_in"][lo:hi]], nt=True, out_dtype=F32, tm=1024, name="proj_" + name)
               for name, lo, hi in SEGMENTS}
        attn = attn_fwd(seg["qkv"], W["attn_sinks"][l])
        wa_t = _lane_tiles(W["lru_wa"][l]).astype(MXU_DTYPE)
        wx_t = _lane_tiles(W["lru_wx"][l]).astype(MXU_DTYPE)
        sp = row(jax.nn.softplus(-W["lru_lambda"][l]))
        lru_consts = (P["conv_w"], row(W["conv_b"][l]), wa_t, row(W["lru_ba"][l]), wx_t, row(W["lru_bx"][l]), sp)
        h_lru, rec = lru_fwd(seg["xr"], seg["yr"], *lru_consts)
        mem_n = rms_fwd(mem0, row(W["mem_norm"][l]))
        kv = mm([mem_n], [P["w_mem_kv"]], nt=False, out_dtype=F32, name="mem_kv")
        cross = mem_attn_fwd(seg["cq"], kv)
        branches = (mm([attn], [P["w_attn_out"]], nt=True, out_dtype=F32, tm=1024, name="branch_attn"),
                    mm([rec], [P["w_lru_out"]], nt=False, out_dtype=F32, tm=1024, name="branch_lru"),
                    mm([cross], [P["w_mem_out"]], nt=True, out_dtype=F32, tm=1024, name="branch_mem"))
        bias = row(W["gate_bias"][l])
        merged = merge_fwd(seg["gates"], bias, branches)
        xs = mm([merged], [P["w_out"]], nt=False, out_dtype=F32, res=x_mix, tm=1024, name="mix_out")
        mix = (x_mix, h, seg, attn, lru_consts, h_lru, rec, mem_n, kv, cross, branches, bias, merged)
        P.update(weights_of(l, "ffn2", xs))
        xs, ffn2 = _ffn_forward(xs, row(W["ffn2_norm"][l]), P["ffn2_w_gate"], P["ffn2_w_up"], P["ffn2_w_down"])
        saved.append((ffn1, mix, ffn2))
        full.append(P)

    loss_lanes, dx, d_final = loss_head(xs, row(W["final_norm"]), target)
    loss = lax.psum(loss_lanes[0, 0], ("x", "y", "c"))

    big_grads = {}
    small_grads = {n: [None] * DEPTH for n in SMALL if n != "final_norm"}
    scatters = {}
    token = jnp.zeros((), F32)

    def send_grads(l, stage):
        chunks = [_split_shards(big_grads[n, l]) for n in STAGE_BIG[stage]]
        scatters[l, stage] = exchange_start(chunks, scatter=True, name=f"scatter_start_{stage}_{l}")
        return scatters[l, stage][3][0, 0]

    for l in reversed(range(DEPTH)):
        P = full[l]
        ffn1, mix, ffn2 = saved[l]
        dx, g_norm, g_gate, g_up, g_down = _ffn_backward(
            (0.5 + token).reshape(1), dx, ffn2, row(W["ffn2_norm"][l]), P["ffn2_w_gate"], P["ffn2_w_up"], P["ffn2_w_down"])
        small_grads["ffn2_norm"][l] = g_norm
        big_grads["ffn2_w_gate", l], big_grads["ffn2_w_up", l], big_grads["ffn2_w_down", l] = g_gate, g_up, g_down
        token = send_grads(l, "ffn2")

        x_mix, h, seg, attn, lru_consts, h_lru, rec, mem_n, kv, cross, branches, bias, merged = mix
        bias = bias + token
        d_merged = mm([dx], [P["w_out"]], nt=True, out_dtype=F32, tm=1024, name="d_merged")
        big_grads["w_out", l] = mm_tn(merged, dx, out_dtype=MXU_DTYPE, name="dw_out")
        d_attn_br, d_lru_br, d_mem_br, d_gates, d_bias = merge_bwd(d_merged, seg["gates"], bias, branches)
        small_grads["gate_bias"][l] = d_bias
        d_attn = mm([d_attn_br], [P["w_attn_out"]], nt=False, out_dtype=F32, tm=1024, name="d_attn")
        d_rec = mm([d_lru_br], [P["w_lru_out"]], nt=True, out_dtype=F32, tm=1024, name="d_rec")
        d_cross = mm([d_mem_br], [P["w_mem_out"]], nt=False, out_dtype=F32, tm=1024, name="d_cross")
        big_grads["w_attn_out", l] = mm_tn(d_attn_br, attn, out_dtype=MXU_DTYPE, name="dw_attn_out")
        big_grads["w_lru_out", l] = mm_tn(rec, d_lru_br, out_dtype=MXU_DTYPE, name="dw_lru_out")
        big_grads["w_mem_out", l] = mm_tn(d_mem_br, cross, out_dtype=MXU_DTYPE, name="dw_mem_out")

        d_qkv, d_sinks = attn_bwd(seg["qkv"], W["attn_sinks"][l], d_attn)
        small_grads["attn_sinks"][l] = d_sinks

        d_xc, d_yr, d_wa_t, d_wx_t, d_ba, d_bx, d_sp, d_cb = lru_bwd(seg["xr"], seg["yr"], h_lru, d_rec, *lru_consts)
        d_xr, d_cw = conv_bwd(seg["xr"], d_xc, lru_consts[0])
        small_grads["conv_w"][l] = d_cw
        small_grads["conv_b"][l] = d_cb
        small_grads["lru_wa"][l] = _from_lane_tiles(d_wa_t)
        small_grads["lru_wx"][l] = _from_lane_tiles(d_wx_t)
        small_grads["lru_ba"][l] = d_ba
        small_grads["lru_bx"][l] = d_bx
        small_grads["lru_lambda"][l] = -d_sp * _sigmoid(-row(W["lru_lambda"][l]))

        d_cq, d_kv = mem_attn_bwd(seg["cq"], kv, d_cross)
        big_grads["w_mem_kv", l] = mm_tn(mem_n, d_kv, out_dtype=MXU_DTYPE, name="dw_mem_kv")
        _, g_mem_norm = mm_rms_bwd([d_kv], [P["w_mem_kv"]], mem0, row(W["mem_norm"][l]), None, nt=True, name="mem_dnorm")
        small_grads["mem_norm"][l] = g_mem_norm

        d_seg = {"qkv": d_qkv, "xr": d_xr, "yr": d_yr, "cq": d_cq, "gates": d_gates}
        big_grads["w_in", l] = jnp.concatenate(
            [mm_tn(d_seg[name], h, out_dtype=MXU_DTYPE, name="dw_in_" + name) for name, _, _ in SEGMENTS], axis=0)
        dx, g_norm = mm_rms_bwd([d_seg[name] for name, _, _ in SEGMENTS], [P["w_in"][lo:hi] for _, lo, hi in SEGMENTS],
                                x_mix, row(W["mix_norm"][l]), dx, nt=False, name="mix_dx")
        small_grads["mix_norm"][l] = g_norm
        token = send_grads(l, "mix")

        dx, g_norm, g_gate, g_up, g_down = _ffn_backward(
            (0.5 + token).reshape(1), dx, ffn1, row(W["ffn1_norm"][l]), P["ffn1_w_gate"], P["ffn1_w_up"], P["ffn1_w_down"])
        small_grads["ffn1_norm"][l] = g_norm
        big_grads["ffn1_w_gate", l], big_grads["ffn1_w_up", l], big_grads["ffn1_w_down", l] = g_gate, g_up, g_down
        token = send_grads(l, "ffn1")

    grad_x = dx[None]

    arrived = {}
    backward_done = dx[:1, :1] + token
    for l in range(DEPTH):
        for stage in STAGES:
            lands = exchange_wait(scatters[l, stage], backward_done, scatter=True, name=f"scatter_wait_{stage}_{l}")
            for n, a in zip(STAGE_BIG[stage], lands):
                arrived[n, l] = a.reshape(N_CHIPS, -1, a.shape[-1])
    partial = [sum_parts([arrived[n, l] for l in range(DEPTH)]) for n in BIG]
    other = sibling_exchange(partial, name="sibling_grads")
    results = {}
    for n, mine, theirs in zip(BIG, partial, other):
        outs = adamw_sum(_as_rows(RW[n]), [mine, theirs], _as_rows(RM[n]), _as_rows(RV[n]))
        outs = [o.reshape(RW[n].shape) for o in outs]
        results[n] = [jnp.swapaxes(o, 1, 2) for o in outs] if n in TRANSPOSED else outs

    small_list = [jnp.stack(small_grads[n]) for n in SMALL if n != "final_norm"] + [d_final]
    small_shapes = [(DEPTH,) + W[n].shape[1:] if n != "conv_w" else (DEPTH, CONV_WIDTH, LRU_WIDTH)
                    for n in SMALL if n != "final_norm"] + [W["final_norm"].shape]
    everyone = all_exchange(_pack(small_list), name="exchange_small")

    def own_columns(full_conv):
        return lax.dynamic_slice_in_dim(full_conv, chip * (LRU_WIDTH // N_CHIPS), LRU_WIDTH // N_CHIPS, axis=2)

    def packed_state(state):
        arrays = []
        for n in SMALL:
            a = state[n]
            if n == "conv_w":
                a = lax.dynamic_update_slice_in_dim(jnp.zeros((DEPTH, CONV_WIDTH, LRU_WIDTH), F32), a,
                                                    chip * (LRU_WIDTH // N_CHIPS), axis=2)
            arrays.append(a)
        return _pack(arrays)

    small_outs = adamw_sum(packed_state(W), [everyone[s] for s in range(N_DEVICES)], packed_state(M), packed_state(V))
    for kind, packed in enumerate(small_outs):
        for n, a in zip(SMALL, _unpack(packed, small_shapes)):
            results.setdefault(n, [None] * 4)[kind] = own_columns(a) if n == "conv_w" else a.reshape(W[n].shape)

    return (loss, grad_x, *[results[n][0] for n in WEIGHTS], *[results[n][1] for n in WEIGHTS],
            *[results[n][2] for n in WEIGHTS], *[results[n][3] for n in WEIGHTS])
```

```python
import functools

import jax
import jax.numpy as jnp
import numpy as np
from jax import lax
from jax.experimental import pallas as pl
from jax.experimental.pallas import tpu as pltpu

F32 = jnp.float32
MXU_DTYPE = jnp.bfloat16

DEPTH = 4
D_MODEL = 1024
CHUNK = 64
HALO = 2 * CHUNK
N_Q_HEADS = 8
HEAD_DIM = 64
ATTN_WIDTH = 512
KV_WIDTH = 128
LRU_WIDTH = 1024
LRU_TILE = 128
CONV_WIDTH = 4
LRU_C = 8.0
MEM_HEADS = 4
MEM_HEAD_DIM = 128
MEM_WIDTH = 512
D_FF = 2816
EPS = 1e-6
ADAM_LR = 0.001
ADAM_B1 = 0.9
ADAM_B2 = 0.999
ADAM_EPS = 1e-08
ADAM_WD = 0.01
ADAM_STEP = 10

N_CHIPS = 4
N_DEVICES = 8
SUBLANES = 8
LANES = 128
VMEM_LIMIT_BYTES = 56 * 1024 * 1024
NEG_BIG = -1e30
MESH = pl.DeviceIdType.MESH

ALIBI_SLOPES = tuple(float(2.0 ** (-8.0 * (i + 1) / N_Q_HEADS)) for i in range(N_Q_HEADS))


def _params(*semantics):
    return pltpu.CompilerParams(dimension_semantics=semantics, vmem_limit_bytes=VMEM_LIMIT_BYTES)


def _tile(n, pref, unit=SUBLANES):
    if n <= pref:
        return n
    for t in range(pref - pref % unit, 0, -unit):
        if n % t == 0:
            return t
    return n


def _dot(a, b, ca, cb):
    return lax.dot_general(a, b, (((ca,), (cb,)), ((), ())), preferred_element_type=F32)


def _sigmoid(x):
    return 0.5 * jnp.tanh(0.5 * x) + 0.5


def rms_fwd(x, g):
    T, D = x.shape
    tm = _tile(T, 512)

    def body(x_ref, g_ref, h_ref):
        xv = x_ref[...]
        r = lax.rsqrt(jnp.mean(xv * xv, axis=-1, keepdims=True) + EPS)
        h_ref[...] = (xv * r * g_ref[...]).astype(h_ref.dtype)

    return pl.pallas_call(
        body, name="rms_fwd", grid=(T // tm,),
        in_specs=[pl.BlockSpec((tm, D), lambda i: (i, 0)), pl.BlockSpec((1, D), lambda i: (0, 0))],
        out_specs=pl.BlockSpec((tm, D), lambda i: (i, 0)),
        out_shape=jax.ShapeDtypeStruct((T, D), MXU_DTYPE),
        compiler_params=_params("parallel"),
    )(x, g)


def mm_rms_bwd(a_list, b_list, x, g, dy, *, nt, tm=512, name="mm_rms_bwd"):
    n_pairs = len(a_list)
    T, D = x.shape
    tm = _tile(T, tm)
    has_dy = dy is not None

    def body(*refs):
        a_refs, b_refs = refs[:n_pairs], refs[n_pairs:2 * n_pairs]
        x_ref, g_ref = refs[2 * n_pairs:2 * n_pairs + 2]
        dx_ref, dg_ref = refs[-2:]
        i = pl.program_id(0)
        dh = None
        for a_ref, b_ref in zip(a_refs, b_refs):
            d = _dot(a_ref[...].astype(MXU_DTYPE), b_ref[...], 1, 1 if nt else 0)
            dh = d if dh is None else dh + d
        xv = x_ref[...]
        r = lax.rsqrt(jnp.mean(xv * xv, axis=-1, keepdims=True) + EPS)
        xh = xv * r
        gd = dh * g_ref[...]
        dx = r * (gd - xh * jnp.mean(gd * xh, axis=-1, keepdims=True))
        dx_ref[...] = refs[2 * n_pairs + 2][...] + dx if has_dy else dx
        part = jnp.sum(dh * xh, axis=0, keepdims=True)

        @pl.when(i == 0)
        def _():
            dg_ref[...] = part

        @pl.when(i > 0)
        def _():
            dg_ref[...] += part

    row = pl.BlockSpec((tm, D), lambda i: (i, 0))
    vec = pl.BlockSpec((1, D), lambda i: (0, 0))
    in_specs = [pl.BlockSpec((tm, a.shape[1]), lambda i: (i, 0)) for a in a_list]
    in_specs += [pl.BlockSpec(b.shape, lambda i: (0, 0), pipeline_mode=pl.Buffered(1)) for b in b_list]
    in_specs += [row, vec] + ([row] if has_dy else [])
    operands = list(a_list) + list(b_list) + [x, g] + ([dy] if has_dy else [])
    return pl.pallas_call(
        body, name=name, grid=(T // tm,), in_specs=in_specs, out_specs=[row, vec],
        out_shape=[jax.ShapeDtypeStruct((T, D), F32), jax.ShapeDtypeStruct((1, D), F32)],
        compiler_params=_params("arbitrary"),
    )(*operands)


def mm(a_list, b_list, *, nt, out_dtype, res=None, scale=1.0, norm=None, tm=512, tn=1024, name="mm"):
    n_pairs = len(a_list)
    T = a_list[0].shape[0]
    N = b_list[0].shape[0] if nt else b_list[0].shape[1]
    tm = _tile(T, tm)
    tn = _tile(N, tn, LANES)
    has_res = res is not None
    has_norm = norm is not None
    assert not has_norm or tn == N

    def body(*refs):
        a_refs = refs[:n_pairs]
        b_refs = refs[n_pairs:2 * n_pairs]
        extra = list(refs[2 * n_pairs:])
        acc = None
        for a_ref, b_ref in zip(a_refs, b_refs):
            d = _dot(a_ref[...].astype(MXU_DTYPE), b_ref[...], 1, 1 if nt else 0)
            acc = d if acc is None else acc + d
        if has_res:
            acc = extra.pop(0)[...] + scale * acc
        elif scale != 1.0:
            acc = scale * acc
        if has_norm:
            g_ref = extra.pop(0)
            r = lax.rsqrt(jnp.mean(acc * acc, axis=-1, keepdims=True) + EPS)
            extra[1][...] = (acc * r * g_ref[...]).astype(extra[1].dtype)
        extra[0][...] = acc.astype(extra[0].dtype)

    in_specs = [pl.BlockSpec((tm, a.shape[1]), lambda j, i: (i, 0)) for a in a_list]
    if nt:
        in_specs += [pl.BlockSpec((tn, b.shape[1]), lambda j, i: (j, 0)) for b in b_list]
    else:
        in_specs += [pl.BlockSpec((b.shape[0], tn), lambda j, i: (0, j)) for b in b_list]
    out_spec = pl.BlockSpec((tm, tn), lambda j, i: (i, j))
    operands = list(a_list) + list(b_list)
    if has_res:
        in_specs.append(out_spec)
        operands.append(res)
    if has_norm:
        in_specs.append(pl.BlockSpec((1, N), lambda j, i: (0, 0)))
        operands.append(norm)
    out_shape = jax.ShapeDtypeStruct((T, N), out_dtype)
    return pl.pallas_call(
        body, name=name, grid=(N // tn, T // tm), in_specs=in_specs,
        out_specs=[out_spec, out_spec] if has_norm else out_spec,
        out_shape=[out_shape, jax.ShapeDtypeStruct((T, N), MXU_DTYPE)] if has_norm else out_shape,
        compiler_params=_params("parallel", "parallel"),
    )(*operands)


def mm_tn(a, b, *, out_dtype, b_scale=1.0, tk=1408, tn=1408, tt=None, name="mm_tn"):
    T, K = a.shape
    N = b.shape[1]
    tk = _tile(K, tk, LANES)
    tn = _tile(N, tn, LANES)
    if tt is None:
        tt = 1024 if b.dtype == F32 else 2048
    tt = _tile(T, tt)
    steps = T // tt

    def body(a_ref, b_ref, o_ref, acc_ref):
        t = pl.program_id(2)
        bv = b_ref[...]
        if b_scale != 1.0:
            bv = b_scale * bv
        d = _dot(a_ref[...].astype(MXU_DTYPE), bv.astype(MXU_DTYPE), 0, 0)

        @pl.when(t == 0)
        def _():
            acc_ref[...] = d

        @pl.when(t > 0)
        def _():
            acc_ref[...] += d

        @pl.when(t == steps - 1)
        def _():
            o_ref[...] = acc_ref[...].astype(o_ref.dtype)

    return pl.pallas_call(
        body, name=name, grid=(K // tk, N // tn, steps),
        in_specs=[pl.BlockSpec((tt, tk), lambda p, q, t: (t, p)), pl.BlockSpec((tt, tn), lambda p, q, t: (t, q))],
        out_specs=pl.BlockSpec((tk, tn), lambda p, q, t: (p, q)),
        out_shape=jax.ShapeDtypeStruct((K, N), out_dtype),
        scratch_shapes=[pltpu.VMEM((tk, tn), F32)],
        compiler_params=_params("parallel", "parallel", "arbitrary"),
    )(a, b)


def _lane_chunks(n, width=3 * LANES):
    return [slice(lo, min(lo + width, n)) for lo in range(0, n, width)]


def ffn_up(h, w_gate_t, w_up_t):
    T, D = h.shape
    F = w_gate_t.shape[0]
    tm = _tile(T, 512)
    tn = _tile(F, 1408, LANES)

    def body(h_ref, wg_ref, wu_ref, g_ref, u_ref, a_ref):
        hv = h_ref[...]
        G = _dot(hv, wg_ref[...], 1, 1)
        U = _dot(hv, wu_ref[...], 1, 1)
        g_ref[...] = G.astype(g_ref.dtype)
        u_ref[...] = U.astype(u_ref.dtype)
        a_ref[...] = (G * _sigmoid(G) * U).astype(a_ref.dtype)

    w_spec = pl.BlockSpec((tn, D), lambda j, i: (j, 0))
    o_spec = pl.BlockSpec((tm, tn), lambda j, i: (i, j))
    return pl.pallas_call(
        body, name="ffn_up", grid=(F // tn, T // tm),
        in_specs=[pl.BlockSpec((tm, D), lambda j, i: (i, 0)), w_spec, w_spec],
        out_specs=[o_spec, o_spec, o_spec],
        out_shape=[jax.ShapeDtypeStruct((T, F), MXU_DTYPE)] * 3,
        compiler_params=_params("parallel", "parallel"),
    )(h, w_gate_t, w_up_t)


def ffn_bwd_act(half, dy, w_down, G, U):
    T, D = dy.shape
    F = w_down.shape[0]
    tm = _tile(T, 512)
    tn = _tile(F, 1408, LANES)

    def body(half_ref, dy_ref, wd_ref, g_ref, u_ref, dg_ref, du_ref):
        d_out = (half_ref[0] * dy_ref[...]).astype(MXU_DTYPE)
        for cols in _lane_chunks(tn):
            dA = _dot(d_out, wd_ref[cols, :], 1, 1)
            Gv = g_ref[:, cols].astype(F32)
            s = _sigmoid(Gv)
            du_ref[:, cols] = (dA * (Gv * s)).astype(du_ref.dtype)
            dg_ref[:, cols] = (dA * u_ref[:, cols].astype(F32) * (s * (1.0 + Gv * (1.0 - s)))).astype(dg_ref.dtype)

    o_spec = pl.BlockSpec((tm, tn), lambda j, i: (i, j))
    return pl.pallas_call(
        body, name="ffn_bwd_act", grid=(F // tn, T // tm),
        in_specs=[pl.BlockSpec(memory_space=pltpu.SMEM), pl.BlockSpec((tm, D), lambda j, i: (i, 0)),
                  pl.BlockSpec((tn, D), lambda j, i: (j, 0)), o_spec, o_spec],
        out_specs=[o_spec, o_spec],
        out_shape=[jax.ShapeDtypeStruct((T, F), MXU_DTYPE), jax.ShapeDtypeStruct((T, F), MXU_DTYPE)],
        compiler_params=_params("parallel", "parallel"),
    )(half, dy, w_down, G, U)


ATTN_ROWS = 256


def _head_variants(kv128):
    lane = lax.broadcasted_iota(jnp.int32, kv128.shape, 1)
    low = lane < HEAD_DIM
    rolled = pltpu.roll(kv128, HEAD_DIM, 1)
    out = {}
    for j in (0, 1):
        for pos in (0, 1):
            src = kv128 if j == pos else rolled
            out[j, pos] = jnp.where(low if pos == 0 else jnp.logical_not(low), src, 0.0).astype(MXU_DTYPE)
    return out


def _fold_variant(d128, j, pos):
    lane = lax.broadcasted_iota(jnp.int32, d128.shape, 1)
    low = lane < HEAD_DIM
    kept = jnp.where(low if pos == 0 else jnp.logical_not(low), d128, 0.0)
    return kept if j == pos else pltpu.roll(kept, HEAD_DIM, 1)


def _attn_mask(block, rows, keys):
    r = lax.broadcasted_iota(jnp.int32, (rows, keys), 0)
    c = lax.broadcasted_iota(jnp.int32, (rows, keys), 1)
    q_chunk = r // CHUNK
    k_chunk = c // CHUNK - HALO // CHUNK
    first_chunk = jnp.where(block > 0, -(HALO // CHUNK), 0)
    valid = (k_chunk <= q_chunk) & (k_chunk >= q_chunk - HALO // CHUNK) & (k_chunk >= first_chunk)
    dist = jnp.abs(r + HALO - c).astype(F32)
    return valid, dist


def _attn_probs(q128, k_var, head, sink, valid, dist):
    s = _dot(q128, k_var, 1, 1) * (HEAD_DIM ** -0.5) - ALIBI_SLOPES[head] * dist
    s = jnp.where(valid, s, NEG_BIG)
    mx = jnp.maximum(jnp.max(s, axis=-1, keepdims=True), sink)
    p = jnp.exp(s - mx)
    p_sink = jnp.exp(sink - mx)
    inv = 1.0 / (jnp.sum(p, axis=-1, keepdims=True) + p_sink)
    return p * inv, p_sink * inv


def _attn_specs(T, tq, order):
    ratio = tq // HALO
    q_spec = pl.BlockSpec((tq, ATTN_WIDTH), lambda i: (order(i), 0))
    cur = lambda col: pl.BlockSpec((tq, KV_WIDTH), lambda i: (order(i), col))
    prev = lambda col: pl.BlockSpec((HALO, KV_WIDTH), lambda i: (jnp.maximum(order(i) * ratio - 1, 0), col))
    k_col, v_col = ATTN_WIDTH // KV_WIDTH, ATTN_WIDTH // KV_WIDTH + 1
    return [pl.BlockSpec(memory_space=pltpu.SMEM), q_spec, prev(k_col), cur(k_col), prev(v_col), cur(v_col)]


def attn_fwd(qkv, sinks):
    T = qkv.shape[0]
    tq = _tile(T, ATTN_ROWS)
    keys = tq + HALO

    def body(sink_ref, q_ref, kp_ref, kc_ref, vp_ref, vc_ref, o_ref):
        block = pl.program_id(0)
        k_var = _head_variants(jnp.concatenate([kp_ref[...], kc_ref[...]], axis=0))
        v_var = _head_variants(jnp.concatenate([vp_ref[...], vc_ref[...]], axis=0))
        valid, dist = _attn_mask(block, tq, keys)
        for pair in range(N_Q_HEADS // 2):
            q128 = q_ref[:, pair * LANES:(pair + 1) * LANES].astype(MXU_DTYPE)
            acc = None
            for pos in (0, 1):
                head = 2 * pair + pos
                j = head // 4
                p, _ = _attn_probs(q128, k_var[j, pos], head, sink_ref[head], valid, dist)
                d = _dot(p.astype(MXU_DTYPE), v_var[j, pos], 1, 0)
                acc = d if acc is None else acc + d
            o_ref[:, pair * LANES:(pair + 1) * LANES] = acc.astype(o_ref.dtype)

    return pl.pallas_call(
        body, name="attn_fwd", grid=(T // tq,),
        in_specs=_attn_specs(T, tq, lambda i: i),
        out_specs=pl.BlockSpec((tq, ATTN_WIDTH), lambda i: (i, 0)),
        out_shape=jax.ShapeDtypeStruct((T, ATTN_WIDTH), MXU_DTYPE),
        compiler_params=_params("parallel"),
    )(sinks, qkv, qkv, qkv, qkv, qkv)


def attn_bwd(qkv, sinks, d_out):
    T = qkv.shape[0]
    tq = _tile(T, ATTN_ROWS)
    keys = tq + HALO
    n_blocks = T // tq
    order = lambda i: n_blocks - 1 - i

    def body(sink_ref, q_ref, kp_ref, kc_ref, vp_ref, vc_ref, do_ref, dqkv_ref, dsink_ref, carry_ref):
        step = pl.program_id(0)
        block = order(step)
        k_var = _head_variants(jnp.concatenate([kp_ref[...], kc_ref[...]], axis=0))
        v_var = _head_variants(jnp.concatenate([vp_ref[...], vc_ref[...]], axis=0))
        valid, dist = _attn_mask(block, tq, keys)

        @pl.when(step == 0)
        def _():
            carry_ref[...] = jnp.zeros_like(carry_ref)
            dsink_ref[...] = jnp.zeros_like(dsink_ref)

        dk = jnp.zeros((keys, KV_WIDTH), F32)
        dv = jnp.zeros((keys, KV_WIDTH), F32)
        for pair in range(N_Q_HEADS // 2):
            q128 = q_ref[:, pair * LANES:(pair + 1) * LANES].astype(MXU_DTYPE)
            do128 = do_ref[:, pair * LANES:(pair + 1) * LANES].astype(MXU_DTYPE)
            dq128 = None
            for pos in (0, 1):
                head = 2 * pair + pos
                j = head // 4
                p, p_sink = _attn_probs(q128, k_var[j, pos], head, sink_ref[head], valid, dist)
                dp = _dot(do128, v_var[j, pos], 1, 1)
                delta = jnp.sum(p * dp, axis=-1, keepdims=True)
                ds = (p * (dp - delta) * (HEAD_DIM ** -0.5)).astype(MXU_DTYPE)
                d = _dot(ds, k_var[j, pos], 1, 0)
                dq128 = d if dq128 is None else dq128 + d
                dk = dk + _fold_variant(_dot(ds, q128, 0, 0), j, pos)
                dv = dv + _fold_variant(_dot(p.astype(MXU_DTYPE), do128, 0, 0), j, pos)
                dsink_ref[pl.ds(head, 1), :] += jnp.broadcast_to(
                    -jnp.sum(p_sink * delta, axis=0, keepdims=True), (1, LANES))
            dqkv_ref[:, pair * LANES:(pair + 1) * LANES] = dq128.astype(dqkv_ref.dtype)
        carried = jnp.concatenate([jnp.zeros((tq - HALO, 2 * KV_WIDTH), F32), carry_ref[...]], axis=0)
        dkv = jnp.concatenate([dk, dv], axis=1)
        dqkv_ref[:, ATTN_WIDTH:] = (dkv[HALO:] + carried).astype(dqkv_ref.dtype)
        carry_ref[...] = dkv[:HALO]

    out_rows = pl.BlockSpec((tq, ATTN_WIDTH + 2 * KV_WIDTH), lambda i: (order(i), 0))
    dqkv, dsink = pl.pallas_call(
        body, name="attn_bwd", grid=(n_blocks,),
        in_specs=_attn_specs(T, tq, order) + [pl.BlockSpec((tq, ATTN_WIDTH), lambda i: (order(i), 0))],
        out_specs=[out_rows, pl.BlockSpec((N_Q_HEADS, LANES), lambda i: (0, 0))],
        out_shape=[jax.ShapeDtypeStruct((T, ATTN_WIDTH + 2 * KV_WIDTH), MXU_DTYPE),
                   jax.ShapeDtypeStruct((N_Q_HEADS, LANES), F32)],
        scratch_shapes=[pltpu.VMEM((HALO, 2 * KV_WIDTH), F32)],
        compiler_params=_params("arbitrary"),
    )(sinks, qkv, qkv, qkv, qkv, qkv, d_out)
    return dqkv, dsink[:, 0]


def _mem_probs(q, k):
    s = _dot(q, k, 1, 1) * (MEM_HEAD_DIM ** -0.5)
    p = jnp.exp(s - jnp.max(s, axis=-1, keepdims=True))
    return p / jnp.sum(p, axis=-1, keepdims=True)


def mem_attn_fwd(cq, kv):
    T = cq.shape[0]
    M = kv.shape[0]
    tq = _tile(T, 512)

    def body(q_ref, kv_ref, o_ref):
        for h in range(MEM_HEADS):
            lo, hi = h * MEM_HEAD_DIM, (h + 1) * MEM_HEAD_DIM
            p = _mem_probs(q_ref[:, lo:hi].astype(MXU_DTYPE), kv_ref[:, lo:hi].astype(MXU_DTYPE))
            v = kv_ref[:, MEM_WIDTH + lo:MEM_WIDTH + hi].astype(MXU_DTYPE)
            o_ref[:, lo:hi] = _dot(p.astype(MXU_DTYPE), v, 1, 0).astype(o_ref.dtype)

    return pl.pallas_call(
        body, name="mem_attn_fwd", grid=(T // tq,),
        in_specs=[pl.BlockSpec((tq, MEM_WIDTH), lambda i: (i, 0)), pl.BlockSpec((M, 2 * MEM_WIDTH), lambda i: (0, 0))],
        out_specs=pl.BlockSpec((tq, MEM_WIDTH), lambda i: (i, 0)),
        out_shape=jax.ShapeDtypeStruct((T, MEM_WIDTH), MXU_DTYPE),
        compiler_params=_params("parallel"),
    )(cq, kv)


def mem_attn_bwd(cq, kv, d_out):
    T = cq.shape[0]
    M = kv.shape[0]
    tq = _tile(T, 512)

    def body(q_ref, kv_ref, do_ref, dq_ref, dkv_ref):
        @pl.when(pl.program_id(0) == 0)
        def _():
            dkv_ref[...] = jnp.zeros_like(dkv_ref)

        for h in range(MEM_HEADS):
            lo, hi = h * MEM_HEAD_DIM, (h + 1) * MEM_HEAD_DIM
            q = q_ref[:, lo:hi].astype(MXU_DTYPE)
            k = kv_ref[:, lo:hi].astype(MXU_DTYPE)
            v = kv_ref[:, MEM_WIDTH + lo:MEM_WIDTH + hi].astype(MXU_DTYPE)
            do = do_ref[:, lo:hi].astype(MXU_DTYPE)
            p = _mem_probs(q, k)
            dp = _dot(do, v, 1, 1)
            ds = (p * (dp - jnp.sum(p * dp, axis=-1, keepdims=True)) * (MEM_HEAD_DIM ** -0.5)).astype(MXU_DTYPE)
            dq_ref[:, lo:hi] = _dot(ds, k, 1, 0).astype(dq_ref.dtype)
            dkv_ref[:, lo:hi] += _dot(ds, q, 0, 0)
            dkv_ref[:, MEM_WIDTH + lo:MEM_WIDTH + hi] += _dot(p.astype(MXU_DTYPE), do, 0, 0)

    rows = pl.BlockSpec((tq, MEM_WIDTH), lambda i: (i, 0))
    whole = pl.BlockSpec((M, 2 * MEM_WIDTH), lambda i: (0, 0))
    return pl.pallas_call(
        body, name="mem_attn_bwd", grid=(T // tq,),
        in_specs=[rows, whole, rows], out_specs=[rows, whole],
        out_shape=[jax.ShapeDtypeStruct((T, MEM_WIDTH), MXU_DTYPE), jax.ShapeDtypeStruct((M, 2 * MEM_WIDTH), F32)],
        compiler_params=_params("arbitrary"),
    )(cq, kv, d_out)


LRU_ROWS = 256


def _shift_down(ext, k, rows):
    return pltpu.roll(ext, k, 0)[SUBLANES:SUBLANES + rows] if k else ext[SUBLANES:SUBLANES + rows]


def _shift_up(ext, k, rows):
    return pltpu.roll(ext, rows + SUBLANES - k, 0)[:rows] if k else ext[:rows]


def _conv(x_ext, conv_w_ref, conv_b_ref, rows):
    xc = conv_b_ref[...] + conv_w_ref[CONV_WIDTH - 1:CONV_WIDTH, :] * _shift_down(x_ext, 0, rows)
    for j in range(CONV_WIDTH - 1):
        xc = xc + conv_w_ref[j:j + 1, :] * _shift_down(x_ext, CONV_WIDTH - 1 - j, rows)
    return xc


def _block_matmul(x, w_ref, cb):
    return jnp.concatenate(
        [_dot(x[:, c * LANES:(c + 1) * LANES].astype(MXU_DTYPE), w_ref[c], 1, cb)
         for c in range(LRU_WIDTH // LRU_TILE)], axis=1)


def _lru_gates(block, xc, wa_ref, ba_ref, wx_ref, bx_ref, sp_ref):
    rows = xc.shape[0]
    ra = _sigmoid(_block_matmul(xc, wa_ref, 0) + ba_ref[...])
    gi = _sigmoid(_block_matmul(xc, wx_ref, 0) + bx_ref[...])
    log_a = -LRU_C * ra * sp_ref[...]
    a = jnp.exp(log_a)
    one_minus = 1.0 - jnp.exp(2.0 * log_a)
    mult = jnp.sqrt(jnp.maximum(one_minus, 0.0))
    row = lax.broadcasted_iota(jnp.int32, (rows, 1), 0)
    first = row == jnp.where(block == 0, 0, -1)
    mult = jnp.where(first, 1.0, mult)
    return ra, gi, a, one_minus, mult, first


def _gelu(x):
    inner = np.sqrt(2.0 / np.pi).astype(np.float32) * (x + 0.044715 * (x * x * x))
    return 0.5 * x * (1.0 + jnp.tanh(inner))


def _gelu_grad(x):
    c = np.sqrt(2.0 / np.pi).astype(np.float32)
    t = jnp.tanh(c * (x + 0.044715 * (x * x * x)))
    return 0.5 * (1.0 + t) + 0.5 * x * (1.0 - t * t) * c * (1.0 + 3.0 * 0.044715 * (x * x))


def _x_ext(block, prev_ref, cur_ref):
    prev = jnp.where(block > 0, prev_ref[...], 0.0)
    return jnp.concatenate([prev, cur_ref[...]], axis=0)


def _lru_in_specs(tb, order, n_rows):
    ratio = tb // SUBLANES
    rows = pl.BlockSpec((tb, LRU_WIDTH), lambda i: (order(i), 0))
    prev8 = pl.BlockSpec((SUBLANES, LRU_WIDTH), lambda i: (jnp.maximum(order(i) * ratio - 1, 0), 0))
    vec = pl.BlockSpec((1, LRU_WIDTH), lambda i: (0, 0))
    conv_w = pl.BlockSpec((CONV_WIDTH, LRU_WIDTH), lambda i: (0, 0))
    tiles = pl.BlockSpec((LRU_WIDTH // LRU_TILE, LRU_TILE, LRU_TILE), lambda i: (0, 0, 0))
    return rows, prev8, vec, conv_w, tiles


def _linear_scan(a, b, carry, *, reverse):
    rows, width = a.shape
    groups = rows // SUBLANES
    a3 = a.reshape(groups, SUBLANES, width)
    b3 = b.reshape(groups, SUBLANES, width)
    sub = lax.broadcasted_iota(jnp.int32, (1, SUBLANES, 1), 1)
    shift = 1
    while shift < SUBLANES:
        keep = (sub < SUBLANES - shift) if reverse else (sub >= shift)
        amount = SUBLANES - shift if reverse else shift
        b3 = b3 + a3 * jnp.where(keep, pltpu.roll(b3, amount, 1), 0.0)
        a3 = a3 * jnp.where(keep, pltpu.roll(a3, amount, 1), 1.0)
        shift *= 2
    out = [None] * groups
    edge = 0 if reverse else SUBLANES - 1
    for g in (reversed(range(groups)) if reverse else range(groups)):
        out[g] = b3[g] + a3[g] * carry
        carry = out[g][edge:edge + 1, :]
    return jnp.concatenate(out, axis=0), carry


def lru_fwd(xr, yr, conv_w, conv_b, wa_t, ba, wx_t, bx, sp):
    T = xr.shape[0]
    tb = _tile(T, LRU_ROWS)

    def body(xp_ref, x_ref, y_ref, cw_ref, cb_ref, wa_ref, ba_ref, wx_ref, bx_ref, sp_ref, h_ref, r_ref, carry_ref):
        block = pl.program_id(0)
        xc = _conv(_x_ext(block, xp_ref, x_ref), cw_ref, cb_ref, tb)
        _, gi, a, _, mult, _ = _lru_gates(block, xc, wa_ref, ba_ref, wx_ref, bx_ref, sp_ref)
        b = mult * gi * xc

        @pl.when(block == 0)
        def _():
            carry_ref[...] = jnp.zeros_like(carry_ref)

        h, carry_ref[...] = _linear_scan(a, b, carry_ref[...], reverse=False)
        h_ref[...] = h
        r_ref[...] = (h * _gelu(y_ref[...])).astype(r_ref.dtype)

    rows, prev8, vec, cw, tiles = _lru_in_specs(tb, lambda i: i, T)
    return pl.pallas_call(
        body, name="lru_fwd", grid=(T // tb,),
        in_specs=[prev8, rows, rows, cw, vec, tiles, vec, tiles, vec, vec],
        out_specs=[rows, rows],
        out_shape=[jax.ShapeDtypeStruct((T, LRU_WIDTH), F32), jax.ShapeDtypeStruct((T, LRU_WIDTH), MXU_DTYPE)],
        scratch_shapes=[pltpu.VMEM((1, LRU_WIDTH), F32)],
        compiler_params=_params("arbitrary"),
    )(xr, xr, yr, conv_w, conv_b, wa_t, ba, wx_t, bx, sp)


def lru_bwd(xr, yr, h, d_r, conv_w, conv_b, wa_t, ba, wx_t, bx, sp):
    T = xr.shape[0]
    tb = _tile(T, LRU_ROWS)
    n_blocks = T // tb
    order = lambda i: n_blocks - 1 - i
    n_tiles = LRU_WIDTH // LRU_TILE

    def body(xp_ref, x_ref, y_ref, hp_ref, h_ref, dr_ref, cw_ref, cb_ref, wa_ref, ba_ref, wx_ref, bx_ref, sp_ref,
             dxc_ref, dy_ref, dwa_ref, dwx_ref, dba_ref, dbx_ref, dsp_ref, dcb_ref, lam_ref, a_next_ref):
        step = pl.program_id(0)
        block = order(step)

        @pl.when(step == 0)
        def _():
            lam_ref[...] = jnp.zeros_like(lam_ref)
            a_next_ref[...] = jnp.zeros_like(a_next_ref)
            for ref in (dwa_ref, dwx_ref, dba_ref, dbx_ref, dsp_ref, dcb_ref):
                ref[...] = jnp.zeros_like(ref)

        xc = _conv(_x_ext(block, xp_ref, x_ref), cw_ref, cb_ref, tb)
        ra, gi, a, one_minus, mult, first = _lru_gates(block, xc, wa_ref, ba_ref, wx_ref, bx_ref, sp_ref)
        yv = y_ref[...]
        hv = h_ref[...]
        drv = dr_ref[...].astype(F32)
        dy_ref[...] = (drv * hv * _gelu_grad(yv)).astype(dy_ref.dtype)

        row = lax.broadcasted_iota(jnp.int32, (tb, 1), 0)
        coef = jnp.where(row == tb - 1, a_next_ref[...], pltpu.roll(a, tb - 1, 0))
        lam, lam_ref[...] = _linear_scan(coef, drv * _gelu(yv), lam_ref[...], reverse=True)
        a_next_ref[...] = a[0:1, :]

        h_prev = _shift_down(jnp.concatenate([jnp.where(block > 0, hp_ref[...], 0.0), hv], axis=0), 1, tb)
        d_a = lam * h_prev
        d_mult = jnp.where(first, 0.0, lam * gi * xc)
        d_gi = lam * mult * xc
        d_xc = lam * mult * gi
        safe = one_minus > 0.0
        d_log_a = d_a * a + d_mult * jnp.where(safe, -(1.0 - one_minus) * lax.rsqrt(jnp.where(safe, one_minus, 1.0)), 0.0)
        d_za = d_log_a * (-LRU_C * sp_ref[...]) * ra * (1.0 - ra)
        d_zx = d_gi * gi * (1.0 - gi)
        dsp_ref[...] += jnp.sum(d_log_a * (-LRU_C * ra), axis=0, keepdims=True)
        dba_ref[...] += jnp.sum(d_za, axis=0, keepdims=True)
        dbx_ref[...] += jnp.sum(d_zx, axis=0, keepdims=True)
        d_xc = d_xc + _block_matmul(d_za, wa_ref, 1) + _block_matmul(d_zx, wx_ref, 1)
        dcb_ref[...] += jnp.sum(d_xc, axis=0, keepdims=True)
        dxc_ref[...] = d_xc
        xc_m = xc.astype(MXU_DTYPE)
        for c in range(n_tiles):
            lanes = slice(c * LANES, (c + 1) * LANES)
            dwa_ref[c] += _dot(xc_m[:, lanes], d_za[:, lanes].astype(MXU_DTYPE), 0, 0)
            dwx_ref[c] += _dot(xc_m[:, lanes], d_zx[:, lanes].astype(MXU_DTYPE), 0, 0)

    rows, prev8, vec, cw, tiles = _lru_in_specs(tb, order, T)
    return pl.pallas_call(
        body, name="lru_bwd", grid=(n_blocks,),
        in_specs=[prev8, rows, rows, prev8, rows, rows, cw, vec, tiles, vec, tiles, vec, vec],
        out_specs=[rows, rows, tiles, tiles, vec, vec, vec, vec],
        out_shape=[jax.ShapeDtypeStruct((T, LRU_WIDTH), F32), jax.ShapeDtypeStruct((T, LRU_WIDTH), MXU_DTYPE),
                   jax.ShapeDtypeStruct((n_tiles, LRU_TILE, LRU_TILE), F32),
                   jax.ShapeDtypeStruct((n_tiles, LRU_TILE, LRU_TILE), F32)]
        + [jax.ShapeDtypeStruct((1, LRU_WIDTH), F32)] * 4,
        scratch_shapes=[pltpu.VMEM((1, LRU_WIDTH), F32), pltpu.VMEM((1, LRU_WIDTH), F32)],
        compiler_params=_params("arbitrary"),
    )(xr, xr, yr, h, h, d_r, conv_w, conv_b, wa_t, ba, wx_t, bx, sp)


def conv_bwd(xr, d_xc, conv_w):
    T = xr.shape[0]
    tb = _tile(T, LRU_ROWS)
    n_blocks = T // tb
    ratio = tb // SUBLANES

    def body(xp_ref, x_ref, d_ref, dn_ref, cw_ref, dx_ref, dcw_ref):
        block = pl.program_id(0)

        @pl.when(block == 0)
        def _():
            dcw_ref[...] = jnp.zeros_like(dcw_ref)

        x_ext = _x_ext(block, xp_ref, x_ref)
        dv = d_ref[...]
        d_ext = jnp.concatenate([dv, jnp.where(block < n_blocks - 1, dn_ref[...], 0.0)], axis=0)
        dx = None
        for j in range(CONV_WIDTH):
            k = CONV_WIDTH - 1 - j
            term = cw_ref[j:j + 1, :] * _shift_up(d_ext, k, tb)
            dx = term if dx is None else dx + term
            dcw_ref[j:j + 1, :] += jnp.sum(dv * _shift_down(x_ext, k, tb), axis=0, keepdims=True)
        dx_ref[...] = dx.astype(dx_ref.dtype)

    rows = pl.BlockSpec((tb, LRU_WIDTH), lambda i: (i, 0))
    prev8 = pl.BlockSpec((SUBLANES, LRU_WIDTH), lambda i: (jnp.maximum(i * ratio - 1, 0), 0))
    next8 = pl.BlockSpec((SUBLANES, LRU_WIDTH), lambda i: (jnp.minimum((i + 1) * ratio, n_blocks * ratio - 1), 0))
    cw = pl.BlockSpec((CONV_WIDTH, LRU_WIDTH), lambda i: (0, 0))
    return pl.pallas_call(
        body, name="conv_bwd", grid=(n_blocks,),
        in_specs=[prev8, rows, rows, next8, cw], out_specs=[rows, cw],
        out_shape=[jax.ShapeDtypeStruct((T, LRU_WIDTH), MXU_DTYPE), jax.ShapeDtypeStruct((CONV_WIDTH, LRU_WIDTH), F32)],
        compiler_params=_params("arbitrary"),
    )(xr, xr, d_xc, d_xc, conv_w)


def _whole(w):
    return pl.BlockSpec(w.shape, lambda i: (0, 0), pipeline_mode=pl.Buffered(1))


def merge_fwd(mixers, weights, gates, bias):
    T = gates.shape[0]
    tm = _tile(T, 512)
    D = D_MODEL
    contract = (1, 0, 1)

    def body(a_ref, r_ref, c_ref, wa_ref, wr_ref, wc_ref, g_ref, b_ref, m_ref, ba_ref, br_ref, bc_ref):
        acc = None
        for k, (x_ref, w_ref, out_ref) in enumerate(((a_ref, wa_ref, ba_ref), (r_ref, wr_ref, br_ref), (c_ref, wc_ref, bc_ref))):
            cols = slice(k * D, (k + 1) * D)
            branch = _dot(x_ref[...], w_ref[...], 1, contract[k])
            out_ref[...] = branch.astype(out_ref.dtype)
            term = _sigmoid(g_ref[:, cols] + b_ref[:, cols]) * branch
            acc = term if acc is None else acc + term
        m_ref[...] = acc.astype(m_ref.dtype)

    rows = pl.BlockSpec((tm, D), lambda i: (i, 0))
    act = jax.ShapeDtypeStruct((T, D), MXU_DTYPE)
    return pl.pallas_call(
        body, name="merge_fwd", grid=(T // tm,),
        in_specs=[pl.BlockSpec((tm, m.shape[1]), lambda i: (i, 0)) for m in mixers] + [_whole(w) for w in weights]
        + [pl.BlockSpec((tm, 3 * D), lambda i: (i, 0)), pl.BlockSpec((1, 3 * D), lambda i: (0, 0))],
        out_specs=[rows] * 4, out_shape=[act] * 4, compiler_params=_params("parallel"),
    )(*mixers, *weights, gates, bias)


def merge_bwd(dy, w_out, weights, gates, bias, branches):
    T = gates.shape[0]
    tm = _tile(T, 256)
    D = D_MODEL
    contract = (0, 1, 0)

    def body(dy_ref, wo_ref, wa_ref, wr_ref, wc_ref, g_ref, b_ref, ba_ref, br_ref, bc_ref,
             dba_ref, dbr_ref, dbc_ref, da_ref, dr_ref, dc_ref, dg_ref, db_ref):
        @pl.when(pl.program_id(0) == 0)
        def _():
            db_ref[...] = jnp.zeros_like(db_ref)

        dm = _dot(dy_ref[...].astype(MXU_DTYPE), wo_ref[...], 1, 1)
        for k, (w_ref, branch_ref, dbranch_ref, dmixer_ref) in enumerate(
                ((wa_ref, ba_ref, dba_ref, da_ref), (wr_ref, br_ref, dbr_ref, dr_ref), (wc_ref, bc_ref, dbc_ref, dc_ref))):
            cols = slice(k * D, (k + 1) * D)
            s = _sigmoid(g_ref[:, cols] + b_ref[:, cols])
            d_branch = (dm * s).astype(MXU_DTYPE)
            dbranch_ref[...] = d_branch
            dmixer_ref[...] = _dot(d_branch, w_ref[...], 1, contract[k]).astype(dmixer_ref.dtype)
            d_pre = dm * branch_ref[...].astype(F32) * s * (1.0 - s)
            dg_ref[:, cols] = d_pre.astype(dg_ref.dtype)
            db_ref[:, cols] += jnp.sum(d_pre, axis=0, keepdims=True)

    rows = pl.BlockSpec((tm, D), lambda i: (i, 0))
    half = pl.BlockSpec((tm, ATTN_WIDTH), lambda i: (i, 0))
    wide = pl.BlockSpec((tm, 3 * D), lambda i: (i, 0))
    vec = pl.BlockSpec((1, 3 * D), lambda i: (0, 0))
    act = jax.ShapeDtypeStruct((T, D), MXU_DTYPE)
    return pl.pallas_call(
        body, name="merge_bwd", grid=(T // tm,),
        in_specs=[rows, _whole(w_out)] + [_whole(w) for w in weights] + [wide, vec, rows, rows, rows],
        out_specs=[rows, rows, rows, half, rows, half, wide, vec],
        out_shape=[act, act, act, jax.ShapeDtypeStruct((T, ATTN_WIDTH), MXU_DTYPE), jax.ShapeDtypeStruct((T, D), F32),
                   jax.ShapeDtypeStruct((T, MEM_WIDTH), MXU_DTYPE), jax.ShapeDtypeStruct((T, 3 * D), MXU_DTYPE),
                   jax.ShapeDtypeStruct((1, 3 * D), F32)],
        compiler_params=_params("arbitrary"),
    )(dy, w_out, *weights, gates, bias, *branches)


def loss_head(x, g, target):
    T, D = x.shape
    tm = _tile(T, 512)

    def body(x_ref, g_ref, t_ref, loss_ref, dx_ref, dg_ref):
        @pl.when(pl.program_id(0) == 0)
        def _():
            loss_ref[...] = jnp.zeros_like(loss_ref)
            dg_ref[...] = jnp.zeros_like(dg_ref)

        xv = x_ref[...]
        gv = g_ref[...]
        r = lax.rsqrt(jnp.mean(xv * xv, axis=-1, keepdims=True) + EPS)
        xh = xv * r
        err = xh * gv - t_ref[...]
        per_row = jnp.mean(err * err, axis=-1, keepdims=True)
        loss_ref[...] += jnp.broadcast_to(0.5 * jnp.sum(per_row, axis=0, keepdims=True), loss_ref.shape)
        dyv = err * (1.0 / D)
        dg_ref[...] += jnp.sum(dyv * xh, axis=0, keepdims=True)
        gd = dyv * gv
        dx_ref[...] = r * (gd - xh * jnp.mean(gd * xh, axis=-1, keepdims=True))

    rows = pl.BlockSpec((tm, D), lambda i: (i, 0))
    vec = pl.BlockSpec((1, D), lambda i: (0, 0))
    return pl.pallas_call(
        body, name="loss_head", grid=(T // tm,),
        in_specs=[rows, vec, rows], out_specs=[pl.BlockSpec((1, LANES), lambda i: (0, 0)), rows, vec],
        out_shape=[jax.ShapeDtypeStruct((1, LANES), F32), jax.ShapeDtypeStruct((T, D), F32),
                   jax.ShapeDtypeStruct((1, D), F32)],
        compiler_params=_params("arbitrary"),
    )(x, g, target)


def _adamw_math(w, g, m, v):
    m = ADAM_B1 * m + (1.0 - ADAM_B1) * g
    v = ADAM_B2 * v + (1.0 - ADAM_B2) * (g * g)
    m_hat = m / (1.0 - ADAM_B1 ** ADAM_STEP)
    v_hat = v / (1.0 - ADAM_B2 ** ADAM_STEP)
    delta = -ADAM_LR * (m_hat / (jnp.sqrt(v_hat) + ADAM_EPS) + ADAM_WD * w)
    return delta, m, v


def _as_rows(a):
    return a.reshape(-1, a.shape[-1])


def sum_parts(layers):
    n, R, C = layers[0].shape
    depth = len(layers)
    tr = _tile(R, 512 if C <= 1024 else 128)
    blocks = R // tr

    def body(*refs):
        o_ref = refs[-1]
        for l in range(depth):
            @pl.when(pl.program_id(0) == l)
            def _(p_ref=refs[l]):
                acc = p_ref[0].astype(F32)
                for k in range(1, n):
                    acc = acc + p_ref[k].astype(F32)
                o_ref[...] = acc

    in_specs = [pl.BlockSpec((n, tr, C), lambda L, i, l=l: (0, jnp.where(L == l, i, 0), 0)) for l in range(depth)]
    return pl.pallas_call(
        body, name="sum_parts", grid=(depth, blocks), in_specs=in_specs,
        out_specs=pl.BlockSpec((tr, C), lambda L, i: (L * blocks + i, 0)),
        out_shape=jax.ShapeDtypeStruct((depth * R, C), F32), compiler_params=_params("arbitrary", "arbitrary"),
    )(*layers)


def adamw_sum(w, parts, m, v):
    n = len(parts)
    R, C = w.shape
    tr = _tile(R, 512 if n <= 2 and C <= 1024 else 256)

    def body(*refs):
        w_ref, m_ref, v_ref = refs[:3]
        g_ref, d_ref, nm_ref, nv_ref = refs[3 + n:]
        g = refs[3][...]
        for p_ref in refs[4:3 + n]:
            g = g + p_ref[...]
        delta, nm, nv = _adamw_math(w_ref[...], g, m_ref[...], v_ref[...])
        g_ref[...] = g
        d_ref[...] = delta
        nm_ref[...] = nm
        nv_ref[...] = nv

    rows = pl.BlockSpec((tr, C), lambda i: (i, 0))
    shape = jax.ShapeDtypeStruct((R, C), F32)
    return pl.pallas_call(
        body, name="adamw_sum", grid=(R // tr,), in_specs=[rows] * (3 + n), out_specs=[rows] * 4,
        out_shape=[shape] * 4, compiler_params=_params("parallel"),
    )(w, m, v, *parts)


def _place():
    return lax.axis_index("x"), lax.axis_index("y"), lax.axis_index("c")


HBM_SPEC = pl.BlockSpec(memory_space=pltpu.HBM)
SEM_SPEC = pl.BlockSpec(memory_space=pltpu.SEMAPHORE)
DATAFLOW = pltpu.SideEffectType.DATAFLOW_SIDE_EFFECTING


def _chip_copies(srcs, lands, send_sems, recv_sems, local_sems, scatter):
    x, y, c = _place()
    me = 2 * x + y
    outgoing, arriving, local = [], [], []
    for k, (src, land) in enumerate(zip(srcs, lands)):
        for d, (px, py) in enumerate([(1 - x, y), (x, 1 - y), (1 - x, 1 - y)]):
            peer = 2 * px + py
            pair = dict(send_sem=send_sems.at[3 * k + d], recv_sem=recv_sems.at[3 * k + d],
                        device_id=(px, py, c), device_id_type=MESH)
            outgoing.append(pltpu.make_async_remote_copy(
                src_ref=src.at[peer] if scatter else src, dst_ref=land.at[me], **pair))
            arriving.append(pltpu.make_async_remote_copy(
                src_ref=src.at[me] if scatter else src, dst_ref=land.at[peer], **pair))
        local.append(pltpu.make_async_copy(src.at[me] if scatter else src, land.at[me], local_sems.at[k]))
    return outgoing, arriving, local


def exchange_start(arrays, *, scatter, name):
    n = len(arrays)
    lands = [lax.empty(a.shape if scatter else (N_CHIPS,) + a.shape, a.dtype) for a in arrays]

    def body(*refs):
        srcs, zones = refs[:n], refs[n:2 * n]
        send_sems, recv_sems, local_sems = refs[2 * n:2 * n + 3]
        token = refs[-1]
        outgoing, _, local = _chip_copies(srcs, zones, send_sems, recv_sems, local_sems, scatter)
        for cp in outgoing + local:
            cp.start()
        token[...] = jnp.zeros_like(token)

    hbm = lambda a: pltpu.HBM(a.shape, a.dtype)
    outs = pl.pallas_call(
        body, name=name,
        out_shape=(pltpu.SemaphoreType.DMA((3 * n,)), pltpu.SemaphoreType.DMA((3 * n,)), pltpu.SemaphoreType.DMA((n,)),
                   *[hbm(a) for a in arrays], *[hbm(a) for a in lands], jax.ShapeDtypeStruct((SUBLANES, LANES), F32)),
        in_specs=[HBM_SPEC] * (2 * n),
        out_specs=(SEM_SPEC, SEM_SPEC, SEM_SPEC, *[HBM_SPEC] * (2 * n), pl.BlockSpec(memory_space=pltpu.VMEM)),
        input_output_aliases={i: 3 + i for i in range(2 * n)},
        compiler_params=pltpu.CompilerParams(has_side_effects=DATAFLOW),
    )(*[pltpu.with_memory_space_constraint(a, pltpu.HBM) for a in list(arrays) + lands])
    return outs[:3], outs[3:3 + n], outs[3 + n:3 + 2 * n], outs[-1]


def exchange_wait(started, after, *, scatter, name):
    sems, srcs, lands, _ = started
    n = len(srcs)

    def body(*refs):
        src_refs, zones = refs[:n], refs[n:2 * n]
        send_sems, recv_sems, local_sems = refs[2 * n:2 * n + 3]
        outgoing, arriving, local = _chip_copies(src_refs, zones, send_sems, recv_sems, local_sems, scatter)
        for sent, landed in zip(outgoing, arriving):
            sent.wait_send()
            landed.wait_recv()
        for cp in local:
            cp.wait()

    hbm = lambda a: pltpu.HBM(a.shape, a.dtype)
    outs = pl.pallas_call(
        body, name=name, out_shape=[hbm(a) for a in list(srcs) + list(lands)],
        in_specs=[HBM_SPEC] * (2 * n) + [SEM_SPEC] * 3 + [pl.BlockSpec(memory_space=pl.ANY)],
        out_specs=[HBM_SPEC] * (2 * n), input_output_aliases={i: i for i in range(2 * n)},
        compiler_params=pltpu.CompilerParams(has_side_effects=DATAFLOW),
    )(*srcs, *lands, *sems, after)
    return outs[n:]


def sibling_exchange(arrays, *, name):
    n = len(arrays)

    def body(*refs):
        ins, outs = refs[:n], refs[n:2 * n]
        send_sems, recv_sems = refs[2 * n:]
        x, y, c = _place()
        copies = [pltpu.make_async_remote_copy(
            src_ref=ins[k], dst_ref=outs[k], send_sem=send_sems.at[k], recv_sem=recv_sems.at[k],
            device_id=(x, y, 1 - c), device_id_type=MESH) for k in range(n)]
        for cp in copies:
            cp.start()
        for cp in copies:
            cp.wait()

    any_spec = pl.BlockSpec(memory_space=pl.ANY)
    return pl.pallas_call(
        body, name=name, in_specs=[any_spec] * n, out_specs=[any_spec] * n,
        out_shape=[jax.ShapeDtypeStruct(a.shape, a.dtype) for a in arrays],
        scratch_shapes=[pltpu.SemaphoreType.DMA((n,)), pltpu.SemaphoreType.DMA((n,))],
    )(*arrays)


def all_exchange(a, *, name):
    def body(in_ref, out_ref, send_sems, recv_sems, local_sem):
        x, y, c = _place()
        me = 4 * x + 2 * y + c
        flips = [(fx, fy, fc) for fx in (0, 1) for fy in (0, 1) for fc in (0, 1)][1:]

        def peer_of(d):
            fx, fy, fc = flips[d]
            return (x ^ fx, y ^ fy, c ^ fc)

        def copy(d, arriving):
            px, py, pc = peer_of(d)
            slot = (4 * px + 2 * py + pc) if arriving else me
            return pltpu.make_async_remote_copy(
                src_ref=in_ref, dst_ref=out_ref.at[slot], send_sem=send_sems.at[d], recv_sem=recv_sems.at[d],
                device_id=(px, py, pc), device_id_type=MESH)

        local = pltpu.make_async_copy(in_ref, out_ref.at[me], local_sem)
        local.start()
        for d in range(N_DEVICES - 1):
            copy(d, False).start()
        for d in range(N_DEVICES - 1):
            copy(d, True).wait()
        local.wait()

    any_spec = pl.BlockSpec(memory_space=pl.ANY)
    return pl.pallas_call(
        body, name=name, in_specs=[any_spec], out_specs=any_spec,
        out_shape=jax.ShapeDtypeStruct((N_DEVICES,) + a.shape, a.dtype),
        scratch_shapes=[pltpu.SemaphoreType.DMA((N_DEVICES - 1,)), pltpu.SemaphoreType.DMA((N_DEVICES - 1,)),
                        pltpu.SemaphoreType.DMA(())],
    )(a)


BIG = ("ffn1_w_gate", "ffn1_w_up", "ffn1_w_down", "w_in", "w_attn_out", "w_lru_out", "w_mem_kv", "w_mem_out",
       "w_out", "ffn2_w_gate", "ffn2_w_up", "ffn2_w_down")
TRANSPOSED = ("ffn1_w_gate", "ffn1_w_up", "w_in", "w_attn_out", "w_mem_out", "ffn2_w_gate", "ffn2_w_up")
SMALL = ("ffn1_norm", "mix_norm", "gate_bias", "attn_sinks", "conv_w", "conv_b", "lru_wa", "lru_ba", "lru_wx", "lru_bx",
         "lru_lambda", "mem_norm", "ffn2_norm", "final_norm")
WEIGHTS = ("ffn1_norm", "ffn1_w_gate", "ffn1_w_up", "ffn1_w_down", "mix_norm", "w_in", "gate_bias", "attn_sinks",
           "w_attn_out", "conv_w", "conv_b", "lru_wa", "lru_ba", "lru_wx", "lru_bx", "lru_lambda", "w_lru_out", "mem_norm",
           "w_mem_kv", "w_mem_out", "w_out", "ffn2_norm", "ffn2_w_gate", "ffn2_w_up", "ffn2_w_down", "final_norm")
SEGMENTS = (("qkv", 0, 768), ("xr", 768, 1792), ("yr", 1792, 2816), ("cq", 2816, 3328), ("gates", 3328, 6400))
PACK_ROW = 1024


STAGES = ("ffn1", "mix", "ffn2")
STAGE_BIG = {"ffn1": ("ffn1_w_gate", "ffn1_w_up", "ffn1_w_down"),
             "mix": ("w_in", "w_attn_out", "w_lru_out", "w_mem_kv", "w_mem_out", "w_out"),
             "ffn2": ("ffn2_w_gate", "ffn2_w_up", "ffn2_w_down")}


def _row_sharded(state, name):
    return jnp.swapaxes(state[name], 1, 2) if name in TRANSPOSED else state[name]


def _join_shards(gathered, name):
    if name == "conv_w":
        return jnp.concatenate([gathered[s] for s in range(N_CHIPS)], axis=-1)
    return gathered.reshape(-1, gathered.shape[-1])


def _split_shards(full):
    return full.reshape(N_CHIPS, -1, full.shape[-1])


def _lane_tiles(w):
    z = jnp.zeros((LRU_WIDTH // LRU_TILE, HEAD_DIM, HEAD_DIM), w.dtype)
    top = jnp.concatenate([w[0::2], z], axis=2)
    bottom = jnp.concatenate([z, w[1::2]], axis=2)
    return jnp.concatenate([top, bottom], axis=1)


def _from_lane_tiles(t):
    pairs = jnp.stack([t[:, :HEAD_DIM, :HEAD_DIM], t[:, HEAD_DIM:, HEAD_DIM:]], axis=1)
    return pairs.reshape(2 * t.shape[0], HEAD_DIM, HEAD_DIM)


def _pack(arrays):
    flat = jnp.concatenate([a.reshape(-1).astype(F32) for a in arrays])
    pad = (-flat.shape[0]) % (SUBLANES * PACK_ROW)
    return jnp.pad(flat, (0, pad)).reshape(-1, PACK_ROW)


def _unpack(packed, shapes):
    flat = packed.reshape(-1)
    out, at = [], 0
    for shape in shapes:
        size = int(np.prod(shape))
        out.append(flat[at:at + size].reshape(shape))
        at += size
    return out


def _ffn_forward(x, h, w_gate_t, w_up_t, w_down, next_norm):
    G, U, A = ffn_up(h, w_gate_t, w_up_t)
    out = mm([A], [w_down], nt=False, out_dtype=F32, res=x, scale=0.5, norm=next_norm, name="ffn_down")
    y, h_next = out if next_norm is not None else (out, None)
    return y, h_next, (x, h, G, U, A)


def _ffn_backward(half, dy, saved, norm, w_gate_t, w_up_t, w_down):
    x, h, G, U, A = saved
    dG, dU = ffn_bwd_act(half, dy, w_down, G, U)
    d_down = mm_tn(A, dy, out_dtype=MXU_DTYPE, b_scale=0.5, name="ffn_dw_down")
    d_gate_t = mm_tn(dG, h, out_dtype=MXU_DTYPE, name="ffn_dw_up")
    d_up_t = mm_tn(dU, h, out_dtype=MXU_DTYPE, name="ffn_dw_up")
    dx, d_norm = mm_rms_bwd([dG, dU], [w_gate_t, w_up_t], x, norm, dy, nt=False, name="ffn_dx")
    return dx, d_norm, d_gate_t, d_up_t, d_down


def kernel(x, mem, ffn1_norm, ffn1_w_gate, ffn1_w_up, ffn1_w_down, mix_norm, w_in, gate_bias, attn_sinks, w_attn_out, conv_w, conv_b, lru_wa, lru_ba, lru_wx, lru_bx, lru_lambda, w_lru_out, mem_norm, w_mem_kv, w_mem_out, w_out, ffn2_norm, ffn2_w_gate, ffn2_w_up, ffn2_w_down, final_norm, loss_target, m_ffn1_norm, m_ffn1_w_gate, m_ffn1_w_up, m_ffn1_w_down, m_mix_norm, m_w_in, m_gate_bias, m_attn_sinks, m_w_attn_out, m_conv_w, m_conv_b, m_lru_wa, m_lru_ba, m_lru_wx, m_lru_bx, m_lru_lambda, m_w_lru_out, m_mem_norm, m_w_mem_kv, m_w_mem_out, m_w_out, m_ffn2_norm, m_ffn2_w_gate, m_ffn2_w_up, m_ffn2_w_down, m_final_norm, v_ffn1_norm, v_ffn1_w_gate, v_ffn1_w_up, v_ffn1_w_down, v_mix_norm, v_w_in, v_gate_bias, v_attn_sinks, v_w_attn_out, v_conv_w, v_conv_b, v_lru_wa, v_lru_ba, v_lru_wx, v_lru_bx, v_lru_lambda, v_w_lru_out, v_mem_norm, v_w_mem_kv, v_w_mem_out, v_w_out, v_ffn2_norm, v_ffn2_w_gate, v_ffn2_w_up, v_ffn2_w_down, v_final_norm):
    args = dict(locals())
    W = {n: args[n] for n in WEIGHTS}
    M = {n: args["m_" + n] for n in WEIGHTS}
    V = {n: args["v_" + n] for n in WEIGHTS}
    chip = 2 * lax.axis_index("x") + lax.axis_index("y")
    x0 = x[0]
    mem0 = mem[0]
    target = loss_target[0]

    def row(v):
        return v.reshape(1, -1)

    def stage_names(stage):
        return STAGE_BIG[stage] + (("conv_w",) if stage == "mix" else ())

    RW, RM, RV = ({n: _row_sharded(state, n) for n in BIG} for state in (W, M, V))
    gathers = {}
    started_all = jnp.zeros((), F32)
    for l in range(DEPTH):
        for stage in STAGES:
            shards = [W[n][l] if n == "conv_w" else RW[n][l].astype(MXU_DTYPE) for n in stage_names(stage)]
            gathers[l, stage] = exchange_start(shards, scatter=False, name=f"gather_start_{stage}_{l}")
            started_all = started_all + gathers[l, stage][3][0, 0]

    def weights_of(l, stage, after):
        lands = exchange_wait(gathers[l, stage], after, scatter=False, name=f"gather_wait_{stage}_{l}")
        return {n: _join_shards(g, n) for n, g in zip(stage_names(stage), lands)}

    saved = []
    full = []
    xs = x0
    h = rms_fwd(x0, row(W["ffn1_norm"][0]) + started_all)
    for l in range(DEPTH):
        P = weights_of(l, "ffn1", xs)
        xs, h, ffn1 = _ffn_forward(xs, h, P["ffn1_w_gate"], P["ffn1_w_up"], P["ffn1_w_down"], row(W["mix_norm"][l]))
        x_mix = xs
        P.update(weights_of(l, "mix", xs))
        seg = {name: mm([h], [P["w_in"][lo:hi]], nt=True, out_dtype=F32, tm=1024, name="proj_" + name)
               for name, lo, hi in SEGMENTS}
        attn = attn_fwd(seg["qkv"], W["attn_sinks"][l])
        wa_t = _lane_tiles(W["lru_wa"][l]).astype(MXU_DTYPE)
        wx_t = _lane_tiles(W["lru_wx"][l]).astype(MXU_DTYPE)
        sp = row(jax.nn.softplus(-W["lru_lambda"][l]))
        lru_consts = (P["conv_w"], row(W["conv_b"][l]), wa_t, row(W["lru_ba"][l]), wx_t, row(W["lru_bx"][l]), sp)
        h_lru, rec = lru_fwd(seg["xr"], seg["yr"], *lru_consts)
        mem_n = rms_fwd(mem0, row(W["mem_norm"][l]))
        kv = mm([mem_n], [P["w_mem_kv"]], nt=False, out_dtype=F32, name="mem_kv")
        cross = mem_attn_fwd(seg["cq"], kv)
        bias = row(W["gate_bias"][l])
        branch_weights = (P["w_attn_out"], P["w_lru_out"], P["w_mem_out"])
        merged, *branches = merge_fwd((attn, rec, cross), branch_weights, seg["gates"], bias)
        h_mix = h
        xs, h = mm([merged], [P["w_out"]], nt=False, out_dtype=F32, res=x_mix, norm=row(W["ffn2_norm"][l]), name="mix_out")
        mix = (x_mix, h_mix, seg, attn, lru_consts, h_lru, rec, mem_n, kv, cross, branches, bias, merged)
        P.update(weights_of(l, "ffn2", xs))
        next_norm = row(W["ffn1_norm"][l + 1]) if l + 1 < DEPTH else None
        xs, h, ffn2 = _ffn_forward(xs, h, P["ffn2_w_gate"], P["ffn2_w_up"], P["ffn2_w_down"], next_norm)
        saved.append((ffn1, mix, ffn2))
        full.append(P)

    loss_lanes, dx, d_final = loss_head(xs, row(W["final_norm"]), target)
    loss = lax.psum(loss_lanes[0, 0], ("x", "y", "c"))

    big_grads = {}
    small_grads = {n: [None] * DEPTH for n in SMALL if n != "final_norm"}
    scatters = {}
    token = jnp.zeros((), F32)

    def send_grads(l, stage):
        chunks = [_split_shards(big_grads[n, l]) for n in STAGE_BIG[stage]]
        scatters[l, stage] = exchange_start(chunks, scatter=True, name=f"scatter_start_{stage}_{l}")
        return scatters[l, stage][3][0, 0]

    for l in reversed(range(DEPTH)):
        P = full[l]
        ffn1, mix, ffn2 = saved[l]
        dx, g_norm, g_gate, g_up, g_down = _ffn_backward(
            (0.5 + token).reshape(1), dx, ffn2, row(W["ffn2_norm"][l]), P["ffn2_w_gate"], P["ffn2_w_up"], P["ffn2_w_down"])
        small_grads["ffn2_norm"][l] = g_norm
        big_grads["ffn2_w_gate", l], big_grads["ffn2_w_up", l], big_grads["ffn2_w_down", l] = g_gate, g_up, g_down
        token = send_grads(l, "ffn2")

        x_mix, h, seg, attn, lru_consts, h_lru, rec, mem_n, kv, cross, branches, bias, merged = mix
        bias = bias + token
        big_grads["w_out", l] = mm_tn(merged, dx, out_dtype=MXU_DTYPE, name="dw_out")
        d_attn_br, d_lru_br, d_mem_br, d_attn, d_rec, d_cross, d_gates, d_bias = merge_bwd(
            dx, P["w_out"], (P["w_attn_out"], P["w_lru_out"], P["w_mem_out"]), seg["gates"], bias, branches)
        small_grads["gate_bias"][l] = d_bias
        big_grads["w_attn_out", l] = mm_tn(d_attn_br, attn, out_dtype=MXU_DTYPE, name="dw_attn_out")
        big_grads["w_lru_out", l] = mm_tn(rec, d_lru_br, out_dtype=MXU_DTYPE, name="dw_lru_out")
        big_grads["w_mem_out", l] = mm_tn(d_mem_br, cross, out_dtype=MXU_DTYPE, name="dw_mem_out")

        d_qkv, d_sinks = attn_bwd(seg["qkv"], W["attn_sinks"][l], d_attn)
        small_grads["attn_sinks"][l] = d_sinks

        d_xc, d_yr, d_wa_t, d_wx_t, d_ba, d_bx, d_sp, d_cb = lru_bwd(seg["xr"], seg["yr"], h_lru, d_rec, *lru_consts)
        d_xr, d_cw = conv_bwd(seg["xr"], d_xc, lru_consts[0])
        small_grads["conv_w"][l] = d_cw
        small_grads["conv_b"][l] = d_cb
        small_grads["lru_wa"][l] = _from_lane_tiles(d_wa_t)
        small_grads["lru_wx"][l] = _from_lane_tiles(d_wx_t)
        small_grads["lru_ba"][l] = d_ba
        small_grads["lru_bx"][l] = d_bx
        small_grads["lru_lambda"][l] = -d_sp * _sigmoid(-row(W["lru_lambda"][l]))

        d_cq, d_kv = mem_attn_bwd(seg["cq"], kv, d_cross)
        big_grads["w_mem_kv", l] = mm_tn(mem_n, d_kv, out_dtype=MXU_DTYPE, name="dw_mem_kv")
        _, g_mem_norm = mm_rms_bwd([d_kv], [P["w_mem_kv"]], mem0, row(W["mem_norm"][l]), None, nt=True, name="mem_dnorm")
        small_grads["mem_norm"][l] = g_mem_norm

        d_seg = {"qkv": d_qkv, "xr": d_xr, "yr": d_yr, "cq": d_cq, "gates": d_gates}
        big_grads["w_in", l] = jnp.concatenate(
            [mm_tn(d_seg[name], h, out_dtype=MXU_DTYPE, name="dw_in_" + name) for name, _, _ in SEGMENTS], axis=0)
        dx, g_norm = mm_rms_bwd([d_seg[name] for name, _, _ in SEGMENTS], [P["w_in"][lo:hi] for _, lo, hi in SEGMENTS],
                                x_mix, row(W["mix_norm"][l]), dx, nt=False, name="mix_dx")
        small_grads["mix_norm"][l] = g_norm
        token = send_grads(l, "mix")

        dx, g_norm, g_gate, g_up, g_down = _ffn_backward(
            (0.5 + token).reshape(1), dx, ffn1, row(W["ffn1_norm"][l]), P["ffn1_w_gate"], P["ffn1_w_up"], P["ffn1_w_down"])
        small_grads["ffn1_norm"][l] = g_norm
        big_grads["ffn1_w_gate", l], big_grads["ffn1_w_up", l], big_grads["ffn1_w_down", l] = g_gate, g_up, g_down
        token = send_grads(l, "ffn1")

    grad_x = dx[None]

    arrived = {}
    backward_done = dx[:1, :1] + token
    for l in range(DEPTH):
        for stage in STAGES:
            lands = exchange_wait(scatters[l, stage], backward_done, scatter=True, name=f"scatter_wait_{stage}_{l}")
            for n, a in zip(STAGE_BIG[stage], lands):
                arrived[n, l] = a.reshape(N_CHIPS, -1, a.shape[-1])
    partial = [sum_parts([arrived[n, l] for l in range(DEPTH)]) for n in BIG]
    other = sibling_exchange(partial, name="sibling_grads")
    results = {}
    for n, mine, theirs in zip(BIG, partial, other):
        outs = adamw_sum(_as_rows(RW[n]), [mine, theirs], _as_rows(RM[n]), _as_rows(RV[n]))
        outs = [o.reshape(RW[n].shape) for o in outs]
        results[n] = [jnp.swapaxes(o, 1, 2) for o in outs] if n in TRANSPOSED else outs

    small_list = [jnp.stack(small_grads[n]) for n in SMALL if n != "final_norm"] + [d_final]
    small_shapes = [(DEPTH,) + W[n].shape[1:] if n != "conv_w" else (DEPTH, CONV_WIDTH, LRU_WIDTH)
                    for n in SMALL if n != "final_norm"] + [W["final_norm"].shape]
    everyone = all_exchange(_pack(small_list), name="exchange_small")

    def own_columns(full_conv):
        return lax.dynamic_slice_in_dim(full_conv, chip * (LRU_WIDTH // N_CHIPS), LRU_WIDTH // N_CHIPS, axis=2)

    def packed_state(state):
        arrays = []
        for n in SMALL:
            a = state[n]
            if n == "conv_w":
                a = lax.dynamic_update_slice_in_dim(jnp.zeros((DEPTH, CONV_WIDTH, LRU_WIDTH), F32), a,
                                                    chip * (LRU_WIDTH // N_CHIPS), axis=2)
            arrays.append(a)
        return _pack(arrays)

    small_outs = adamw_sum(packed_state(W), [everyone[s] for s in range(N_DEVICES)], packed_state(M), packed_state(V))
    for kind, packed in enumerate(small_outs):
        for n, a in zip(SMALL, _unpack(packed, small_shapes)):
            results.setdefault(n, [None] * 4)[kind] = own_columns(a) if n == "conv_w" else a.reshape(W[n].shape)

    return (loss, grad_x, *[results[n][0] for n in WEIGHTS], *[results[n][1] for n in WEIGHTS],
            *[results[n][2] for n in WEIGHTS], *[results[n][3] for n in WEIGHTS])
```

```python
import functools

import jax
import jax.numpy as jnp
import numpy as np
from jax import lax
from jax.experimental import pallas as pl
from jax.experimental.pallas import tpu as pltpu

F32 = jnp.float32
MXU_DTYPE = jnp.bfloat16

DEPTH = 4
D_MODEL = 1024
CHUNK = 64
HALO = 2 * CHUNK
N_Q_HEADS = 8
HEAD_DIM = 64
ATTN_WIDTH = 512
KV_WIDTH = 128
LRU_WIDTH = 1024
LRU_TILE = 128
CONV_WIDTH = 4
LRU_C = 8.0
MEM_HEADS = 4
MEM_HEAD_DIM = 128
MEM_WIDTH = 512
D_FF = 2816
EPS = 1e-6
ADAM_LR = 0.001
ADAM_B1 = 0.9
ADAM_B2 = 0.999
ADAM_EPS = 1e-08
ADAM_WD = 0.01
ADAM_STEP = 10

N_CHIPS = 4
N_DEVICES = 8
SUBLANES = 8
LANES = 128
VMEM_LIMIT_BYTES = 56 * 1024 * 1024
NEG_BIG = -1e30
MESH = pl.DeviceIdType.MESH

ALIBI_SLOPES = tuple(float(2.0 ** (-8.0 * (i + 1) / N_Q_HEADS)) for i in range(N_Q_HEADS))


def _params(*semantics):
    return pltpu.CompilerParams(dimension_semantics=semantics, vmem_limit_bytes=VMEM_LIMIT_BYTES)


def _tile(n, pref, unit=SUBLANES):
    if n <= pref:
        return n
    for t in range(pref - pref % unit, 0, -unit):
        if n % t == 0:
            return t
    return n


def _dot(a, b, ca, cb):
    return lax.dot_general(a, b, (((ca,), (cb,)), ((), ())), preferred_element_type=F32)


def _sigmoid(x):
    return 0.5 * jnp.tanh(0.5 * x) + 0.5


def rms_fwd(x, g):
    T, D = x.shape
    tm = _tile(T, 512)

    def body(x_ref, g_ref, h_ref):
        xv = x_ref[...]
        r = lax.rsqrt(jnp.mean(xv * xv, axis=-1, keepdims=True) + EPS)
        h_ref[...] = (xv * r * g_ref[...]).astype(h_ref.dtype)

    return pl.pallas_call(
        body, name="rms_fwd", grid=(T // tm,),
        in_specs=[pl.BlockSpec((tm, D), lambda i: (i, 0)), pl.BlockSpec((1, D), lambda i: (0, 0))],
        out_specs=pl.BlockSpec((tm, D), lambda i: (i, 0)),
        out_shape=jax.ShapeDtypeStruct((T, D), MXU_DTYPE),
        compiler_params=_params("parallel"),
    )(x, g)


def mm_rms_bwd(a_list, b_list, x, g, dy, *, nt, tm=512, name="mm_rms_bwd"):
    n_pairs = len(a_list)
    T, D = x.shape
    tm = _tile(T, tm)
    has_dy = dy is not None

    def body(*refs):
        a_refs, b_refs = refs[:n_pairs], refs[n_pairs:2 * n_pairs]
        x_ref, g_ref = refs[2 * n_pairs:2 * n_pairs + 2]
        dx_ref, dg_ref = refs[-2:]
        i = pl.program_id(0)
        dh = None
        for a_ref, b_ref in zip(a_refs, b_refs):
            d = _dot(a_ref[...].astype(MXU_DTYPE), b_ref[...], 1, 1 if nt else 0)
            dh = d if dh is None else dh + d
        xv = x_ref[...]
        r = lax.rsqrt(jnp.mean(xv * xv, axis=-1, keepdims=True) + EPS)
        xh = xv * r
        gd = dh * g_ref[...]
        dx = r * (gd - xh * jnp.mean(gd * xh, axis=-1, keepdims=True))
        dx_ref[...] = refs[2 * n_pairs + 2][...] + dx if has_dy else dx
        part = jnp.sum(dh * xh, axis=0, keepdims=True)

        @pl.when(i == 0)
        def _():
            dg_ref[...] = part

        @pl.when(i > 0)
        def _():
            dg_ref[...] += part

    row = pl.BlockSpec((tm, D), lambda i: (i, 0))
    vec = pl.BlockSpec((1, D), lambda i: (0, 0))
    in_specs = [pl.BlockSpec((tm, a.shape[1]), lambda i: (i, 0)) for a in a_list]
    in_specs += [pl.BlockSpec(b.shape, lambda i: (0, 0), pipeline_mode=pl.Buffered(1)) for b in b_list]
    in_specs += [row, vec] + ([row] if has_dy else [])
    operands = list(a_list) + list(b_list) + [x, g] + ([dy] if has_dy else [])
    return pl.pallas_call(
        body, name=name, grid=(T // tm,), in_specs=in_specs, out_specs=[row, vec],
        out_shape=[jax.ShapeDtypeStruct((T, D), F32), jax.ShapeDtypeStruct((1, D), F32)],
        compiler_params=_params("arbitrary"),
    )(*operands)


def mm(a_list, b_list, *, nt, out_dtype, res=None, scale=1.0, norm=None, tm=512, tn=1024, name="mm"):
    n_pairs = len(a_list)
    T = a_list[0].shape[0]
    N = b_list[0].shape[0] if nt else b_list[0].shape[1]
    tm = _tile(T, tm)
    tn = _tile(N, tn, LANES)
    has_res = res is not None
    has_norm = norm is not None
    assert not has_norm or tn == N

    def body(*refs):
        a_refs = refs[:n_pairs]
        b_refs = refs[n_pairs:2 * n_pairs]
        extra = list(refs[2 * n_pairs:])
        acc = None
        for a_ref, b_ref in zip(a_refs, b_refs):
            d = _dot(a_ref[...].astype(MXU_DTYPE), b_ref[...], 1, 1 if nt else 0)
            acc = d if acc is None else acc + d
        if has_res:
            acc = extra.pop(0)[...] + scale * acc
        elif scale != 1.0:
            acc = scale * acc
        if has_norm:
            g_ref = extra.pop(0)
            r = lax.rsqrt(jnp.mean(acc * acc, axis=-1, keepdims=True) + EPS)
            extra[1][...] = (acc * r * g_ref[...]).astype(extra[1].dtype)
        extra[0][...] = acc.astype(extra[0].dtype)

    in_specs = [pl.BlockSpec((tm, a.shape[1]), lambda j, i: (i, 0)) for a in a_list]
    if nt:
        in_specs += [pl.BlockSpec((tn, b.shape[1]), lambda j, i: (j, 0)) for b in b_list]
    else:
        in_specs += [pl.BlockSpec((b.shape[0], tn), lambda j, i: (0, j)) for b in b_list]
    out_spec = pl.BlockSpec((tm, tn), lambda j, i: (i, j))
    operands = list(a_list) + list(b_list)
    if has_res:
        in_specs.append(out_spec)
        operands.append(res)
    if has_norm:
        in_specs.append(pl.BlockSpec((1, N), lambda j, i: (0, 0)))
        operands.append(norm)
    out_shape = jax.ShapeDtypeStruct((T, N), out_dtype)
    return pl.pallas_call(
        body, name=name, grid=(N // tn, T // tm), in_specs=in_specs,
        out_specs=[out_spec, out_spec] if has_norm else out_spec,
        out_shape=[out_shape, jax.ShapeDtypeStruct((T, N), MXU_DTYPE)] if has_norm else out_shape,
        compiler_params=_params("parallel", "parallel"),
    )(*operands)


def mm_tn(a, b, *, out_dtype, b_scale=1.0, tk=1408, tn=1408, tt=None, name="mm_tn"):
    T, K = a.shape
    N = b.shape[1]
    tk = _tile(K, tk, LANES)
    tn = _tile(N, tn, LANES)
    if tt is None:
        tt = 1024 if b.dtype == F32 else 2048
    tt = _tile(T, tt)
    steps = T // tt

    def body(a_ref, b_ref, o_ref, acc_ref):
        t = pl.program_id(2)
        bv = b_ref[...]
        if b_scale != 1.0:
            bv = b_scale * bv
        d = _dot(a_ref[...].astype(MXU_DTYPE), bv.astype(MXU_DTYPE), 0, 0)

        @pl.when(t == 0)
        def _():
            acc_ref[...] = d

        @pl.when(t > 0)
        def _():
            acc_ref[...] += d

        @pl.when(t == steps - 1)
        def _():
            o_ref[...] = acc_ref[...].astype(o_ref.dtype)

    return pl.pallas_call(
        body, name=name, grid=(K // tk, N // tn, steps),
        in_specs=[pl.BlockSpec((tt, tk), lambda p, q, t: (t, p)), pl.BlockSpec((tt, tn), lambda p, q, t: (t, q))],
        out_specs=pl.BlockSpec((tk, tn), lambda p, q, t: (p, q)),
        out_shape=jax.ShapeDtypeStruct((K, N), out_dtype),
        scratch_shapes=[pltpu.VMEM((tk, tn), F32)],
        compiler_params=_params("parallel", "parallel", "arbitrary"),
    )(a, b)


def _lane_chunks(n, width=3 * LANES):
    return [slice(lo, min(lo + width, n)) for lo in range(0, n, width)]


def ffn_up(h, w_gate_t, w_up_t):
    T, D = h.shape
    F = w_gate_t.shape[0]
    tm = _tile(T, 512)
    tn = _tile(F, 1408, LANES)

    def body(h_ref, wg_ref, wu_ref, g_ref, u_ref, a_ref):
        hv = h_ref[...]
        G = _dot(hv, wg_ref[...], 1, 1)
        U = _dot(hv, wu_ref[...], 1, 1)
        g_ref[...] = G.astype(g_ref.dtype)
        u_ref[...] = U.astype(u_ref.dtype)
        a_ref[...] = (G * _sigmoid(G) * U).astype(a_ref.dtype)

    w_spec = pl.BlockSpec((tn, D), lambda j, i: (j, 0))
    o_spec = pl.BlockSpec((tm, tn), lambda j, i: (i, j))
    return pl.pallas_call(
        body, name="ffn_up", grid=(F // tn, T // tm),
        in_specs=[pl.BlockSpec((tm, D), lambda j, i: (i, 0)), w_spec, w_spec],
        out_specs=[o_spec, o_spec, o_spec],
        out_shape=[jax.ShapeDtypeStruct((T, F), MXU_DTYPE)] * 3,
        compiler_params=_params("parallel", "parallel"),
    )(h, w_gate_t, w_up_t)


def ffn_bwd_act(half, dy, w_down, G, U):
    T, D = dy.shape
    F = w_down.shape[0]
    tm = _tile(T, 512)

    def body(half_ref, dy_ref, wd_ref, g_ref, u_ref, dg_ref, du_ref):
        d_out = (half_ref[0] * dy_ref[...]).astype(MXU_DTYPE)
        for cols in _lane_chunks(F):
            dA = _dot(d_out, wd_ref[cols, :], 1, 1)
            Gv = g_ref[:, cols].astype(F32)
            s = _sigmoid(Gv)
            du_ref[:, cols] = (dA * (Gv * s)).astype(du_ref.dtype)
            dg_ref[:, cols] = (dA * u_ref[:, cols].astype(F32) * (s * (1.0 + Gv * (1.0 - s)))).astype(dg_ref.dtype)

    o_spec = pl.BlockSpec((tm, F), lambda i: (i, 0))
    return pl.pallas_call(
        body, name="ffn_bwd_act", grid=(T // tm,),
        in_specs=[pl.BlockSpec(memory_space=pltpu.SMEM), pl.BlockSpec((tm, D), lambda i: (i, 0)),
                  pl.BlockSpec(w_down.shape, lambda i: (0, 0), pipeline_mode=pl.Buffered(1)), o_spec, o_spec],
        out_specs=[o_spec, o_spec],
        out_shape=[jax.ShapeDtypeStruct((T, F), MXU_DTYPE), jax.ShapeDtypeStruct((T, F), MXU_DTYPE)],
        compiler_params=_params("parallel"),
    )(half, dy, w_down, G, U)


ATTN_ROWS = 256


def _head_variants(kv128):
    lane = lax.broadcasted_iota(jnp.int32, kv128.shape, 1)
    low = lane < HEAD_DIM
    rolled = pltpu.roll(kv128, HEAD_DIM, 1)
    out = {}
    for j in (0, 1):
        for pos in (0, 1):
            src = kv128 if j == pos else rolled
            out[j, pos] = jnp.where(low if pos == 0 else jnp.logical_not(low), src, 0.0).astype(MXU_DTYPE)
    return out


def _fold_variant(d128, j, pos):
    lane = lax.broadcasted_iota(jnp.int32, d128.shape, 1)
    low = lane < HEAD_DIM
    kept = jnp.where(low if pos == 0 else jnp.logical_not(low), d128, 0.0)
    return kept if j == pos else pltpu.roll(kept, HEAD_DIM, 1)


def _attn_mask(block, rows, keys):
    r = lax.broadcasted_iota(jnp.int32, (rows, keys), 0)
    c = lax.broadcasted_iota(jnp.int32, (rows, keys), 1)
    q_chunk = r // CHUNK
    k_chunk = c // CHUNK - HALO // CHUNK
    first_chunk = jnp.where(block > 0, -(HALO // CHUNK), 0)
    valid = (k_chunk <= q_chunk) & (k_chunk >= q_chunk - HALO // CHUNK) & (k_chunk >= first_chunk)
    dist = jnp.abs(r + HALO - c).astype(F32)
    return valid, dist


def _attn_probs(q128, k_var, head, sink, valid, dist):
    s = _dot(q128, k_var, 1, 1) * (HEAD_DIM ** -0.5) - ALIBI_SLOPES[head] * dist
    s = jnp.where(valid, s, NEG_BIG)
    mx = jnp.maximum(jnp.max(s, axis=-1, keepdims=True), sink)
    p = jnp.exp(s - mx)
    p_sink = jnp.exp(sink - mx)
    inv = 1.0 / (jnp.sum(p, axis=-1, keepdims=True) + p_sink)
    return p * inv, p_sink * inv


def _attn_specs(T, tq, order):
    ratio = tq // HALO
    q_spec = pl.BlockSpec((tq, ATTN_WIDTH), lambda i: (order(i), 0))
    cur = lambda col: pl.BlockSpec((tq, KV_WIDTH), lambda i: (order(i), col))
    prev = lambda col: pl.BlockSpec((HALO, KV_WIDTH), lambda i: (jnp.maximum(order(i) * ratio - 1, 0), col))
    k_col, v_col = ATTN_WIDTH // KV_WIDTH, ATTN_WIDTH // KV_WIDTH + 1
    return [pl.BlockSpec(memory_space=pltpu.SMEM), q_spec, prev(k_col), cur(k_col), prev(v_col), cur(v_col)]


def attn_fwd(qkv, sinks):
    T = qkv.shape[0]
    tq = _tile(T, ATTN_ROWS)
    keys = tq + HALO

    def body(sink_ref, q_ref, kp_ref, kc_ref, vp_ref, vc_ref, o_ref):
        block = pl.program_id(0)
        k_var = _head_variants(jnp.concatenate([kp_ref[...], kc_ref[...]], axis=0))
        v_var = _head_variants(jnp.concatenate([vp_ref[...], vc_ref[...]], axis=0))
        valid, dist = _attn_mask(block, tq, keys)
        for pair in range(N_Q_HEADS // 2):
            q128 = q_ref[:, pair * LANES:(pair + 1) * LANES].astype(MXU_DTYPE)
            acc = None
            for pos in (0, 1):
                head = 2 * pair + pos
                j = head // 4
                p, _ = _attn_probs(q128, k_var[j, pos], head, sink_ref[head], valid, dist)
                d = _dot(p.astype(MXU_DTYPE), v_var[j, pos], 1, 0)
                acc = d if acc is None else acc + d
            o_ref[:, pair * LANES:(pair + 1) * LANES] = acc.astype(o_ref.dtype)

    return pl.pallas_call(
        body, name="attn_fwd", grid=(T // tq,),
        in_specs=_attn_specs(T, tq, lambda i: i),
        out_specs=pl.BlockSpec((tq, ATTN_WIDTH), lambda i: (i, 0)),
        out_shape=jax.ShapeDtypeStruct((T, ATTN_WIDTH), MXU_DTYPE),
        compiler_params=_params("parallel"),
    )(sinks, qkv, qkv, qkv, qkv, qkv)


def attn_bwd(qkv, sinks, d_out):
    T = qkv.shape[0]
    tq = _tile(T, ATTN_ROWS)
    keys = tq + HALO
    n_blocks = T // tq
    order = lambda i: n_blocks - 1 - i

    def body(sink_ref, q_ref, kp_ref, kc_ref, vp_ref, vc_ref, do_ref, dqkv_ref, dsink_ref, carry_ref):
        step = pl.program_id(0)
        block = order(step)
        k_var = _head_variants(jnp.concatenate([kp_ref[...], kc_ref[...]], axis=0))
        v_var = _head_variants(jnp.concatenate([vp_ref[...], vc_ref[...]], axis=0))
        valid, dist = _attn_mask(block, tq, keys)

        @pl.when(step == 0)
        def _():
            carry_ref[...] = jnp.zeros_like(carry_ref)
            dsink_ref[...] = jnp.zeros_like(dsink_ref)

        dk = jnp.zeros((keys, KV_WIDTH), F32)
        dv = jnp.zeros((keys, KV_WIDTH), F32)
        for pair in range(N_Q_HEADS // 2):
            q128 = q_ref[:, pair * LANES:(pair + 1) * LANES].astype(MXU_DTYPE)
            do128 = do_ref[:, pair * LANES:(pair + 1) * LANES].astype(MXU_DTYPE)
            dq128 = None
            for pos in (0, 1):
                head = 2 * pair + pos
                j = head // 4
                p, p_sink = _attn_probs(q128, k_var[j, pos], head, sink_ref[head], valid, dist)
                dp = _dot(do128, v_var[j, pos], 1, 1)
                delta = jnp.sum(p * dp, axis=-1, keepdims=True)
                ds = (p * (dp - delta) * (HEAD_DIM ** -0.5)).astype(MXU_DTYPE)
                d = _dot(ds, k_var[j, pos], 1, 0)
                dq128 = d if dq128 is None else dq128 + d
                dk = dk + _fold_variant(_dot(ds, q128, 0, 0), j, pos)
                dv = dv + _fold_variant(_dot(p.astype(MXU_DTYPE), do128, 0, 0), j, pos)
                dsink_ref[pl.ds(head, 1), :] += jnp.broadcast_to(
                    -jnp.sum(p_sink * delta, axis=0, keepdims=True), (1, LANES))
            dqkv_ref[:, pair * LANES:(pair + 1) * LANES] = dq128.astype(dqkv_ref.dtype)
        carried = jnp.concatenate([jnp.zeros((tq - HALO, 2 * KV_WIDTH), F32), carry_ref[...]], axis=0)
        dkv = jnp.concatenate([dk, dv], axis=1)
        dqkv_ref[:, ATTN_WIDTH:] = (dkv[HALO:] + carried).astype(dqkv_ref.dtype)
        carry_ref[...] = dkv[:HALO]

    out_rows = pl.BlockSpec((tq, ATTN_WIDTH + 2 * KV_WIDTH), lambda i: (order(i), 0))
    dqkv, dsink = pl.pallas_call(
        body, name="attn_bwd", grid=(n_blocks,),
        in_specs=_attn_specs(T, tq, order) + [pl.BlockSpec((tq, ATTN_WIDTH), lambda i: (order(i), 0))],
        out_specs=[out_rows, pl.BlockSpec((N_Q_HEADS, LANES), lambda i: (0, 0))],
        out_shape=[jax.ShapeDtypeStruct((T, ATTN_WIDTH + 2 * KV_WIDTH), MXU_DTYPE),
                   jax.ShapeDtypeStruct((N_Q_HEADS, LANES), F32)],
        scratch_shapes=[pltpu.VMEM((HALO, 2 * KV_WIDTH), F32)],
        compiler_params=_params("arbitrary"),
    )(sinks, qkv, qkv, qkv, qkv, qkv, d_out)
    return dqkv, dsink[:, 0]


def _mem_probs(q, k):
    s = _dot(q, k, 1, 1) * (MEM_HEAD_DIM ** -0.5)
    p = jnp.exp(s - jnp.max(s, axis=-1, keepdims=True))
    return p / jnp.sum(p, axis=-1, keepdims=True)


def mem_attn_fwd(cq, kv):
    T = cq.shape[0]
    M = kv.shape[0]
    tq = _tile(T, 512)

    def body(q_ref, kv_ref, o_ref):
        for h in range(MEM_HEADS):
            lo, hi = h * MEM_HEAD_DIM, (h + 1) * MEM_HEAD_DIM
            p = _mem_probs(q_ref[:, lo:hi].astype(MXU_DTYPE), kv_ref[:, lo:hi].astype(MXU_DTYPE))
            v = kv_ref[:, MEM_WIDTH + lo:MEM_WIDTH + hi].astype(MXU_DTYPE)
            o_ref[:, lo:hi] = _dot(p.astype(MXU_DTYPE), v, 1, 0).astype(o_ref.dtype)

    return pl.pallas_call(
        body, name="mem_attn_fwd", grid=(T // tq,),
        in_specs=[pl.BlockSpec((tq, MEM_WIDTH), lambda i: (i, 0)), pl.BlockSpec((M, 2 * MEM_WIDTH), lambda i: (0, 0))],
        out_specs=pl.BlockSpec((tq, MEM_WIDTH), lambda i: (i, 0)),
        out_shape=jax.ShapeDtypeStruct((T, MEM_WIDTH), MXU_DTYPE),
        compiler_params=_params("parallel"),
    )(cq, kv)


def mem_attn_bwd(cq, kv, d_out):
    T = cq.shape[0]
    M = kv.shape[0]
    tq = _tile(T, 512)

    def body(q_ref, kv_ref, do_ref, dq_ref, dkv_ref):
        @pl.when(pl.program_id(0) == 0)
        def _():
            dkv_ref[...] = jnp.zeros_like(dkv_ref)

        for h in range(MEM_HEADS):
            lo, hi = h * MEM_HEAD_DIM, (h + 1) * MEM_HEAD_DIM
            q = q_ref[:, lo:hi].astype(MXU_DTYPE)
            k = kv_ref[:, lo:hi].astype(MXU_DTYPE)
            v = kv_ref[:, MEM_WIDTH + lo:MEM_WIDTH + hi].astype(MXU_DTYPE)
            do = do_ref[:, lo:hi].astype(MXU_DTYPE)
            p = _mem_probs(q, k)
            dp = _dot(do, v, 1, 1)
            ds = (p * (dp - jnp.sum(p * dp, axis=-1, keepdims=True)) * (MEM_HEAD_DIM ** -0.5)).astype(MXU_DTYPE)
            dq_ref[:, lo:hi] = _dot(ds, k, 1, 0).astype(dq_ref.dtype)
            dkv_ref[:, lo:hi] += _dot(ds, q, 0, 0)
            dkv_ref[:, MEM_WIDTH + lo:MEM_WIDTH + hi] += _dot(p.astype(MXU_DTYPE), do, 0, 0)

    rows = pl.BlockSpec((tq, MEM_WIDTH), lambda i: (i, 0))
    whole = pl.BlockSpec((M, 2 * MEM_WIDTH), lambda i: (0, 0))
    return pl.pallas_call(
        body, name="mem_attn_bwd", grid=(T // tq,),
        in_specs=[rows, whole, rows], out_specs=[rows, whole],
        out_shape=[jax.ShapeDtypeStruct((T, MEM_WIDTH), MXU_DTYPE), jax.ShapeDtypeStruct((M, 2 * MEM_WIDTH), F32)],
        compiler_params=_params("arbitrary"),
    )(cq, kv, d_out)


LRU_ROWS = 256


def _shift_down(ext, k, rows):
    return pltpu.roll(ext, k, 0)[SUBLANES:SUBLANES + rows] if k else ext[SUBLANES:SUBLANES + rows]


def _shift_up(ext, k, rows):
    return pltpu.roll(ext, rows + SUBLANES - k, 0)[:rows] if k else ext[:rows]


def _conv(x_ext, conv_w_ref, conv_b_ref, rows):
    xc = conv_b_ref[...] + conv_w_ref[CONV_WIDTH - 1:CONV_WIDTH, :] * _shift_down(x_ext, 0, rows)
    for j in range(CONV_WIDTH - 1):
        xc = xc + conv_w_ref[j:j + 1, :] * _shift_down(x_ext, CONV_WIDTH - 1 - j, rows)
    return xc


def _block_matmul(x, w_ref, cb):
    return jnp.concatenate(
        [_dot(x[:, c * LANES:(c + 1) * LANES].astype(MXU_DTYPE), w_ref[c], 1, cb)
         for c in range(LRU_WIDTH // LRU_TILE)], axis=1)


def _lru_gates(block, xc, wa_ref, ba_ref, wx_ref, bx_ref, sp_ref):
    rows = xc.shape[0]
    ra = _sigmoid(_block_matmul(xc, wa_ref, 0) + ba_ref[...])
    gi = _sigmoid(_block_matmul(xc, wx_ref, 0) + bx_ref[...])
    log_a = -LRU_C * ra * sp_ref[...]
    a = jnp.exp(log_a)
    one_minus = 1.0 - jnp.exp(2.0 * log_a)
    mult = jnp.sqrt(jnp.maximum(one_minus, 0.0))
    row = lax.broadcasted_iota(jnp.int32, (rows, 1), 0)
    first = row == jnp.where(block == 0, 0, -1)
    mult = jnp.where(first, 1.0, mult)
    return ra, gi, a, one_minus, mult, first


def _gelu(x):
    inner = np.sqrt(2.0 / np.pi).astype(np.float32) * (x + 0.044715 * (x * x * x))
    return 0.5 * x * (1.0 + jnp.tanh(inner))


def _gelu_grad(x):
    c = np.sqrt(2.0 / np.pi).astype(np.float32)
    t = jnp.tanh(c * (x + 0.044715 * (x * x * x)))
    return 0.5 * (1.0 + t) + 0.5 * x * (1.0 - t * t) * c * (1.0 + 3.0 * 0.044715 * (x * x))


def _x_ext(block, prev_ref, cur_ref):
    prev = jnp.where(block > 0, prev_ref[...], 0.0)
    return jnp.concatenate([prev, cur_ref[...]], axis=0)


def _lru_in_specs(tb, order, n_rows):
    ratio = tb // SUBLANES
    rows = pl.BlockSpec((tb, LRU_WIDTH), lambda i: (order(i), 0))
    prev8 = pl.BlockSpec((SUBLANES, LRU_WIDTH), lambda i: (jnp.maximum(order(i) * ratio - 1, 0), 0))
    vec = pl.BlockSpec((1, LRU_WIDTH), lambda i: (0, 0))
    conv_w = pl.BlockSpec((CONV_WIDTH, LRU_WIDTH), lambda i: (0, 0))
    tiles = pl.BlockSpec((LRU_WIDTH // LRU_TILE, LRU_TILE, LRU_TILE), lambda i: (0, 0, 0))
    return rows, prev8, vec, conv_w, tiles


def _linear_scan(a, b, carry, *, reverse):
    rows, width = a.shape
    groups = rows // SUBLANES
    a3 = a.reshape(groups, SUBLANES, width)
    b3 = b.reshape(groups, SUBLANES, width)
    sub = lax.broadcasted_iota(jnp.int32, (1, SUBLANES, 1), 1)
    shift = 1
    while shift < SUBLANES:
        keep = (sub < SUBLANES - shift) if reverse else (sub >= shift)
        amount = SUBLANES - shift if reverse else shift
        b3 = b3 + a3 * jnp.where(keep, pltpu.roll(b3, amount, 1), 0.0)
        a3 = a3 * jnp.where(keep, pltpu.roll(a3, amount, 1), 1.0)
        shift *= 2
    out = [None] * groups
    edge = 0 if reverse else SUBLANES - 1
    for g in (reversed(range(groups)) if reverse else range(groups)):
        out[g] = b3[g] + a3[g] * carry
        carry = out[g][edge:edge + 1, :]
    return jnp.concatenate(out, axis=0), carry


def lru_fwd(xr, yr, conv_w, conv_b, wa_t, ba, wx_t, bx, sp):
    T = xr.shape[0]
    tb = _tile(T, LRU_ROWS)

    def body(xp_ref, x_ref, y_ref, cw_ref, cb_ref, wa_ref, ba_ref, wx_ref, bx_ref, sp_ref, h_ref, r_ref, carry_ref):
        block = pl.program_id(0)
        xc = _conv(_x_ext(block, xp_ref, x_ref), cw_ref, cb_ref, tb)
        _, gi, a, _, mult, _ = _lru_gates(block, xc, wa_ref, ba_ref, wx_ref, bx_ref, sp_ref)
        b = mult * gi * xc

        @pl.when(block == 0)
        def _():
            carry_ref[...] = jnp.zeros_like(carry_ref)

        h, carry_ref[...] = _linear_scan(a, b, carry_ref[...], reverse=False)
        h_ref[...] = h
        r_ref[...] = (h * _gelu(y_ref[...])).astype(r_ref.dtype)

    rows, prev8, vec, cw, tiles = _lru_in_specs(tb, lambda i: i, T)
    return pl.pallas_call(
        body, name="lru_fwd", grid=(T // tb,),
        in_specs=[prev8, rows, rows, cw, vec, tiles, vec, tiles, vec, vec],
        out_specs=[rows, rows],
        out_shape=[jax.ShapeDtypeStruct((T, LRU_WIDTH), F32), jax.ShapeDtypeStruct((T, LRU_WIDTH), MXU_DTYPE)],
        scratch_shapes=[pltpu.VMEM((1, LRU_WIDTH), F32)],
        compiler_params=_params("arbitrary"),
    )(xr, xr, yr, conv_w, conv_b, wa_t, ba, wx_t, bx, sp)


def lru_bwd(xr, yr, h, d_r, conv_w, conv_b, wa_t, ba, wx_t, bx, sp):
    T = xr.shape[0]
    tb = _tile(T, LRU_ROWS)
    n_blocks = T // tb
    order = lambda i: n_blocks - 1 - i
    n_tiles = LRU_WIDTH // LRU_TILE

    def body(xp_ref, x_ref, y_ref, hp_ref, h_ref, dr_ref, cw_ref, cb_ref, wa_ref, ba_ref, wx_ref, bx_ref, sp_ref,
             dxc_ref, dy_ref, dwa_ref, dwx_ref, dba_ref, dbx_ref, dsp_ref, dcb_ref, lam_ref, a_next_ref):
        step = pl.program_id(0)
        block = order(step)

        @pl.when(step == 0)
        def _():
            lam_ref[...] = jnp.zeros_like(lam_ref)
            a_next_ref[...] = jnp.zeros_like(a_next_ref)
            for ref in (dwa_ref, dwx_ref, dba_ref, dbx_ref, dsp_ref, dcb_ref):
                ref[...] = jnp.zeros_like(ref)

        xc = _conv(_x_ext(block, xp_ref, x_ref), cw_ref, cb_ref, tb)
        ra, gi, a, one_minus, mult, first = _lru_gates(block, xc, wa_ref, ba_ref, wx_ref, bx_ref, sp_ref)
        yv = y_ref[...]
        hv = h_ref[...]
        drv = dr_ref[...].astype(F32)
        dy_ref[...] = (drv * hv * _gelu_grad(yv)).astype(dy_ref.dtype)

        row = lax.broadcasted_iota(jnp.int32, (tb, 1), 0)
        coef = jnp.where(row == tb - 1, a_next_ref[...], pltpu.roll(a, tb - 1, 0))
        lam, lam_ref[...] = _linear_scan(coef, drv * _gelu(yv), lam_ref[...], reverse=True)
        a_next_ref[...] = a[0:1, :]

        h_prev = _shift_down(jnp.concatenate([jnp.where(block > 0, hp_ref[...], 0.0), hv], axis=0), 1, tb)
        d_a = lam * h_prev
        d_mult = jnp.where(first, 0.0, lam * gi * xc)
        d_gi = lam * mult * xc
        d_xc = lam * mult * gi
        safe = one_minus > 0.0
        d_log_a = d_a * a + d_mult * jnp.where(safe, -(1.0 - one_minus) * lax.rsqrt(jnp.where(safe, one_minus, 1.0)), 0.0)
        d_za = d_log_a * (-LRU_C * sp_ref[...]) * ra * (1.0 - ra)
        d_zx = d_gi * gi * (1.0 - gi)
        dsp_ref[...] += jnp.sum(d_log_a * (-LRU_C * ra), axis=0, keepdims=True)
        dba_ref[...] += jnp.sum(d_za, axis=0, keepdims=True)
        dbx_ref[...] += jnp.sum(d_zx, axis=0, keepdims=True)
        d_xc = d_xc + _block_matmul(d_za, wa_ref, 1) + _block_matmul(d_zx, wx_ref, 1)
        dcb_ref[...] += jnp.sum(d_xc, axis=0, keepdims=True)
        dxc_ref[...] = d_xc
        xc_m = xc.astype(MXU_DTYPE)
        for c in range(n_tiles):
            lanes = slice(c * LANES, (c + 1) * LANES)
            dwa_ref[c] += _dot(xc_m[:, lanes], d_za[:, lanes].astype(MXU_DTYPE), 0, 0)
            dwx_ref[c] += _dot(xc_m[:, lanes], d_zx[:, lanes].astype(MXU_DTYPE), 0, 0)

    rows, prev8, vec, cw, tiles = _lru_in_specs(tb, order, T)
    return pl.pallas_call(
        body, name="lru_bwd", grid=(n_blocks,),
        in_specs=[prev8, rows, rows, prev8, rows, rows, cw, vec, tiles, vec, tiles, vec, vec],
        out_specs=[rows, rows, tiles, tiles, vec, vec, vec, vec],
        out_shape=[jax.ShapeDtypeStruct((T, LRU_WIDTH), F32), jax.ShapeDtypeStruct((T, LRU_WIDTH), MXU_DTYPE),
                   jax.ShapeDtypeStruct((n_tiles, LRU_TILE, LRU_TILE), F32),
                   jax.ShapeDtypeStruct((n_tiles, LRU_TILE, LRU_TILE), F32)]
        + [jax.ShapeDtypeStruct((1, LRU_WIDTH), F32)] * 4,
        scratch_shapes=[pltpu.VMEM((1, LRU_WIDTH), F32), pltpu.VMEM((1, LRU_WIDTH), F32)],
        compiler_params=_params("arbitrary"),
    )(xr, xr, yr, h, h, d_r, conv_w, conv_b, wa_t, ba, wx_t, bx, sp)


def conv_bwd(xr, d_xc, conv_w):
    T = xr.shape[0]
    tb = _tile(T, LRU_ROWS)
    n_blocks = T // tb
    ratio = tb // SUBLANES

    def body(xp_ref, x_ref, d_ref, dn_ref, cw_ref, dx_ref, dcw_ref):
        block = pl.program_id(0)

        @pl.when(block == 0)
        def _():
            dcw_ref[...] = jnp.zeros_like(dcw_ref)

        x_ext = _x_ext(block, xp_ref, x_ref)
        dv = d_ref[...]
        d_ext = jnp.concatenate([dv, jnp.where(block < n_blocks - 1, dn_ref[...], 0.0)], axis=0)
        dx = None
        for j in range(CONV_WIDTH):
            k = CONV_WIDTH - 1 - j
            term = cw_ref[j:j + 1, :] * _shift_up(d_ext, k, tb)
            dx = term if dx is None else dx + term
            dcw_ref[j:j + 1, :] += jnp.sum(dv * _shift_down(x_ext, k, tb), axis=0, keepdims=True)
        dx_ref[...] = dx.astype(dx_ref.dtype)

    rows = pl.BlockSpec((tb, LRU_WIDTH), lambda i: (i, 0))
    prev8 = pl.BlockSpec((SUBLANES, LRU_WIDTH), lambda i: (jnp.maximum(i * ratio - 1, 0), 0))
    next8 = pl.BlockSpec((SUBLANES, LRU_WIDTH), lambda i: (jnp.minimum((i + 1) * ratio, n_blocks * ratio - 1), 0))
    cw = pl.BlockSpec((CONV_WIDTH, LRU_WIDTH), lambda i: (0, 0))
    return pl.pallas_call(
        body, name="conv_bwd", grid=(n_blocks,),
        in_specs=[prev8, rows, rows, next8, cw], out_specs=[rows, cw],
        out_shape=[jax.ShapeDtypeStruct((T, LRU_WIDTH), MXU_DTYPE), jax.ShapeDtypeStruct((CONV_WIDTH, LRU_WIDTH), F32)],
        compiler_params=_params("arbitrary"),
    )(xr, xr, d_xc, d_xc, conv_w)


def _whole(w):
    return pl.BlockSpec(w.shape, lambda i: (0, 0), pipeline_mode=pl.Buffered(1))


def merge_fwd(mixers, weights, gates, bias):
    T = gates.shape[0]
    tm = _tile(T, 512)
    D = D_MODEL
    contract = (1, 0, 1)

    def body(a_ref, r_ref, c_ref, wa_ref, wr_ref, wc_ref, g_ref, b_ref, m_ref, ba_ref, br_ref, bc_ref):
        acc = None
        for k, (x_ref, w_ref, out_ref) in enumerate(((a_ref, wa_ref, ba_ref), (r_ref, wr_ref, br_ref), (c_ref, wc_ref, bc_ref))):
            cols = slice(k * D, (k + 1) * D)
            branch = _dot(x_ref[...], w_ref[...], 1, contract[k])
            out_ref[...] = branch.astype(out_ref.dtype)
            term = _sigmoid(g_ref[:, cols] + b_ref[:, cols]) * branch
            acc = term if acc is None else acc + term
        m_ref[...] = acc.astype(m_ref.dtype)

    rows = pl.BlockSpec((tm, D), lambda i: (i, 0))
    act = jax.ShapeDtypeStruct((T, D), MXU_DTYPE)
    return pl.pallas_call(
        body, name="merge_fwd", grid=(T // tm,),
        in_specs=[pl.BlockSpec((tm, m.shape[1]), lambda i: (i, 0)) for m in mixers] + [_whole(w) for w in weights]
        + [pl.BlockSpec((tm, 3 * D), lambda i: (i, 0)), pl.BlockSpec((1, 3 * D), lambda i: (0, 0))],
        out_specs=[rows] * 4, out_shape=[act] * 4, compiler_params=_params("parallel"),
    )(*mixers, *weights, gates, bias)


def merge_bwd(dy, w_out, weights, gates, bias, branches):
    T = gates.shape[0]
    tm = _tile(T, 256)
    D = D_MODEL
    contract = (0, 1, 0)

    def body(dy_ref, wo_ref, wa_ref, wr_ref, wc_ref, g_ref, b_ref, ba_ref, br_ref, bc_ref,
             dba_ref, dbr_ref, dbc_ref, da_ref, dr_ref, dc_ref, dg_ref, db_ref):
        @pl.when(pl.program_id(0) == 0)
        def _():
            db_ref[...] = jnp.zeros_like(db_ref)

        dm = _dot(dy_ref[...].astype(MXU_DTYPE), wo_ref[...], 1, 1)
        for k, (w_ref, branch_ref, dbranch_ref, dmixer_ref) in enumerate(
                ((wa_ref, ba_ref, dba_ref, da_ref), (wr_ref, br_ref, dbr_ref, dr_ref), (wc_ref, bc_ref, dbc_ref, dc_ref))):
            cols = slice(k * D, (k + 1) * D)
            s = _sigmoid(g_ref[:, cols] + b_ref[:, cols])
            d_branch = (dm * s).astype(MXU_DTYPE)
            dbranch_ref[...] = d_branch
            dmixer_ref[...] = _dot(d_branch, w_ref[...], 1, contract[k]).astype(dmixer_ref.dtype)
            d_pre = dm * branch_ref[...].astype(F32) * s * (1.0 - s)
            dg_ref[:, cols] = d_pre.astype(dg_ref.dtype)
            db_ref[:, cols] += jnp.sum(d_pre, axis=0, keepdims=True)

    rows = pl.BlockSpec((tm, D), lambda i: (i, 0))
    half = pl.BlockSpec((tm, ATTN_WIDTH), lambda i: (i, 0))
    wide = pl.BlockSpec((tm, 3 * D), lambda i: (i, 0))
    vec = pl.BlockSpec((1, 3 * D), lambda i: (0, 0))
    act = jax.ShapeDtypeStruct((T, D), MXU_DTYPE)
    return pl.pallas_call(
        body, name="merge_bwd", grid=(T // tm,),
        in_specs=[rows, _whole(w_out)] + [_whole(w) for w in weights] + [wide, vec, rows, rows, rows],
        out_specs=[rows, rows, rows, half, rows, half, wide, vec],
        out_shape=[act, act, act, jax.ShapeDtypeStruct((T, ATTN_WIDTH), MXU_DTYPE), jax.ShapeDtypeStruct((T, D), F32),
                   jax.ShapeDtypeStruct((T, MEM_WIDTH), MXU_DTYPE), jax.ShapeDtypeStruct((T, 3 * D), MXU_DTYPE),
                   jax.ShapeDtypeStruct((1, 3 * D), F32)],
        compiler_params=_params("arbitrary"),
    )(dy, w_out, *weights, gates, bias, *branches)


def loss_head(x, g, target):
    T, D = x.shape
    tm = _tile(T, 512)

    def body(x_ref, g_ref, t_ref, loss_ref, dx_ref, dg_ref):
        @pl.when(pl.program_id(0) == 0)
        def _():
            loss_ref[...] = jnp.zeros_like(loss_ref)
            dg_ref[...] = jnp.zeros_like(dg_ref)

        xv = x_ref[...]
        gv = g_ref[...]
        r = lax.rsqrt(jnp.mean(xv * xv, axis=-1, keepdims=True) + EPS)
        xh = xv * r
        err = xh * gv - t_ref[...]
        per_row = jnp.mean(err * err, axis=-1, keepdims=True)
        loss_ref[...] += jnp.broadcast_to(0.5 * jnp.sum(per_row, axis=0, keepdims=True), loss_ref.shape)
        dyv = err * (1.0 / D)
        dg_ref[...] += jnp.sum(dyv * xh, axis=0, keepdims=True)
        gd = dyv * gv
        dx_ref[...] = r * (gd - xh * jnp.mean(gd * xh, axis=-1, keepdims=True))

    rows = pl.BlockSpec((tm, D), lambda i: (i, 0))
    vec = pl.BlockSpec((1, D), lambda i: (0, 0))
    return pl.pallas_call(
        body, name="loss_head", grid=(T // tm,),
        in_specs=[rows, vec, rows], out_specs=[pl.BlockSpec((1, LANES), lambda i: (0, 0)), rows, vec],
        out_shape=[jax.ShapeDtypeStruct((1, LANES), F32), jax.ShapeDtypeStruct((T, D), F32),
                   jax.ShapeDtypeStruct((1, D), F32)],
        compiler_params=_params("arbitrary"),
    )(x, g, target)


def _adamw_math(w, g, m, v):
    m = ADAM_B1 * m + (1.0 - ADAM_B1) * g
    v = ADAM_B2 * v + (1.0 - ADAM_B2) * (g * g)
    m_hat = m / (1.0 - ADAM_B1 ** ADAM_STEP)
    v_hat = v / (1.0 - ADAM_B2 ** ADAM_STEP)
    delta = -ADAM_LR * (m_hat / (jnp.sqrt(v_hat) + ADAM_EPS) + ADAM_WD * w)
    return delta, m, v


def _as_rows(a):
    return a.reshape(-1, a.shape[-1])


def sum_parts(layers):
    n, R, C = layers[0].shape
    depth = len(layers)
    tr = _tile(R, 512 if C <= 1024 else 128)
    blocks = R // tr

    def body(*refs):
        o_ref = refs[-1]
        for l in range(depth):
            @pl.when(pl.program_id(0) == l)
            def _(p_ref=refs[l]):
                acc = p_ref[0].astype(F32)
                for k in range(1, n):
                    acc = acc + p_ref[k].astype(F32)
                o_ref[...] = acc

    in_specs = [pl.BlockSpec((n, tr, C), lambda L, i, l=l: (0, jnp.where(L == l, i, 0), 0)) for l in range(depth)]
    return pl.pallas_call(
        body, name="sum_parts", grid=(depth, blocks), in_specs=in_specs,
        out_specs=pl.BlockSpec((tr, C), lambda L, i: (L * blocks + i, 0)),
        out_shape=jax.ShapeDtypeStruct((depth * R, C), F32), compiler_params=_params("arbitrary", "arbitrary"),
    )(*layers)


def adamw_sum(w, parts, m, v):
    n = len(parts)
    R, C = w.shape
    tr = _tile(R, 512 if n <= 2 and C <= 1024 else 256)

    def body(*refs):
        w_ref, m_ref, v_ref = refs[:3]
        g_ref, d_ref, nm_ref, nv_ref = refs[3 + n:]
        g = refs[3][...]
        for p_ref in refs[4:3 + n]:
            g = g + p_ref[...]
        delta, nm, nv = _adamw_math(w_ref[...], g, m_ref[...], v_ref[...])
        g_ref[...] = g
        d_ref[...] = delta
        nm_ref[...] = nm
        nv_ref[...] = nv

    rows = pl.BlockSpec((tr, C), lambda i: (i, 0))
    shape = jax.ShapeDtypeStruct((R, C), F32)
    return pl.pallas_call(
        body, name="adamw_sum", grid=(R // tr,), in_specs=[rows] * (3 + n), out_specs=[rows] * 4,
        out_shape=[shape] * 4, compiler_params=_params("parallel"),
    )(w, m, v, *parts)


def _place():
    return lax.axis_index("x"), lax.axis_index("y"), lax.axis_index("c")


HBM_SPEC = pl.BlockSpec(memory_space=pltpu.HBM)
SEM_SPEC = pl.BlockSpec(memory_space=pltpu.SEMAPHORE)
DATAFLOW = pltpu.SideEffectType.DATAFLOW_SIDE_EFFECTING


PATTERNS = {"gather": ([(1, 0, 0), (0, 1, 0), (1, 1, 0)], N_CHIPS), "scatter": ([(1, 0, 0), (0, 1, 0), (1, 1, 0)], None),
            "sibling": ([(0, 0, 1)], None),
            "all": ([(fx, fy, fc) for fx in (0, 1) for fy in (0, 1) for fc in (0, 1)][1:], N_DEVICES)}


def _exchange_copies(pattern, srcs, lands, send_sems, recv_sems, local_sems):
    flips, _ = PATTERNS[pattern]
    x, y, c = _place()

    def slot(px, py, pc):
        return 4 * px + 2 * py + pc if pattern == "all" else 2 * px + py

    me = slot(x, y, c)
    outgoing, arriving, local = [], [], []
    for k, (src, land) in enumerate(zip(srcs, lands)):
        for d, (fx, fy, fc) in enumerate(flips):
            peer = (x ^ fx, y ^ fy, c ^ fc)
            pair = dict(send_sem=send_sems.at[len(flips) * k + d], recv_sem=recv_sems.at[len(flips) * k + d],
                        device_id=peer, device_id_type=MESH)
            if pattern == "sibling":
                outgoing.append(pltpu.make_async_remote_copy(src_ref=src, dst_ref=land, **pair))
                arriving.append(outgoing[-1])
                continue
            scatter = pattern == "scatter"
            outgoing.append(pltpu.make_async_remote_copy(
                src_ref=src.at[slot(*peer)] if scatter else src, dst_ref=land.at[me], **pair))
            arriving.append(pltpu.make_async_remote_copy(
                src_ref=src.at[me] if scatter else src, dst_ref=land.at[slot(*peer)], **pair))
        if pattern != "sibling":
            local.append(pltpu.make_async_copy(src.at[me] if pattern == "scatter" else src, land.at[me], local_sems.at[k]))
    return outgoing, arriving, local


def _semaphore_shapes(pattern, n):
    flips, _ = PATTERNS[pattern]
    local = [] if pattern == "sibling" else [pltpu.SemaphoreType.DMA((n,))]
    return [pltpu.SemaphoreType.DMA((len(flips) * n,)), pltpu.SemaphoreType.DMA((len(flips) * n,))] + local


def exchange_start(arrays, *, pattern, name):
    n = len(arrays)
    lead = PATTERNS[pattern][1]
    lands = [lax.empty(a.shape if lead is None else (lead,) + a.shape, a.dtype) for a in arrays]
    sem_shapes = _semaphore_shapes(pattern, n)
    n_sems = len(sem_shapes)

    def body(*refs):
        srcs, zones = refs[:n], refs[n:2 * n]
        sems = list(refs[2 * n:2 * n + n_sems]) + [None]
        token = refs[-1]
        outgoing, _, local = _exchange_copies(pattern, srcs, zones, *sems[:3])
        for cp in outgoing + local:
            cp.start()
        token[...] = jnp.zeros_like(token)

    hbm = lambda a: pltpu.HBM(a.shape, a.dtype)
    outs = pl.pallas_call(
        body, name=name,
        out_shape=(*sem_shapes, *[hbm(a) for a in arrays], *[hbm(a) for a in lands],
                   jax.ShapeDtypeStruct((SUBLANES, LANES), F32)),
        in_specs=[HBM_SPEC] * (2 * n),
        out_specs=(*[SEM_SPEC] * n_sems, *[HBM_SPEC] * (2 * n), pl.BlockSpec(memory_space=pltpu.VMEM)),
        input_output_aliases={i: n_sems + i for i in range(2 * n)},
        compiler_params=pltpu.CompilerParams(has_side_effects=DATAFLOW),
    )(*[pltpu.with_memory_space_constraint(a, pltpu.HBM) for a in list(arrays) + lands])
    return outs[:n_sems], outs[n_sems:n_sems + n], outs[n_sems + n:n_sems + 2 * n], outs[-1]


def exchange_wait(started, after, *, pattern, name, with_sources=False):
    sems, srcs, lands, _ = started
    n = len(srcs)
    n_sems = len(sems)

    def body(*refs):
        src_refs, zones = refs[:n], refs[n:2 * n]
        sem_refs = list(refs[2 * n:2 * n + n_sems]) + [None]
        outgoing, arriving, local = _exchange_copies(pattern, src_refs, zones, *sem_refs[:3])
        for sent, landed in zip(outgoing, arriving):
            sent.wait_send()
            landed.wait_recv()
        for cp in local:
            cp.wait()

    hbm = lambda a: pltpu.HBM(a.shape, a.dtype)
    outs = pl.pallas_call(
        body, name=name, out_shape=[hbm(a) for a in list(srcs) + list(lands)],
        in_specs=[HBM_SPEC] * (2 * n) + [SEM_SPEC] * n_sems + [pl.BlockSpec(memory_space=pl.ANY)],
        out_specs=[HBM_SPEC] * (2 * n), input_output_aliases={i: i for i in range(2 * n)},
        compiler_params=pltpu.CompilerParams(has_side_effects=DATAFLOW),
    )(*srcs, *lands, *sems, after)
    return (outs[:n], outs[n:]) if with_sources else outs[n:]


BIG = ("ffn1_w_gate", "ffn1_w_up", "ffn1_w_down", "w_in", "w_attn_out", "w_lru_out", "w_mem_kv", "w_mem_out",
       "w_out", "ffn2_w_gate", "ffn2_w_up", "ffn2_w_down")
TRANSPOSED = ("ffn1_w_gate", "ffn1_w_up", "w_in", "w_attn_out", "w_mem_out", "ffn2_w_gate", "ffn2_w_up")
SMALL = ("ffn1_norm", "mix_norm", "gate_bias", "attn_sinks", "conv_w", "conv_b", "lru_wa", "lru_ba", "lru_wx", "lru_bx",
         "lru_lambda", "mem_norm", "ffn2_norm", "final_norm")
WEIGHTS = ("ffn1_norm", "ffn1_w_gate", "ffn1_w_up", "ffn1_w_down", "mix_norm", "w_in", "gate_bias", "attn_sinks",
           "w_attn_out", "conv_w", "conv_b", "lru_wa", "lru_ba", "lru_wx", "lru_bx", "lru_lambda", "w_lru_out", "mem_norm",
           "w_mem_kv", "w_mem_out", "w_out", "ffn2_norm", "ffn2_w_gate", "ffn2_w_up", "ffn2_w_down", "final_norm")
SEGMENTS = (("qkv", 0, 768), ("xr", 768, 1792), ("yr", 1792, 2816), ("cq", 2816, 3328), ("gates", 3328, 6400))
PACK_ROW = 1024


STAGES = ("ffn1", "mix", "ffn2")
STAGE_BIG = {"ffn1": ("ffn1_w_gate", "ffn1_w_up", "ffn1_w_down"),
             "mix": ("w_in", "w_attn_out", "w_lru_out", "w_mem_kv", "w_mem_out", "w_out"),
             "ffn2": ("ffn2_w_gate", "ffn2_w_up", "ffn2_w_down")}


def _row_sharded(state, name):
    return jnp.swapaxes(state[name], 1, 2) if name in TRANSPOSED else state[name]


def _join_shards(gathered, name):
    if name == "conv_w":
        return jnp.concatenate([gathered[s] for s in range(N_CHIPS)], axis=-1)
    return gathered.reshape(-1, gathered.shape[-1])


def _split_shards(full):
    return full.reshape(N_CHIPS, -1, full.shape[-1])


def _lane_tiles(w):
    z = jnp.zeros((LRU_WIDTH // LRU_TILE, HEAD_DIM, HEAD_DIM), w.dtype)
    top = jnp.concatenate([w[0::2], z], axis=2)
    bottom = jnp.concatenate([z, w[1::2]], axis=2)
    return jnp.concatenate([top, bottom], axis=1)


def _from_lane_tiles(t):
    pairs = jnp.stack([t[:, :HEAD_DIM, :HEAD_DIM], t[:, HEAD_DIM:, HEAD_DIM:]], axis=1)
    return pairs.reshape(2 * t.shape[0], HEAD_DIM, HEAD_DIM)


def _pack(arrays):
    flat = jnp.concatenate([a.reshape(-1).astype(F32) for a in arrays])
    pad = (-flat.shape[0]) % (SUBLANES * PACK_ROW)
    return jnp.pad(flat, (0, pad)).reshape(-1, PACK_ROW)


def _unpack(packed, shapes):
    flat = packed.reshape(-1)
    out, at = [], 0
    for shape in shapes:
        size = int(np.prod(shape))
        out.append(flat[at:at + size].reshape(shape))
        at += size
    return out


def _ffn_forward(x, h, w_gate_t, w_up_t, w_down, next_norm):
    G, U, A = ffn_up(h, w_gate_t, w_up_t)
    out = mm([A], [w_down], nt=False, out_dtype=F32, res=x, scale=0.5, norm=next_norm, name="ffn_down")
    y, h_next = out if next_norm is not None else (out, None)
    return y, h_next, (x, h, G, U, A)


def _ffn_backward(half, dy, saved, norm, w_gate_t, w_up_t, w_down):
    x, h, G, U, A = saved
    dG, dU = ffn_bwd_act(half, dy, w_down, G, U)
    d_down = mm_tn(A, dy, out_dtype=MXU_DTYPE, b_scale=0.5, name="ffn_dw_down")
    d_gate_t = mm_tn(dG, h, out_dtype=MXU_DTYPE, name="ffn_dw_up")
    d_up_t = mm_tn(dU, h, out_dtype=MXU_DTYPE, name="ffn_dw_up")
    dx, d_norm = mm_rms_bwd([dG, dU], [w_gate_t, w_up_t], x, norm, dy, nt=False, name="ffn_dx")
    return dx, d_norm, d_gate_t, d_up_t, d_down


def kernel(x, mem, ffn1_norm, ffn1_w_gate, ffn1_w_up, ffn1_w_down, mix_norm, w_in, gate_bias, attn_sinks, w_attn_out, conv_w, conv_b, lru_wa, lru_ba, lru_wx, lru_bx, lru_lambda, w_lru_out, mem_norm, w_mem_kv, w_mem_out, w_out, ffn2_norm, ffn2_w_gate, ffn2_w_up, ffn2_w_down, final_norm, loss_target, m_ffn1_norm, m_ffn1_w_gate, m_ffn1_w_up, m_ffn1_w_down, m_mix_norm, m_w_in, m_gate_bias, m_attn_sinks, m_w_attn_out, m_conv_w, m_conv_b, m_lru_wa, m_lru_ba, m_lru_wx, m_lru_bx, m_lru_lambda, m_w_lru_out, m_mem_norm, m_w_mem_kv, m_w_mem_out, m_w_out, m_ffn2_norm, m_ffn2_w_gate, m_ffn2_w_up, m_ffn2_w_down, m_final_norm, v_ffn1_norm, v_ffn1_w_gate, v_ffn1_w_up, v_ffn1_w_down, v_mix_norm, v_w_in, v_gate_bias, v_attn_sinks, v_w_attn_out, v_conv_w, v_conv_b, v_lru_wa, v_lru_ba, v_lru_wx, v_lru_bx, v_lru_lambda, v_w_lru_out, v_mem_norm, v_w_mem_kv, v_w_mem_out, v_w_out, v_ffn2_norm, v_ffn2_w_gate, v_ffn2_w_up, v_ffn2_w_down, v_final_norm):
    args = dict(locals())
    W = {n: args[n] for n in WEIGHTS}
    M = {n: args["m_" + n] for n in WEIGHTS}
    V = {n: args["v_" + n] for n in WEIGHTS}
    chip = 2 * lax.axis_index("x") + lax.axis_index("y")
    x0 = x[0]
    mem0 = mem[0]
    target = loss_target[0]

    def row(v):
        return v.reshape(1, -1)

    def stage_names(stage):
        return STAGE_BIG[stage] + (("conv_w",) if stage == "mix" else ())

    RW, RM, RV = ({n: _row_sharded(state, n) for n in BIG} for state in (W, M, V))
    gathers = {}
    started_all = jnp.zeros((), F32)
    for l in range(DEPTH):
        for stage in STAGES:
            shards = [W[n][l] if n == "conv_w" else RW[n][l].astype(MXU_DTYPE) for n in stage_names(stage)]
            gathers[l, stage] = exchange_start(shards, pattern="gather", name=f"gather_start_{stage}_{l}")
            started_all = started_all + gathers[l, stage][3][0, 0]

    def weights_of(l, stage, after):
        lands = exchange_wait(gathers[l, stage], after, pattern="gather", name=f"gather_wait_{stage}_{l}")
        return {n: _join_shards(g, n) for n, g in zip(stage_names(stage), lands)}

    saved = []
    full = []
    xs = x0
    h = rms_fwd(x0, row(W["ffn1_norm"][0]) + started_all)
    for l in range(DEPTH):
        P = weights_of(l, "ffn1", xs)
        xs, h, ffn1 = _ffn_forward(xs, h, P["ffn1_w_gate"], P["ffn1_w_up"], P["ffn1_w_down"], row(W["mix_norm"][l]))
        x_mix = xs
        P.update(weights_of(l, "mix", xs))
        seg = {name: mm([h], [P["w_in"][lo:hi]], nt=True, out_dtype=F32, tm=1024, name="proj_" + name)
               for name, lo, hi in SEGMENTS}
        attn = attn_fwd(seg["qkv"], W["attn_sinks"][l])
        wa_t = _lane_tiles(W["lru_wa"][l]).astype(MXU_DTYPE)
        wx_t = _lane_tiles(W["lru_wx"][l]).astype(MXU_DTYPE)
        sp = row(jax.nn.softplus(-W["lru_lambda"][l]))
        lru_consts = (P["conv_w"], row(W["conv_b"][l]), wa_t, row(W["lru_ba"][l]), wx_t, row(W["lru_bx"][l]), sp)
        h_lru, rec = lru_fwd(seg["xr"], seg["yr"], *lru_consts)
        mem_n = rms_fwd(mem0, row(W["mem_norm"][l]))
        kv = mm([mem_n], [P["w_mem_kv"]], nt=False, out_dtype=F32, name="mem_kv")
        cross = mem_attn_fwd(seg["cq"], kv)
        bias = row(W["gate_bias"][l])
        branch_weights = (P["w_attn_out"], P["w_lru_out"], P["w_mem_out"])
        merged, *branches = merge_fwd((attn, rec, cross), branch_weights, seg["gates"], bias)
        h_mix = h
        xs, h = mm([merged], [P["w_out"]], nt=False, out_dtype=F32, res=x_mix, norm=row(W["ffn2_norm"][l]), name="mix_out")
        mix = (x_mix, h_mix, seg, attn, lru_consts, h_lru, rec, mem_n, kv, cross, branches, bias, merged)
        P.update(weights_of(l, "ffn2", xs))
        next_norm = row(W["ffn1_norm"][l + 1]) if l + 1 < DEPTH else None
        xs, h, ffn2 = _ffn_forward(xs, h, P["ffn2_w_gate"], P["ffn2_w_up"], P["ffn2_w_down"], next_norm)
        saved.append((ffn1, mix, ffn2))
        full.append(P)

    loss_lanes, dx, d_final = loss_head(xs, row(W["final_norm"]), target)
    loss = lax.psum(loss_lanes[0, 0], ("x", "y", "c"))

    big_grads = {}
    small_grads = {n: [None] * DEPTH for n in SMALL if n != "final_norm"}
    scatters = {}
    token = jnp.zeros((), F32)

    def send_grads(l, stage):
        chunks = [_split_shards(big_grads[n, l]) for n in STAGE_BIG[stage]]
        scatters[l, stage] = exchange_start(chunks, pattern="scatter", name=f"scatter_start_{stage}_{l}")
        return scatters[l, stage][3][0, 0]

    for l in reversed(range(DEPTH)):
        P = full[l]
        ffn1, mix, ffn2 = saved[l]
        dx, g_norm, g_gate, g_up, g_down = _ffn_backward(
            (0.5 + token).reshape(1), dx, ffn2, row(W["ffn2_norm"][l]), P["ffn2_w_gate"], P["ffn2_w_up"], P["ffn2_w_down"])
        small_grads["ffn2_norm"][l] = g_norm
        big_grads["ffn2_w_gate", l], big_grads["ffn2_w_up", l], big_grads["ffn2_w_down", l] = g_gate, g_up, g_down
        token = send_grads(l, "ffn2")

        x_mix, h, seg, attn, lru_consts, h_lru, rec, mem_n, kv, cross, branches, bias, merged = mix
        bias = bias + token
        big_grads["w_out", l] = mm_tn(merged, dx, out_dtype=MXU_DTYPE, name="dw_out")
        d_attn_br, d_lru_br, d_mem_br, d_attn, d_rec, d_cross, d_gates, d_bias = merge_bwd(
            dx, P["w_out"], (P["w_attn_out"], P["w_lru_out"], P["w_mem_out"]), seg["gates"], bias, branches)
        small_grads["gate_bias"][l] = d_bias
        big_grads["w_attn_out", l] = mm_tn(d_attn_br, attn, out_dtype=MXU_DTYPE, name="dw_attn_out")
        big_grads["w_lru_out", l] = mm_tn(rec, d_lru_br, out_dtype=MXU_DTYPE, name="dw_lru_out")
        big_grads["w_mem_out", l] = mm_tn(d_mem_br, cross, out_dtype=MXU_DTYPE, name="dw_mem_out")

        d_qkv, d_sinks = attn_bwd(seg["qkv"], W["attn_sinks"][l], d_attn)
        small_grads["attn_sinks"][l] = d_sinks

        d_xc, d_yr, d_wa_t, d_wx_t, d_ba, d_bx, d_sp, d_cb = lru_bwd(seg["xr"], seg["yr"], h_lru, d_rec, *lru_consts)
        d_xr, d_cw = conv_bwd(seg["xr"], d_xc, lru_consts[0])
        small_grads["conv_w"][l] = d_cw
        small_grads["conv_b"][l] = d_cb
        small_grads["lru_wa"][l] = _from_lane_tiles(d_wa_t)
        small_grads["lru_wx"][l] = _from_lane_tiles(d_wx_t)
        small_grads["lru_ba"][l] = d_ba
        small_grads["lru_bx"][l] = d_bx
        small_grads["lru_lambda"][l] = -d_sp * _sigmoid(-row(W["lru_lambda"][l]))

        d_cq, d_kv = mem_attn_bwd(seg["cq"], kv, d_cross)
        big_grads["w_mem_kv", l] = mm_tn(mem_n, d_kv, out_dtype=MXU_DTYPE, name="dw_mem_kv")
        _, g_mem_norm = mm_rms_bwd([d_kv], [P["w_mem_kv"]], mem0, row(W["mem_norm"][l]), None, nt=True, name="mem_dnorm")
        small_grads["mem_norm"][l] = g_mem_norm

        d_seg = {"qkv": d_qkv, "xr": d_xr, "yr": d_yr, "cq": d_cq, "gates": d_gates}
        big_grads["w_in", l] = jnp.concatenate(
            [mm_tn(d_seg[name], h, out_dtype=MXU_DTYPE, name="dw_in_" + name) for name, _, _ in SEGMENTS], axis=0)
        dx, g_norm = mm_rms_bwd([d_seg[name] for name, _, _ in SEGMENTS], [P["w_in"][lo:hi] for _, lo, hi in SEGMENTS],
                                x_mix, row(W["mix_norm"][l]), dx, nt=False, name="mix_dx")
        small_grads["mix_norm"][l] = g_norm
        token = send_grads(l, "mix")

        dx, g_norm, g_gate, g_up, g_down = _ffn_backward(
            (0.5 + token).reshape(1), dx, ffn1, row(W["ffn1_norm"][l]), P["ffn1_w_gate"], P["ffn1_w_up"], P["ffn1_w_down"])
        small_grads["ffn1_norm"][l] = g_norm
        big_grads["ffn1_w_gate", l], big_grads["ffn1_w_up", l], big_grads["ffn1_w_down", l] = g_gate, g_up, g_down
        token = send_grads(l, "ffn1")

    grad_x = dx[None]

    small_list = [jnp.stack(small_grads[n]) for n in SMALL if n != "final_norm"] + [d_final]
    small_shapes = [(DEPTH,) + W[n].shape[1:] if n != "conv_w" else (DEPTH, CONV_WIDTH, LRU_WIDTH)
                    for n in SMALL if n != "final_norm"] + [W["final_norm"].shape]
    small_exchange = exchange_start([_pack(small_list)], pattern="all", name="small_start")

    arrived = {}
    backward_done = dx[:1, :1] + token + small_exchange[3][0, 0]
    for l in range(DEPTH):
        for stage in STAGES:
            lands = exchange_wait(scatters[l, stage], backward_done, pattern="scatter", name=f"scatter_wait_{stage}_{l}")
            for n, a in zip(STAGE_BIG[stage], lands):
                arrived[n, l] = a.reshape(N_CHIPS, -1, a.shape[-1])
    siblings = {}
    after = jnp.zeros((1, 1), F32)
    for stage in STAGES:
        partial = [sum_parts([arrived[n, l] for l in range(DEPTH)]) for n in STAGE_BIG[stage]]
        siblings[stage] = exchange_start(partial, pattern="sibling", name=f"sibling_start_{stage}")
        after = after + siblings[stage][3][0, 0]
    results = {}
    for stage in STAGES:
        own, theirs = exchange_wait(siblings[stage], after, pattern="sibling", name=f"sibling_wait_{stage}", with_sources=True)
        for n, mine, other in zip(STAGE_BIG[stage], own, theirs):
            outs = adamw_sum(_as_rows(RW[n]), [mine, other], _as_rows(RM[n]), _as_rows(RV[n]))
            after = outs[0]
            outs = [o.reshape(RW[n].shape) for o in outs]
            results[n] = [jnp.swapaxes(o, 1, 2) for o in outs] if n in TRANSPOSED else outs
    everyone = exchange_wait(small_exchange, after, pattern="all", name="small_wait")[0]

    def own_columns(full_conv):
        return lax.dynamic_slice_in_dim(full_conv, chip * (LRU_WIDTH // N_CHIPS), LRU_WIDTH // N_CHIPS, axis=2)

    def packed_state(state):
        arrays = []
        for n in SMALL:
            a = state[n]
            if n == "conv_w":
                a = lax.dynamic_update_slice_in_dim(jnp.zeros((DEPTH, CONV_WIDTH, LRU_WIDTH), F32), a,
                                                    chip * (LRU_WIDTH // N_CHIPS), axis=2)
            arrays.append(a)
        return _pack(arrays)

    small_outs = adamw_sum(packed_state(W), [everyone[s] for s in range(N_DEVICES)], packed_state(M), packed_state(V))
    for kind, packed in enumerate(small_outs):
        for n, a in zip(SMALL, _unpack(packed, small_shapes)):
            results.setdefault(n, [None] * 4)[kind] = own_columns(a) if n == "conv_w" else a.reshape(W[n].shape)

    return (loss, grad_x, *[results[n][0] for n in WEIGHTS], *[results[n][1] for n in WEIGHTS],
            *[results[n][2] for n in WEIGHTS], *[results[n][3] for n in WEIGHTS])
```

```python
import functools

import jax
import jax.numpy as jnp
import numpy as np
from jax import lax
from jax.experimental import pallas as pl
from jax.experimental.pallas import tpu as pltpu

F32 = jnp.float32
MXU_DTYPE = jnp.bfloat16

DEPTH = 4
D_MODEL = 1024
CHUNK = 64
HALO = 2 * CHUNK
N_Q_HEADS = 8
HEAD_DIM = 64
ATTN_WIDTH = 512
KV_WIDTH = 128
LRU_WIDTH = 1024
LRU_TILE = 128
CONV_WIDTH = 4
LRU_C = 8.0
MEM_HEADS = 4
MEM_HEAD_DIM = 128
MEM_WIDTH = 512
D_FF = 2816
EPS = 1e-6
ADAM_LR = 0.001
ADAM_B1 = 0.9
ADAM_B2 = 0.999
ADAM_EPS = 1e-08
ADAM_WD = 0.01
ADAM_STEP = 10

N_CHIPS = 4
N_DEVICES = 8
SUBLANES = 8
LANES = 128
VMEM_LIMIT_BYTES = 56 * 1024 * 1024
NEG_BIG = -1e30
MESH = pl.DeviceIdType.MESH

ALIBI_SLOPES = tuple(float(2.0 ** (-8.0 * (i + 1) / N_Q_HEADS)) for i in range(N_Q_HEADS))


def _params(*semantics):
    return pltpu.CompilerParams(dimension_semantics=semantics, vmem_limit_bytes=VMEM_LIMIT_BYTES)


def _tile(n, pref, unit=SUBLANES):
    if n <= pref:
        return n
    for t in range(pref - pref % unit, 0, -unit):
        if n % t == 0:
            return t
    return n


def _dot(a, b, ca, cb):
    return lax.dot_general(a, b, (((ca,), (cb,)), ((), ())), preferred_element_type=F32)


def _sigmoid(x):
    return 0.5 * jnp.tanh(0.5 * x) + 0.5


def rms_fwd(x, g):
    T, D = x.shape
    tm = _tile(T, 512)

    def body(x_ref, g_ref, h_ref):
        xv = x_ref[...]
        r = lax.rsqrt(jnp.mean(xv * xv, axis=-1, keepdims=True) + EPS)
        h_ref[...] = (xv * r * g_ref[...]).astype(h_ref.dtype)

    return pl.pallas_call(
        body, name="rms_fwd", grid=(T // tm,),
        in_specs=[pl.BlockSpec((tm, D), lambda i: (i, 0)), pl.BlockSpec((1, D), lambda i: (0, 0))],
        out_specs=pl.BlockSpec((tm, D), lambda i: (i, 0)),
        out_shape=jax.ShapeDtypeStruct((T, D), MXU_DTYPE),
        compiler_params=_params("parallel"),
    )(x, g)


def mm_rms_bwd(a_list, b_list, x, g, dy, *, nt, b_rows=None, tm=512, name="mm_rms_bwd"):
    n_a, n_b = len(a_list), len(b_list)
    T, D = x.shape
    tm = _tile(T, tm)
    has_dy = dy is not None

    def body(*refs):
        a_refs, b_refs = refs[:n_a], refs[n_a:n_a + n_b]
        x_ref, g_ref = refs[n_a + n_b:n_a + n_b + 2]
        dx_ref, dg_ref = refs[-2:]
        i = pl.program_id(0)
        dh = None
        for p, a_ref in enumerate(a_refs):
            b = b_refs[p][...] if b_rows is None else b_refs[0][b_rows[p][0]:b_rows[p][1], :]
            d = _dot(a_ref[...].astype(MXU_DTYPE), b, 1, 1 if nt else 0)
            dh = d if dh is None else dh + d
        xv = x_ref[...]
        r = lax.rsqrt(jnp.mean(xv * xv, axis=-1, keepdims=True) + EPS)
        xh = xv * r
        gd = dh * g_ref[...]
        dx = r * (gd - xh * jnp.mean(gd * xh, axis=-1, keepdims=True))
        dx_ref[...] = refs[n_a + n_b + 2][...] + dx if has_dy else dx
        part = jnp.sum(dh * xh, axis=0, keepdims=True)

        @pl.when(i == 0)
        def _():
            dg_ref[...] = part

        @pl.when(i > 0)
        def _():
            dg_ref[...] += part

    row = pl.BlockSpec((tm, D), lambda i: (i, 0))
    vec = pl.BlockSpec((1, D), lambda i: (0, 0))
    in_specs = [pl.BlockSpec((tm, a.shape[1]), lambda i: (i, 0)) for a in a_list]
    in_specs += [pl.BlockSpec(b.shape, lambda i: (0, 0), pipeline_mode=pl.Buffered(1)) for b in b_list]
    in_specs += [row, vec] + ([row] if has_dy else [])
    operands = list(a_list) + list(b_list) + [x, g] + ([dy] if has_dy else [])
    return pl.pallas_call(
        body, name=name, grid=(T // tm,), in_specs=in_specs, out_specs=[row, vec],
        out_shape=[jax.ShapeDtypeStruct((T, D), F32), jax.ShapeDtypeStruct((1, D), F32)],
        compiler_params=_params("arbitrary"),
    )(*operands)


def mm(a_list, b_list, *, nt, out_dtype, res=None, scale=1.0, norm=None, tm=512, tn=1024, name="mm"):
    n_pairs = len(a_list)
    T = a_list[0].shape[0]
    N = b_list[0].shape[0] if nt else b_list[0].shape[1]
    tm = _tile(T, tm)
    tn = _tile(N, tn, LANES)
    has_res = res is not None
    has_norm = norm is not None
    assert not has_norm or tn == N

    def body(*refs):
        a_refs = refs[:n_pairs]
        b_refs = refs[n_pairs:2 * n_pairs]
        extra = list(refs[2 * n_pairs:])
        acc = None
        for a_ref, b_ref in zip(a_refs, b_refs):
            d = _dot(a_ref[...].astype(MXU_DTYPE), b_ref[...], 1, 1 if nt else 0)
            acc = d if acc is None else acc + d
        if has_res:
            acc = extra.pop(0)[...] + scale * acc
        elif scale != 1.0:
            acc = scale * acc
        if has_norm:
            g_ref = extra.pop(0)
            r = lax.rsqrt(jnp.mean(acc * acc, axis=-1, keepdims=True) + EPS)
            extra[1][...] = (acc * r * g_ref[...]).astype(extra[1].dtype)
        extra[0][...] = acc.astype(extra[0].dtype)

    in_specs = [pl.BlockSpec((tm, a.shape[1]), lambda j, i: (i, 0)) for a in a_list]
    if nt:
        in_specs += [pl.BlockSpec((tn, b.shape[1]), lambda j, i: (j, 0)) for b in b_list]
    else:
        in_specs += [pl.BlockSpec((b.shape[0], tn), lambda j, i: (0, j)) for b in b_list]
    out_spec = pl.BlockSpec((tm, tn), lambda j, i: (i, j))
    operands = list(a_list) + list(b_list)
    if has_res:
        in_specs.append(out_spec)
        operands.append(res)
    if has_norm:
        in_specs.append(pl.BlockSpec((1, N), lambda j, i: (0, 0)))
        operands.append(norm)
    out_shape = jax.ShapeDtypeStruct((T, N), out_dtype)
    return pl.pallas_call(
        body, name=name, grid=(N // tn, T // tm), in_specs=in_specs,
        out_specs=[out_spec, out_spec] if has_norm else out_spec,
        out_shape=[out_shape, jax.ShapeDtypeStruct((T, N), MXU_DTYPE)] if has_norm else out_shape,
        compiler_params=_params("parallel", "parallel"),
    )(*operands)


def proj_fwd(h, w_in_t, segments):
    T, D = h.shape
    tm = _tile(T, 256)

    def body(h_ref, w_ref, *out_refs):
        hv = h_ref[...]
        for (_, lo, hi), o_ref in zip(segments, out_refs):
            o_ref[...] = _dot(hv, w_ref[lo:hi, :], 1, 1)

    return pl.pallas_call(
        body, name="proj_fwd", grid=(T // tm,),
        in_specs=[pl.BlockSpec((tm, D), lambda i: (i, 0)),
                  pl.BlockSpec(w_in_t.shape, lambda i: (0, 0), pipeline_mode=pl.Buffered(1))],
        out_specs=[pl.BlockSpec((tm, hi - lo), lambda i: (i, 0)) for _, lo, hi in segments],
        out_shape=[jax.ShapeDtypeStruct((T, hi - lo), F32) for _, lo, hi in segments],
        compiler_params=_params("parallel"),
    )(h, w_in_t)


def mm_tn(a, b, *, out_dtype, b_scale=1.0, tk=1408, tn=1408, tt=None, name="mm_tn"):
    T, K = a.shape
    N = b.shape[1]
    tk = _tile(K, tk, LANES)
    tn = _tile(N, tn, LANES)
    if tt is None:
        tt = 1024 if b.dtype == F32 else 2048
    tt = _tile(T, tt)
    steps = T // tt

    def body(a_ref, b_ref, o_ref, acc_ref):
        t = pl.program_id(2)
        bv = b_ref[...]
        if b_scale != 1.0:
            bv = b_scale * bv
        d = _dot(a_ref[...].astype(MXU_DTYPE), bv.astype(MXU_DTYPE), 0, 0)

        @pl.when(t == 0)
        def _():
            acc_ref[...] = d

        @pl.when(t > 0)
        def _():
            acc_ref[...] += d

        @pl.when(t == steps - 1)
        def _():
            o_ref[...] = acc_ref[...].astype(o_ref.dtype)

    return pl.pallas_call(
        body, name=name, grid=(K // tk, N // tn, steps),
        in_specs=[pl.BlockSpec((tt, tk), lambda p, q, t: (t, p)), pl.BlockSpec((tt, tn), lambda p, q, t: (t, q))],
        out_specs=pl.BlockSpec((tk, tn), lambda p, q, t: (p, q)),
        out_shape=jax.ShapeDtypeStruct((K, N), out_dtype),
        scratch_shapes=[pltpu.VMEM((tk, tn), F32)],
        compiler_params=_params("parallel", "parallel", "arbitrary"),
    )(a, b)


def _lane_chunks(n, width=3 * LANES):
    return [slice(lo, min(lo + width, n)) for lo in range(0, n, width)]


def ffn_up(h, w_gate_t, w_up_t):
    T, D = h.shape
    F = w_gate_t.shape[0]
    tm = _tile(T, 512)
    tn = _tile(F, 1408, LANES)

    def body(h_ref, wg_ref, wu_ref, g_ref, u_ref, a_ref):
        hv = h_ref[...]
        G = _dot(hv, wg_ref[...], 1, 1)
        U = _dot(hv, wu_ref[...], 1, 1)
        g_ref[...] = G.astype(g_ref.dtype)
        u_ref[...] = U.astype(u_ref.dtype)
        a_ref[...] = (G * _sigmoid(G) * U).astype(a_ref.dtype)

    w_spec = pl.BlockSpec((tn, D), lambda j, i: (j, 0))
    o_spec = pl.BlockSpec((tm, tn), lambda j, i: (i, j))
    return pl.pallas_call(
        body, name="ffn_up", grid=(F // tn, T // tm),
        in_specs=[pl.BlockSpec((tm, D), lambda j, i: (i, 0)), w_spec, w_spec],
        out_specs=[o_spec, o_spec, o_spec],
        out_shape=[jax.ShapeDtypeStruct((T, F), MXU_DTYPE)] * 3,
        compiler_params=_params("parallel", "parallel"),
    )(h, w_gate_t, w_up_t)


def ffn_bwd_act(half, dy, w_down, G, U):
    T, D = dy.shape
    F = w_down.shape[0]
    tm = _tile(T, 512)

    def body(half_ref, dy_ref, wd_ref, g_ref, u_ref, dg_ref, du_ref):
        d_out = (half_ref[0] * dy_ref[...]).astype(MXU_DTYPE)
        for cols in _lane_chunks(F):
            dA = _dot(d_out, wd_ref[cols, :], 1, 1)
            Gv = g_ref[:, cols].astype(F32)
            s = _sigmoid(Gv)
            du_ref[:, cols] = (dA * (Gv * s)).astype(du_ref.dtype)
            dg_ref[:, cols] = (dA * u_ref[:, cols].astype(F32) * (s * (1.0 + Gv * (1.0 - s)))).astype(dg_ref.dtype)

    o_spec = pl.BlockSpec((tm, F), lambda i: (i, 0))
    return pl.pallas_call(
        body, name="ffn_bwd_act", grid=(T // tm,),
        in_specs=[pl.BlockSpec(memory_space=pltpu.SMEM), pl.BlockSpec((tm, D), lambda i: (i, 0)),
                  pl.BlockSpec(w_down.shape, lambda i: (0, 0), pipeline_mode=pl.Buffered(1)), o_spec, o_spec],
        out_specs=[o_spec, o_spec],
        out_shape=[jax.ShapeDtypeStruct((T, F), MXU_DTYPE), jax.ShapeDtypeStruct((T, F), MXU_DTYPE)],
        compiler_params=_params("parallel"),
    )(half, dy, w_down, G, U)


ATTN_ROWS = 256


def _head_variants(kv128):
    lane = lax.broadcasted_iota(jnp.int32, kv128.shape, 1)
    low = lane < HEAD_DIM
    rolled = pltpu.roll(kv128, HEAD_DIM, 1)
    out = {}
    for j in (0, 1):
        for pos in (0, 1):
            src = kv128 if j == pos else rolled
            out[j, pos] = jnp.where(low if pos == 0 else jnp.logical_not(low), src, 0.0).astype(MXU_DTYPE)
    return out


def _fold_variant(d128, j, pos):
    lane = lax.broadcasted_iota(jnp.int32, d128.shape, 1)
    low = lane < HEAD_DIM
    kept = jnp.where(low if pos == 0 else jnp.logical_not(low), d128, 0.0)
    return kept if j == pos else pltpu.roll(kept, HEAD_DIM, 1)


def _attn_mask(block, rows, keys):
    r = lax.broadcasted_iota(jnp.int32, (rows, keys), 0)
    c = lax.broadcasted_iota(jnp.int32, (rows, keys), 1)
    q_chunk = r // CHUNK
    k_chunk = c // CHUNK - HALO // CHUNK
    first_chunk = jnp.where(block > 0, -(HALO // CHUNK), 0)
    valid = (k_chunk <= q_chunk) & (k_chunk >= q_chunk - HALO // CHUNK) & (k_chunk >= first_chunk)
    dist = jnp.abs(r + HALO - c).astype(F32)
    return valid, dist


def _attn_probs(q128, k_var, head, sink, valid, dist):
    s = _dot(q128, k_var, 1, 1) * (HEAD_DIM ** -0.5) - ALIBI_SLOPES[head] * dist
    s = jnp.where(valid, s, NEG_BIG)
    mx = jnp.maximum(jnp.max(s, axis=-1, keepdims=True), sink)
    p = jnp.exp(s - mx)
    p_sink = jnp.exp(sink - mx)
    inv = 1.0 / (jnp.sum(p, axis=-1, keepdims=True) + p_sink)
    return p * inv, p_sink * inv


def _attn_specs(T, tq, order):
    ratio = tq // HALO
    q_spec = pl.BlockSpec((tq, ATTN_WIDTH), lambda i: (order(i), 0))
    cur = lambda col: pl.BlockSpec((tq, KV_WIDTH), lambda i: (order(i), col))
    prev = lambda col: pl.BlockSpec((HALO, KV_WIDTH), lambda i: (jnp.maximum(order(i) * ratio - 1, 0), col))
    k_col, v_col = ATTN_WIDTH // KV_WIDTH, ATTN_WIDTH // KV_WIDTH + 1
    return [pl.BlockSpec(memory_space=pltpu.SMEM), q_spec, prev(k_col), cur(k_col), prev(v_col), cur(v_col)]


def attn_fwd(qkv, sinks):
    T = qkv.shape[0]
    tq = _tile(T, ATTN_ROWS)
    keys = tq + HALO

    def body(sink_ref, q_ref, kp_ref, kc_ref, vp_ref, vc_ref, o_ref):
        block = pl.program_id(0)
        k_var = _head_variants(jnp.concatenate([kp_ref[...], kc_ref[...]], axis=0))
        v_var = _head_variants(jnp.concatenate([vp_ref[...], vc_ref[...]], axis=0))
        valid, dist = _attn_mask(block, tq, keys)
        for pair in range(N_Q_HEADS // 2):
            q128 = q_ref[:, pair * LANES:(pair + 1) * LANES].astype(MXU_DTYPE)
            acc = None
            for pos in (0, 1):
                head = 2 * pair + pos
                j = head // 4
                p, _ = _attn_probs(q128, k_var[j, pos], head, sink_ref[head], valid, dist)
                d = _dot(p.astype(MXU_DTYPE), v_var[j, pos], 1, 0)
                acc = d if acc is None else acc + d
            o_ref[:, pair * LANES:(pair + 1) * LANES] = acc.astype(o_ref.dtype)

    return pl.pallas_call(
        body, name="attn_fwd", grid=(T // tq,),
        in_specs=_attn_specs(T, tq, lambda i: i),
        out_specs=pl.BlockSpec((tq, ATTN_WIDTH), lambda i: (i, 0)),
        out_shape=jax.ShapeDtypeStruct((T, ATTN_WIDTH), MXU_DTYPE),
        compiler_params=_params("parallel"),
    )(sinks, qkv, qkv, qkv, qkv, qkv)


def attn_bwd(qkv, sinks, d_out):
    T = qkv.shape[0]
    tq = _tile(T, ATTN_ROWS)
    keys = tq + HALO
    n_blocks = T // tq
    order = lambda i: n_blocks - 1 - i

    def body(sink_ref, q_ref, kp_ref, kc_ref, vp_ref, vc_ref, do_ref, dqkv_ref, dsink_ref, carry_ref):
        step = pl.program_id(0)
        block = order(step)
        k_var = _head_variants(jnp.concatenate([kp_ref[...], kc_ref[...]], axis=0))
        v_var = _head_variants(jnp.concatenate([vp_ref[...], vc_ref[...]], axis=0))
        valid, dist = _attn_mask(block, tq, keys)

        @pl.when(step == 0)
        def _():
            carry_ref[...] = jnp.zeros_like(carry_ref)
            dsink_ref[...] = jnp.zeros_like(dsink_ref)

        dk = jnp.zeros((keys, KV_WIDTH), F32)
        dv = jnp.zeros((keys, KV_WIDTH), F32)
        for pair in range(N_Q_HEADS // 2):
            q128 = q_ref[:, pair * LANES:(pair + 1) * LANES].astype(MXU_DTYPE)
            do128 = do_ref[:, pair * LANES:(pair + 1) * LANES].astype(MXU_DTYPE)
            dq128 = None
            for pos in (0, 1):
                head = 2 * pair + pos
                j = head // 4
                p, p_sink = _attn_probs(q128, k_var[j, pos], head, sink_ref[head], valid, dist)
                dp = _dot(do128, v_var[j, pos], 1, 1)
                delta = jnp.sum(p * dp, axis=-1, keepdims=True)
                ds = (p * (dp - delta) * (HEAD_DIM ** -0.5)).astype(MXU_DTYPE)
                d = _dot(ds, k_var[j, pos], 1, 0)
                dq128 = d if dq128 is None else dq128 + d
                dk = dk + _fold_variant(_dot(ds, q128, 0, 0), j, pos)
                dv = dv + _fold_variant(_dot(p.astype(MXU_DTYPE), do128, 0, 0), j, pos)
                dsink_ref[pl.ds(head, 1), :] += jnp.broadcast_to(
                    -jnp.sum(p_sink * delta, axis=0, keepdims=True), (1, LANES))
            dqkv_ref[:, pair * LANES:(pair + 1) * LANES] = dq128.astype(dqkv_ref.dtype)
        carried = jnp.concatenate([jnp.zeros((tq - HALO, 2 * KV_WIDTH), F32), carry_ref[...]], axis=0)
        dkv = jnp.concatenate([dk, dv], axis=1)
        dqkv_ref[:, ATTN_WIDTH:] = (dkv[HALO:] + carried).astype(dqkv_ref.dtype)
        carry_ref[...] = dkv[:HALO]

    out_rows = pl.BlockSpec((tq, ATTN_WIDTH + 2 * KV_WIDTH), lambda i: (order(i), 0))
    dqkv, dsink = pl.pallas_call(
        body, name="attn_bwd", grid=(n_blocks,),
        in_specs=_attn_specs(T, tq, order) + [pl.BlockSpec((tq, ATTN_WIDTH), lambda i: (order(i), 0))],
        out_specs=[out_rows, pl.BlockSpec((N_Q_HEADS, LANES), lambda i: (0, 0))],
        out_shape=[jax.ShapeDtypeStruct((T, ATTN_WIDTH + 2 * KV_WIDTH), MXU_DTYPE),
                   jax.ShapeDtypeStruct((N_Q_HEADS, LANES), F32)],
        scratch_shapes=[pltpu.VMEM((HALO, 2 * KV_WIDTH), F32)],
        compiler_params=_params("arbitrary"),
    )(sinks, qkv, qkv, qkv, qkv, qkv, d_out)
    return dqkv, dsink[:, 0]


def _mem_probs(q, k):
    s = _dot(q, k, 1, 1) * (MEM_HEAD_DIM ** -0.5)
    p = jnp.exp(s - jnp.max(s, axis=-1, keepdims=True))
    return p / jnp.sum(p, axis=-1, keepdims=True)


def mem_attn_fwd(cq, kv):
    T = cq.shape[0]
    M = kv.shape[0]
    tq = _tile(T, 512)

    def body(q_ref, kv_ref, o_ref):
        for h in range(MEM_HEADS):
            lo, hi = h * MEM_HEAD_DIM, (h + 1) * MEM_HEAD_DIM
            p = _mem_probs(q_ref[:, lo:hi].astype(MXU_DTYPE), kv_ref[:, lo:hi].astype(MXU_DTYPE))
            v = kv_ref[:, MEM_WIDTH + lo:MEM_WIDTH + hi].astype(MXU_DTYPE)
            o_ref[:, lo:hi] = _dot(p.astype(MXU_DTYPE), v, 1, 0).astype(o_ref.dtype)

    return pl.pallas_call(
        body, name="mem_attn_fwd", grid=(T // tq,),
        in_specs=[pl.BlockSpec((tq, MEM_WIDTH), lambda i: (i, 0)), pl.BlockSpec((M, 2 * MEM_WIDTH), lambda i: (0, 0))],
        out_specs=pl.BlockSpec((tq, MEM_WIDTH), lambda i: (i, 0)),
        out_shape=jax.ShapeDtypeStruct((T, MEM_WIDTH), MXU_DTYPE),
        compiler_params=_params("parallel"),
    )(cq, kv)


def mem_attn_bwd(cq, kv, d_out):
    T = cq.shape[0]
    M = kv.shape[0]
    tq = _tile(T, 512)

    def body(q_ref, kv_ref, do_ref, dq_ref, dkv_ref):
        @pl.when(pl.program_id(0) == 0)
        def _():
            dkv_ref[...] = jnp.zeros_like(dkv_ref)

        for h in range(MEM_HEADS):
            lo, hi = h * MEM_HEAD_DIM, (h + 1) * MEM_HEAD_DIM
            q = q_ref[:, lo:hi].astype(MXU_DTYPE)
            k = kv_ref[:, lo:hi].astype(MXU_DTYPE)
            v = kv_ref[:, MEM_WIDTH + lo:MEM_WIDTH + hi].astype(MXU_DTYPE)
            do = do_ref[:, lo:hi].astype(MXU_DTYPE)
            p = _mem_probs(q, k)
            dp = _dot(do, v, 1, 1)
            ds = (p * (dp - jnp.sum(p * dp, axis=-1, keepdims=True)) * (MEM_HEAD_DIM ** -0.5)).astype(MXU_DTYPE)
            dq_ref[:, lo:hi] = _dot(ds, k, 1, 0).astype(dq_ref.dtype)
            dkv_ref[:, lo:hi] += _dot(ds, q, 0, 0)
            dkv_ref[:, MEM_WIDTH + lo:MEM_WIDTH + hi] += _dot(p.astype(MXU_DTYPE), do, 0, 0)

    rows = pl.BlockSpec((tq, MEM_WIDTH), lambda i: (i, 0))
    whole = pl.BlockSpec((M, 2 * MEM_WIDTH), lambda i: (0, 0))
    return pl.pallas_call(
        body, name="mem_attn_bwd", grid=(T // tq,),
        in_specs=[rows, whole, rows], out_specs=[rows, whole],
        out_shape=[jax.ShapeDtypeStruct((T, MEM_WIDTH), MXU_DTYPE), jax.ShapeDtypeStruct((M, 2 * MEM_WIDTH), F32)],
        compiler_params=_params("arbitrary"),
    )(cq, kv, d_out)


LRU_ROWS = 256


def _shift_down(ext, k, rows):
    return pltpu.roll(ext, k, 0)[SUBLANES:SUBLANES + rows] if k else ext[SUBLANES:SUBLANES + rows]


def _shift_up(ext, k, rows):
    return pltpu.roll(ext, rows + SUBLANES - k, 0)[:rows] if k else ext[:rows]


def _conv(x_ext, conv_w_ref, conv_b_ref, rows):
    xc = conv_b_ref[...] + conv_w_ref[CONV_WIDTH - 1:CONV_WIDTH, :] * _shift_down(x_ext, 0, rows)
    for j in range(CONV_WIDTH - 1):
        xc = xc + conv_w_ref[j:j + 1, :] * _shift_down(x_ext, CONV_WIDTH - 1 - j, rows)
    return xc


def _block_matmul(x, w_ref, cb):
    return jnp.concatenate(
        [_dot(x[:, c * LANES:(c + 1) * LANES].astype(MXU_DTYPE), w_ref[c], 1, cb)
         for c in range(LRU_WIDTH // LRU_TILE)], axis=1)


def _lru_gates(block, xc, wa_ref, ba_ref, wx_ref, bx_ref, sp_ref):
    rows = xc.shape[0]
    ra = _sigmoid(_block_matmul(xc, wa_ref, 0) + ba_ref[...])
    gi = _sigmoid(_block_matmul(xc, wx_ref, 0) + bx_ref[...])
    log_a = -LRU_C * ra * sp_ref[...]
    a = jnp.exp(log_a)
    one_minus = 1.0 - jnp.exp(2.0 * log_a)
    mult = jnp.sqrt(jnp.maximum(one_minus, 0.0))
    row = lax.broadcasted_iota(jnp.int32, (rows, 1), 0)
    first = row == jnp.where(block == 0, 0, -1)
    mult = jnp.where(first, 1.0, mult)
    return ra, gi, a, one_minus, mult, first


def _gelu(x):
    inner = np.sqrt(2.0 / np.pi).astype(np.float32) * (x + 0.044715 * (x * x * x))
    return 0.5 * x * (1.0 + jnp.tanh(inner))


def _gelu_grad(x):
    c = np.sqrt(2.0 / np.pi).astype(np.float32)
    t = jnp.tanh(c * (x + 0.044715 * (x * x * x)))
    return 0.5 * (1.0 + t) + 0.5 * x * (1.0 - t * t) * c * (1.0 + 3.0 * 0.044715 * (x * x))


def _x_ext(block, prev_ref, cur_ref):
    prev = jnp.where(block > 0, prev_ref[...], 0.0)
    return jnp.concatenate([prev, cur_ref[...]], axis=0)


def _lru_in_specs(tb, order, n_rows):
    ratio = tb // SUBLANES
    rows = pl.BlockSpec((tb, LRU_WIDTH), lambda i: (order(i), 0))
    prev8 = pl.BlockSpec((SUBLANES, LRU_WIDTH), lambda i: (jnp.maximum(order(i) * ratio - 1, 0), 0))
    vec = pl.BlockSpec((1, LRU_WIDTH), lambda i: (0, 0))
    conv_w = pl.BlockSpec((CONV_WIDTH, LRU_WIDTH), lambda i: (0, 0))
    tiles = pl.BlockSpec((LRU_WIDTH // LRU_TILE, LRU_TILE, LRU_TILE), lambda i: (0, 0, 0))
    return rows, prev8, vec, conv_w, tiles


def _linear_scan(a, b, carry, *, reverse):
    rows, width = a.shape
    groups = rows // SUBLANES
    a3 = a.reshape(groups, SUBLANES, width)
    b3 = b.reshape(groups, SUBLANES, width)
    sub = lax.broadcasted_iota(jnp.int32, (1, SUBLANES, 1), 1)
    shift = 1
    while shift < SUBLANES:
        keep = (sub < SUBLANES - shift) if reverse else (sub >= shift)
        amount = SUBLANES - shift if reverse else shift
        b3 = b3 + a3 * jnp.where(keep, pltpu.roll(b3, amount, 1), 0.0)
        a3 = a3 * jnp.where(keep, pltpu.roll(a3, amount, 1), 1.0)
        shift *= 2
    out = [None] * groups
    edge = 0 if reverse else SUBLANES - 1
    for g in (reversed(range(groups)) if reverse else range(groups)):
        out[g] = b3[g] + a3[g] * carry
        carry = out[g][edge:edge + 1, :]
    return jnp.concatenate(out, axis=0), carry


def lru_fwd(xr, yr, conv_w, conv_b, wa_t, ba, wx_t, bx, sp):
    T = xr.shape[0]
    tb = _tile(T, LRU_ROWS)

    def body(xp_ref, x_ref, y_ref, cw_ref, cb_ref, wa_ref, ba_ref, wx_ref, bx_ref, sp_ref, h_ref, r_ref, carry_ref):
        block = pl.program_id(0)
        xc = _conv(_x_ext(block, xp_ref, x_ref), cw_ref, cb_ref, tb)
        _, gi, a, _, mult, _ = _lru_gates(block, xc, wa_ref, ba_ref, wx_ref, bx_ref, sp_ref)
        b = mult * gi * xc

        @pl.when(block == 0)
        def _():
            carry_ref[...] = jnp.zeros_like(carry_ref)

        h, carry_ref[...] = _linear_scan(a, b, carry_ref[...], reverse=False)
        h_ref[...] = h
        r_ref[...] = (h * _gelu(y_ref[...])).astype(r_ref.dtype)

    rows, prev8, vec, cw, tiles = _lru_in_specs(tb, lambda i: i, T)
    return pl.pallas_call(
        body, name="lru_fwd", grid=(T // tb,),
        in_specs=[prev8, rows, rows, cw, vec, tiles, vec, tiles, vec, vec],
        out_specs=[rows, rows],
        out_shape=[jax.ShapeDtypeStruct((T, LRU_WIDTH), F32), jax.ShapeDtypeStruct((T, LRU_WIDTH), MXU_DTYPE)],
        scratch_shapes=[pltpu.VMEM((1, LRU_WIDTH), F32)],
        compiler_params=_params("arbitrary"),
    )(xr, xr, yr, conv_w, conv_b, wa_t, ba, wx_t, bx, sp)


def lru_bwd(xr, yr, h, d_r, conv_w, conv_b, wa_t, ba, wx_t, bx, sp):
    T = xr.shape[0]
    tb = _tile(T, LRU_ROWS)
    n_blocks = T // tb
    order = lambda i: n_blocks - 1 - i
    n_tiles = LRU_WIDTH // LRU_TILE

    def body(xp_ref, x_ref, y_ref, hp_ref, h_ref, dr_ref, cw_ref, cb_ref, wa_ref, ba_ref, wx_ref, bx_ref, sp_ref,
             dx_ref, dy_ref, dwa_ref, dwx_ref, dba_ref, dbx_ref, dsp_ref, dcb_ref, dcw_ref, lam_ref, a_next_ref, dxc_next_ref):
        step = pl.program_id(0)
        block = order(step)

        @pl.when(step == 0)
        def _():
            for ref in (lam_ref, a_next_ref, dxc_next_ref, dwa_ref, dwx_ref, dba_ref, dbx_ref, dsp_ref, dcb_ref, dcw_ref):
                ref[...] = jnp.zeros_like(ref)

        x_ext = _x_ext(block, xp_ref, x_ref)
        xc = _conv(x_ext, cw_ref, cb_ref, tb)
        ra, gi, a, one_minus, mult, first = _lru_gates(block, xc, wa_ref, ba_ref, wx_ref, bx_ref, sp_ref)
        yv = y_ref[...]
        hv = h_ref[...]
        drv = dr_ref[...].astype(F32)
        dy_ref[...] = (drv * hv * _gelu_grad(yv)).astype(dy_ref.dtype)

        row = lax.broadcasted_iota(jnp.int32, (tb, 1), 0)
        coef = jnp.where(row == tb - 1, a_next_ref[...], pltpu.roll(a, tb - 1, 0))
        lam, lam_ref[...] = _linear_scan(coef, drv * _gelu(yv), lam_ref[...], reverse=True)
        a_next_ref[...] = a[0:1, :]

        h_prev = _shift_down(jnp.concatenate([jnp.where(block > 0, hp_ref[...], 0.0), hv], axis=0), 1, tb)
        d_a = lam * h_prev
        d_mult = jnp.where(first, 0.0, lam * gi * xc)
        d_gi = lam * mult * xc
        d_xc = lam * mult * gi
        safe = one_minus > 0.0
        d_log_a = d_a * a + d_mult * jnp.where(safe, -(1.0 - one_minus) * lax.rsqrt(jnp.where(safe, one_minus, 1.0)), 0.0)
        d_za = d_log_a * (-LRU_C * sp_ref[...]) * ra * (1.0 - ra)
        d_zx = d_gi * gi * (1.0 - gi)
        dsp_ref[...] += jnp.sum(d_log_a * (-LRU_C * ra), axis=0, keepdims=True)
        dba_ref[...] += jnp.sum(d_za, axis=0, keepdims=True)
        dbx_ref[...] += jnp.sum(d_zx, axis=0, keepdims=True)
        d_xc = d_xc + _block_matmul(d_za, wa_ref, 1) + _block_matmul(d_zx, wx_ref, 1)
        dcb_ref[...] += jnp.sum(d_xc, axis=0, keepdims=True)
        d_ext = jnp.concatenate([d_xc, dxc_next_ref[...]], axis=0)
        dxc_next_ref[...] = d_xc[:SUBLANES]
        dx = None
        for j in range(CONV_WIDTH):
            k = CONV_WIDTH - 1 - j
            term = cw_ref[j:j + 1, :] * _shift_up(d_ext, k, tb)
            dx = term if dx is None else dx + term
            dcw_ref[j:j + 1, :] += jnp.sum(d_xc * _shift_down(x_ext, k, tb), axis=0, keepdims=True)
        dx_ref[...] = dx.astype(dx_ref.dtype)
        xc_m = xc.astype(MXU_DTYPE)
        for c in range(n_tiles):
            lanes = slice(c * LANES, (c + 1) * LANES)
            dwa_ref[c] += _dot(xc_m[:, lanes], d_za[:, lanes].astype(MXU_DTYPE), 0, 0)
            dwx_ref[c] += _dot(xc_m[:, lanes], d_zx[:, lanes].astype(MXU_DTYPE), 0, 0)

    rows, prev8, vec, cw, tiles = _lru_in_specs(tb, order, T)
    return pl.pallas_call(
        body, name="lru_bwd", grid=(n_blocks,),
        in_specs=[prev8, rows, rows, prev8, rows, rows, cw, vec, tiles, vec, tiles, vec, vec],
        out_specs=[rows, rows, tiles, tiles, vec, vec, vec, vec, cw],
        out_shape=[jax.ShapeDtypeStruct((T, LRU_WIDTH), MXU_DTYPE), jax.ShapeDtypeStruct((T, LRU_WIDTH), MXU_DTYPE),
                   jax.ShapeDtypeStruct((n_tiles, LRU_TILE, LRU_TILE), F32),
                   jax.ShapeDtypeStruct((n_tiles, LRU_TILE, LRU_TILE), F32)]
        + [jax.ShapeDtypeStruct((1, LRU_WIDTH), F32)] * 4 + [jax.ShapeDtypeStruct((CONV_WIDTH, LRU_WIDTH), F32)],
        scratch_shapes=[pltpu.VMEM((1, LRU_WIDTH), F32), pltpu.VMEM((1, LRU_WIDTH), F32),
                        pltpu.VMEM((SUBLANES, LRU_WIDTH), F32)],
        compiler_params=_params("arbitrary"),
    )(xr, xr, yr, h, h, d_r, conv_w, conv_b, wa_t, ba, wx_t, bx, sp)


def _whole(w):
    return pl.BlockSpec(w.shape, lambda i: (0, 0), pipeline_mode=pl.Buffered(1))


def merge_fwd(mixers, weights, gates, bias):
    T = gates.shape[0]
    tm = _tile(T, 512)
    D = D_MODEL
    contract = (1, 0, 1)

    def body(a_ref, r_ref, c_ref, wa_ref, wr_ref, wc_ref, g_ref, b_ref, m_ref, ba_ref, br_ref, bc_ref):
        acc = None
        for k, (x_ref, w_ref, out_ref) in enumerate(((a_ref, wa_ref, ba_ref), (r_ref, wr_ref, br_ref), (c_ref, wc_ref, bc_ref))):
            cols = slice(k * D, (k + 1) * D)
            branch = _dot(x_ref[...], w_ref[...], 1, contract[k])
            out_ref[...] = branch.astype(out_ref.dtype)
            term = _sigmoid(g_ref[:, cols] + b_ref[:, cols]) * branch
            acc = term if acc is None else acc + term
        m_ref[...] = acc.astype(m_ref.dtype)

    rows = pl.BlockSpec((tm, D), lambda i: (i, 0))
    act = jax.ShapeDtypeStruct((T, D), MXU_DTYPE)
    return pl.pallas_call(
        body, name="merge_fwd", grid=(T // tm,),
        in_specs=[pl.BlockSpec((tm, m.shape[1]), lambda i: (i, 0)) for m in mixers] + [_whole(w) for w in weights]
        + [pl.BlockSpec((tm, 3 * D), lambda i: (i, 0)), pl.BlockSpec((1, 3 * D), lambda i: (0, 0))],
        out_specs=[rows] * 4, out_shape=[act] * 4, compiler_params=_params("parallel"),
    )(*mixers, *weights, gates, bias)


def merge_bwd(dy, w_out, weights, gates, bias, branches):
    T = gates.shape[0]
    tm = _tile(T, 256)
    D = D_MODEL
    contract = (0, 1, 0)

    def body(dy_ref, wo_ref, wa_ref, wr_ref, wc_ref, g_ref, b_ref, ba_ref, br_ref, bc_ref,
             dba_ref, dbr_ref, dbc_ref, da_ref, dr_ref, dc_ref, dg_ref, db_ref):
        @pl.when(pl.program_id(0) == 0)
        def _():
            db_ref[...] = jnp.zeros_like(db_ref)

        dm = _dot(dy_ref[...].astype(MXU_DTYPE), wo_ref[...], 1, 1)
        for k, (w_ref, branch_ref, dbranch_ref, dmixer_ref) in enumerate(
                ((wa_ref, ba_ref, dba_ref, da_ref), (wr_ref, br_ref, dbr_ref, dr_ref), (wc_ref, bc_ref, dbc_ref, dc_ref))):
            cols = slice(k * D, (k + 1) * D)
            s = _sigmoid(g_ref[:, cols] + b_ref[:, cols])
            d_branch = (dm * s).astype(MXU_DTYPE)
            dbranch_ref[...] = d_branch
            dmixer_ref[...] = _dot(d_branch, w_ref[...], 1, contract[k]).astype(dmixer_ref.dtype)
            d_pre = dm * branch_ref[...].astype(F32) * s * (1.0 - s)
            dg_ref[:, cols] = d_pre.astype(dg_ref.dtype)
            db_ref[:, cols] += jnp.sum(d_pre, axis=0, keepdims=True)

    rows = pl.BlockSpec((tm, D), lambda i: (i, 0))
    half = pl.BlockSpec((tm, ATTN_WIDTH), lambda i: (i, 0))
    wide = pl.BlockSpec((tm, 3 * D), lambda i: (i, 0))
    vec = pl.BlockSpec((1, 3 * D), lambda i: (0, 0))
    act = jax.ShapeDtypeStruct((T, D), MXU_DTYPE)
    return pl.pallas_call(
        body, name="merge_bwd", grid=(T // tm,),
        in_specs=[rows, _whole(w_out)] + [_whole(w) for w in weights] + [wide, vec, rows, rows, rows],
        out_specs=[rows, rows, rows, half, rows, half, wide, vec],
        out_shape=[act, act, act, jax.ShapeDtypeStruct((T, ATTN_WIDTH), MXU_DTYPE), jax.ShapeDtypeStruct((T, D), F32),
                   jax.ShapeDtypeStruct((T, MEM_WIDTH), MXU_DTYPE), jax.ShapeDtypeStruct((T, 3 * D), MXU_DTYPE),
                   jax.ShapeDtypeStruct((1, 3 * D), F32)],
        compiler_params=_params("arbitrary"),
    )(dy, w_out, *weights, gates, bias, *branches)


def loss_head(x, g, target):
    T, D = x.shape
    tm = _tile(T, 512)

    def body(x_ref, g_ref, t_ref, loss_ref, dx_ref, dg_ref):
        @pl.when(pl.program_id(0) == 0)
        def _():
            loss_ref[...] = jnp.zeros_like(loss_ref)
            dg_ref[...] = jnp.zeros_like(dg_ref)

        xv = x_ref[...]
        gv = g_ref[...]
        r = lax.rsqrt(jnp.mean(xv * xv, axis=-1, keepdims=True) + EPS)
        xh = xv * r
        err = xh * gv - t_ref[...]
        per_row = jnp.mean(err * err, axis=-1, keepdims=True)
        loss_ref[...] += jnp.broadcast_to(0.5 * jnp.sum(per_row, axis=0, keepdims=True), loss_ref.shape)
        dyv = err * (1.0 / D)
        dg_ref[...] += jnp.sum(dyv * xh, axis=0, keepdims=True)
        gd = dyv * gv
        dx_ref[...] = r * (gd - xh * jnp.mean(gd * xh, axis=-1, keepdims=True))

    rows = pl.BlockSpec((tm, D), lambda i: (i, 0))
    vec = pl.BlockSpec((1, D), lambda i: (0, 0))
    return pl.pallas_call(
        body, name="loss_head", grid=(T // tm,),
        in_specs=[rows, vec, rows], out_specs=[pl.BlockSpec((1, LANES), lambda i: (0, 0)), rows, vec],
        out_shape=[jax.ShapeDtypeStruct((1, LANES), F32), jax.ShapeDtypeStruct((T, D), F32),
                   jax.ShapeDtypeStruct((1, D), F32)],
        compiler_params=_params("arbitrary"),
    )(x, g, target)


def _adamw_math(w, g, m, v):
    m = ADAM_B1 * m + (1.0 - ADAM_B1) * g
    v = ADAM_B2 * v + (1.0 - ADAM_B2) * (g * g)
    m_hat = m / (1.0 - ADAM_B1 ** ADAM_STEP)
    v_hat = v / (1.0 - ADAM_B2 ** ADAM_STEP)
    delta = -ADAM_LR * (m_hat / (jnp.sqrt(v_hat) + ADAM_EPS) + ADAM_WD * w)
    return delta, m, v


def _as_rows(a):
    return a.reshape(-1, a.shape[-1])


def sum_parts(layers):
    n, R, C = layers[0].shape
    depth = len(layers)
    tr = _tile(R, 512 if C <= 1024 else 128)
    blocks = R // tr

    def body(*refs):
        o_ref = refs[-1]
        for l in range(depth):
            @pl.when(pl.program_id(0) == l)
            def _(p_ref=refs[l]):
                acc = p_ref[0].astype(F32)
                for k in range(1, n):
                    acc = acc + p_ref[k].astype(F32)
                o_ref[...] = acc

    in_specs = [pl.BlockSpec((n, tr, C), lambda L, i, l=l: (0, jnp.where(L == l, i, 0), 0)) for l in range(depth)]
    return pl.pallas_call(
        body, name="sum_parts", grid=(depth, blocks), in_specs=in_specs,
        out_specs=pl.BlockSpec((tr, C), lambda L, i: (L * blocks + i, 0)),
        out_shape=jax.ShapeDtypeStruct((depth * R, C), F32), compiler_params=_params("arbitrary", "arbitrary"),
    )(*layers)


def adamw_sum(w, parts, m, v):
    stacked = not isinstance(parts, (list, tuple))
    n = parts.shape[0] if stacked else len(parts)
    R, C = w.shape
    tr = _tile(R, 512 if n <= 2 and C <= 1024 else 256)

    def body(*refs):
        w_ref, m_ref, v_ref = refs[:3]
        g_ref, d_ref, nm_ref, nv_ref = refs[-4:]
        terms = [refs[3][k] for k in range(n)] if stacked else [p_ref[...] for p_ref in refs[3:3 + n]]
        g = terms[0]
        for term in terms[1:]:
            g = g + term
        delta, nm, nv = _adamw_math(w_ref[...], g, m_ref[...], v_ref[...])
        g_ref[...] = g
        d_ref[...] = delta
        nm_ref[...] = nm
        nv_ref[...] = nv

    rows = pl.BlockSpec((tr, C), lambda i: (i, 0))
    part_specs = [pl.BlockSpec((n, tr, C), lambda i: (0, i, 0))] if stacked else [rows] * n
    shape = jax.ShapeDtypeStruct((R, C), F32)
    return pl.pallas_call(
        body, name="adamw_sum", grid=(R // tr,), in_specs=[rows] * 3 + part_specs, out_specs=[rows] * 4,
        out_shape=[shape] * 4, compiler_params=_params("parallel"),
    )(w, m, v, *([parts] if stacked else parts))


def _place():
    return lax.axis_index("x"), lax.axis_index("y"), lax.axis_index("c")


HBM_SPEC = pl.BlockSpec(memory_space=pltpu.HBM)
SEM_SPEC = pl.BlockSpec(memory_space=pltpu.SEMAPHORE)
DATAFLOW = pltpu.SideEffectType.DATAFLOW_SIDE_EFFECTING


PATTERNS = {"gather": ([(1, 0, 0), (0, 1, 0), (1, 1, 0)], N_CHIPS), "scatter": ([(1, 0, 0), (0, 1, 0), (1, 1, 0)], None),
            "sibling": ([(0, 0, 1)], None),
            "all": ([(fx, fy, fc) for fx in (0, 1) for fy in (0, 1) for fc in (0, 1)][1:], N_DEVICES)}


def _exchange_copies(pattern, srcs, lands, send_sems, recv_sems, local_sems):
    flips, _ = PATTERNS[pattern]
    x, y, c = _place()

    def slot(px, py, pc):
        return 4 * px + 2 * py + pc if pattern == "all" else 2 * px + py

    me = slot(x, y, c)
    outgoing, arriving, local = [], [], []
    for k, (src, land) in enumerate(zip(srcs, lands)):
        for d, (fx, fy, fc) in enumerate(flips):
            peer = (x ^ fx, y ^ fy, c ^ fc)
            pair = dict(send_sem=send_sems.at[len(flips) * k + d], recv_sem=recv_sems.at[len(flips) * k + d],
                        device_id=peer, device_id_type=MESH)
            if pattern == "sibling":
                outgoing.append(pltpu.make_async_remote_copy(src_ref=src, dst_ref=land, **pair))
                arriving.append(outgoing[-1])
                continue
            scatter = pattern == "scatter"
            outgoing.append(pltpu.make_async_remote_copy(
                src_ref=src.at[slot(*peer)] if scatter else src, dst_ref=land.at[me], **pair))
            arriving.append(pltpu.make_async_remote_copy(
                src_ref=src.at[me] if scatter else src, dst_ref=land.at[slot(*peer)], **pair))
        if pattern != "sibling":
            local.append(pltpu.make_async_copy(src.at[me] if pattern == "scatter" else src, land.at[me], local_sems.at[k]))
    return outgoing, arriving, local


def _semaphore_shapes(pattern, n):
    flips, _ = PATTERNS[pattern]
    local = [] if pattern == "sibling" else [pltpu.SemaphoreType.DMA((n,))]
    return [pltpu.SemaphoreType.DMA((len(flips) * n,)), pltpu.SemaphoreType.DMA((len(flips) * n,))] + local


def exchange_start(arrays, *, pattern, name):
    n = len(arrays)
    lead = PATTERNS[pattern][1]
    lands = [lax.empty(a.shape if lead is None else (lead,) + a.shape, a.dtype) for a in arrays]
    sem_shapes = _semaphore_shapes(pattern, n)
    n_sems = len(sem_shapes)

    def body(*refs):
        srcs, zones = refs[:n], refs[n:2 * n]
        sems = list(refs[2 * n:2 * n + n_sems]) + [None]
        token = refs[-1]
        outgoing, _, local = _exchange_copies(pattern, srcs, zones, *sems[:3])
        for cp in outgoing + local:
            cp.start()
        token[...] = jnp.zeros_like(token)

    hbm = lambda a: pltpu.HBM(a.shape, a.dtype)
    outs = pl.pallas_call(
        body, name=name,
        out_shape=(*sem_shapes, *[hbm(a) for a in arrays], *[hbm(a) for a in lands],
                   jax.ShapeDtypeStruct((SUBLANES, LANES), F32)),
        in_specs=[HBM_SPEC] * (2 * n),
        out_specs=(*[SEM_SPEC] * n_sems, *[HBM_SPEC] * (2 * n), pl.BlockSpec(memory_space=pltpu.VMEM)),
        input_output_aliases={i: n_sems + i for i in range(2 * n)},
        compiler_params=pltpu.CompilerParams(has_side_effects=DATAFLOW),
    )(*[pltpu.with_memory_space_constraint(a, pltpu.HBM) for a in list(arrays) + lands])
    return outs[:n_sems], outs[n_sems:n_sems + n], outs[n_sems + n:n_sems + 2 * n], outs[-1]


def exchange_wait(started, after, *, pattern, name, with_sources=False):
    sems, srcs, lands, _ = started
    n = len(srcs)
    n_sems = len(sems)

    def body(*refs):
        src_refs, zones = refs[:n], refs[n:2 * n]
        sem_refs = list(refs[2 * n:2 * n + n_sems]) + [None]
        outgoing, arriving, local = _exchange_copies(pattern, src_refs, zones, *sem_refs[:3])
        for sent, landed in zip(outgoing, arriving):
            sent.wait_send()
            landed.wait_recv()
        for cp in local:
            cp.wait()

    hbm = lambda a: pltpu.HBM(a.shape, a.dtype)
    outs = pl.pallas_call(
        body, name=name, out_shape=[hbm(a) for a in list(srcs) + list(lands)],
        in_specs=[HBM_SPEC] * (2 * n) + [SEM_SPEC] * n_sems + [pl.BlockSpec(memory_space=pl.ANY)],
        out_specs=[HBM_SPEC] * (2 * n), input_output_aliases={i: i for i in range(2 * n)},
        compiler_params=pltpu.CompilerParams(has_side_effects=DATAFLOW),
    )(*srcs, *lands, *sems, after)
    return (outs[:n], outs[n:]) if with_sources else outs[n:]


BIG = ("ffn1_w_gate", "ffn1_w_up", "ffn1_w_down", "w_in", "w_attn_out", "w_lru_out", "w_mem_kv", "w_mem_out",
       "w_out", "ffn2_w_gate", "ffn2_w_up", "ffn2_w_down")
TRANSPOSED = ("ffn1_w_gate", "ffn1_w_up", "w_in", "w_attn_out", "w_mem_out", "ffn2_w_gate", "ffn2_w_up")
SMALL = ("ffn1_norm", "mix_norm", "gate_bias", "attn_sinks", "conv_w", "conv_b", "lru_wa", "lru_ba", "lru_wx", "lru_bx",
         "lru_lambda", "mem_norm", "ffn2_norm", "final_norm")
WEIGHTS = ("ffn1_norm", "ffn1_w_gate", "ffn1_w_up", "ffn1_w_down", "mix_norm", "w_in", "gate_bias", "attn_sinks",
           "w_attn_out", "conv_w", "conv_b", "lru_wa", "lru_ba", "lru_wx", "lru_bx", "lru_lambda", "w_lru_out", "mem_norm",
           "w_mem_kv", "w_mem_out", "w_out", "ffn2_norm", "ffn2_w_gate", "ffn2_w_up", "ffn2_w_down", "final_norm")
SEGMENTS = (("qkv", 0, 768), ("xr", 768, 1792), ("yr", 1792, 2816), ("cq", 2816, 3328), ("gates", 3328, 6400))
PACK_ROW = 1024


STAGES = ("ffn1", "mix", "ffn2")
STAGE_BIG = {"ffn1": ("ffn1_w_gate", "ffn1_w_up", "ffn1_w_down"),
             "mix": ("w_in", "w_attn_out", "w_lru_out", "w_mem_kv", "w_mem_out", "w_out"),
             "ffn2": ("ffn2_w_gate", "ffn2_w_up", "ffn2_w_down")}


def _row_sharded(state, name):
    return jnp.swapaxes(state[name], 1, 2) if name in TRANSPOSED else state[name]


def _join_shards(gathered, name):
    if name == "conv_w":
        return jnp.concatenate([gathered[s] for s in range(N_CHIPS)], axis=-1)
    return gathered.reshape(-1, gathered.shape[-1])


def _split_shards(full):
    return full.reshape(N_CHIPS, -1, full.shape[-1])


def _lane_tiles(w):
    z = jnp.zeros((LRU_WIDTH // LRU_TILE, HEAD_DIM, HEAD_DIM), w.dtype)
    top = jnp.concatenate([w[0::2], z], axis=2)
    bottom = jnp.concatenate([z, w[1::2]], axis=2)
    return jnp.concatenate([top, bottom], axis=1)


def _from_lane_tiles(t):
    pairs = jnp.stack([t[:, :HEAD_DIM, :HEAD_DIM], t[:, HEAD_DIM:, HEAD_DIM:]], axis=1)
    return pairs.reshape(2 * t.shape[0], HEAD_DIM, HEAD_DIM)


def _pack_rows(size):
    return -(-size // (SUBLANES * PACK_ROW)) * SUBLANES


def _pack(arrays):
    blocks = []
    for a in arrays:
        flat = a.reshape(-1).astype(F32)
        rows = _pack_rows(flat.shape[0])
        blocks.append(jnp.pad(flat, (0, rows * PACK_ROW - flat.shape[0])).reshape(rows, PACK_ROW))
    return jnp.concatenate(blocks, axis=0)


def _unpack(packed, shapes):
    out, at = [], 0
    for shape in shapes:
        size = int(np.prod(shape))
        rows = _pack_rows(size)
        out.append(packed[at:at + rows].reshape(-1)[:size].reshape(shape))
        at += rows
    return out


def _ffn_forward(x, h, w_gate_t, w_up_t, w_down, next_norm):
    G, U, A = ffn_up(h, w_gate_t, w_up_t)
    out = mm([A], [w_down], nt=False, out_dtype=F32, res=x, scale=0.5, norm=next_norm, name="ffn_down")
    y, h_next = out if next_norm is not None else (out, None)
    return y, h_next, (x, h, G, U, A)


def _ffn_backward(half, dy, saved, norm, w_gate_t, w_up_t, w_down):
    x, h, G, U, A = saved
    dG, dU = ffn_bwd_act(half, dy, w_down, G, U)
    d_down = mm_tn(A, dy, out_dtype=MXU_DTYPE, b_scale=0.5, name="ffn_dw_down")
    d_gate_t = mm_tn(dG, h, out_dtype=MXU_DTYPE, name="ffn_dw_up")
    d_up_t = mm_tn(dU, h, out_dtype=MXU_DTYPE, name="ffn_dw_up")
    dx, d_norm = mm_rms_bwd([dG, dU], [w_gate_t, w_up_t], x, norm, dy, nt=False, name="ffn_dx")
    return dx, d_norm, d_gate_t, d_up_t, d_down


def kernel(x, mem, ffn1_norm, ffn1_w_gate, ffn1_w_up, ffn1_w_down, mix_norm, w_in, gate_bias, attn_sinks, w_attn_out, conv_w, conv_b, lru_wa, lru_ba, lru_wx, lru_bx, lru_lambda, w_lru_out, mem_norm, w_mem_kv, w_mem_out, w_out, ffn2_norm, ffn2_w_gate, ffn2_w_up, ffn2_w_down, final_norm, loss_target, m_ffn1_norm, m_ffn1_w_gate, m_ffn1_w_up, m_ffn1_w_down, m_mix_norm, m_w_in, m_gate_bias, m_attn_sinks, m_w_attn_out, m_conv_w, m_conv_b, m_lru_wa, m_lru_ba, m_lru_wx, m_lru_bx, m_lru_lambda, m_w_lru_out, m_mem_norm, m_w_mem_kv, m_w_mem_out, m_w_out, m_ffn2_norm, m_ffn2_w_gate, m_ffn2_w_up, m_ffn2_w_down, m_final_norm, v_ffn1_norm, v_ffn1_w_gate, v_ffn1_w_up, v_ffn1_w_down, v_mix_norm, v_w_in, v_gate_bias, v_attn_sinks, v_w_attn_out, v_conv_w, v_conv_b, v_lru_wa, v_lru_ba, v_lru_wx, v_lru_bx, v_lru_lambda, v_w_lru_out, v_mem_norm, v_w_mem_kv, v_w_mem_out, v_w_out, v_ffn2_norm, v_ffn2_w_gate, v_ffn2_w_up, v_ffn2_w_down, v_final_norm):
    args = dict(locals())
    W = {n: args[n] for n in WEIGHTS}
    M = {n: args["m_" + n] for n in WEIGHTS}
    V = {n: args["v_" + n] for n in WEIGHTS}
    chip = 2 * lax.axis_index("x") + lax.axis_index("y")
    x0 = x[0]
    mem0 = mem[0]
    target = loss_target[0]

    def row(v):
        return v.reshape(1, -1)

    def stage_names(stage):
        return STAGE_BIG[stage] + (("conv_w",) if stage == "mix" else ())

    RW, RM, RV = ({n: _row_sharded(state, n) for n in BIG} for state in (W, M, V))
    gathers = {}
    started_all = jnp.zeros((), F32)
    for l in range(DEPTH):
        for stage in STAGES:
            shards = [W[n][l] if n == "conv_w" else RW[n][l].astype(MXU_DTYPE) for n in stage_names(stage)]
            gathers[l, stage] = exchange_start(shards, pattern="gather", name=f"gather_start_{stage}_{l}")
            started_all = started_all + gathers[l, stage][3][0, 0]

    def weights_of(l, stage, after):
        lands = exchange_wait(gathers[l, stage], after, pattern="gather", name=f"gather_wait_{stage}_{l}")
        return {n: _join_shards(g, n) for n, g in zip(stage_names(stage), lands)}

    saved = []
    full = []
    xs = x0
    h = rms_fwd(x0, row(W["ffn1_norm"][0]) + started_all)
    for l in range(DEPTH):
        P = weights_of(l, "ffn1", xs)
        xs, h, ffn1 = _ffn_forward(xs, h, P["ffn1_w_gate"], P["ffn1_w_up"], P["ffn1_w_down"], row(W["mix_norm"][l]))
        x_mix = xs
        P.update(weights_of(l, "mix", xs))
        seg = dict(zip([name for name, _, _ in SEGMENTS], proj_fwd(h, P["w_in"], SEGMENTS)))
        attn = attn_fwd(seg["qkv"], W["attn_sinks"][l])
        wa_t = _lane_tiles(W["lru_wa"][l]).astype(MXU_DTYPE)
        wx_t = _lane_tiles(W["lru_wx"][l]).astype(MXU_DTYPE)
        sp = row(jax.nn.softplus(-W["lru_lambda"][l]))
        lru_consts = (P["conv_w"], row(W["conv_b"][l]), wa_t, row(W["lru_ba"][l]), wx_t, row(W["lru_bx"][l]), sp)
        h_lru, rec = lru_fwd(seg["xr"], seg["yr"], *lru_consts)
        mem_n = rms_fwd(mem0, row(W["mem_norm"][l]))
        kv = mm([mem_n], [P["w_mem_kv"]], nt=False, out_dtype=F32, name="mem_kv")
        cross = mem_attn_fwd(seg["cq"], kv)
        bias = row(W["gate_bias"][l])
        branch_weights = (P["w_attn_out"], P["w_lru_out"], P["w_mem_out"])
        merged, *branches = merge_fwd((attn, rec, cross), branch_weights, seg["gates"], bias)
        h_mix = h
        xs, h = mm([merged], [P["w_out"]], nt=False, out_dtype=F32, res=x_mix, norm=row(W["ffn2_norm"][l]), name="mix_out")
        mix = (x_mix, h_mix, seg, attn, lru_consts, h_lru, rec, mem_n, kv, cross, branches, bias, merged)
        P.update(weights_of(l, "ffn2", xs))
        next_norm = row(W["ffn1_norm"][l + 1]) if l + 1 < DEPTH else None
        xs, h, ffn2 = _ffn_forward(xs, h, P["ffn2_w_gate"], P["ffn2_w_up"], P["ffn2_w_down"], next_norm)
        saved.append((ffn1, mix, ffn2))
        full.append(P)

    loss_lanes, dx, d_final = loss_head(xs, row(W["final_norm"]), target)
    loss = lax.psum(loss_lanes[0, 0], ("x", "y", "c"))

    big_grads = {}
    small_grads = {n: [None] * DEPTH for n in SMALL if n != "final_norm"}
    scatters = {}
    token = jnp.zeros((), F32)

    def send_grads(l, stage):
        chunks = [_split_shards(big_grads[n, l]) for n in STAGE_BIG[stage]]
        scatters[l, stage] = exchange_start(chunks, pattern="scatter", name=f"scatter_start_{stage}_{l}")
        return scatters[l, stage][3][0, 0]

    for l in reversed(range(DEPTH)):
        P = full[l]
        ffn1, mix, ffn2 = saved[l]
        dx, g_norm, g_gate, g_up, g_down = _ffn_backward(
            (0.5 + token).reshape(1), dx, ffn2, row(W["ffn2_norm"][l]), P["ffn2_w_gate"], P["ffn2_w_up"], P["ffn2_w_down"])
        small_grads["ffn2_norm"][l] = g_norm
        big_grads["ffn2_w_gate", l], big_grads["ffn2_w_up", l], big_grads["ffn2_w_down", l] = g_gate, g_up, g_down
        token = send_grads(l, "ffn2")

        x_mix, h, seg, attn, lru_consts, h_lru, rec, mem_n, kv, cross, branches, bias, merged = mix
        bias = bias + token
        big_grads["w_out", l] = mm_tn(merged, dx, out_dtype=MXU_DTYPE, name="dw_out")
        d_attn_br, d_lru_br, d_mem_br, d_attn, d_rec, d_cross, d_gates, d_bias = merge_bwd(
            dx, P["w_out"], (P["w_attn_out"], P["w_lru_out"], P["w_mem_out"]), seg["gates"], bias, branches)
        small_grads["gate_bias"][l] = d_bias
        big_grads["w_attn_out", l] = mm_tn(d_attn_br, attn, out_dtype=MXU_DTYPE, name="dw_attn_out")
        big_grads["w_lru_out", l] = mm_tn(rec, d_lru_br, out_dtype=MXU_DTYPE, name="dw_lru_out")
        big_grads["w_mem_out", l] = mm_tn(d_mem_br, cross, out_dtype=MXU_DTYPE, name="dw_mem_out")

        d_qkv, d_sinks = attn_bwd(seg["qkv"], W["attn_sinks"][l], d_attn)
        small_grads["attn_sinks"][l] = d_sinks

        d_xr, d_yr, d_wa_t, d_wx_t, d_ba, d_bx, d_sp, d_cb, d_cw = lru_bwd(seg["xr"], seg["yr"], h_lru, d_rec, *lru_consts)
        small_grads["conv_w"][l] = d_cw
        small_grads["conv_b"][l] = d_cb
        small_grads["lru_wa"][l] = _from_lane_tiles(d_wa_t)
        small_grads["lru_wx"][l] = _from_lane_tiles(d_wx_t)
        small_grads["lru_ba"][l] = d_ba
        small_grads["lru_bx"][l] = d_bx
        small_grads["lru_lambda"][l] = -d_sp * _sigmoid(-row(W["lru_lambda"][l]))

        d_cq, d_kv = mem_attn_bwd(seg["cq"], kv, d_cross)
        big_grads["w_mem_kv", l] = mm_tn(mem_n, d_kv, out_dtype=MXU_DTYPE, name="dw_mem_kv")
        _, g_mem_norm = mm_rms_bwd([d_kv], [P["w_mem_kv"]], mem0, row(W["mem_norm"][l]), None, nt=True, name="mem_dnorm")
        small_grads["mem_norm"][l] = g_mem_norm

        d_seg = {"qkv": d_qkv, "xr": d_xr, "yr": d_yr, "cq": d_cq, "gates": d_gates}
        big_grads["w_in", l] = jnp.concatenate(
            [mm_tn(d_seg[name], h, out_dtype=MXU_DTYPE, name="dw_in_" + name) for name, _, _ in SEGMENTS], axis=0)
        dx, g_norm = mm_rms_bwd([d_seg[name] for name, _, _ in SEGMENTS], [P["w_in"]], x_mix, row(W["mix_norm"][l]), dx,
                                nt=False, b_rows=[(lo, hi) for _, lo, hi in SEGMENTS], name="mix_dx")
        small_grads["mix_norm"][l] = g_norm
        token = send_grads(l, "mix")

        dx, g_norm, g_gate, g_up, g_down = _ffn_backward(
            (0.5 + token).reshape(1), dx, ffn1, row(W["ffn1_norm"][l]), P["ffn1_w_gate"], P["ffn1_w_up"], P["ffn1_w_down"])
        small_grads["ffn1_norm"][l] = g_norm
        big_grads["ffn1_w_gate", l], big_grads["ffn1_w_up", l], big_grads["ffn1_w_down", l] = g_gate, g_up, g_down
        token = send_grads(l, "ffn1")

    grad_x = dx[None]

    small_list = [jnp.stack(small_grads[n]) for n in SMALL if n != "final_norm"] + [d_final]
    small_shapes = [(DEPTH,) + W[n].shape[1:] if n != "conv_w" else (DEPTH, CONV_WIDTH, LRU_WIDTH)
                    for n in SMALL if n != "final_norm"] + [W["final_norm"].shape]
    small_exchange = exchange_start([_pack(small_list)], pattern="all", name="small_start")

    arrived = {}
    backward_done = dx[:1, :1] + token + small_exchange[3][0, 0]
    for l in range(DEPTH):
        for stage in STAGES:
            lands = exchange_wait(scatters[l, stage], backward_done, pattern="scatter", name=f"scatter_wait_{stage}_{l}")
            for n, a in zip(STAGE_BIG[stage], lands):
                arrived[n, l] = a.reshape(N_CHIPS, -1, a.shape[-1])
    siblings = {}
    after = jnp.zeros((1, 1), F32)
    for stage in STAGES:
        partial = [sum_parts([arrived[n, l] for l in range(DEPTH)]) for n in STAGE_BIG[stage]]
        siblings[stage] = exchange_start(partial, pattern="sibling", name=f"sibling_start_{stage}")
        after = after + siblings[stage][3][0, 0]
    results = {}
    for stage in STAGES:
        own, theirs = exchange_wait(siblings[stage], after, pattern="sibling", name=f"sibling_wait_{stage}", with_sources=True)
        for n, mine, other in zip(STAGE_BIG[stage], own, theirs):
            outs = adamw_sum(_as_rows(RW[n]), [mine, other], _as_rows(RM[n]), _as_rows(RV[n]))
            after = outs[0]
            outs = [o.reshape(RW[n].shape) for o in outs]
            results[n] = [jnp.swapaxes(o, 1, 2) for o in outs] if n in TRANSPOSED else outs
    everyone = exchange_wait(small_exchange, after, pattern="all", name="small_wait")[0]

    def own_columns(full_conv):
        return lax.dynamic_slice_in_dim(full_conv, chip * (LRU_WIDTH // N_CHIPS), LRU_WIDTH // N_CHIPS, axis=2)

    def packed_state(state):
        arrays = []
        for n in SMALL:
            a = state[n]
            if n == "conv_w":
                a = lax.dynamic_update_slice_in_dim(jnp.zeros((DEPTH, CONV_WIDTH, LRU_WIDTH), F32), a,
                                                    chip * (LRU_WIDTH // N_CHIPS), axis=2)
            arrays.append(a)
        return _pack(arrays)

    small_outs = adamw_sum(packed_state(W), everyone, packed_state(M), packed_state(V))
    for kind, packed in enumerate(small_outs):
        for n, a in zip(SMALL, _unpack(packed, small_shapes)):
            results.setdefault(n, [None] * 4)[kind] = own_columns(a) if n == "conv_w" else a.reshape(W[n].shape)

    return (loss, grad_x, *[results[n][0] for n in WEIGHTS], *[results[n][1] for n in WEIGHTS],
            *[results[n][2] for n in WEIGHTS], *[results[n][3] for n in WEIGHTS])
```

```python
import functools

import jax
import jax.numpy as jnp
import numpy as np
from jax import lax
from jax.experimental import pallas as pl
from jax.experimental.pallas import tpu as pltpu

F32 = jnp.float32
MXU_DTYPE = jnp.bfloat16

DEPTH = 4
D_MODEL = 1024
CHUNK = 64
HALO = 2 * CHUNK
N_Q_HEADS = 8
HEAD_DIM = 64
ATTN_WIDTH = 512
KV_WIDTH = 128
LRU_WIDTH = 1024
LRU_TILE = 128
CONV_WIDTH = 4
LRU_C = 8.0
MEM_HEADS = 4
MEM_HEAD_DIM = 128
MEM_WIDTH = 512
D_FF = 2816
EPS = 1e-6
ADAM_LR = 0.001
ADAM_B1 = 0.9
ADAM_B2 = 0.999
ADAM_EPS = 1e-08
ADAM_WD = 0.01
ADAM_STEP = 10

N_CHIPS = 4
N_DEVICES = 8
SUBLANES = 8
LANES = 128
VMEM_LIMIT_BYTES = 56 * 1024 * 1024
NEG_BIG = -1e30
MESH = pl.DeviceIdType.MESH

ALIBI_SLOPES = tuple(float(2.0 ** (-8.0 * (i + 1) / N_Q_HEADS)) for i in range(N_Q_HEADS))


def _params(*semantics):
    return pltpu.CompilerParams(dimension_semantics=semantics, vmem_limit_bytes=VMEM_LIMIT_BYTES)


def _tile(n, pref, unit=SUBLANES):
    if n <= pref:
        return n
    for t in range(pref - pref % unit, 0, -unit):
        if n % t == 0:
            return t
    return n


def _dot(a, b, ca, cb):
    return lax.dot_general(a, b, (((ca,), (cb,)), ((), ())), preferred_element_type=F32)


def _sigmoid(x):
    return 0.5 * jnp.tanh(0.5 * x) + 0.5


def rms_fwd(x, g):
    T, D = x.shape
    tm = _tile(T, 512)

    def body(x_ref, g_ref, h_ref):
        xv = x_ref[...]
        r = lax.rsqrt(jnp.mean(xv * xv, axis=-1, keepdims=True) + EPS)
        h_ref[...] = (xv * r * g_ref[...]).astype(h_ref.dtype)

    return pl.pallas_call(
        body, name="rms_fwd", grid=(T // tm,),
        in_specs=[pl.BlockSpec((tm, D), lambda i: (i, 0)), pl.BlockSpec((1, D), lambda i: (0, 0))],
        out_specs=pl.BlockSpec((tm, D), lambda i: (i, 0)),
        out_shape=jax.ShapeDtypeStruct((T, D), MXU_DTYPE),
        compiler_params=_params("parallel"),
    )(x, g)


def mm_rms_bwd(a_list, b_list, x, g, dy, *, nt, b_rows=None, tm=512, name="mm_rms_bwd"):
    n_a, n_b = len(a_list), len(b_list)
    T, D = x.shape
    tm = _tile(T, tm)
    has_dy = dy is not None

    def body(*refs):
        a_refs, b_refs = refs[:n_a], refs[n_a:n_a + n_b]
        x_ref, g_ref = refs[n_a + n_b:n_a + n_b + 2]
        dx_ref, dg_ref = refs[-2:]
        i = pl.program_id(0)
        dh = None
        for p, a_ref in enumerate(a_refs):
            b = b_refs[p][...] if b_rows is None else b_refs[0][b_rows[p][0]:b_rows[p][1], :]
            d = _dot(a_ref[...].astype(MXU_DTYPE), b, 1, 1 if nt else 0)
            dh = d if dh is None else dh + d
        xv = x_ref[...]
        r = lax.rsqrt(jnp.mean(xv * xv, axis=-1, keepdims=True) + EPS)
        xh = xv * r
        gd = dh * g_ref[...]
        dx = r * (gd - xh * jnp.mean(gd * xh, axis=-1, keepdims=True))
        dx_ref[...] = refs[n_a + n_b + 2][...] + dx if has_dy else dx
        part = jnp.sum(dh * xh, axis=0, keepdims=True)

        @pl.when(i == 0)
        def _():
            dg_ref[...] = part

        @pl.when(i > 0)
        def _():
            dg_ref[...] += part

    row = pl.BlockSpec((tm, D), lambda i: (i, 0))
    vec = pl.BlockSpec((1, D), lambda i: (0, 0))
    in_specs = [pl.BlockSpec((tm, a.shape[1]), lambda i: (i, 0)) for a in a_list]
    in_specs += [pl.BlockSpec(b.shape, lambda i: (0, 0), pipeline_mode=pl.Buffered(1)) for b in b_list]
    in_specs += [row, vec] + ([row] if has_dy else [])
    operands = list(a_list) + list(b_list) + [x, g] + ([dy] if has_dy else [])
    return pl.pallas_call(
        body, name=name, grid=(T // tm,), in_specs=in_specs, out_specs=[row, vec],
        out_shape=[jax.ShapeDtypeStruct((T, D), F32), jax.ShapeDtypeStruct((1, D), F32)],
        compiler_params=_params("arbitrary"),
    )(*operands)


def mm(a_list, b_list, *, nt, out_dtype, res=None, scale=1.0, norm=None, tm=512, tn=1024, name="mm"):
    n_pairs = len(a_list)
    T = a_list[0].shape[0]
    N = b_list[0].shape[0] if nt else b_list[0].shape[1]
    tm = _tile(T, tm)
    tn = _tile(N, tn, LANES)
    has_res = res is not None
    has_norm = norm is not None
    assert not has_norm or tn == N

    def body(*refs):
        a_refs = refs[:n_pairs]
        b_refs = refs[n_pairs:2 * n_pairs]
        extra = list(refs[2 * n_pairs:])
        acc = None
        for a_ref, b_ref in zip(a_refs, b_refs):
            d = _dot(a_ref[...].astype(MXU_DTYPE), b_ref[...], 1, 1 if nt else 0)
            acc = d if acc is None else acc + d
        if has_res:
            acc = extra.pop(0)[...] + scale * acc
        elif scale != 1.0:
            acc = scale * acc
        if has_norm:
            g_ref = extra.pop(0)
            r = lax.rsqrt(jnp.mean(acc * acc, axis=-1, keepdims=True) + EPS)
            extra[1][...] = (acc * r * g_ref[...]).astype(extra[1].dtype)
        extra[0][...] = acc.astype(extra[0].dtype)

    in_specs = [pl.BlockSpec((tm, a.shape[1]), lambda j, i: (i, 0)) for a in a_list]
    if nt:
        in_specs += [pl.BlockSpec((tn, b.shape[1]), lambda j, i: (j, 0)) for b in b_list]
    else:
        in_specs += [pl.BlockSpec((b.shape[0], tn), lambda j, i: (0, j)) for b in b_list]
    out_spec = pl.BlockSpec((tm, tn), lambda j, i: (i, j))
    operands = list(a_list) + list(b_list)
    if has_res:
        in_specs.append(out_spec)
        operands.append(res)
    if has_norm:
        in_specs.append(pl.BlockSpec((1, N), lambda j, i: (0, 0)))
        operands.append(norm)
    out_shape = jax.ShapeDtypeStruct((T, N), out_dtype)
    return pl.pallas_call(
        body, name=name, grid=(N // tn, T // tm), in_specs=in_specs,
        out_specs=[out_spec, out_spec] if has_norm else out_spec,
        out_shape=[out_shape, jax.ShapeDtypeStruct((T, N), MXU_DTYPE)] if has_norm else out_shape,
        compiler_params=_params("parallel", "parallel"),
    )(*operands)


def proj_fwd(h, w_in_t, segments):
    T, D = h.shape
    tm = _tile(T, 256)

    def body(h_ref, w_ref, *out_refs):
        hv = h_ref[...]
        for (_, lo, hi), o_ref in zip(segments, out_refs):
            o_ref[...] = _dot(hv, w_ref[lo:hi, :], 1, 1)

    return pl.pallas_call(
        body, name="proj_fwd", grid=(T // tm,),
        in_specs=[pl.BlockSpec((tm, D), lambda i: (i, 0)),
                  pl.BlockSpec(w_in_t.shape, lambda i: (0, 0), pipeline_mode=pl.Buffered(1))],
        out_specs=[pl.BlockSpec((tm, hi - lo), lambda i: (i, 0)) for _, lo, hi in segments],
        out_shape=[jax.ShapeDtypeStruct((T, hi - lo), F32) for _, lo, hi in segments],
        compiler_params=_params("parallel"),
    )(h, w_in_t)


def mm_tn(a, b, *, out_dtype, b_scale=1.0, tk=1408, tn=1408, tt=None, name="mm_tn"):
    T, K = a.shape
    N = b.shape[1]
    tk = _tile(K, tk, LANES)
    tn = _tile(N, tn, LANES)
    if tt is None:
        tt = 1024 if b.dtype == F32 else 2048
    tt = _tile(T, tt)
    steps = T // tt

    def body(a_ref, b_ref, o_ref, acc_ref):
        t = pl.program_id(2)
        bv = b_ref[...]
        if b_scale != 1.0:
            bv = b_scale * bv
        d = _dot(a_ref[...].astype(MXU_DTYPE), bv.astype(MXU_DTYPE), 0, 0)

        @pl.when(t == 0)
        def _():
            acc_ref[...] = d

        @pl.when(t > 0)
        def _():
            acc_ref[...] += d

        @pl.when(t == steps - 1)
        def _():
            o_ref[...] = acc_ref[...].astype(o_ref.dtype)

    return pl.pallas_call(
        body, name=name, grid=(K // tk, N // tn, steps),
        in_specs=[pl.BlockSpec((tt, tk), lambda p, q, t: (t, p)), pl.BlockSpec((tt, tn), lambda p, q, t: (t, q))],
        out_specs=pl.BlockSpec((tk, tn), lambda p, q, t: (p, q)),
        out_shape=jax.ShapeDtypeStruct((K, N), out_dtype),
        scratch_shapes=[pltpu.VMEM((tk, tn), F32)],
        compiler_params=_params("parallel", "parallel", "arbitrary"),
    )(a, b)


def _lane_chunks(n, width=3 * LANES):
    return [slice(lo, min(lo + width, n)) for lo in range(0, n, width)]


def ffn_up(h, w_gate_t, w_up_t):
    T, D = h.shape
    F = w_gate_t.shape[0]
    tm = _tile(T, 512)
    tn = _tile(F, 1408, LANES)

    def body(h_ref, wg_ref, wu_ref, g_ref, u_ref, a_ref):
        hv = h_ref[...]
        G = _dot(hv, wg_ref[...], 1, 1)
        U = _dot(hv, wu_ref[...], 1, 1)
        g_ref[...] = G.astype(g_ref.dtype)
        u_ref[...] = U.astype(u_ref.dtype)
        a_ref[...] = (G * _sigmoid(G) * U).astype(a_ref.dtype)

    w_spec = pl.BlockSpec((tn, D), lambda j, i: (j, 0))
    o_spec = pl.BlockSpec((tm, tn), lambda j, i: (i, j))
    return pl.pallas_call(
        body, name="ffn_up", grid=(F // tn, T // tm),
        in_specs=[pl.BlockSpec((tm, D), lambda j, i: (i, 0)), w_spec, w_spec],
        out_specs=[o_spec, o_spec, o_spec],
        out_shape=[jax.ShapeDtypeStruct((T, F), MXU_DTYPE)] * 3,
        compiler_params=_params("parallel", "parallel"),
    )(h, w_gate_t, w_up_t)


def ffn_bwd_act(half, dy, w_down, G, U):
    T, D = dy.shape
    F = w_down.shape[0]
    tm = _tile(T, 512)

    def body(half_ref, dy_ref, wd_ref, g_ref, u_ref, dg_ref, du_ref):
        d_out = (half_ref[0] * dy_ref[...]).astype(MXU_DTYPE)
        for cols in _lane_chunks(F):
            dA = _dot(d_out, wd_ref[cols, :], 1, 1)
            Gv = g_ref[:, cols].astype(F32)
            s = _sigmoid(Gv)
            du_ref[:, cols] = (dA * (Gv * s)).astype(du_ref.dtype)
            dg_ref[:, cols] = (dA * u_ref[:, cols].astype(F32) * (s * (1.0 + Gv * (1.0 - s)))).astype(dg_ref.dtype)

    o_spec = pl.BlockSpec((tm, F), lambda i: (i, 0))
    return pl.pallas_call(
        body, name="ffn_bwd_act", grid=(T // tm,),
        in_specs=[pl.BlockSpec(memory_space=pltpu.SMEM), pl.BlockSpec((tm, D), lambda i: (i, 0)),
                  pl.BlockSpec(w_down.shape, lambda i: (0, 0), pipeline_mode=pl.Buffered(1)), o_spec, o_spec],
        out_specs=[o_spec, o_spec],
        out_shape=[jax.ShapeDtypeStruct((T, F), MXU_DTYPE), jax.ShapeDtypeStruct((T, F), MXU_DTYPE)],
        compiler_params=_params("parallel"),
    )(half, dy, w_down, G, U)


ATTN_ROWS = 256


def _head_variants(kv128):
    lane = lax.broadcasted_iota(jnp.int32, kv128.shape, 1)
    low = lane < HEAD_DIM
    rolled = pltpu.roll(kv128, HEAD_DIM, 1)
    out = {}
    for j in (0, 1):
        for pos in (0, 1):
            src = kv128 if j == pos else rolled
            out[j, pos] = jnp.where(low if pos == 0 else jnp.logical_not(low), src, 0.0).astype(MXU_DTYPE)
    return out


def _fold_variant(d128, j, pos):
    lane = lax.broadcasted_iota(jnp.int32, d128.shape, 1)
    low = lane < HEAD_DIM
    kept = jnp.where(low if pos == 0 else jnp.logical_not(low), d128, 0.0)
    return kept if j == pos else pltpu.roll(kept, HEAD_DIM, 1)


def _attn_mask(block, rows, keys):
    r = lax.broadcasted_iota(jnp.int32, (rows, keys), 0)
    c = lax.broadcasted_iota(jnp.int32, (rows, keys), 1)
    q_chunk = r // CHUNK
    k_chunk = c // CHUNK - HALO // CHUNK
    first_chunk = jnp.where(block > 0, -(HALO // CHUNK), 0)
    valid = (k_chunk <= q_chunk) & (k_chunk >= q_chunk - HALO // CHUNK) & (k_chunk >= first_chunk)
    dist = jnp.abs(r + HALO - c).astype(F32)
    return valid, dist


def _attn_probs(q128, k_var, head, sink, valid, dist):
    s = _dot(q128, k_var, 1, 1) * (HEAD_DIM ** -0.5) - ALIBI_SLOPES[head] * dist
    s = jnp.where(valid, s, NEG_BIG)
    mx = jnp.maximum(jnp.max(s, axis=-1, keepdims=True), sink)
    p = jnp.exp(s - mx)
    p_sink = jnp.exp(sink - mx)
    inv = 1.0 / (jnp.sum(p, axis=-1, keepdims=True) + p_sink)
    return p * inv, p_sink * inv


def _attn_specs(T, tq, order):
    ratio = tq // HALO
    q_spec = pl.BlockSpec((tq, ATTN_WIDTH), lambda i: (order(i), 0))
    cur = lambda col: pl.BlockSpec((tq, KV_WIDTH), lambda i: (order(i), col))
    prev = lambda col: pl.BlockSpec((HALO, KV_WIDTH), lambda i: (jnp.maximum(order(i) * ratio - 1, 0), col))
    k_col, v_col = ATTN_WIDTH // KV_WIDTH, ATTN_WIDTH // KV_WIDTH + 1
    return [pl.BlockSpec(memory_space=pltpu.SMEM), q_spec, prev(k_col), cur(k_col), prev(v_col), cur(v_col)]


def attn_fwd(qkv, sinks):
    T = qkv.shape[0]
    tq = _tile(T, ATTN_ROWS)
    keys = tq + HALO

    def body(sink_ref, q_ref, kp_ref, kc_ref, vp_ref, vc_ref, o_ref):
        block = pl.program_id(0)
        k_var = _head_variants(jnp.concatenate([kp_ref[...], kc_ref[...]], axis=0))
        v_var = _head_variants(jnp.concatenate([vp_ref[...], vc_ref[...]], axis=0))
        valid, dist = _attn_mask(block, tq, keys)
        for pair in range(N_Q_HEADS // 2):
            q128 = q_ref[:, pair * LANES:(pair + 1) * LANES].astype(MXU_DTYPE)
            acc = None
            for pos in (0, 1):
                head = 2 * pair + pos
                j = head // 4
                p, _ = _attn_probs(q128, k_var[j, pos], head, sink_ref[head], valid, dist)
                d = _dot(p.astype(MXU_DTYPE), v_var[j, pos], 1, 0)
                acc = d if acc is None else acc + d
            o_ref[:, pair * LANES:(pair + 1) * LANES] = acc.astype(o_ref.dtype)

    return pl.pallas_call(
        body, name="attn_fwd", grid=(T // tq,),
        in_specs=_attn_specs(T, tq, lambda i: i),
        out_specs=pl.BlockSpec((tq, ATTN_WIDTH), lambda i: (i, 0)),
        out_shape=jax.ShapeDtypeStruct((T, ATTN_WIDTH), MXU_DTYPE),
        compiler_params=_params("parallel"),
    )(sinks, qkv, qkv, qkv, qkv, qkv)


def attn_bwd(qkv, sinks, d_out):
    T = qkv.shape[0]
    tq = _tile(T, ATTN_ROWS)
    keys = tq + HALO
    n_blocks = T // tq
    order = lambda i: n_blocks - 1 - i

    def body(sink_ref, q_ref, kp_ref, kc_ref, vp_ref, vc_ref, do_ref, dqkv_ref, dsink_ref, carry_ref):
        step = pl.program_id(0)
        block = order(step)
        k_var = _head_variants(jnp.concatenate([kp_ref[...], kc_ref[...]], axis=0))
        v_var = _head_variants(jnp.concatenate([vp_ref[...], vc_ref[...]], axis=0))
        valid, dist = _attn_mask(block, tq, keys)

        @pl.when(step == 0)
        def _():
            carry_ref[...] = jnp.zeros_like(carry_ref)
            dsink_ref[...] = jnp.zeros_like(dsink_ref)

        dk = jnp.zeros((keys, KV_WIDTH), F32)
        dv = jnp.zeros((keys, KV_WIDTH), F32)
        for pair in range(N_Q_HEADS // 2):
            q128 = q_ref[:, pair * LANES:(pair + 1) * LANES].astype(MXU_DTYPE)
            do128 = do_ref[:, pair * LANES:(pair + 1) * LANES].astype(MXU_DTYPE)
            dq128 = None
            for pos in (0, 1):
                head = 2 * pair + pos
                j = head // 4
                p, p_sink = _attn_probs(q128, k_var[j, pos], head, sink_ref[head], valid, dist)
                dp = _dot(do128, v_var[j, pos], 1, 1)
                delta = jnp.sum(p * dp, axis=-1, keepdims=True)
                ds = (p * (dp - delta) * (HEAD_DIM ** -0.5)).astype(MXU_DTYPE)
                d = _dot(ds, k_var[j, pos], 1, 0)
                dq128 = d if dq128 is None else dq128 + d
                dk = dk + _fold_variant(_dot(ds, q128, 0, 0), j, pos)
                dv = dv + _fold_variant(_dot(p.astype(MXU_DTYPE), do128, 0, 0), j, pos)
                dsink_ref[pl.ds(head, 1), :] += jnp.broadcast_to(
                    -jnp.sum(p_sink * delta, axis=0, keepdims=True), (1, LANES))
            dqkv_ref[:, pair * LANES:(pair + 1) * LANES] = dq128.astype(dqkv_ref.dtype)
        carried = carry_ref[...]
        if tq > HALO:
            carried = jnp.concatenate([jnp.zeros((tq - HALO, 2 * KV_WIDTH), F32), carried], axis=0)
        dkv = jnp.concatenate([dk, dv], axis=1)
        dqkv_ref[:, ATTN_WIDTH:] = (dkv[HALO:] + carried).astype(dqkv_ref.dtype)
        carry_ref[...] = dkv[:HALO]

    out_rows = pl.BlockSpec((tq, ATTN_WIDTH + 2 * KV_WIDTH), lambda i: (order(i), 0))
    dqkv, dsink = pl.pallas_call(
        body, name="attn_bwd", grid=(n_blocks,),
        in_specs=_attn_specs(T, tq, order) + [pl.BlockSpec((tq, ATTN_WIDTH), lambda i: (order(i), 0))],
        out_specs=[out_rows, pl.BlockSpec((N_Q_HEADS, LANES), lambda i: (0, 0))],
        out_shape=[jax.ShapeDtypeStruct((T, ATTN_WIDTH + 2 * KV_WIDTH), MXU_DTYPE),
                   jax.ShapeDtypeStruct((N_Q_HEADS, LANES), F32)],
        scratch_shapes=[pltpu.VMEM((HALO, 2 * KV_WIDTH), F32)],
        compiler_params=_params("arbitrary"),
    )(sinks, qkv, qkv, qkv, qkv, qkv, d_out)
    return dqkv, dsink[:, 0]


def _mem_probs(q, k):
    s = _dot(q, k, 1, 1) * (MEM_HEAD_DIM ** -0.5)
    p = jnp.exp(s - jnp.max(s, axis=-1, keepdims=True))
    return p / jnp.sum(p, axis=-1, keepdims=True)


def mem_attn_fwd(cq, kv):
    T = cq.shape[0]
    M = kv.shape[0]
    tq = _tile(T, 512)

    def body(q_ref, kv_ref, o_ref):
        for h in range(MEM_HEADS):
            lo, hi = h * MEM_HEAD_DIM, (h + 1) * MEM_HEAD_DIM
            p = _mem_probs(q_ref[:, lo:hi].astype(MXU_DTYPE), kv_ref[:, lo:hi].astype(MXU_DTYPE))
            v = kv_ref[:, MEM_WIDTH + lo:MEM_WIDTH + hi].astype(MXU_DTYPE)
            o_ref[:, lo:hi] = _dot(p.astype(MXU_DTYPE), v, 1, 0).astype(o_ref.dtype)

    return pl.pallas_call(
        body, name="mem_attn_fwd", grid=(T // tq,),
        in_specs=[pl.BlockSpec((tq, MEM_WIDTH), lambda i: (i, 0)), pl.BlockSpec((M, 2 * MEM_WIDTH), lambda i: (0, 0))],
        out_specs=pl.BlockSpec((tq, MEM_WIDTH), lambda i: (i, 0)),
        out_shape=jax.ShapeDtypeStruct((T, MEM_WIDTH), MXU_DTYPE),
        compiler_params=_params("parallel"),
    )(cq, kv)


def mem_attn_bwd(cq, kv, d_out):
    T = cq.shape[0]
    M = kv.shape[0]
    tq = _tile(T, 512)

    def body(q_ref, kv_ref, do_ref, dq_ref, dkv_ref):
        @pl.when(pl.program_id(0) == 0)
        def _():
            dkv_ref[...] = jnp.zeros_like(dkv_ref)

        for h in range(MEM_HEADS):
            lo, hi = h * MEM_HEAD_DIM, (h + 1) * MEM_HEAD_DIM
            q = q_ref[:, lo:hi].astype(MXU_DTYPE)
            k = kv_ref[:, lo:hi].astype(MXU_DTYPE)
            v = kv_ref[:, MEM_WIDTH + lo:MEM_WIDTH + hi].astype(MXU_DTYPE)
            do = do_ref[:, lo:hi].astype(MXU_DTYPE)
            p = _mem_probs(q, k)
            dp = _dot(do, v, 1, 1)
            ds = (p * (dp - jnp.sum(p * dp, axis=-1, keepdims=True)) * (MEM_HEAD_DIM ** -0.5)).astype(MXU_DTYPE)
            dq_ref[:, lo:hi] = _dot(ds, k, 1, 0).astype(dq_ref.dtype)
            dkv_ref[:, lo:hi] += _dot(ds, q, 0, 0)
            dkv_ref[:, MEM_WIDTH + lo:MEM_WIDTH + hi] += _dot(p.astype(MXU_DTYPE), do, 0, 0)

    rows = pl.BlockSpec((tq, MEM_WIDTH), lambda i: (i, 0))
    whole = pl.BlockSpec((M, 2 * MEM_WIDTH), lambda i: (0, 0))
    return pl.pallas_call(
        body, name="mem_attn_bwd", grid=(T // tq,),
        in_specs=[rows, whole, rows], out_specs=[rows, whole],
        out_shape=[jax.ShapeDtypeStruct((T, MEM_WIDTH), MXU_DTYPE), jax.ShapeDtypeStruct((M, 2 * MEM_WIDTH), F32)],
        compiler_params=_params("arbitrary"),
    )(cq, kv, d_out)


LRU_ROWS = 256


def _shift_down(ext, k, rows):
    return pltpu.roll(ext, k, 0)[SUBLANES:SUBLANES + rows] if k else ext[SUBLANES:SUBLANES + rows]


def _shift_up(ext, k, rows):
    return pltpu.roll(ext, rows + SUBLANES - k, 0)[:rows] if k else ext[:rows]


def _conv(x_ext, conv_w_ref, conv_b_ref, rows):
    xc = conv_b_ref[...] + conv_w_ref[CONV_WIDTH - 1:CONV_WIDTH, :] * _shift_down(x_ext, 0, rows)
    for j in range(CONV_WIDTH - 1):
        xc = xc + conv_w_ref[j:j + 1, :] * _shift_down(x_ext, CONV_WIDTH - 1 - j, rows)
    return xc


def _block_matmul(x, w_ref, cb):
    return jnp.concatenate(
        [_dot(x[:, c * LANES:(c + 1) * LANES].astype(MXU_DTYPE), w_ref[c], 1, cb)
         for c in range(LRU_WIDTH // LRU_TILE)], axis=1)


def _lru_gates(block, xc, wa_ref, ba_ref, wx_ref, bx_ref, sp_ref):
    rows = xc.shape[0]
    ra = _sigmoid(_block_matmul(xc, wa_ref, 0) + ba_ref[...])
    gi = _sigmoid(_block_matmul(xc, wx_ref, 0) + bx_ref[...])
    log_a = -LRU_C * ra * sp_ref[...]
    a = jnp.exp(log_a)
    one_minus = 1.0 - jnp.exp(2.0 * log_a)
    mult = jnp.sqrt(jnp.maximum(one_minus, 0.0))
    row = lax.broadcasted_iota(jnp.int32, (rows, 1), 0)
    first = row == jnp.where(block == 0, 0, -1)
    mult = jnp.where(first, 1.0, mult)
    return ra, gi, a, one_minus, mult, first


def _gelu(x):
    inner = np.sqrt(2.0 / np.pi).astype(np.float32) * (x + 0.044715 * (x * x * x))
    return 0.5 * x * (1.0 + jnp.tanh(inner))


def _gelu_grad(x):
    c = np.sqrt(2.0 / np.pi).astype(np.float32)
    t = jnp.tanh(c * (x + 0.044715 * (x * x * x)))
    return 0.5 * (1.0 + t) + 0.5 * x * (1.0 - t * t) * c * (1.0 + 3.0 * 0.044715 * (x * x))


def _x_ext(block, prev_ref, cur_ref):
    prev = jnp.where(block > 0, prev_ref[...], 0.0)
    return jnp.concatenate([prev, cur_ref[...]], axis=0)


def _lru_in_specs(tb, order, n_rows):
    ratio = tb // SUBLANES
    rows = pl.BlockSpec((tb, LRU_WIDTH), lambda i: (order(i), 0))
    prev8 = pl.BlockSpec((SUBLANES, LRU_WIDTH), lambda i: (jnp.maximum(order(i) * ratio - 1, 0), 0))
    vec = pl.BlockSpec((1, LRU_WIDTH), lambda i: (0, 0))
    conv_w = pl.BlockSpec((CONV_WIDTH, LRU_WIDTH), lambda i: (0, 0))
    tiles = pl.BlockSpec((LRU_WIDTH // LRU_TILE, LRU_TILE, LRU_TILE), lambda i: (0, 0, 0))
    return rows, prev8, vec, conv_w, tiles


def _linear_scan(a, b, carry, *, reverse):
    rows, width = a.shape
    groups = rows // SUBLANES
    a3 = a.reshape(groups, SUBLANES, width)
    b3 = b.reshape(groups, SUBLANES, width)
    sub = lax.broadcasted_iota(jnp.int32, (1, SUBLANES, 1), 1)
    shift = 1
    while shift < SUBLANES:
        keep = (sub < SUBLANES - shift) if reverse else (sub >= shift)
        amount = SUBLANES - shift if reverse else shift
        b3 = b3 + a3 * jnp.where(keep, pltpu.roll(b3, amount, 1), 0.0)
        a3 = a3 * jnp.where(keep, pltpu.roll(a3, amount, 1), 1.0)
        shift *= 2
    out = [None] * groups
    edge = 0 if reverse else SUBLANES - 1
    for g in (reversed(range(groups)) if reverse else range(groups)):
        out[g] = b3[g] + a3[g] * carry
        carry = out[g][edge:edge + 1, :]
    return jnp.concatenate(out, axis=0), carry


def lru_fwd(xr, yr, conv_w, conv_b, wa_t, ba, wx_t, bx, sp):
    T = xr.shape[0]
    tb = _tile(T, LRU_ROWS)

    def body(xp_ref, x_ref, y_ref, cw_ref, cb_ref, wa_ref, ba_ref, wx_ref, bx_ref, sp_ref, h_ref, r_ref, carry_ref):
        block = pl.program_id(0)
        xc = _conv(_x_ext(block, xp_ref, x_ref), cw_ref, cb_ref, tb)
        _, gi, a, _, mult, _ = _lru_gates(block, xc, wa_ref, ba_ref, wx_ref, bx_ref, sp_ref)
        b = mult * gi * xc

        @pl.when(block == 0)
        def _():
            carry_ref[...] = jnp.zeros_like(carry_ref)

        h, carry_ref[...] = _linear_scan(a, b, carry_ref[...], reverse=False)
        h_ref[...] = h
        r_ref[...] = (h * _gelu(y_ref[...])).astype(r_ref.dtype)

    rows, prev8, vec, cw, tiles = _lru_in_specs(tb, lambda i: i, T)
    return pl.pallas_call(
        body, name="lru_fwd", grid=(T // tb,),
        in_specs=[prev8, rows, rows, cw, vec, tiles, vec, tiles, vec, vec],
        out_specs=[rows, rows],
        out_shape=[jax.ShapeDtypeStruct((T, LRU_WIDTH), F32), jax.ShapeDtypeStruct((T, LRU_WIDTH), MXU_DTYPE)],
        scratch_shapes=[pltpu.VMEM((1, LRU_WIDTH), F32)],
        compiler_params=_params("arbitrary"),
    )(xr, xr, yr, conv_w, conv_b, wa_t, ba, wx_t, bx, sp)


def lru_bwd(xr, yr, h, d_r, conv_w, conv_b, wa_t, ba, wx_t, bx, sp):
    T = xr.shape[0]
    tb = _tile(T, LRU_ROWS)
    n_blocks = T // tb
    order = lambda i: n_blocks - 1 - i
    n_tiles = LRU_WIDTH // LRU_TILE

    def body(xp_ref, x_ref, y_ref, hp_ref, h_ref, dr_ref, cw_ref, cb_ref, wa_ref, ba_ref, wx_ref, bx_ref, sp_ref,
             dx_ref, dy_ref, dwa_ref, dwx_ref, dba_ref, dbx_ref, dsp_ref, dcb_ref, dcw_ref, lam_ref, a_next_ref, dxc_next_ref):
        step = pl.program_id(0)
        block = order(step)

        @pl.when(step == 0)
        def _():
            for ref in (lam_ref, a_next_ref, dxc_next_ref, dwa_ref, dwx_ref, dba_ref, dbx_ref, dsp_ref, dcb_ref, dcw_ref):
                ref[...] = jnp.zeros_like(ref)

        x_ext = _x_ext(block, xp_ref, x_ref)
        xc = _conv(x_ext, cw_ref, cb_ref, tb)
        ra, gi, a, one_minus, mult, first = _lru_gates(block, xc, wa_ref, ba_ref, wx_ref, bx_ref, sp_ref)
        yv = y_ref[...]
        hv = h_ref[...]
        drv = dr_ref[...].astype(F32)
        dy_ref[...] = (drv * hv * _gelu_grad(yv)).astype(dy_ref.dtype)

        row = lax.broadcasted_iota(jnp.int32, (tb, 1), 0)
        coef = jnp.where(row == tb - 1, a_next_ref[...], pltpu.roll(a, tb - 1, 0))
        lam, lam_ref[...] = _linear_scan(coef, drv * _gelu(yv), lam_ref[...], reverse=True)
        a_next_ref[...] = a[0:1, :]

        h_prev = _shift_down(jnp.concatenate([jnp.where(block > 0, hp_ref[...], 0.0), hv], axis=0), 1, tb)
        d_a = lam * h_prev
        d_mult = jnp.where(first, 0.0, lam * gi * xc)
        d_gi = lam * mult * xc
        d_xc = lam * mult * gi
        safe = one_minus > 0.0
        d_log_a = d_a * a + d_mult * jnp.where(safe, -(1.0 - one_minus) * lax.rsqrt(jnp.where(safe, one_minus, 1.0)), 0.0)
        d_za = d_log_a * (-LRU_C * sp_ref[...]) * ra * (1.0 - ra)
        d_zx = d_gi * gi * (1.0 - gi)
        dsp_ref[...] += jnp.sum(d_log_a * (-LRU_C * ra), axis=0, keepdims=True)
        dba_ref[...] += jnp.sum(d_za, axis=0, keepdims=True)
        dbx_ref[...] += jnp.sum(d_zx, axis=0, keepdims=True)
        d_xc = d_xc + _block_matmul(d_za, wa_ref, 1) + _block_matmul(d_zx, wx_ref, 1)
        dcb_ref[...] += jnp.sum(d_xc, axis=0, keepdims=True)
        d_ext = jnp.concatenate([d_xc, dxc_next_ref[...]], axis=0)
        dxc_next_ref[...] = d_xc[:SUBLANES]
        dx = None
        for j in range(CONV_WIDTH):
            k = CONV_WIDTH - 1 - j
            term = cw_ref[j:j + 1, :] * _shift_up(d_ext, k, tb)
            dx = term if dx is None else dx + term
            dcw_ref[j:j + 1, :] += jnp.sum(d_xc * _shift_down(x_ext, k, tb), axis=0, keepdims=True)
        dx_ref[...] = dx.astype(dx_ref.dtype)
        xc_m = xc.astype(MXU_DTYPE)
        for c in range(n_tiles):
            lanes = slice(c * LANES, (c + 1) * LANES)
            dwa_ref[c] += _dot(xc_m[:, lanes], d_za[:, lanes].astype(MXU_DTYPE), 0, 0)
            dwx_ref[c] += _dot(xc_m[:, lanes], d_zx[:, lanes].astype(MXU_DTYPE), 0, 0)

    rows, prev8, vec, cw, tiles = _lru_in_specs(tb, order, T)
    return pl.pallas_call(
        body, name="lru_bwd", grid=(n_blocks,),
        in_specs=[prev8, rows, rows, prev8, rows, rows, cw, vec, tiles, vec, tiles, vec, vec],
        out_specs=[rows, rows, tiles, tiles, vec, vec, vec, vec, cw],
        out_shape=[jax.ShapeDtypeStruct((T, LRU_WIDTH), MXU_DTYPE), jax.ShapeDtypeStruct((T, LRU_WIDTH), MXU_DTYPE),
                   jax.ShapeDtypeStruct((n_tiles, LRU_TILE, LRU_TILE), F32),
                   jax.ShapeDtypeStruct((n_tiles, LRU_TILE, LRU_TILE), F32)]
        + [jax.ShapeDtypeStruct((1, LRU_WIDTH), F32)] * 4 + [jax.ShapeDtypeStruct((CONV_WIDTH, LRU_WIDTH), F32)],
        scratch_shapes=[pltpu.VMEM((1, LRU_WIDTH), F32), pltpu.VMEM((1, LRU_WIDTH), F32),
                        pltpu.VMEM((SUBLANES, LRU_WIDTH), F32)],
        compiler_params=_params("arbitrary"),
    )(xr, xr, yr, h, h, d_r, conv_w, conv_b, wa_t, ba, wx_t, bx, sp)


def _whole(w):
    return pl.BlockSpec(w.shape, lambda i: (0, 0), pipeline_mode=pl.Buffered(1))


def merge_fwd(mixers, weights, gates, bias):
    T = gates.shape[0]
    tm = _tile(T, 512)
    D = D_MODEL
    contract = (1, 0, 1)

    def body(a_ref, r_ref, c_ref, wa_ref, wr_ref, wc_ref, g_ref, b_ref, m_ref, ba_ref, br_ref, bc_ref):
        acc = None
        for k, (x_ref, w_ref, out_ref) in enumerate(((a_ref, wa_ref, ba_ref), (r_ref, wr_ref, br_ref), (c_ref, wc_ref, bc_ref))):
            cols = slice(k * D, (k + 1) * D)
            branch = _dot(x_ref[...], w_ref[...], 1, contract[k])
            out_ref[...] = branch.astype(out_ref.dtype)
            term = _sigmoid(g_ref[:, cols] + b_ref[:, cols]) * branch
            acc = term if acc is None else acc + term
        m_ref[...] = acc.astype(m_ref.dtype)

    rows = pl.BlockSpec((tm, D), lambda i: (i, 0))
    act = jax.ShapeDtypeStruct((T, D), MXU_DTYPE)
    return pl.pallas_call(
        body, name="merge_fwd", grid=(T // tm,),
        in_specs=[pl.BlockSpec((tm, m.shape[1]), lambda i: (i, 0)) for m in mixers] + [_whole(w) for w in weights]
        + [pl.BlockSpec((tm, 3 * D), lambda i: (i, 0)), pl.BlockSpec((1, 3 * D), lambda i: (0, 0))],
        out_specs=[rows] * 4, out_shape=[act] * 4, compiler_params=_params("parallel"),
    )(*mixers, *weights, gates, bias)


def merge_bwd(dy, w_out, weights, gates, bias, branches):
    T = gates.shape[0]
    tm = _tile(T, 256)
    D = D_MODEL
    contract = (0, 1, 0)

    def body(dy_ref, wo_ref, wa_ref, wr_ref, wc_ref, g_ref, b_ref, ba_ref, br_ref, bc_ref,
             dba_ref, dbr_ref, dbc_ref, da_ref, dr_ref, dc_ref, dg_ref, db_ref):
        @pl.when(pl.program_id(0) == 0)
        def _():
            db_ref[...] = jnp.zeros_like(db_ref)

        dm = _dot(dy_ref[...].astype(MXU_DTYPE), wo_ref[...], 1, 1)
        for k, (w_ref, branch_ref, dbranch_ref, dmixer_ref) in enumerate(
                ((wa_ref, ba_ref, dba_ref, da_ref), (wr_ref, br_ref, dbr_ref, dr_ref), (wc_ref, bc_ref, dbc_ref, dc_ref))):
            cols = slice(k * D, (k + 1) * D)
            s = _sigmoid(g_ref[:, cols] + b_ref[:, cols])
            d_branch = (dm * s).astype(MXU_DTYPE)
            dbranch_ref[...] = d_branch
            dmixer_ref[...] = _dot(d_branch, w_ref[...], 1, contract[k]).astype(dmixer_ref.dtype)
            d_pre = dm * branch_ref[...].astype(F32) * s * (1.0 - s)
            dg_ref[:, cols] = d_pre.astype(dg_ref.dtype)
            db_ref[:, cols] += jnp.sum(d_pre, axis=0, keepdims=True)

    rows = pl.BlockSpec((tm, D), lambda i: (i, 0))
    half = pl.BlockSpec((tm, ATTN_WIDTH), lambda i: (i, 0))
    wide = pl.BlockSpec((tm, 3 * D), lambda i: (i, 0))
    vec = pl.BlockSpec((1, 3 * D), lambda i: (0, 0))
    act = jax.ShapeDtypeStruct((T, D), MXU_DTYPE)
    return pl.pallas_call(
        body, name="merge_bwd", grid=(T // tm,),
        in_specs=[rows, _whole(w_out)] + [_whole(w) for w in weights] + [wide, vec, rows, rows, rows],
        out_specs=[rows, rows, rows, half, rows, half, wide, vec],
        out_shape=[act, act, act, jax.ShapeDtypeStruct((T, ATTN_WIDTH), MXU_DTYPE), jax.ShapeDtypeStruct((T, D), F32),
                   jax.ShapeDtypeStruct((T, MEM_WIDTH), MXU_DTYPE), jax.ShapeDtypeStruct((T, 3 * D), MXU_DTYPE),
                   jax.ShapeDtypeStruct((1, 3 * D), F32)],
        compiler_params=_params("arbitrary"),
    )(dy, w_out, *weights, gates, bias, *branches)


def loss_head(x, g, target):
    T, D = x.shape
    tm = _tile(T, 512)

    def body(x_ref, g_ref, t_ref, loss_ref, dx_ref, dg_ref):
        @pl.when(pl.program_id(0) == 0)
        def _():
            loss_ref[...] = jnp.zeros_like(loss_ref)
            dg_ref[...] = jnp.zeros_like(dg_ref)

        xv = x_ref[...]
        gv = g_ref[...]
        r = lax.rsqrt(jnp.mean(xv * xv, axis=-1, keepdims=True) + EPS)
        xh = xv * r
        err = xh * gv - t_ref[...]
        per_row = jnp.mean(err * err, axis=-1, keepdims=True)
        loss_ref[...] += jnp.broadcast_to(0.5 * jnp.sum(per_row, axis=0, keepdims=True), loss_ref.shape)
        dyv = err * (1.0 / D)
        dg_ref[...] += jnp.sum(dyv * xh, axis=0, keepdims=True)
        gd = dyv * gv
        dx_ref[...] = r * (gd - xh * jnp.mean(gd * xh, axis=-1, keepdims=True))

    rows = pl.BlockSpec((tm, D), lambda i: (i, 0))
    vec = pl.BlockSpec((1, D), lambda i: (0, 0))
    return pl.pallas_call(
        body, name="loss_head", grid=(T // tm,),
        in_specs=[rows, vec, rows], out_specs=[pl.BlockSpec((1, LANES), lambda i: (0, 0)), rows, vec],
        out_shape=[jax.ShapeDtypeStruct((1, LANES), F32), jax.ShapeDtypeStruct((T, D), F32),
                   jax.ShapeDtypeStruct((1, D), F32)],
        compiler_params=_params("arbitrary"),
    )(x, g, target)


def _adamw_math(w, g, m, v):
    m = ADAM_B1 * m + (1.0 - ADAM_B1) * g
    v = ADAM_B2 * v + (1.0 - ADAM_B2) * (g * g)
    m_hat = m / (1.0 - ADAM_B1 ** ADAM_STEP)
    v_hat = v / (1.0 - ADAM_B2 ** ADAM_STEP)
    delta = -ADAM_LR * (m_hat / (jnp.sqrt(v_hat) + ADAM_EPS) + ADAM_WD * w)
    return delta, m, v


def _as_rows(a):
    return a.reshape(-1, a.shape[-1])


def sum_parts(layers):
    n, R, C = layers[0].shape
    depth = len(layers)
    tr = _tile(R, 512 if C <= 1024 else 128)
    blocks = R // tr

    def body(*refs):
        o_ref = refs[-1]
        for l in range(depth):
            @pl.when(pl.program_id(0) == l)
            def _(p_ref=refs[l]):
                acc = p_ref[0].astype(F32)
                for k in range(1, n):
                    acc = acc + p_ref[k].astype(F32)
                o_ref[...] = acc

    in_specs = [pl.BlockSpec((n, tr, C), lambda L, i, l=l: (0, jnp.where(L == l, i, 0), 0)) for l in range(depth)]
    return pl.pallas_call(
        body, name="sum_parts", grid=(depth, blocks), in_specs=in_specs,
        out_specs=pl.BlockSpec((tr, C), lambda L, i: (L * blocks + i, 0)),
        out_shape=jax.ShapeDtypeStruct((depth * R, C), F32), compiler_params=_params("arbitrary", "arbitrary"),
    )(*layers)


def adamw_sum(w, parts, m, v):
    stacked = not isinstance(parts, (list, tuple))
    n = parts.shape[0] if stacked else len(parts)
    R, C = w.shape
    tr = _tile(R, 512 if n <= 2 and C <= 1024 else 256)

    def body(*refs):
        w_ref, m_ref, v_ref = refs[:3]
        g_ref, d_ref, nm_ref, nv_ref = refs[-4:]
        terms = [refs[3][k] for k in range(n)] if stacked else [p_ref[...] for p_ref in refs[3:3 + n]]
        g = terms[0]
        for term in terms[1:]:
            g = g + term
        delta, nm, nv = _adamw_math(w_ref[...], g, m_ref[...], v_ref[...])
        g_ref[...] = g
        d_ref[...] = delta
        nm_ref[...] = nm
        nv_ref[...] = nv

    rows = pl.BlockSpec((tr, C), lambda i: (i, 0))
    part_specs = [pl.BlockSpec((n, tr, C), lambda i: (0, i, 0))] if stacked else [rows] * n
    shape = jax.ShapeDtypeStruct((R, C), F32)
    return pl.pallas_call(
        body, name="adamw_sum", grid=(R // tr,), in_specs=[rows] * 3 + part_specs, out_specs=[rows] * 4,
        out_shape=[shape] * 4, compiler_params=_params("parallel"),
    )(w, m, v, *([parts] if stacked else parts))


def _place():
    return lax.axis_index("x"), lax.axis_index("y"), lax.axis_index("c")


HBM_SPEC = pl.BlockSpec(memory_space=pltpu.HBM)
SEM_SPEC = pl.BlockSpec(memory_space=pltpu.SEMAPHORE)
DATAFLOW = pltpu.SideEffectType.DATAFLOW_SIDE_EFFECTING


PATTERNS = {"gather": ([(1, 0, 0), (0, 1, 0), (1, 1, 0)], N_CHIPS), "scatter": ([(1, 0, 0), (0, 1, 0), (1, 1, 0)], None),
            "sibling": ([(0, 0, 1)], None),
            "all": ([(fx, fy, fc) for fx in (0, 1) for fy in (0, 1) for fc in (0, 1)][1:], N_DEVICES)}


def _exchange_copies(pattern, srcs, lands, send_sems, recv_sems, local_sems):
    flips, _ = PATTERNS[pattern]
    x, y, c = _place()

    def slot(px, py, pc):
        return 4 * px + 2 * py + pc if pattern == "all" else 2 * px + py

    me = slot(x, y, c)
    outgoing, arriving, local = [], [], []
    for k, (src, land) in enumerate(zip(srcs, lands)):
        for d, (fx, fy, fc) in enumerate(flips):
            peer = (x ^ fx, y ^ fy, c ^ fc)
            pair = dict(send_sem=send_sems.at[len(flips) * k + d], recv_sem=recv_sems.at[len(flips) * k + d],
                        device_id=peer, device_id_type=MESH)
            if pattern == "sibling":
                outgoing.append(pltpu.make_async_remote_copy(src_ref=src, dst_ref=land, **pair))
                arriving.append(outgoing[-1])
                continue
            scatter = pattern == "scatter"
            outgoing.append(pltpu.make_async_remote_copy(
                src_ref=src.at[slot(*peer)] if scatter else src, dst_ref=land.at[me], **pair))
            arriving.append(pltpu.make_async_remote_copy(
                src_ref=src.at[me] if scatter else src, dst_ref=land.at[slot(*peer)], **pair))
        if pattern != "sibling":
            local.append(pltpu.make_async_copy(src.at[me] if pattern == "scatter" else src, land.at[me], local_sems.at[k]))
    return outgoing, arriving, local


def _semaphore_shapes(pattern, n):
    flips, _ = PATTERNS[pattern]
    local = [] if pattern == "sibling" else [pltpu.SemaphoreType.DMA((n,))]
    return [pltpu.SemaphoreType.DMA((len(flips) * n,)), pltpu.SemaphoreType.DMA((len(flips) * n,))] + local


def exchange_start(arrays, *, pattern, name):
    n = len(arrays)
    lead = PATTERNS[pattern][1]
    lands = [lax.empty(a.shape if lead is None else (lead,) + a.shape, a.dtype) for a in arrays]
    sem_shapes = _semaphore_shapes(pattern, n)
    n_sems = len(sem_shapes)

    def body(*refs):
        srcs, zones = refs[:n], refs[n:2 * n]
        sems = list(refs[2 * n:2 * n + n_sems]) + [None]
        token = refs[-1]
        outgoing, _, local = _exchange_copies(pattern, srcs, zones, *sems[:3])
        for cp in outgoing + local:
            cp.start()
        token[...] = jnp.zeros_like(token)

    hbm = lambda a: pltpu.HBM(a.shape, a.dtype)
    outs = pl.pallas_call(
        body, name=name,
        out_shape=(*sem_shapes, *[hbm(a) for a in arrays], *[hbm(a) for a in lands],
                   jax.ShapeDtypeStruct((SUBLANES, LANES), F32)),
        in_specs=[HBM_SPEC] * (2 * n),
        out_specs=(*[SEM_SPEC] * n_sems, *[HBM_SPEC] * (2 * n), pl.BlockSpec(memory_space=pltpu.VMEM)),
        input_output_aliases={i: n_sems + i for i in range(2 * n)},
        compiler_params=pltpu.CompilerParams(has_side_effects=DATAFLOW),
    )(*[pltpu.with_memory_space_constraint(a, pltpu.HBM) for a in list(arrays) + lands])
    return outs[:n_sems], outs[n_sems:n_sems + n], outs[n_sems + n:n_sems + 2 * n], outs[-1]


def exchange_wait(started, after, *, pattern, name, with_sources=False):
    sems, srcs, lands, _ = started
    n = len(srcs)
    n_sems = len(sems)

    def body(*refs):
        src_refs, zones = refs[:n], refs[n:2 * n]
        sem_refs = list(refs[2 * n:2 * n + n_sems]) + [None]
        outgoing, arriving, local = _exchange_copies(pattern, src_refs, zones, *sem_refs[:3])
        for sent, landed in zip(outgoing, arriving):
            sent.wait_send()
            landed.wait_recv()
        for cp in local:
            cp.wait()

    hbm = lambda a: pltpu.HBM(a.shape, a.dtype)
    outs = pl.pallas_call(
        body, name=name, out_shape=[hbm(a) for a in list(srcs) + list(lands)],
        in_specs=[HBM_SPEC] * (2 * n) + [SEM_SPEC] * n_sems + [pl.BlockSpec(memory_space=pl.ANY)],
        out_specs=[HBM_SPEC] * (2 * n), input_output_aliases={i: i for i in range(2 * n)},
        compiler_params=pltpu.CompilerParams(has_side_effects=DATAFLOW),
    )(*srcs, *lands, *sems, after)
    return (outs[:n], outs[n:]) if with_sources else outs[n:]


BIG = ("ffn1_w_gate", "ffn1_w_up", "ffn1_w_down", "w_in", "w_attn_out", "w_lru_out", "w_mem_kv", "w_mem_out",
       "w_out", "ffn2_w_gate", "ffn2_w_up", "ffn2_w_down")
TRANSPOSED = ("ffn1_w_gate", "ffn1_w_up", "w_in", "w_attn_out", "w_mem_out", "ffn2_w_gate", "ffn2_w_up")
SMALL = ("ffn1_norm", "mix_norm", "gate_bias", "attn_sinks", "conv_w", "conv_b", "lru_wa", "lru_ba", "lru_wx", "lru_bx",
         "lru_lambda", "mem_norm", "ffn2_norm", "final_norm")
WEIGHTS = ("ffn1_norm", "ffn1_w_gate", "ffn1_w_up", "ffn1_w_down", "mix_norm", "w_in", "gate_bias", "attn_sinks",
           "w_attn_out", "conv_w", "conv_b", "lru_wa", "lru_ba", "lru_wx", "lru_bx", "lru_lambda", "w_lru_out", "mem_norm",
           "w_mem_kv", "w_mem_out", "w_out", "ffn2_norm", "ffn2_w_gate", "ffn2_w_up", "ffn2_w_down", "final_norm")
SEGMENTS = (("qkv", 0, 768), ("xr", 768, 1792), ("yr", 1792, 2816), ("cq", 2816, 3328), ("gates", 3328, 6400))
PACK_ROW = 1024


STAGES = ("ffn1", "mix", "ffn2")
STAGE_BIG = {"ffn1": ("ffn1_w_gate", "ffn1_w_up", "ffn1_w_down"),
             "mix": ("w_in", "w_attn_out", "w_lru_out", "w_mem_kv", "w_mem_out", "w_out"),
             "ffn2": ("ffn2_w_gate", "ffn2_w_up", "ffn2_w_down")}


def _row_sharded(state, name):
    return jnp.swapaxes(state[name], 1, 2) if name in TRANSPOSED else state[name]


def _join_shards(gathered, name):
    if name == "conv_w":
        return jnp.concatenate([gathered[s] for s in range(N_CHIPS)], axis=-1)
    return gathered.reshape(-1, gathered.shape[-1])


def _split_shards(full):
    return full.reshape(N_CHIPS, -1, full.shape[-1])


def _lane_tiles(w):
    z = jnp.zeros((LRU_WIDTH // LRU_TILE, HEAD_DIM, HEAD_DIM), w.dtype)
    top = jnp.concatenate([w[0::2], z], axis=2)
    bottom = jnp.concatenate([z, w[1::2]], axis=2)
    return jnp.concatenate([top, bottom], axis=1)


def _from_lane_tiles(t):
    pairs = jnp.stack([t[:, :HEAD_DIM, :HEAD_DIM], t[:, HEAD_DIM:, HEAD_DIM:]], axis=1)
    return pairs.reshape(2 * t.shape[0], HEAD_DIM, HEAD_DIM)


def _pack_rows(size):
    return -(-size // (SUBLANES * PACK_ROW)) * SUBLANES


def _pack(arrays):
    blocks = []
    for a in arrays:
        flat = a.reshape(-1).astype(F32)
        rows = _pack_rows(flat.shape[0])
        blocks.append(jnp.pad(flat, (0, rows * PACK_ROW - flat.shape[0])).reshape(rows, PACK_ROW))
    return jnp.concatenate(blocks, axis=0)


def _unpack(packed, shapes):
    out, at = [], 0
    for shape in shapes:
        size = int(np.prod(shape))
        rows = _pack_rows(size)
        out.append(packed[at:at + rows].reshape(-1)[:size].reshape(shape))
        at += rows
    return out


def _ffn_forward(x, h, w_gate_t, w_up_t, w_down_after, next_norm):
    G, U, A = ffn_up(h, w_gate_t, w_up_t)
    out = mm([A], [w_down_after(G)], nt=False, out_dtype=F32, res=x, scale=0.5, norm=next_norm, name="ffn_down")
    y, h_next = out if next_norm is not None else (out, None)
    return y, h_next, (x, h, G, U, A)


def _ffn_backward(half, dy, saved, norm, w_gate_t, w_up_t, w_down):
    x, h, G, U, A = saved
    dG, dU = ffn_bwd_act(half, dy, w_down, G, U)
    d_down = mm_tn(A, dy, out_dtype=MXU_DTYPE, b_scale=0.5, name="ffn_dw_down")
    d_gate_t = mm_tn(dG, h, out_dtype=MXU_DTYPE, name="ffn_dw_up")
    d_up_t = mm_tn(dU, h, out_dtype=MXU_DTYPE, name="ffn_dw_up")
    dx, d_norm = mm_rms_bwd([dG, dU], [w_gate_t, w_up_t], x, norm, dy, nt=False, name="ffn_dx")
    return dx, d_norm, d_gate_t, d_up_t, d_down


def kernel(x, mem, ffn1_norm, ffn1_w_gate, ffn1_w_up, ffn1_w_down, mix_norm, w_in, gate_bias, attn_sinks, w_attn_out, conv_w, conv_b, lru_wa, lru_ba, lru_wx, lru_bx, lru_lambda, w_lru_out, mem_norm, w_mem_kv, w_mem_out, w_out, ffn2_norm, ffn2_w_gate, ffn2_w_up, ffn2_w_down, final_norm, loss_target, m_ffn1_norm, m_ffn1_w_gate, m_ffn1_w_up, m_ffn1_w_down, m_mix_norm, m_w_in, m_gate_bias, m_attn_sinks, m_w_attn_out, m_conv_w, m_conv_b, m_lru_wa, m_lru_ba, m_lru_wx, m_lru_bx, m_lru_lambda, m_w_lru_out, m_mem_norm, m_w_mem_kv, m_w_mem_out, m_w_out, m_ffn2_norm, m_ffn2_w_gate, m_ffn2_w_up, m_ffn2_w_down, m_final_norm, v_ffn1_norm, v_ffn1_w_gate, v_ffn1_w_up, v_ffn1_w_down, v_mix_norm, v_w_in, v_gate_bias, v_attn_sinks, v_w_attn_out, v_conv_w, v_conv_b, v_lru_wa, v_lru_ba, v_lru_wx, v_lru_bx, v_lru_lambda, v_w_lru_out, v_mem_norm, v_w_mem_kv, v_w_mem_out, v_w_out, v_ffn2_norm, v_ffn2_w_gate, v_ffn2_w_up, v_ffn2_w_down, v_final_norm):
    args = dict(locals())
    W = {n: args[n] for n in WEIGHTS}
    M = {n: args["m_" + n] for n in WEIGHTS}
    V = {n: args["v_" + n] for n in WEIGHTS}
    chip = 2 * lax.axis_index("x") + lax.axis_index("y")
    x0 = x[0]
    mem0 = mem[0]
    target = loss_target[0]

    def row(v):
        return v.reshape(1, -1)

    def stage_names(stage):
        return STAGE_BIG[stage] + (("conv_w",) if stage == "mix" else ())

    RW, RM, RV = ({n: _row_sharded(state, n) for n in BIG} for state in (W, M, V))
    gathers = {}
    started_all = jnp.zeros((), F32)
    groups = {}
    for l in range(DEPTH):
        for stage in STAGES:
            groups[l, stage] = stage_names(stage)
            if (l, stage) == (0, "ffn1"):
                groups[l, stage], groups[l, "ffn1_down"] = groups[l, stage][:2], groups[l, stage][2:]
    for (l, stage), names in groups.items():
        shards = [W[n][l] if n == "conv_w" else RW[n][l].astype(MXU_DTYPE) for n in names]
        gathers[l, stage] = exchange_start(shards, pattern="gather", name=f"gather_start_{stage}_{l}")
        started_all = started_all + gathers[l, stage][3][0, 0]

    def weights_of(l, stage, after):
        lands = exchange_wait(gathers[l, stage], after, pattern="gather", name=f"gather_wait_{stage}_{l}")
        return {n: _join_shards(g, n) for n, g in zip(groups[l, stage], lands)}

    saved = []
    full = []
    xs = x0
    h = rms_fwd(x0, row(W["ffn1_norm"][0]) + started_all)
    for l in range(DEPTH):
        P = weights_of(l, "ffn1", xs)

        def ffn1_w_down(after, P=P, l=l):
            if (l, "ffn1_down") in groups:
                P.update(weights_of(l, "ffn1_down", after))
            return P["ffn1_w_down"]

        xs, h, ffn1 = _ffn_forward(xs, h, P["ffn1_w_gate"], P["ffn1_w_up"], ffn1_w_down, row(W["mix_norm"][l]))
        x_mix = xs
        P.update(weights_of(l, "mix", xs))
        seg = dict(zip([name for name, _, _ in SEGMENTS], proj_fwd(h, P["w_in"], SEGMENTS)))
        attn = attn_fwd(seg["qkv"], W["attn_sinks"][l])
        wa_t = _lane_tiles(W["lru_wa"][l]).astype(MXU_DTYPE)
        wx_t = _lane_tiles(W["lru_wx"][l]).astype(MXU_DTYPE)
        sp = row(jax.nn.softplus(-W["lru_lambda"][l]))
        lru_consts = (P["conv_w"], row(W["conv_b"][l]), wa_t, row(W["lru_ba"][l]), wx_t, row(W["lru_bx"][l]), sp)
        h_lru, rec = lru_fwd(seg["xr"], seg["yr"], *lru_consts)
        mem_n = rms_fwd(mem0, row(W["mem_norm"][l]))
        kv = mm([mem_n], [P["w_mem_kv"]], nt=False, out_dtype=F32, name="mem_kv")
        cross = mem_attn_fwd(seg["cq"], kv)
        bias = row(W["gate_bias"][l])
        branch_weights = (P["w_attn_out"], P["w_lru_out"], P["w_mem_out"])
        merged, *branches = merge_fwd((attn, rec, cross), branch_weights, seg["gates"], bias)
        h_mix = h
        xs, h = mm([merged], [P["w_out"]], nt=False, out_dtype=F32, res=x_mix, norm=row(W["ffn2_norm"][l]), name="mix_out")
        mix = (x_mix, h_mix, seg, attn, lru_consts, h_lru, rec, mem_n, kv, cross, branches, bias, merged)
        P.update(weights_of(l, "ffn2", xs))
        next_norm = row(W["ffn1_norm"][l + 1]) if l + 1 < DEPTH else None
        xs, h, ffn2 = _ffn_forward(xs, h, P["ffn2_w_gate"], P["ffn2_w_up"], lambda after, P=P: P["ffn2_w_down"], next_norm)
        saved.append((ffn1, mix, ffn2))
        full.append(P)

    loss_lanes, dx, d_final = loss_head(xs, row(W["final_norm"]), target)
    loss = lax.psum(loss_lanes[0, 0], ("x", "y", "c"))

    big_grads = {}
    small_grads = {n: [None] * DEPTH for n in SMALL if n != "final_norm"}
    scatters = {}
    token = jnp.zeros((), F32)

    def send_grads(l, stage):
        chunks = [_split_shards(big_grads[n, l]) for n in STAGE_BIG[stage]]
        scatters[l, stage] = exchange_start(chunks, pattern="scatter", name=f"scatter_start_{stage}_{l}")
        return scatters[l, stage][3][0, 0]

    for l in reversed(range(DEPTH)):
        P = full[l]
        ffn1, mix, ffn2 = saved[l]
        dx, g_norm, g_gate, g_up, g_down = _ffn_backward(
            (0.5 + token).reshape(1), dx, ffn2, row(W["ffn2_norm"][l]), P["ffn2_w_gate"], P["ffn2_w_up"], P["ffn2_w_down"])
        small_grads["ffn2_norm"][l] = g_norm
        big_grads["ffn2_w_gate", l], big_grads["ffn2_w_up", l], big_grads["ffn2_w_down", l] = g_gate, g_up, g_down
        token = send_grads(l, "ffn2")

        x_mix, h, seg, attn, lru_consts, h_lru, rec, mem_n, kv, cross, branches, bias, merged = mix
        bias = bias + token
        big_grads["w_out", l] = mm_tn(merged, dx, out_dtype=MXU_DTYPE, name="dw_out")
        d_attn_br, d_lru_br, d_mem_br, d_attn, d_rec, d_cross, d_gates, d_bias = merge_bwd(
            dx, P["w_out"], (P["w_attn_out"], P["w_lru_out"], P["w_mem_out"]), seg["gates"], bias, branches)
        small_grads["gate_bias"][l] = d_bias
        big_grads["w_attn_out", l] = mm_tn(d_attn_br, attn, out_dtype=MXU_DTYPE, name="dw_attn_out")
        big_grads["w_lru_out", l] = mm_tn(rec, d_lru_br, out_dtype=MXU_DTYPE, name="dw_lru_out")
        big_grads["w_mem_out", l] = mm_tn(d_mem_br, cross, out_dtype=MXU_DTYPE, name="dw_mem_out")

        d_qkv, d_sinks = attn_bwd(seg["qkv"], W["attn_sinks"][l], d_attn)
        small_grads["attn_sinks"][l] = d_sinks

        d_xr, d_yr, d_wa_t, d_wx_t, d_ba, d_bx, d_sp, d_cb, d_cw = lru_bwd(seg["xr"], seg["yr"], h_lru, d_rec, *lru_consts)
        small_grads["conv_w"][l] = d_cw
        small_grads["conv_b"][l] = d_cb
        small_grads["lru_wa"][l] = _from_lane_tiles(d_wa_t)
        small_grads["lru_wx"][l] = _from_lane_tiles(d_wx_t)
        small_grads["lru_ba"][l] = d_ba
        small_grads["lru_bx"][l] = d_bx
        small_grads["lru_lambda"][l] = -d_sp * _sigmoid(-row(W["lru_lambda"][l]))

        d_cq, d_kv = mem_attn_bwd(seg["cq"], kv, d_cross)
        big_grads["w_mem_kv", l] = mm_tn(mem_n, d_kv, out_dtype=MXU_DTYPE, name="dw_mem_kv")
        _, g_mem_norm = mm_rms_bwd([d_kv], [P["w_mem_kv"]], mem0, row(W["mem_norm"][l]), None, nt=True, name="mem_dnorm")
        small_grads["mem_norm"][l] = g_mem_norm

        d_seg = {"qkv": d_qkv, "xr": d_xr, "yr": d_yr, "cq": d_cq, "gates": d_gates}
        big_grads["w_in", l] = jnp.concatenate(
            [mm_tn(d_seg[name], h, out_dtype=MXU_DTYPE, name="dw_in_" + name) for name, _, _ in SEGMENTS], axis=0)
        dx, g_norm = mm_rms_bwd([d_seg[name] for name, _, _ in SEGMENTS], [P["w_in"]], x_mix, row(W["mix_norm"][l]), dx,
                                nt=False, b_rows=[(lo, hi) for _, lo, hi in SEGMENTS], name="mix_dx")
        small_grads["mix_norm"][l] = g_norm
        token = send_grads(l, "mix")

        dx, g_norm, g_gate, g_up, g_down = _ffn_backward(
            (0.5 + token).reshape(1), dx, ffn1, row(W["ffn1_norm"][l]), P["ffn1_w_gate"], P["ffn1_w_up"], P["ffn1_w_down"])
        small_grads["ffn1_norm"][l] = g_norm
        big_grads["ffn1_w_gate", l], big_grads["ffn1_w_up", l], big_grads["ffn1_w_down", l] = g_gate, g_up, g_down
        token = send_grads(l, "ffn1")

    grad_x = dx[None]

    small_list = [jnp.stack(small_grads[n]) for n in SMALL if n != "final_norm"] + [d_final]
    small_shapes = [(DEPTH,) + W[n].shape[1:] if n != "conv_w" else (DEPTH, CONV_WIDTH, LRU_WIDTH)
                    for n in SMALL if n != "final_norm"] + [W["final_norm"].shape]
    small_exchange = exchange_start([_pack(small_list)], pattern="all", name="small_start")

    arrived = {}
    backward_done = dx[:1, :1] + token + small_exchange[3][0, 0]
    for l in range(DEPTH):
        for stage in STAGES:
            lands = exchange_wait(scatters[l, stage], backward_done, pattern="scatter", name=f"scatter_wait_{stage}_{l}")
            for n, a in zip(STAGE_BIG[stage], lands):
                arrived[n, l] = a.reshape(N_CHIPS, -1, a.shape[-1])
    siblings = {}
    after = jnp.zeros((1, 1), F32)
    for stage in STAGES:
        partial = [sum_parts([arrived[n, l] for l in range(DEPTH)]) for n in STAGE_BIG[stage]]
        siblings[stage] = exchange_start(partial, pattern="sibling", name=f"sibling_start_{stage}")
        after = after + siblings[stage][3][0, 0]
    results = {}
    for stage in STAGES:
        own, theirs = exchange_wait(siblings[stage], after, pattern="sibling", name=f"sibling_wait_{stage}", with_sources=True)
        for n, mine, other in zip(STAGE_BIG[stage], own, theirs):
            outs = adamw_sum(_as_rows(RW[n]), [mine, other], _as_rows(RM[n]), _as_rows(RV[n]))
            after = outs[0]
            outs = [o.reshape(RW[n].shape) for o in outs]
            results[n] = [jnp.swapaxes(o, 1, 2) for o in outs] if n in TRANSPOSED else outs
    everyone = exchange_wait(small_exchange, after, pattern="all", name="small_wait")[0]

    def own_columns(full_conv):
        return lax.dynamic_slice_in_dim(full_conv, chip * (LRU_WIDTH // N_CHIPS), LRU_WIDTH // N_CHIPS, axis=2)

    def packed_state(state):
        arrays = []
        for n in SMALL:
            a = state[n]
            if n == "conv_w":
                a = lax.dynamic_update_slice_in_dim(jnp.zeros((DEPTH, CONV_WIDTH, LRU_WIDTH), F32), a,
                                                    chip * (LRU_WIDTH // N_CHIPS), axis=2)
            arrays.append(a)
        return _pack(arrays)

    small_outs = adamw_sum(packed_state(W), everyone, packed_state(M), packed_state(V))
    for kind, packed in enumerate(small_outs):
        for n, a in zip(SMALL, _unpack(packed, small_shapes)):
            results.setdefault(n, [None] * 4)[kind] = own_columns(a) if n == "conv_w" else a.reshape(W[n].shape)

    return (loss, grad_x, *[results[n][0] for n in WEIGHTS], *[results[n][1] for n in WEIGHTS],
            *[results[n][2] for n in WEIGHTS], *[results[n][3] for n in WEIGHTS])
```

```python
import functools

import jax
import jax.numpy as jnp
import numpy as np
from jax import lax
from jax.experimental import pallas as pl
from jax.experimental.pallas import tpu as pltpu

F32 = jnp.float32
MXU_DTYPE = jnp.bfloat16

DEPTH = 4
D_MODEL = 1024
CHUNK = 64
HALO = 2 * CHUNK
N_Q_HEADS = 8
HEAD_DIM = 64
ATTN_WIDTH = 512
KV_WIDTH = 128
LRU_WIDTH = 1024
LRU_TILE = 128
CONV_WIDTH = 4
LRU_C = 8.0
MEM_HEADS = 4
MEM_HEAD_DIM = 128
MEM_WIDTH = 512
D_FF = 2816
EPS = 1e-6
ADAM_LR = 0.001
ADAM_B1 = 0.9
ADAM_B2 = 0.999
ADAM_EPS = 1e-08
ADAM_WD = 0.01
ADAM_STEP = 10

N_CHIPS = 4
N_DEVICES = 8
SUBLANES = 8
LANES = 128
VMEM_LIMIT_BYTES = 56 * 1024 * 1024
NEG_BIG = -1e30
MESH = pl.DeviceIdType.MESH

ALIBI_SLOPES = tuple(float(2.0 ** (-8.0 * (i + 1) / N_Q_HEADS)) for i in range(N_Q_HEADS))


def _params(*semantics):
    return pltpu.CompilerParams(dimension_semantics=semantics, vmem_limit_bytes=VMEM_LIMIT_BYTES)


def _tile(n, pref, unit=SUBLANES):
    if n <= pref:
        return n
    for t in range(pref - pref % unit, 0, -unit):
        if n % t == 0:
            return t
    return n


def _dot(a, b, ca, cb):
    return lax.dot_general(a, b, (((ca,), (cb,)), ((), ())), preferred_element_type=F32)


def _sigmoid(x):
    return 0.5 * jnp.tanh(0.5 * x) + 0.5


def rms_fwd(x, g):
    T, D = x.shape
    tm = _tile(T, 512)

    def body(x_ref, g_ref, h_ref):
        xv = x_ref[...]
        r = lax.rsqrt(jnp.mean(xv * xv, axis=-1, keepdims=True) + EPS)
        h_ref[...] = (xv * r * g_ref[...]).astype(h_ref.dtype)

    return pl.pallas_call(
        body, name="rms_fwd", grid=(T // tm,),
        in_specs=[pl.BlockSpec((tm, D), lambda i: (i, 0)), pl.BlockSpec((1, D), lambda i: (0, 0))],
        out_specs=pl.BlockSpec((tm, D), lambda i: (i, 0)),
        out_shape=jax.ShapeDtypeStruct((T, D), MXU_DTYPE),
        compiler_params=_params("parallel"),
    )(x, g)


def mm_rms_bwd(a_list, b_list, x, g, dy, *, nt, b_rows=None, tm=512, name="mm_rms_bwd"):
    n_a, n_b = len(a_list), len(b_list)
    T, D = x.shape
    tm = _tile(T, tm)
    has_dy = dy is not None

    def body(*refs):
        a_refs, b_refs = refs[:n_a], refs[n_a:n_a + n_b]
        x_ref, g_ref = refs[n_a + n_b:n_a + n_b + 2]
        dx_ref, dg_ref = refs[-2:]
        i = pl.program_id(0)
        dh = None
        for p, a_ref in enumerate(a_refs):
            b = b_refs[p][...] if b_rows is None else b_refs[0][b_rows[p][0]:b_rows[p][1], :]
            d = _dot(a_ref[...].astype(MXU_DTYPE), b, 1, 1 if nt else 0)
            dh = d if dh is None else dh + d
        xv = x_ref[...]
        r = lax.rsqrt(jnp.mean(xv * xv, axis=-1, keepdims=True) + EPS)
        xh = xv * r
        gd = dh * g_ref[...]
        dx = r * (gd - xh * jnp.mean(gd * xh, axis=-1, keepdims=True))
        dx_ref[...] = refs[n_a + n_b + 2][...] + dx if has_dy else dx
        part = jnp.sum(dh * xh, axis=0, keepdims=True)

        @pl.when(i == 0)
        def _():
            dg_ref[...] = part

        @pl.when(i > 0)
        def _():
            dg_ref[...] += part

    row = pl.BlockSpec((tm, D), lambda i: (i, 0))
    vec = pl.BlockSpec((1, D), lambda i: (0, 0))
    in_specs = [pl.BlockSpec((tm, a.shape[1]), lambda i: (i, 0)) for a in a_list]
    in_specs += [pl.BlockSpec(b.shape, lambda i: (0, 0), pipeline_mode=pl.Buffered(1)) for b in b_list]
    in_specs += [row, vec] + ([row] if has_dy else [])
    operands = list(a_list) + list(b_list) + [x, g] + ([dy] if has_dy else [])
    return pl.pallas_call(
        body, name=name, grid=(T // tm,), in_specs=in_specs, out_specs=[row, vec],
        out_shape=[jax.ShapeDtypeStruct((T, D), F32), jax.ShapeDtypeStruct((1, D), F32)],
        compiler_params=_params("arbitrary"),
    )(*operands)


def mm(a_list, b_list, *, nt, out_dtype, res=None, scale=1.0, norm=None, tm=512, tn=1024, name="mm"):
    n_pairs = len(a_list)
    T = a_list[0].shape[0]
    N = b_list[0].shape[0] if nt else b_list[0].shape[1]
    tm = _tile(T, tm)
    tn = _tile(N, tn, LANES)
    has_res = res is not None
    has_norm = norm is not None
    assert not has_norm or tn == N

    def body(*refs):
        a_refs = refs[:n_pairs]
        b_refs = refs[n_pairs:2 * n_pairs]
        extra = list(refs[2 * n_pairs:])
        acc = None
        for a_ref, b_ref in zip(a_refs, b_refs):
            d = _dot(a_ref[...].astype(MXU_DTYPE), b_ref[...], 1, 1 if nt else 0)
            acc = d if acc is None else acc + d
        if has_res:
            acc = extra.pop(0)[...] + scale * acc
        elif scale != 1.0:
            acc = scale * acc
        if has_norm:
            g_ref = extra.pop(0)
            r = lax.rsqrt(jnp.mean(acc * acc, axis=-1, keepdims=True) + EPS)
            extra[1][...] = (acc * r * g_ref[...]).astype(extra[1].dtype)
        extra[0][...] = acc.astype(extra[0].dtype)

    in_specs = [pl.BlockSpec((tm, a.shape[1]), lambda j, i: (i, 0)) for a in a_list]
    if nt:
        in_specs += [pl.BlockSpec((tn, b.shape[1]), lambda j, i: (j, 0)) for b in b_list]
    else:
        in_specs += [pl.BlockSpec((b.shape[0], tn), lambda j, i: (0, j)) for b in b_list]
    out_spec = pl.BlockSpec((tm, tn), lambda j, i: (i, j))
    operands = list(a_list) + list(b_list)
    if has_res:
        in_specs.append(out_spec)
        operands.append(res)
    if has_norm:
        in_specs.append(pl.BlockSpec((1, N), lambda j, i: (0, 0)))
        operands.append(norm)
    out_shape = jax.ShapeDtypeStruct((T, N), out_dtype)
    return pl.pallas_call(
        body, name=name, grid=(N // tn, T // tm), in_specs=in_specs,
        out_specs=[out_spec, out_spec] if has_norm else out_spec,
        out_shape=[out_shape, jax.ShapeDtypeStruct((T, N), MXU_DTYPE)] if has_norm else out_shape,
        compiler_params=_params("parallel", "parallel"),
    )(*operands)


def proj_fwd(h, w_in_t, segments):
    T, D = h.shape
    tm = _tile(T, 256)

    def body(h_ref, w_ref, *out_refs):
        hv = h_ref[...]
        for (_, lo, hi), o_ref in zip(segments, out_refs):
            o_ref[...] = _dot(hv, w_ref[lo:hi, :], 1, 1)

    return pl.pallas_call(
        body, name="proj_fwd", grid=(T // tm,),
        in_specs=[pl.BlockSpec((tm, D), lambda i: (i, 0)),
                  pl.BlockSpec(w_in_t.shape, lambda i: (0, 0), pipeline_mode=pl.Buffered(1))],
        out_specs=[pl.BlockSpec((tm, hi - lo), lambda i: (i, 0)) for _, lo, hi in segments],
        out_shape=[jax.ShapeDtypeStruct((T, hi - lo), F32) for _, lo, hi in segments],
        compiler_params=_params("parallel"),
    )(h, w_in_t)


def mm_tn(a, b, *, out_dtype, b_scale=1.0, tk=1408, tn=1408, tt=None, name="mm_tn"):
    T, K = a.shape
    N = b.shape[1]
    tk = _tile(K, tk, LANES)
    tn = _tile(N, tn, LANES)
    if tt is None:
        tt = 1024 if b.dtype == F32 else 2048
    tt = _tile(T, tt)
    steps = T // tt

    def body(a_ref, b_ref, o_ref, acc_ref):
        t = pl.program_id(2)
        bv = b_ref[...]
        if b_scale != 1.0:
            bv = b_scale * bv
        d = _dot(a_ref[...].astype(MXU_DTYPE), bv.astype(MXU_DTYPE), 0, 0)

        @pl.when(t == 0)
        def _():
            acc_ref[...] = d

        @pl.when(t > 0)
        def _():
            acc_ref[...] += d

        @pl.when(t == steps - 1)
        def _():
            o_ref[...] = acc_ref[...].astype(o_ref.dtype)

    return pl.pallas_call(
        body, name=name, grid=(K // tk, N // tn, steps),
        in_specs=[pl.BlockSpec((tt, tk), lambda p, q, t: (t, p)), pl.BlockSpec((tt, tn), lambda p, q, t: (t, q))],
        out_specs=pl.BlockSpec((tk, tn), lambda p, q, t: (p, q)),
        out_shape=jax.ShapeDtypeStruct((K, N), out_dtype),
        scratch_shapes=[pltpu.VMEM((tk, tn), F32)],
        compiler_params=_params("parallel", "parallel", "arbitrary"),
    )(a, b)


def _lane_chunks(n, width=3 * LANES):
    return [slice(lo, min(lo + width, n)) for lo in range(0, n, width)]


def ffn_up(h, w_gate_t, w_up_t):
    T, D = h.shape
    F = w_gate_t.shape[0]
    tm = _tile(T, 512)
    tn = _tile(F, 1408, LANES)

    def body(h_ref, wg_ref, wu_ref, g_ref, u_ref, a_ref):
        hv = h_ref[...]
        G = _dot(hv, wg_ref[...], 1, 1)
        U = _dot(hv, wu_ref[...], 1, 1)
        g_ref[...] = G.astype(g_ref.dtype)
        u_ref[...] = U.astype(u_ref.dtype)
        a_ref[...] = (G * _sigmoid(G) * U).astype(a_ref.dtype)

    w_spec = pl.BlockSpec((tn, D), lambda j, i: (j, 0))
    o_spec = pl.BlockSpec((tm, tn), lambda j, i: (i, j))
    return pl.pallas_call(
        body, name="ffn_up", grid=(F // tn, T // tm),
        in_specs=[pl.BlockSpec((tm, D), lambda j, i: (i, 0)), w_spec, w_spec],
        out_specs=[o_spec, o_spec, o_spec],
        out_shape=[jax.ShapeDtypeStruct((T, F), MXU_DTYPE)] * 3,
        compiler_params=_params("parallel", "parallel"),
    )(h, w_gate_t, w_up_t)


def ffn_bwd_act(half, dy, w_down, G, U):
    T, D = dy.shape
    F = w_down.shape[0]
    tm = _tile(T, 512)

    def body(half_ref, dy_ref, wd_ref, g_ref, u_ref, dg_ref, du_ref):
        d_out = (half_ref[0] * dy_ref[...]).astype(MXU_DTYPE)
        for cols in _lane_chunks(F):
            dA = _dot(d_out, wd_ref[cols, :], 1, 1)
            Gv = g_ref[:, cols].astype(F32)
            s = _sigmoid(Gv)
            du_ref[:, cols] = (dA * (Gv * s)).astype(du_ref.dtype)
            dg_ref[:, cols] = (dA * u_ref[:, cols].astype(F32) * (s * (1.0 + Gv * (1.0 - s)))).astype(dg_ref.dtype)

    o_spec = pl.BlockSpec((tm, F), lambda i: (i, 0))
    return pl.pallas_call(
        body, name="ffn_bwd_act", grid=(T // tm,),
        in_specs=[pl.BlockSpec(memory_space=pltpu.SMEM), pl.BlockSpec((tm, D), lambda i: (i, 0)),
                  pl.BlockSpec(w_down.shape, lambda i: (0, 0), pipeline_mode=pl.Buffered(1)), o_spec, o_spec],
        out_specs=[o_spec, o_spec],
        out_shape=[jax.ShapeDtypeStruct((T, F), MXU_DTYPE), jax.ShapeDtypeStruct((T, F), MXU_DTYPE)],
        compiler_params=_params("parallel"),
    )(half, dy, w_down, G, U)


ATTN_ROWS = 256


def _head_variants(kv128):
    lane = lax.broadcasted_iota(jnp.int32, kv128.shape, 1)
    low = lane < HEAD_DIM
    rolled = pltpu.roll(kv128, HEAD_DIM, 1)
    out = {}
    for j in (0, 1):
        for pos in (0, 1):
            src = kv128 if j == pos else rolled
            out[j, pos] = jnp.where(low if pos == 0 else jnp.logical_not(low), src, 0.0).astype(MXU_DTYPE)
    return out


def _fold_variant(d128, j, pos):
    lane = lax.broadcasted_iota(jnp.int32, d128.shape, 1)
    low = lane < HEAD_DIM
    kept = jnp.where(low if pos == 0 else jnp.logical_not(low), d128, 0.0)
    return kept if j == pos else pltpu.roll(kept, HEAD_DIM, 1)


def _attn_mask(block, rows, keys):
    r = lax.broadcasted_iota(jnp.int32, (rows, keys), 0)
    c = lax.broadcasted_iota(jnp.int32, (rows, keys), 1)
    q_chunk = r // CHUNK
    k_chunk = c // CHUNK - HALO // CHUNK
    first_chunk = jnp.where(block > 0, -(HALO // CHUNK), 0)
    valid = (k_chunk <= q_chunk) & (k_chunk >= q_chunk - HALO // CHUNK) & (k_chunk >= first_chunk)
    dist = jnp.abs(r + HALO - c).astype(F32)
    return valid, dist


def _attn_probs(q128, k_var, head, sink, valid, dist):
    s = _dot(q128, k_var, 1, 1) * (HEAD_DIM ** -0.5) - ALIBI_SLOPES[head] * dist
    s = jnp.where(valid, s, NEG_BIG)
    mx = jnp.maximum(jnp.max(s, axis=-1, keepdims=True), sink)
    p = jnp.exp(s - mx)
    p_sink = jnp.exp(sink - mx)
    inv = 1.0 / (jnp.sum(p, axis=-1, keepdims=True) + p_sink)
    return p * inv, p_sink * inv


def _attn_specs(T, tq, order):
    ratio = tq // HALO
    q_spec = pl.BlockSpec((tq, ATTN_WIDTH), lambda i: (order(i), 0))
    cur = lambda col: pl.BlockSpec((tq, KV_WIDTH), lambda i: (order(i), col))
    prev = lambda col: pl.BlockSpec((HALO, KV_WIDTH), lambda i: (jnp.maximum(order(i) * ratio - 1, 0), col))
    k_col, v_col = ATTN_WIDTH // KV_WIDTH, ATTN_WIDTH // KV_WIDTH + 1
    return [pl.BlockSpec(memory_space=pltpu.SMEM), q_spec, prev(k_col), cur(k_col), prev(v_col), cur(v_col)]


def attn_fwd(qkv, sinks):
    T = qkv.shape[0]
    tq = _tile(T, ATTN_ROWS)
    keys = tq + HALO

    def body(sink_ref, q_ref, kp_ref, kc_ref, vp_ref, vc_ref, o_ref):
        block = pl.program_id(0)
        k_var = _head_variants(jnp.concatenate([kp_ref[...], kc_ref[...]], axis=0))
        v_var = _head_variants(jnp.concatenate([vp_ref[...], vc_ref[...]], axis=0))
        valid, dist = _attn_mask(block, tq, keys)
        for pair in range(N_Q_HEADS // 2):
            q128 = q_ref[:, pair * LANES:(pair + 1) * LANES].astype(MXU_DTYPE)
            acc = None
            for pos in (0, 1):
                head = 2 * pair + pos
                j = head // 4
                p, _ = _attn_probs(q128, k_var[j, pos], head, sink_ref[head], valid, dist)
                d = _dot(p.astype(MXU_DTYPE), v_var[j, pos], 1, 0)
                acc = d if acc is None else acc + d
            o_ref[:, pair * LANES:(pair + 1) * LANES] = acc.astype(o_ref.dtype)

    return pl.pallas_call(
        body, name="attn_fwd", grid=(T // tq,),
        in_specs=_attn_specs(T, tq, lambda i: i),
        out_specs=pl.BlockSpec((tq, ATTN_WIDTH), lambda i: (i, 0)),
        out_shape=jax.ShapeDtypeStruct((T, ATTN_WIDTH), MXU_DTYPE),
        compiler_params=_params("parallel"),
    )(sinks, qkv, qkv, qkv, qkv, qkv)


def attn_bwd(qkv, sinks, d_out):
    T = qkv.shape[0]
    tq = _tile(T, ATTN_ROWS)
    keys = tq + HALO
    n_blocks = T // tq
    order = lambda i: n_blocks - 1 - i

    def body(sink_ref, q_ref, kp_ref, kc_ref, vp_ref, vc_ref, do_ref, dqkv_ref, dsink_ref, carry_ref):
        step = pl.program_id(0)
        block = order(step)
        k_var = _head_variants(jnp.concatenate([kp_ref[...], kc_ref[...]], axis=0))
        v_var = _head_variants(jnp.concatenate([vp_ref[...], vc_ref[...]], axis=0))
        valid, dist = _attn_mask(block, tq, keys)

        @pl.when(step == 0)
        def _():
            carry_ref[...] = jnp.zeros_like(carry_ref)
            dsink_ref[...] = jnp.zeros_like(dsink_ref)

        dk = jnp.zeros((keys, KV_WIDTH), F32)
        dv = jnp.zeros((keys, KV_WIDTH), F32)
        for pair in range(N_Q_HEADS // 2):
            q128 = q_ref[:, pair * LANES:(pair + 1) * LANES].astype(MXU_DTYPE)
            do128 = do_ref[:, pair * LANES:(pair + 1) * LANES].astype(MXU_DTYPE)
            dq128 = None
            for pos in (0, 1):
                head = 2 * pair + pos
                j = head // 4
                p, p_sink = _attn_probs(q128, k_var[j, pos], head, sink_ref[head], valid, dist)
                dp = _dot(do128, v_var[j, pos], 1, 1)
                delta = jnp.sum(p * dp, axis=-1, keepdims=True)
                ds = (p * (dp - delta) * (HEAD_DIM ** -0.5)).astype(MXU_DTYPE)
                d = _dot(ds, k_var[j, pos], 1, 0)
                dq128 = d if dq128 is None else dq128 + d
                dk = dk + _fold_variant(_dot(ds, q128, 0, 0), j, pos)
                dv = dv + _fold_variant(_dot(p.astype(MXU_DTYPE), do128, 0, 0), j, pos)
                dsink_ref[pl.ds(head, 1), :] += jnp.broadcast_to(
                    -jnp.sum(p_sink * delta, axis=0, keepdims=True), (1, LANES))
            dqkv_ref[:, pair * LANES:(pair + 1) * LANES] = dq128.astype(dqkv_ref.dtype)
        carried = carry_ref[...]
        if tq > HALO:
            carried = jnp.concatenate([jnp.zeros((tq - HALO, 2 * KV_WIDTH), F32), carried], axis=0)
        dkv = jnp.concatenate([dk, dv], axis=1)
        dqkv_ref[:, ATTN_WIDTH:] = (dkv[HALO:] + carried).astype(dqkv_ref.dtype)
        carry_ref[...] = dkv[:HALO]

    out_rows = pl.BlockSpec((tq, ATTN_WIDTH + 2 * KV_WIDTH), lambda i: (order(i), 0))
    dqkv, dsink = pl.pallas_call(
        body, name="attn_bwd", grid=(n_blocks,),
        in_specs=_attn_specs(T, tq, order) + [pl.BlockSpec((tq, ATTN_WIDTH), lambda i: (order(i), 0))],
        out_specs=[out_rows, pl.BlockSpec((N_Q_HEADS, LANES), lambda i: (0, 0))],
        out_shape=[jax.ShapeDtypeStruct((T, ATTN_WIDTH + 2 * KV_WIDTH), MXU_DTYPE),
                   jax.ShapeDtypeStruct((N_Q_HEADS, LANES), F32)],
        scratch_shapes=[pltpu.VMEM((HALO, 2 * KV_WIDTH), F32)],
        compiler_params=_params("arbitrary"),
    )(sinks, qkv, qkv, qkv, qkv, qkv, d_out)
    return dqkv, dsink[:, 0]


def _mem_probs(q, k):
    s = _dot(q, k, 1, 1) * (MEM_HEAD_DIM ** -0.5)
    p = jnp.exp(s - jnp.max(s, axis=-1, keepdims=True))
    return p / jnp.sum(p, axis=-1, keepdims=True)


def mem_attn_fwd(cq, kv):
    T = cq.shape[0]
    M = kv.shape[0]
    tq = _tile(T, 512)

    def body(q_ref, kv_ref, o_ref):
        for h in range(MEM_HEADS):
            lo, hi = h * MEM_HEAD_DIM, (h + 1) * MEM_HEAD_DIM
            p = _mem_probs(q_ref[:, lo:hi].astype(MXU_DTYPE), kv_ref[:, lo:hi].astype(MXU_DTYPE))
            v = kv_ref[:, MEM_WIDTH + lo:MEM_WIDTH + hi].astype(MXU_DTYPE)
            o_ref[:, lo:hi] = _dot(p.astype(MXU_DTYPE), v, 1, 0).astype(o_ref.dtype)

    return pl.pallas_call(
        body, name="mem_attn_fwd", grid=(T // tq,),
        in_specs=[pl.BlockSpec((tq, MEM_WIDTH), lambda i: (i, 0)), pl.BlockSpec((M, 2 * MEM_WIDTH), lambda i: (0, 0))],
        out_specs=pl.BlockSpec((tq, MEM_WIDTH), lambda i: (i, 0)),
        out_shape=jax.ShapeDtypeStruct((T, MEM_WIDTH), MXU_DTYPE),
        compiler_params=_params("parallel"),
    )(cq, kv)


def mem_attn_bwd(cq, kv, d_out):
    T = cq.shape[0]
    M = kv.shape[0]
    tq = _tile(T, 512)

    def body(q_ref, kv_ref, do_ref, dq_ref, dkv_ref):
        @pl.when(pl.program_id(0) == 0)
        def _():
            dkv_ref[...] = jnp.zeros_like(dkv_ref)

        for h in range(MEM_HEADS):
            lo, hi = h * MEM_HEAD_DIM, (h + 1) * MEM_HEAD_DIM
            q = q_ref[:, lo:hi].astype(MXU_DTYPE)
            k = kv_ref[:, lo:hi].astype(MXU_DTYPE)
            v = kv_ref[:, MEM_WIDTH + lo:MEM_WIDTH + hi].astype(MXU_DTYPE)
            do = do_ref[:, lo:hi].astype(MXU_DTYPE)
            p = _mem_probs(q, k)
            dp = _dot(do, v, 1, 1)
            ds = (p * (dp - jnp.sum(p * dp, axis=-1, keepdims=True)) * (MEM_HEAD_DIM ** -0.5)).astype(MXU_DTYPE)
            dq_ref[:, lo:hi] = _dot(ds, k, 1, 0).astype(dq_ref.dtype)
            dkv_ref[:, lo:hi] += _dot(ds, q, 0, 0)
            dkv_ref[:, MEM_WIDTH + lo:MEM_WIDTH + hi] += _dot(p.astype(MXU_DTYPE), do, 0, 0)

    rows = pl.BlockSpec((tq, MEM_WIDTH), lambda i: (i, 0))
    whole = pl.BlockSpec((M, 2 * MEM_WIDTH), lambda i: (0, 0))
    return pl.pallas_call(
        body, name="mem_attn_bwd", grid=(T // tq,),
        in_specs=[rows, whole, rows], out_specs=[rows, whole],
        out_shape=[jax.ShapeDtypeStruct((T, MEM_WIDTH), MXU_DTYPE), jax.ShapeDtypeStruct((M, 2 * MEM_WIDTH), F32)],
        compiler_params=_params("arbitrary"),
    )(cq, kv, d_out)


LRU_ROWS = 256


def _shift_down(ext, k, rows):
    return pltpu.roll(ext, k, 0)[SUBLANES:SUBLANES + rows] if k else ext[SUBLANES:SUBLANES + rows]


def _shift_up(ext, k, rows):
    return pltpu.roll(ext, rows + SUBLANES - k, 0)[:rows] if k else ext[:rows]


def _conv(x_ext, conv_w_ref, conv_b_ref, rows):
    xc = conv_b_ref[...] + conv_w_ref[CONV_WIDTH - 1:CONV_WIDTH, :] * _shift_down(x_ext, 0, rows)
    for j in range(CONV_WIDTH - 1):
        xc = xc + conv_w_ref[j:j + 1, :] * _shift_down(x_ext, CONV_WIDTH - 1 - j, rows)
    return xc


def _block_matmul(x, w_ref, cb):
    return jnp.concatenate(
        [_dot(x[:, c * LANES:(c + 1) * LANES].astype(MXU_DTYPE), w_ref[c], 1, cb)
         for c in range(LRU_WIDTH // LRU_TILE)], axis=1)


def _lru_gates(block, xc, wa_ref, ba_ref, wx_ref, bx_ref, sp_ref):
    rows = xc.shape[0]
    ra = _sigmoid(_block_matmul(xc, wa_ref, 0) + ba_ref[...])
    gi = _sigmoid(_block_matmul(xc, wx_ref, 0) + bx_ref[...])
    log_a = -LRU_C * ra * sp_ref[...]
    a = jnp.exp(log_a)
    one_minus = 1.0 - jnp.exp(2.0 * log_a)
    mult = jnp.sqrt(jnp.maximum(one_minus, 0.0))
    row = lax.broadcasted_iota(jnp.int32, (rows, 1), 0)
    first = row == jnp.where(block == 0, 0, -1)
    mult = jnp.where(first, 1.0, mult)
    return ra, gi, a, one_minus, mult, first


def _gelu(x):
    inner = np.sqrt(2.0 / np.pi).astype(np.float32) * (x + 0.044715 * (x * x * x))
    return 0.5 * x * (1.0 + jnp.tanh(inner))


def _gelu_and_grad(x):
    c = np.sqrt(2.0 / np.pi).astype(np.float32)
    x2 = x * x
    t = jnp.tanh(c * (x + 0.044715 * (x2 * x)))
    half = 0.5 * (1.0 + t)
    return x * half, half + 0.5 * x * (1.0 - t * t) * c * (1.0 + 3.0 * 0.044715 * x2)


def _x_ext(block, prev_ref, cur_ref):
    prev = jnp.where(block > 0, prev_ref[...], 0.0)
    return jnp.concatenate([prev, cur_ref[...]], axis=0)


def _lru_in_specs(tb, order, n_rows):
    ratio = tb // SUBLANES
    rows = pl.BlockSpec((tb, LRU_WIDTH), lambda i: (order(i), 0))
    prev8 = pl.BlockSpec((SUBLANES, LRU_WIDTH), lambda i: (jnp.maximum(order(i) * ratio - 1, 0), 0))
    vec = pl.BlockSpec((1, LRU_WIDTH), lambda i: (0, 0))
    conv_w = pl.BlockSpec((CONV_WIDTH, LRU_WIDTH), lambda i: (0, 0))
    tiles = pl.BlockSpec((LRU_WIDTH // LRU_TILE, LRU_TILE, LRU_TILE), lambda i: (0, 0, 0))
    return rows, prev8, vec, conv_w, tiles


def _linear_scan(a, b, carry, *, reverse):
    rows, width = a.shape
    groups = rows // SUBLANES
    a3 = a.reshape(groups, SUBLANES, width)
    b3 = b.reshape(groups, SUBLANES, width)
    sub = lax.broadcasted_iota(jnp.int32, (1, SUBLANES, 1), 1)
    shift = 1
    while shift < SUBLANES:
        keep = (sub < SUBLANES - shift) if reverse else (sub >= shift)
        amount = SUBLANES - shift if reverse else shift
        b3 = b3 + a3 * jnp.where(keep, pltpu.roll(b3, amount, 1), 0.0)
        a3 = a3 * jnp.where(keep, pltpu.roll(a3, amount, 1), 1.0)
        shift *= 2
    out = [None] * groups
    edge = 0 if reverse else SUBLANES - 1
    for g in (reversed(range(groups)) if reverse else range(groups)):
        out[g] = b3[g] + a3[g] * carry
        carry = out[g][edge:edge + 1, :]
    return jnp.concatenate(out, axis=0), carry


def lru_fwd(xr, yr, conv_w, conv_b, wa_t, ba, wx_t, bx, sp):
    T = xr.shape[0]
    tb = _tile(T, LRU_ROWS)

    def body(xp_ref, x_ref, y_ref, cw_ref, cb_ref, wa_ref, ba_ref, wx_ref, bx_ref, sp_ref, h_ref, r_ref, carry_ref):
        block = pl.program_id(0)
        xc = _conv(_x_ext(block, xp_ref, x_ref), cw_ref, cb_ref, tb)
        _, gi, a, _, mult, _ = _lru_gates(block, xc, wa_ref, ba_ref, wx_ref, bx_ref, sp_ref)
        b = mult * gi * xc

        @pl.when(block == 0)
        def _():
            carry_ref[...] = jnp.zeros_like(carry_ref)

        h, carry_ref[...] = _linear_scan(a, b, carry_ref[...], reverse=False)
        h_ref[...] = h
        r_ref[...] = (h * _gelu(y_ref[...])).astype(r_ref.dtype)

    rows, prev8, vec, cw, tiles = _lru_in_specs(tb, lambda i: i, T)
    return pl.pallas_call(
        body, name="lru_fwd", grid=(T // tb,),
        in_specs=[prev8, rows, rows, cw, vec, tiles, vec, tiles, vec, vec],
        out_specs=[rows, rows],
        out_shape=[jax.ShapeDtypeStruct((T, LRU_WIDTH), F32), jax.ShapeDtypeStruct((T, LRU_WIDTH), MXU_DTYPE)],
        scratch_shapes=[pltpu.VMEM((1, LRU_WIDTH), F32)],
        compiler_params=_params("arbitrary"),
    )(xr, xr, yr, conv_w, conv_b, wa_t, ba, wx_t, bx, sp)


def lru_bwd(xr, yr, h, d_r, conv_w, conv_b, wa_t, ba, wx_t, bx, sp):
    T = xr.shape[0]
    tb = _tile(T, LRU_ROWS)
    n_blocks = T // tb
    order = lambda i: n_blocks - 1 - i
    n_tiles = LRU_WIDTH // LRU_TILE

    def body(xp_ref, x_ref, y_ref, hp_ref, h_ref, dr_ref, cw_ref, cb_ref, wa_ref, ba_ref, wx_ref, bx_ref, sp_ref,
             dx_ref, dy_ref, dwa_ref, dwx_ref, dba_ref, dbx_ref, dsp_ref, dcb_ref, dcw_ref, lam_ref, a_next_ref, dxc_next_ref):
        step = pl.program_id(0)
        block = order(step)

        @pl.when(step == 0)
        def _():
            for ref in (lam_ref, a_next_ref, dxc_next_ref, dwa_ref, dwx_ref, dba_ref, dbx_ref, dsp_ref, dcb_ref, dcw_ref):
                ref[...] = jnp.zeros_like(ref)

        x_ext = _x_ext(block, xp_ref, x_ref)
        xc = _conv(x_ext, cw_ref, cb_ref, tb)
        ra, gi, a, one_minus, mult, first = _lru_gates(block, xc, wa_ref, ba_ref, wx_ref, bx_ref, sp_ref)
        yv = y_ref[...]
        hv = h_ref[...]
        drv = dr_ref[...].astype(F32)
        gelu_y, gelu_grad_y = _gelu_and_grad(yv)
        dy_ref[...] = (drv * hv * gelu_grad_y).astype(dy_ref.dtype)

        row = lax.broadcasted_iota(jnp.int32, (tb, 1), 0)
        coef = jnp.where(row == tb - 1, a_next_ref[...], pltpu.roll(a, tb - 1, 0))
        lam, lam_ref[...] = _linear_scan(coef, drv * gelu_y, lam_ref[...], reverse=True)
        a_next_ref[...] = a[0:1, :]

        h_prev = _shift_down(jnp.concatenate([jnp.where(block > 0, hp_ref[...], 0.0), hv], axis=0), 1, tb)
        d_a = lam * h_prev
        d_mult = jnp.where(first, 0.0, lam * gi * xc)
        d_gi = lam * mult * xc
        d_xc = lam * mult * gi
        safe = one_minus > 0.0
        d_log_a = d_a * a + d_mult * jnp.where(safe, -(1.0 - one_minus) * lax.rsqrt(jnp.where(safe, one_minus, 1.0)), 0.0)
        d_za = d_log_a * (-LRU_C * sp_ref[...]) * ra * (1.0 - ra)
        d_zx = d_gi * gi * (1.0 - gi)
        dsp_ref[...] += jnp.sum(d_log_a * (-LRU_C * ra), axis=0, keepdims=True)
        dba_ref[...] += jnp.sum(d_za, axis=0, keepdims=True)
        dbx_ref[...] += jnp.sum(d_zx, axis=0, keepdims=True)
        d_xc = d_xc + _block_matmul(d_za, wa_ref, 1) + _block_matmul(d_zx, wx_ref, 1)
        dcb_ref[...] += jnp.sum(d_xc, axis=0, keepdims=True)
        d_ext = jnp.concatenate([d_xc, dxc_next_ref[...]], axis=0)
        dxc_next_ref[...] = d_xc[:SUBLANES]
        dx = None
        for j in range(CONV_WIDTH):
            k = CONV_WIDTH - 1 - j
            term = cw_ref[j:j + 1, :] * _shift_up(d_ext, k, tb)
            dx = term if dx is None else dx + term
            dcw_ref[j:j + 1, :] += jnp.sum(d_xc * _shift_down(x_ext, k, tb), axis=0, keepdims=True)
        dx_ref[...] = dx.astype(dx_ref.dtype)
        xc_m = xc.astype(MXU_DTYPE)
        for c in range(n_tiles):
            lanes = slice(c * LANES, (c + 1) * LANES)
            dwa_ref[c] += _dot(xc_m[:, lanes], d_za[:, lanes].astype(MXU_DTYPE), 0, 0)
            dwx_ref[c] += _dot(xc_m[:, lanes], d_zx[:, lanes].astype(MXU_DTYPE), 0, 0)

    rows, prev8, vec, cw, tiles = _lru_in_specs(tb, order, T)
    return pl.pallas_call(
        body, name="lru_bwd", grid=(n_blocks,),
        in_specs=[prev8, rows, rows, prev8, rows, rows, cw, vec, tiles, vec, tiles, vec, vec],
        out_specs=[rows, rows, tiles, tiles, vec, vec, vec, vec, cw],
        out_shape=[jax.ShapeDtypeStruct((T, LRU_WIDTH), MXU_DTYPE), jax.ShapeDtypeStruct((T, LRU_WIDTH), MXU_DTYPE),
                   jax.ShapeDtypeStruct((n_tiles, LRU_TILE, LRU_TILE), F32),
                   jax.ShapeDtypeStruct((n_tiles, LRU_TILE, LRU_TILE), F32)]
        + [jax.ShapeDtypeStruct((1, LRU_WIDTH), F32)] * 4 + [jax.ShapeDtypeStruct((CONV_WIDTH, LRU_WIDTH), F32)],
        scratch_shapes=[pltpu.VMEM((1, LRU_WIDTH), F32), pltpu.VMEM((1, LRU_WIDTH), F32),
                        pltpu.VMEM((SUBLANES, LRU_WIDTH), F32)],
        compiler_params=_params("arbitrary"),
    )(xr, xr, yr, h, h, d_r, conv_w, conv_b, wa_t, ba, wx_t, bx, sp)


def _whole(w):
    return pl.BlockSpec(w.shape, lambda i: (0, 0), pipeline_mode=pl.Buffered(1))


def merge_fwd(mixers, weights, gates, bias):
    T = gates.shape[0]
    tm = _tile(T, 512)
    D = D_MODEL
    contract = (1, 0, 1)

    def body(a_ref, r_ref, c_ref, wa_ref, wr_ref, wc_ref, g_ref, b_ref, m_ref, ba_ref, br_ref, bc_ref):
        acc = None
        for k, (x_ref, w_ref, out_ref) in enumerate(((a_ref, wa_ref, ba_ref), (r_ref, wr_ref, br_ref), (c_ref, wc_ref, bc_ref))):
            cols = slice(k * D, (k + 1) * D)
            branch = _dot(x_ref[...], w_ref[...], 1, contract[k])
            out_ref[...] = branch.astype(out_ref.dtype)
            term = _sigmoid(g_ref[:, cols] + b_ref[:, cols]) * branch
            acc = term if acc is None else acc + term
        m_ref[...] = acc.astype(m_ref.dtype)

    rows = pl.BlockSpec((tm, D), lambda i: (i, 0))
    act = jax.ShapeDtypeStruct((T, D), MXU_DTYPE)
    return pl.pallas_call(
        body, name="merge_fwd", grid=(T // tm,),
        in_specs=[pl.BlockSpec((tm, m.shape[1]), lambda i: (i, 0)) for m in mixers] + [_whole(w) for w in weights]
        + [pl.BlockSpec((tm, 3 * D), lambda i: (i, 0)), pl.BlockSpec((1, 3 * D), lambda i: (0, 0))],
        out_specs=[rows] * 4, out_shape=[act] * 4, compiler_params=_params("parallel"),
    )(*mixers, *weights, gates, bias)


def merge_bwd(dy, w_out, weights, gates, bias, branches):
    T = gates.shape[0]
    tm = _tile(T, 256)
    D = D_MODEL
    contract = (0, 1, 0)

    def body(dy_ref, wo_ref, wa_ref, wr_ref, wc_ref, g_ref, b_ref, ba_ref, br_ref, bc_ref,
             dba_ref, dbr_ref, dbc_ref, da_ref, dr_ref, dc_ref, dg_ref, db_ref):
        @pl.when(pl.program_id(0) == 0)
        def _():
            db_ref[...] = jnp.zeros_like(db_ref)

        dm = _dot(dy_ref[...].astype(MXU_DTYPE), wo_ref[...], 1, 1)
        for k, (w_ref, branch_ref, dbranch_ref, dmixer_ref) in enumerate(
                ((wa_ref, ba_ref, dba_ref, da_ref), (wr_ref, br_ref, dbr_ref, dr_ref), (wc_ref, bc_ref, dbc_ref, dc_ref))):
            cols = slice(k * D, (k + 1) * D)
            s = _sigmoid(g_ref[:, cols] + b_ref[:, cols])
            d_branch = (dm * s).astype(MXU_DTYPE)
            dbranch_ref[...] = d_branch
            dmixer_ref[...] = _dot(d_branch, w_ref[...], 1, contract[k]).astype(dmixer_ref.dtype)
            d_pre = dm * branch_ref[...].astype(F32) * s * (1.0 - s)
            dg_ref[:, cols] = d_pre.astype(dg_ref.dtype)
            db_ref[:, cols] += jnp.sum(d_pre, axis=0, keepdims=True)

    rows = pl.BlockSpec((tm, D), lambda i: (i, 0))
    half = pl.BlockSpec((tm, ATTN_WIDTH), lambda i: (i, 0))
    wide = pl.BlockSpec((tm, 3 * D), lambda i: (i, 0))
    vec = pl.BlockSpec((1, 3 * D), lambda i: (0, 0))
    act = jax.ShapeDtypeStruct((T, D), MXU_DTYPE)
    return pl.pallas_call(
        body, name="merge_bwd", grid=(T // tm,),
        in_specs=[rows, _whole(w_out)] + [_whole(w) for w in weights] + [wide, vec, rows, rows, rows],
        out_specs=[rows, rows, rows, half, rows, half, wide, vec],
        out_shape=[act, act, act, jax.ShapeDtypeStruct((T, ATTN_WIDTH), MXU_DTYPE), jax.ShapeDtypeStruct((T, D), F32),
                   jax.ShapeDtypeStruct((T, MEM_WIDTH), MXU_DTYPE), jax.ShapeDtypeStruct((T, 3 * D), MXU_DTYPE),
                   jax.ShapeDtypeStruct((1, 3 * D), F32)],
        compiler_params=_params("arbitrary"),
    )(dy, w_out, *weights, gates, bias, *branches)


def loss_head(x, g, target):
    T, D = x.shape
    tm = _tile(T, 512)

    def body(x_ref, g_ref, t_ref, loss_ref, dx_ref, dg_ref):
        @pl.when(pl.program_id(0) == 0)
        def _():
            loss_ref[...] = jnp.zeros_like(loss_ref)
            dg_ref[...] = jnp.zeros_like(dg_ref)

        xv = x_ref[...]
        gv = g_ref[...]
        r = lax.rsqrt(jnp.mean(xv * xv, axis=-1, keepdims=True) + EPS)
        xh = xv * r
        err = xh * gv - t_ref[...]
        per_row = jnp.mean(err * err, axis=-1, keepdims=True)
        loss_ref[...] += jnp.broadcast_to(0.5 * jnp.sum(per_row, axis=0, keepdims=True), loss_ref.shape)
        dyv = err * (1.0 / D)
        dg_ref[...] += jnp.sum(dyv * xh, axis=0, keepdims=True)
        gd = dyv * gv
        dx_ref[...] = r * (gd - xh * jnp.mean(gd * xh, axis=-1, keepdims=True))

    rows = pl.BlockSpec((tm, D), lambda i: (i, 0))
    vec = pl.BlockSpec((1, D), lambda i: (0, 0))
    return pl.pallas_call(
        body, name="loss_head", grid=(T // tm,),
        in_specs=[rows, vec, rows], out_specs=[pl.BlockSpec((1, LANES), lambda i: (0, 0)), rows, vec],
        out_shape=[jax.ShapeDtypeStruct((1, LANES), F32), jax.ShapeDtypeStruct((T, D), F32),
                   jax.ShapeDtypeStruct((1, D), F32)],
        compiler_params=_params("arbitrary"),
    )(x, g, target)


def _adamw_math(w, g, m, v):
    m = ADAM_B1 * m + (1.0 - ADAM_B1) * g
    v = ADAM_B2 * v + (1.0 - ADAM_B2) * (g * g)
    m_hat = m / (1.0 - ADAM_B1 ** ADAM_STEP)
    v_hat = v / (1.0 - ADAM_B2 ** ADAM_STEP)
    delta = -ADAM_LR * (m_hat / (jnp.sqrt(v_hat) + ADAM_EPS) + ADAM_WD * w)
    return delta, m, v


def _as_rows(a):
    return a.reshape(-1, a.shape[-1])


def sum_parts(layers):
    n, R, C = layers[0].shape
    depth = len(layers)
    tr = _tile(R, 512 if C <= 1024 else 128)
    blocks = R // tr

    def body(*refs):
        o_ref = refs[-1]
        for l in range(depth):
            @pl.when(pl.program_id(0) == l)
            def _(p_ref=refs[l]):
                acc = p_ref[0].astype(F32)
                for k in range(1, n):
                    acc = acc + p_ref[k].astype(F32)
                o_ref[...] = acc

    in_specs = [pl.BlockSpec((n, tr, C), lambda L, i, l=l: (0, jnp.where(L == l, i, 0), 0)) for l in range(depth)]
    return pl.pallas_call(
        body, name="sum_parts", grid=(depth, blocks), in_specs=in_specs,
        out_specs=pl.BlockSpec((tr, C), lambda L, i: (L * blocks + i, 0)),
        out_shape=jax.ShapeDtypeStruct((depth * R, C), F32), compiler_params=_params("arbitrary", "arbitrary"),
    )(*layers)


def adamw_sum(w, parts, m, v):
    stacked = not isinstance(parts, (list, tuple))
    n = parts.shape[0] if stacked else len(parts)
    R, C = w.shape
    tr = _tile(R, 512 if n <= 2 and C <= 1024 else 256)

    def body(*refs):
        w_ref, m_ref, v_ref = refs[:3]
        g_ref, d_ref, nm_ref, nv_ref = refs[-4:]
        terms = [refs[3][k] for k in range(n)] if stacked else [p_ref[...] for p_ref in refs[3:3 + n]]
        g = terms[0]
        for term in terms[1:]:
            g = g + term
        delta, nm, nv = _adamw_math(w_ref[...], g, m_ref[...], v_ref[...])
        g_ref[...] = g
        d_ref[...] = delta
        nm_ref[...] = nm
        nv_ref[...] = nv

    rows = pl.BlockSpec((tr, C), lambda i: (i, 0))
    part_specs = [pl.BlockSpec((n, tr, C), lambda i: (0, i, 0))] if stacked else [rows] * n
    shape = jax.ShapeDtypeStruct((R, C), F32)
    return pl.pallas_call(
        body, name="adamw_sum", grid=(R // tr,), in_specs=[rows] * 3 + part_specs, out_specs=[rows] * 4,
        out_shape=[shape] * 4, compiler_params=_params("parallel"),
    )(w, m, v, *([parts] if stacked else parts))


def _place():
    return lax.axis_index("x"), lax.axis_index("y"), lax.axis_index("c")


HBM_SPEC = pl.BlockSpec(memory_space=pltpu.HBM)
SEM_SPEC = pl.BlockSpec(memory_space=pltpu.SEMAPHORE)
DATAFLOW = pltpu.SideEffectType.DATAFLOW_SIDE_EFFECTING


PATTERNS = {"gather": ([(1, 0, 0), (0, 1, 0), (1, 1, 0)], N_CHIPS), "scatter": ([(1, 0, 0), (0, 1, 0), (1, 1, 0)], None),
            "sibling": ([(0, 0, 1)], None),
            "all": ([(fx, fy, fc) for fx in (0, 1) for fy in (0, 1) for fc in (0, 1)][1:], N_DEVICES)}


def _exchange_copies(pattern, srcs, lands, send_sems, recv_sems, local_sems):
    flips, _ = PATTERNS[pattern]
    x, y, c = _place()

    def slot(px, py, pc):
        return 4 * px + 2 * py + pc if pattern == "all" else 2 * px + py

    me = slot(x, y, c)
    outgoing, arriving, local = [], [], []
    for k, (src, land) in enumerate(zip(srcs, lands)):
        for d, (fx, fy, fc) in enumerate(flips):
            peer = (x ^ fx, y ^ fy, c ^ fc)
            pair = dict(send_sem=send_sems.at[len(flips) * k + d], recv_sem=recv_sems.at[len(flips) * k + d],
                        device_id=peer, device_id_type=MESH)
            if pattern == "sibling":
                outgoing.append(pltpu.make_async_remote_copy(src_ref=src, dst_ref=land, **pair))
                arriving.append(outgoing[-1])
                continue
            scatter = pattern == "scatter"
            outgoing.append(pltpu.make_async_remote_copy(
                src_ref=src.at[slot(*peer)] if scatter else src, dst_ref=land.at[me], **pair))
            arriving.append(pltpu.make_async_remote_copy(
                src_ref=src.at[me] if scatter else src, dst_ref=land.at[slot(*peer)], **pair))
        if pattern != "sibling":
            local.append(pltpu.make_async_copy(src.at[me] if pattern == "scatter" else src, land.at[me], local_sems.at[k]))
    return outgoing, arriving, local


def _semaphore_shapes(pattern, n):
    flips, _ = PATTERNS[pattern]
    local = [] if pattern == "sibling" else [pltpu.SemaphoreType.DMA((n,))]
    return [pltpu.SemaphoreType.DMA((len(flips) * n,)), pltpu.SemaphoreType.DMA((len(flips) * n,))] + local


def exchange_start(arrays, *, pattern, name):
    n = len(arrays)
    lead = PATTERNS[pattern][1]
    lands = [lax.empty(a.shape if lead is None else (lead,) + a.shape, a.dtype) for a in arrays]
    sem_shapes = _semaphore_shapes(pattern, n)
    n_sems = len(sem_shapes)

    def body(*refs):
        srcs, zones = refs[:n], refs[n:2 * n]
        sems = list(refs[2 * n:2 * n + n_sems]) + [None]
        token = refs[-1]
        outgoing, _, local = _exchange_copies(pattern, srcs, zones, *sems[:3])
        for cp in outgoing + local:
            cp.start()
        token[...] = jnp.zeros_like(token)

    hbm = lambda a: pltpu.HBM(a.shape, a.dtype)
    outs = pl.pallas_call(
        body, name=name,
        out_shape=(*sem_shapes, *[hbm(a) for a in arrays], *[hbm(a) for a in lands],
                   jax.ShapeDtypeStruct((SUBLANES, LANES), F32)),
        in_specs=[HBM_SPEC] * (2 * n),
        out_specs=(*[SEM_SPEC] * n_sems, *[HBM_SPEC] * (2 * n), pl.BlockSpec(memory_space=pltpu.VMEM)),
        input_output_aliases={i: n_sems + i for i in range(2 * n)},
        compiler_params=pltpu.CompilerParams(has_side_effects=DATAFLOW),
    )(*[pltpu.with_memory_space_constraint(a, pltpu.HBM) for a in list(arrays) + lands])
    return outs[:n_sems], outs[n_sems:n_sems + n], outs[n_sems + n:n_sems + 2 * n], outs[-1]


def exchange_wait(started, after, *, pattern, name, with_sources=False):
    sems, srcs, lands, _ = started
    n = len(srcs)
    n_sems = len(sems)

    def body(*refs):
        src_refs, zones = refs[:n], refs[n:2 * n]
        sem_refs = list(refs[2 * n:2 * n + n_sems]) + [None]
        outgoing, arriving, local = _exchange_copies(pattern, src_refs, zones, *sem_refs[:3])
        for sent, landed in zip(outgoing, arriving):
            sent.wait_send()
            landed.wait_recv()
        for cp in local:
            cp.wait()

    hbm = lambda a: pltpu.HBM(a.shape, a.dtype)
    outs = pl.pallas_call(
        body, name=name, out_shape=[hbm(a) for a in list(srcs) + list(lands)],
        in_specs=[HBM_SPEC] * (2 * n) + [SEM_SPEC] * n_sems + [pl.BlockSpec(memory_space=pl.ANY)],
        out_specs=[HBM_SPEC] * (2 * n), input_output_aliases={i: i for i in range(2 * n)},
        compiler_params=pltpu.CompilerParams(has_side_effects=DATAFLOW),
    )(*srcs, *lands, *sems, after)
    return (outs[:n], outs[n:]) if with_sources else outs[n:]


BIG = ("ffn1_w_gate", "ffn1_w_up", "ffn1_w_down", "w_in", "w_attn_out", "w_lru_out", "w_mem_kv", "w_mem_out",
       "w_out", "ffn2_w_gate", "ffn2_w_up", "ffn2_w_down")
TRANSPOSED = ("ffn1_w_gate", "ffn1_w_up", "w_in", "w_attn_out", "w_mem_out", "ffn2_w_gate", "ffn2_w_up")
SMALL = ("ffn1_norm", "mix_norm", "gate_bias", "attn_sinks", "conv_w", "conv_b", "lru_wa", "lru_ba", "lru_wx", "lru_bx",
         "lru_lambda", "mem_norm", "ffn2_norm", "final_norm")
WEIGHTS = ("ffn1_norm", "ffn1_w_gate", "ffn1_w_up", "ffn1_w_down", "mix_norm", "w_in", "gate_bias", "attn_sinks",
           "w_attn_out", "conv_w", "conv_b", "lru_wa", "lru_ba", "lru_wx", "lru_bx", "lru_lambda", "w_lru_out", "mem_norm",
           "w_mem_kv", "w_mem_out", "w_out", "ffn2_norm", "ffn2_w_gate", "ffn2_w_up", "ffn2_w_down", "final_norm")
SEGMENTS = (("qkv", 0, 768), ("xr", 768, 1792), ("yr", 1792, 2816), ("cq", 2816, 3328), ("gates", 3328, 6400))
PACK_ROW = 1024


STAGES = ("ffn1", "mix", "ffn2")
STAGE_BIG = {"ffn1": ("ffn1_w_gate", "ffn1_w_up", "ffn1_w_down"),
             "mix": ("w_in", "w_attn_out", "w_lru_out", "w_mem_kv", "w_mem_out", "w_out"),
             "ffn2": ("ffn2_w_gate", "ffn2_w_up", "ffn2_w_down")}


def _row_sharded(state, name):
    return jnp.swapaxes(state[name], 1, 2) if name in TRANSPOSED else state[name]


def _join_shards(gathered, name):
    if name == "conv_w":
        return jnp.concatenate([gathered[s] for s in range(N_CHIPS)], axis=-1)
    return gathered.reshape(-1, gathered.shape[-1])


def _split_shards(full):
    return full.reshape(N_CHIPS, -1, full.shape[-1])


def _lane_tiles(w):
    z = jnp.zeros((LRU_WIDTH // LRU_TILE, HEAD_DIM, HEAD_DIM), w.dtype)
    top = jnp.concatenate([w[0::2], z], axis=2)
    bottom = jnp.concatenate([z, w[1::2]], axis=2)
    return jnp.concatenate([top, bottom], axis=1)


def _from_lane_tiles(t):
    pairs = jnp.stack([t[:, :HEAD_DIM, :HEAD_DIM], t[:, HEAD_DIM:, HEAD_DIM:]], axis=1)
    return pairs.reshape(2 * t.shape[0], HEAD_DIM, HEAD_DIM)


def _pack_rows(size):
    return -(-size // (SUBLANES * PACK_ROW)) * SUBLANES


def _pack(arrays):
    blocks = []
    for a in arrays:
        flat = a.reshape(-1).astype(F32)
        rows = _pack_rows(flat.shape[0])
        blocks.append(jnp.pad(flat, (0, rows * PACK_ROW - flat.shape[0])).reshape(rows, PACK_ROW))
    return jnp.concatenate(blocks, axis=0)


def _unpack(packed, shapes):
    out, at = [], 0
    for shape in shapes:
        size = int(np.prod(shape))
        rows = _pack_rows(size)
        out.append(packed[at:at + rows].reshape(-1)[:size].reshape(shape))
        at += rows
    return out


def _ffn_forward(x, h, w_gate_t, w_up_t, w_down_after, next_norm):
    G, U, A = ffn_up(h, w_gate_t, w_up_t)
    out = mm([A], [w_down_after(G)], nt=False, out_dtype=F32, res=x, scale=0.5, norm=next_norm, name="ffn_down")
    y, h_next = out if next_norm is not None else (out, None)
    return y, h_next, (x, h, G, U, A)


def _ffn_backward(half, dy, saved, norm, w_gate_t, w_up_t, w_down):
    x, h, G, U, A = saved
    dG, dU = ffn_bwd_act(half, dy, w_down, G, U)
    d_down = mm_tn(A, dy, out_dtype=MXU_DTYPE, b_scale=0.5, name="ffn_dw_down")
    d_gate_t = mm_tn(dG, h, out_dtype=MXU_DTYPE, name="ffn_dw_up")
    d_up_t = mm_tn(dU, h, out_dtype=MXU_DTYPE, name="ffn_dw_up")
    dx, d_norm = mm_rms_bwd([dG, dU], [w_gate_t, w_up_t], x, norm, dy, nt=False, name="ffn_dx")
    return dx, d_norm, d_gate_t, d_up_t, d_down


def kernel(x, mem, ffn1_norm, ffn1_w_gate, ffn1_w_up, ffn1_w_down, mix_norm, w_in, gate_bias, attn_sinks, w_attn_out, conv_w, conv_b, lru_wa, lru_ba, lru_wx, lru_bx, lru_lambda, w_lru_out, mem_norm, w_mem_kv, w_mem_out, w_out, ffn2_norm, ffn2_w_gate, ffn2_w_up, ffn2_w_down, final_norm, loss_target, m_ffn1_norm, m_ffn1_w_gate, m_ffn1_w_up, m_ffn1_w_down, m_mix_norm, m_w_in, m_gate_bias, m_attn_sinks, m_w_attn_out, m_conv_w, m_conv_b, m_lru_wa, m_lru_ba, m_lru_wx, m_lru_bx, m_lru_lambda, m_w_lru_out, m_mem_norm, m_w_mem_kv, m_w_mem_out, m_w_out, m_ffn2_norm, m_ffn2_w_gate, m_ffn2_w_up, m_ffn2_w_down, m_final_norm, v_ffn1_norm, v_ffn1_w_gate, v_ffn1_w_up, v_ffn1_w_down, v_mix_norm, v_w_in, v_gate_bias, v_attn_sinks, v_w_attn_out, v_conv_w, v_conv_b, v_lru_wa, v_lru_ba, v_lru_wx, v_lru_bx, v_lru_lambda, v_w_lru_out, v_mem_norm, v_w_mem_kv, v_w_mem_out, v_w_out, v_ffn2_norm, v_ffn2_w_gate, v_ffn2_w_up, v_ffn2_w_down, v_final_norm):
    args = dict(locals())
    W = {n: args[n] for n in WEIGHTS}
    M = {n: args["m_" + n] for n in WEIGHTS}
    V = {n: args["v_" + n] for n in WEIGHTS}
    chip = 2 * lax.axis_index("x") + lax.axis_index("y")
    x0 = x[0]
    mem0 = mem[0]
    target = loss_target[0]

    def row(v):
        return v.reshape(1, -1)

    def stage_names(stage):
        return STAGE_BIG[stage] + (("conv_w",) if stage == "mix" else ())

    RW, RM, RV = ({n: _row_sharded(state, n) for n in BIG} for state in (W, M, V))
    gathers = {}
    started_all = jnp.zeros((), F32)
    groups = {}
    for l in range(DEPTH):
        for stage in STAGES:
            groups[l, stage] = stage_names(stage)
            if (l, stage) == (0, "ffn1"):
                groups[l, stage], groups[l, "ffn1_down"] = groups[l, stage][:2], groups[l, stage][2:]
    for (l, stage), names in groups.items():
        shards = [W[n][l] if n == "conv_w" else RW[n][l].astype(MXU_DTYPE) for n in names]
        gathers[l, stage] = exchange_start(shards, pattern="gather", name=f"gather_start_{stage}_{l}")
        started_all = started_all + gathers[l, stage][3][0, 0]

    def weights_of(l, stage, after):
        lands = exchange_wait(gathers[l, stage], after, pattern="gather", name=f"gather_wait_{stage}_{l}")
        return {n: _join_shards(g, n) for n, g in zip(groups[l, stage], lands)}

    saved = []
    full = []
    xs = x0
    h = rms_fwd(x0, row(W["ffn1_norm"][0]) + started_all)
    for l in range(DEPTH):
        P = weights_of(l, "ffn1", xs)

        def ffn1_w_down(after, P=P, l=l):
            if (l, "ffn1_down") in groups:
                P.update(weights_of(l, "ffn1_down", after))
            return P["ffn1_w_down"]

        xs, h, ffn1 = _ffn_forward(xs, h, P["ffn1_w_gate"], P["ffn1_w_up"], ffn1_w_down, row(W["mix_norm"][l]))
        x_mix = xs
        P.update(weights_of(l, "mix", xs))
        seg = dict(zip([name for name, _, _ in SEGMENTS], proj_fwd(h, P["w_in"], SEGMENTS)))
        attn = attn_fwd(seg["qkv"], W["attn_sinks"][l])
        wa_t = _lane_tiles(W["lru_wa"][l]).astype(MXU_DTYPE)
        wx_t = _lane_tiles(W["lru_wx"][l]).astype(MXU_DTYPE)
        sp = row(jax.nn.softplus(-W["lru_lambda"][l]))
        lru_consts = (P["conv_w"], row(W["conv_b"][l]), wa_t, row(W["lru_ba"][l]), wx_t, row(W["lru_bx"][l]), sp)
        h_lru, rec = lru_fwd(seg["xr"], seg["yr"], *lru_consts)
        mem_n = rms_fwd(mem0, row(W["mem_norm"][l]))
        kv = mm([mem_n], [P["w_mem_kv"]], nt=False, out_dtype=F32, name="mem_kv")
        cross = mem_attn_fwd(seg["cq"], kv)
        bias = row(W["gate_bias"][l])
        branch_weights = (P["w_attn_out"], P["w_lru_out"], P["w_mem_out"])
        merged, *branches = merge_fwd((attn, rec, cross), branch_weights, seg["gates"], bias)
        h_mix = h
        xs, h = mm([merged], [P["w_out"]], nt=False, out_dtype=F32, res=x_mix, norm=row(W["ffn2_norm"][l]), name="mix_out")
        mix = (x_mix, h_mix, seg, attn, lru_consts, h_lru, rec, mem_n, kv, cross, branches, bias, merged)
        P.update(weights_of(l, "ffn2", xs))
        next_norm = row(W["ffn1_norm"][l + 1]) if l + 1 < DEPTH else None
        xs, h, ffn2 = _ffn_forward(xs, h, P["ffn2_w_gate"], P["ffn2_w_up"], lambda after, P=P: P["ffn2_w_down"], next_norm)
        saved.append((ffn1, mix, ffn2))
        full.append(P)

    loss_lanes, dx, d_final = loss_head(xs, row(W["final_norm"]), target)
    loss = lax.psum(loss_lanes[0, 0], ("x", "y", "c"))

    big_grads = {}
    small_grads = {n: [None] * DEPTH for n in SMALL if n != "final_norm"}
    scatters = {}
    token = jnp.zeros((), F32)

    def send_grads(l, stage):
        chunks = [_split_shards(big_grads[n, l]) for n in STAGE_BIG[stage]]
        scatters[l, stage] = exchange_start(chunks, pattern="scatter", name=f"scatter_start_{stage}_{l}")
        return scatters[l, stage][3][0, 0]

    for l in reversed(range(DEPTH)):
        P = full[l]
        ffn1, mix, ffn2 = saved[l]
        dx, g_norm, g_gate, g_up, g_down = _ffn_backward(
            (0.5 + token).reshape(1), dx, ffn2, row(W["ffn2_norm"][l]), P["ffn2_w_gate"], P["ffn2_w_up"], P["ffn2_w_down"])
        small_grads["ffn2_norm"][l] = g_norm
        big_grads["ffn2_w_gate", l], big_grads["ffn2_w_up", l], big_grads["ffn2_w_down", l] = g_gate, g_up, g_down
        token = send_grads(l, "ffn2")

        x_mix, h, seg, attn, lru_consts, h_lru, rec, mem_n, kv, cross, branches, bias, merged = mix
        bias = bias + token
        big_grads["w_out", l] = mm_tn(merged, dx, out_dtype=MXU_DTYPE, name="dw_out")
        d_attn_br, d_lru_br, d_mem_br, d_attn, d_rec, d_cross, d_gates, d_bias = merge_bwd(
            dx, P["w_out"], (P["w_attn_out"], P["w_lru_out"], P["w_mem_out"]), seg["gates"], bias, branches)
        small_grads["gate_bias"][l] = d_bias
        big_grads["w_attn_out", l] = mm_tn(d_attn_br, attn, out_dtype=MXU_DTYPE, name="dw_attn_out")
        big_grads["w_lru_out", l] = mm_tn(rec, d_lru_br, out_dtype=MXU_DTYPE, name="dw_lru_out")
        big_grads["w_mem_out", l] = mm_tn(d_mem_br, cross, out_dtype=MXU_DTYPE, name="dw_mem_out")

        d_qkv, d_sinks = attn_bwd(seg["qkv"], W["attn_sinks"][l], d_attn)
        small_grads["attn_sinks"][l] = d_sinks

        d_xr, d_yr, d_wa_t, d_wx_t, d_ba, d_bx, d_sp, d_cb, d_cw = lru_bwd(seg["xr"], seg["yr"], h_lru, d_rec, *lru_consts)
        small_grads["conv_w"][l] = d_cw
        small_grads["conv_b"][l] = d_cb
        small_grads["lru_wa"][l] = _from_lane_tiles(d_wa_t)
        small_grads["lru_wx"][l] = _from_lane_tiles(d_wx_t)
        small_grads["lru_ba"][l] = d_ba
        small_grads["lru_bx"][l] = d_bx
        small_grads["lru_lambda"][l] = -d_sp * _sigmoid(-row(W["lru_lambda"][l]))

        d_cq, d_kv = mem_attn_bwd(seg["cq"], kv, d_cross)
        big_grads["w_mem_kv", l] = mm_tn(mem_n, d_kv, out_dtype=MXU_DTYPE, name="dw_mem_kv")
        _, g_mem_norm = mm_rms_bwd([d_kv], [P["w_mem_kv"]], mem0, row(W["mem_norm"][l]), None, nt=True, name="mem_dnorm")
        small_grads["mem_norm"][l] = g_mem_norm

        d_seg = {"qkv": d_qkv, "xr": d_xr, "yr": d_yr, "cq": d_cq, "gates": d_gates}
        big_grads["w_in", l] = jnp.concatenate(
            [mm_tn(d_seg[name], h, out_dtype=MXU_DTYPE, name="dw_in_" + name) for name, _, _ in SEGMENTS], axis=0)
        dx, g_norm = mm_rms_bwd([d_seg[name] for name, _, _ in SEGMENTS], [P["w_in"]], x_mix, row(W["mix_norm"][l]), dx,
                                nt=False, b_rows=[(lo, hi) for _, lo, hi in SEGMENTS], name="mix_dx")
        small_grads["mix_norm"][l] = g_norm
        token = send_grads(l, "mix")

        dx, g_norm, g_gate, g_up, g_down = _ffn_backward(
            (0.5 + token).reshape(1), dx, ffn1, row(W["ffn1_norm"][l]), P["ffn1_w_gate"], P["ffn1_w_up"], P["ffn1_w_down"])
        small_grads["ffn1_norm"][l] = g_norm
        big_grads["ffn1_w_gate", l], big_grads["ffn1_w_up", l], big_grads["ffn1_w_down", l] = g_gate, g_up, g_down
        token = send_grads(l, "ffn1")

    grad_x = dx[None]

    small_list = [jnp.stack(small_grads[n]) for n in SMALL if n != "final_norm"] + [d_final]
    small_shapes = [(DEPTH,) + W[n].shape[1:] if n != "conv_w" else (DEPTH, CONV_WIDTH, LRU_WIDTH)
                    for n in SMALL if n != "final_norm"] + [W["final_norm"].shape]
    small_exchange = exchange_start([_pack(small_list)], pattern="all", name="small_start")

    arrived = {}
    backward_done = dx[:1, :1] + token + small_exchange[3][0, 0]
    for l in range(DEPTH):
        for stage in STAGES:
            lands = exchange_wait(scatters[l, stage], backward_done, pattern="scatter", name=f"scatter_wait_{stage}_{l}")
            for n, a in zip(STAGE_BIG[stage], lands):
                arrived[n, l] = a.reshape(N_CHIPS, -1, a.shape[-1])
    siblings = {}
    after = jnp.zeros((1, 1), F32)
    for stage in STAGES:
        partial = [sum_parts([arrived[n, l] for l in range(DEPTH)]) for n in STAGE_BIG[stage]]
        siblings[stage] = exchange_start(partial, pattern="sibling", name=f"sibling_start_{stage}")
        after = after + siblings[stage][3][0, 0]
    results = {}
    for stage in STAGES:
        own, theirs = exchange_wait(siblings[stage], after, pattern="sibling", name=f"sibling_wait_{stage}", with_sources=True)
        for n, mine, other in zip(STAGE_BIG[stage], own, theirs):
            outs = adamw_sum(_as_rows(RW[n]), [mine, other], _as_rows(RM[n]), _as_rows(RV[n]))
            after = outs[0]
            outs = [o.reshape(RW[n].shape) for o in outs]
            results[n] = [jnp.swapaxes(o, 1, 2) for o in outs] if n in TRANSPOSED else outs
    everyone = exchange_wait(small_exchange, after, pattern="all", name="small_wait")[0]

    def own_columns(full_conv):
        return lax.dynamic_slice_in_dim(full_conv, chip * (LRU_WIDTH // N_CHIPS), LRU_WIDTH // N_CHIPS, axis=2)

    def packed_state(state):
        arrays = []
        for n in SMALL:
            a = state[n]
            if n == "conv_w":
                a = lax.dynamic_update_slice_in_dim(jnp.zeros((DEPTH, CONV_WIDTH, LRU_WIDTH), F32), a,
                                                    chip * (LRU_WIDTH // N_CHIPS), axis=2)
            arrays.append(a)
        return _pack(arrays)

    small_outs = adamw_sum(packed_state(W), everyone, packed_state(M), packed_state(V))
    for kind, packed in enumerate(small_outs):
        for n, a in zip(SMALL, _unpack(packed, small_shapes)):
            results.setdefault(n, [None] * 4)[kind] = own_columns(a) if n == "conv_w" else a.reshape(W[n].shape)

    return (loss, grad_x, *[results[n][0] for n in WEIGHTS], *[results[n][1] for n in WEIGHTS],
            *[results[n][2] for n in WEIGHTS], *[results[n][3] for n in WEIGHTS])
```

```python
import functools

import jax
import jax.numpy as jnp
import numpy as np
from jax import lax
from jax.experimental import pallas as pl
from jax.experimental.pallas import tpu as pltpu

F32 = jnp.float32
MXU_DTYPE = jnp.bfloat16

DEPTH = 4
D_MODEL = 1024
CHUNK = 64
HALO = 2 * CHUNK
N_Q_HEADS = 8
HEAD_DIM = 64
ATTN_WIDTH = 512
KV_WIDTH = 128
LRU_WIDTH = 1024
LRU_TILE = 128
CONV_WIDTH = 4
LRU_C = 8.0
MEM_HEADS = 4
MEM_HEAD_DIM = 128
MEM_WIDTH = 512
D_FF = 2816
EPS = 1e-6
ADAM_LR = 0.001
ADAM_B1 = 0.9
ADAM_B2 = 0.999
ADAM_EPS = 1e-08
ADAM_WD = 0.01
ADAM_STEP = 10

N_CHIPS = 4
N_DEVICES = 8
SUBLANES = 8
LANES = 128
VMEM_LIMIT_BYTES = 56 * 1024 * 1024
NEG_BIG = -1e30
MESH = pl.DeviceIdType.MESH

ALIBI_SLOPES = tuple(float(2.0 ** (-8.0 * (i + 1) / N_Q_HEADS)) for i in range(N_Q_HEADS))


def _params(*semantics):
    return pltpu.CompilerParams(dimension_semantics=semantics, vmem_limit_bytes=VMEM_LIMIT_BYTES)


def _tile(n, pref, unit=SUBLANES):
    if n <= pref:
        return n
    for t in range(pref - pref % unit, 0, -unit):
        if n % t == 0:
            return t
    return n


def _dot(a, b, ca, cb):
    return lax.dot_general(a, b, (((ca,), (cb,)), ((), ())), preferred_element_type=F32)


def _sigmoid(x):
    return 0.5 * jnp.tanh(0.5 * x) + 0.5


def rms_fwd(x, g):
    T, D = x.shape
    tm = _tile(T, 512)

    def body(x_ref, g_ref, h_ref):
        xv = x_ref[...]
        r = lax.rsqrt(jnp.mean(xv * xv, axis=-1, keepdims=True) + EPS)
        h_ref[...] = (xv * r * g_ref[...]).astype(h_ref.dtype)

    return pl.pallas_call(
        body, name="rms_fwd", grid=(T // tm,),
        in_specs=[pl.BlockSpec((tm, D), lambda i: (i, 0)), pl.BlockSpec((1, D), lambda i: (0, 0))],
        out_specs=pl.BlockSpec((tm, D), lambda i: (i, 0)),
        out_shape=jax.ShapeDtypeStruct((T, D), MXU_DTYPE),
        compiler_params=_params("parallel"),
    )(x, g)


def mm_rms_bwd(a_list, b_list, x, g, dy, *, nt, b_rows=None, tm=512, name="mm_rms_bwd"):
    n_a, n_b = len(a_list), len(b_list)
    T, D = x.shape
    tm = _tile(T, tm)
    has_dy = dy is not None

    def body(*refs):
        a_refs, b_refs = refs[:n_a], refs[n_a:n_a + n_b]
        x_ref, g_ref = refs[n_a + n_b:n_a + n_b + 2]
        dx_ref, dx_lo_ref, dg_ref = refs[-3:]
        i = pl.program_id(0)
        dh = None
        for p, a_ref in enumerate(a_refs):
            b = b_refs[p][...] if b_rows is None else b_refs[0][b_rows[p][0]:b_rows[p][1], :]
            d = _dot(a_ref[...].astype(MXU_DTYPE), b, 1, 1 if nt else 0)
            dh = d if dh is None else dh + d
        xv = x_ref[...]
        r = lax.rsqrt(jnp.mean(xv * xv, axis=-1, keepdims=True) + EPS)
        xh = xv * r
        gd = dh * g_ref[...]
        dx = r * (gd - xh * jnp.mean(gd * xh, axis=-1, keepdims=True))
        dx = refs[n_a + n_b + 2][...] + dx if has_dy else dx
        dx_ref[...] = dx
        dx_lo_ref[...] = dx.astype(dx_lo_ref.dtype)
        part = jnp.sum(dh * xh, axis=0, keepdims=True)

        @pl.when(i == 0)
        def _():
            dg_ref[...] = part

        @pl.when(i > 0)
        def _():
            dg_ref[...] += part

    row = pl.BlockSpec((tm, D), lambda i: (i, 0))
    vec = pl.BlockSpec((1, D), lambda i: (0, 0))
    in_specs = [pl.BlockSpec((tm, a.shape[1]), lambda i: (i, 0)) for a in a_list]
    in_specs += [pl.BlockSpec(b.shape, lambda i: (0, 0), pipeline_mode=pl.Buffered(1)) for b in b_list]
    in_specs += [row, vec] + ([row] if has_dy else [])
    operands = list(a_list) + list(b_list) + [x, g] + ([dy] if has_dy else [])
    return pl.pallas_call(
        body, name=name, grid=(T // tm,), in_specs=in_specs, out_specs=[row, row, vec],
        out_shape=[jax.ShapeDtypeStruct((T, D), F32), jax.ShapeDtypeStruct((T, D), MXU_DTYPE),
                   jax.ShapeDtypeStruct((1, D), F32)],
        compiler_params=_params("arbitrary"),
    )(*operands)


def mm(a_list, b_list, *, nt, out_dtype, res=None, scale=1.0, norm=None, tm=512, tn=1024, name="mm"):
    n_pairs = len(a_list)
    T = a_list[0].shape[0]
    N = b_list[0].shape[0] if nt else b_list[0].shape[1]
    tm = _tile(T, tm)
    tn = _tile(N, tn, LANES)
    has_res = res is not None
    has_norm = norm is not None
    assert not has_norm or tn == N

    def body(*refs):
        a_refs = refs[:n_pairs]
        b_refs = refs[n_pairs:2 * n_pairs]
        extra = list(refs[2 * n_pairs:])
        acc = None
        for a_ref, b_ref in zip(a_refs, b_refs):
            d = _dot(a_ref[...].astype(MXU_DTYPE), b_ref[...], 1, 1 if nt else 0)
            acc = d if acc is None else acc + d
        if has_res:
            acc = extra.pop(0)[...] + scale * acc
        elif scale != 1.0:
            acc = scale * acc
        if has_norm:
            g_ref = extra.pop(0)
            r = lax.rsqrt(jnp.mean(acc * acc, axis=-1, keepdims=True) + EPS)
            extra[1][...] = (acc * r * g_ref[...]).astype(extra[1].dtype)
        extra[0][...] = acc.astype(extra[0].dtype)

    in_specs = [pl.BlockSpec((tm, a.shape[1]), lambda j, i: (i, 0)) for a in a_list]
    if nt:
        in_specs += [pl.BlockSpec((tn, b.shape[1]), lambda j, i: (j, 0)) for b in b_list]
    else:
        in_specs += [pl.BlockSpec((b.shape[0], tn), lambda j, i: (0, j)) for b in b_list]
    out_spec = pl.BlockSpec((tm, tn), lambda j, i: (i, j))
    operands = list(a_list) + list(b_list)
    if has_res:
        in_specs.append(out_spec)
        operands.append(res)
    if has_norm:
        in_specs.append(pl.BlockSpec((1, N), lambda j, i: (0, 0)))
        operands.append(norm)
    out_shape = jax.ShapeDtypeStruct((T, N), out_dtype)
    return pl.pallas_call(
        body, name=name, grid=(N // tn, T // tm), in_specs=in_specs,
        out_specs=[out_spec, out_spec] if has_norm else out_spec,
        out_shape=[out_shape, jax.ShapeDtypeStruct((T, N), MXU_DTYPE)] if has_norm else out_shape,
        compiler_params=_params("parallel", "parallel"),
    )(*operands)


def proj_fwd(h, w_in_t, segments):
    T, D = h.shape
    tm = _tile(T, 256)

    def body(h_ref, w_ref, *out_refs):
        hv = h_ref[...]
        for (_, lo, hi), o_ref in zip(segments, out_refs):
            o_ref[...] = _dot(hv, w_ref[lo:hi, :], 1, 1)

    return pl.pallas_call(
        body, name="proj_fwd", grid=(T // tm,),
        in_specs=[pl.BlockSpec((tm, D), lambda i: (i, 0)),
                  pl.BlockSpec(w_in_t.shape, lambda i: (0, 0), pipeline_mode=pl.Buffered(1))],
        out_specs=[pl.BlockSpec((tm, hi - lo), lambda i: (i, 0)) for _, lo, hi in segments],
        out_shape=[jax.ShapeDtypeStruct((T, hi - lo), F32) for _, lo, hi in segments],
        compiler_params=_params("parallel"),
    )(h, w_in_t)


def mm_tn(a, b, *, out_dtype, b_scale=1.0, tk=1408, tn=1408, tt=None, name="mm_tn"):
    T, K = a.shape
    N = b.shape[1]
    tk = _tile(K, tk, LANES)
    tn = _tile(N, tn, LANES)
    if tt is None:
        tt = 1024 if b.dtype == F32 else 2048
    tt = _tile(T, tt)
    steps = T // tt

    def body(a_ref, b_ref, o_ref, acc_ref):
        t = pl.program_id(2)
        bv = b_ref[...]
        if b_scale != 1.0:
            bv = b_scale * bv
        d = _dot(a_ref[...].astype(MXU_DTYPE), bv.astype(MXU_DTYPE), 0, 0)

        @pl.when(t == 0)
        def _():
            acc_ref[...] = d

        @pl.when(t > 0)
        def _():
            acc_ref[...] += d

        @pl.when(t == steps - 1)
        def _():
            o_ref[...] = acc_ref[...].astype(o_ref.dtype)

    return pl.pallas_call(
        body, name=name, grid=(K // tk, N // tn, steps),
        in_specs=[pl.BlockSpec((tt, tk), lambda p, q, t: (t, p)), pl.BlockSpec((tt, tn), lambda p, q, t: (t, q))],
        out_specs=pl.BlockSpec((tk, tn), lambda p, q, t: (p, q)),
        out_shape=jax.ShapeDtypeStruct((K, N), out_dtype),
        scratch_shapes=[pltpu.VMEM((tk, tn), F32)],
        compiler_params=_params("parallel", "parallel", "arbitrary"),
    )(a, b)


def _lane_chunks(n, width=3 * LANES):
    return [slice(lo, min(lo + width, n)) for lo in range(0, n, width)]


def ffn_up(h, w_gate_t, w_up_t):
    T, D = h.shape
    F = w_gate_t.shape[0]
    tm = _tile(T, 512)
    tn = _tile(F, 1408, LANES)

    def body(h_ref, wg_ref, wu_ref, g_ref, u_ref, a_ref):
        hv = h_ref[...]
        G = _dot(hv, wg_ref[...], 1, 1)
        U = _dot(hv, wu_ref[...], 1, 1)
        g_ref[...] = G.astype(g_ref.dtype)
        u_ref[...] = U.astype(u_ref.dtype)
        a_ref[...] = (G * _sigmoid(G) * U).astype(a_ref.dtype)

    w_spec = pl.BlockSpec((tn, D), lambda j, i: (j, 0))
    o_spec = pl.BlockSpec((tm, tn), lambda j, i: (i, j))
    return pl.pallas_call(
        body, name="ffn_up", grid=(F // tn, T // tm),
        in_specs=[pl.BlockSpec((tm, D), lambda j, i: (i, 0)), w_spec, w_spec],
        out_specs=[o_spec, o_spec, o_spec],
        out_shape=[jax.ShapeDtypeStruct((T, F), MXU_DTYPE)] * 3,
        compiler_params=_params("parallel", "parallel"),
    )(h, w_gate_t, w_up_t)


def ffn_bwd_act(half, dy, w_down, G, U):
    T, D = dy.shape
    F = w_down.shape[0]
    tm = _tile(T, 512)

    def body(half_ref, dy_ref, wd_ref, g_ref, u_ref, dg_ref, du_ref):
        d_out = (half_ref[0] * dy_ref[...]).astype(MXU_DTYPE)
        for cols in _lane_chunks(F):
            dA = _dot(d_out, wd_ref[cols, :], 1, 1)
            Gv = g_ref[:, cols].astype(F32)
            s = _sigmoid(Gv)
            du_ref[:, cols] = (dA * (Gv * s)).astype(du_ref.dtype)
            dg_ref[:, cols] = (dA * u_ref[:, cols].astype(F32) * (s * (1.0 + Gv * (1.0 - s)))).astype(dg_ref.dtype)

    o_spec = pl.BlockSpec((tm, F), lambda i: (i, 0))
    return pl.pallas_call(
        body, name="ffn_bwd_act", grid=(T // tm,),
        in_specs=[pl.BlockSpec(memory_space=pltpu.SMEM), pl.BlockSpec((tm, D), lambda i: (i, 0)),
                  pl.BlockSpec(w_down.shape, lambda i: (0, 0), pipeline_mode=pl.Buffered(1)), o_spec, o_spec],
        out_specs=[o_spec, o_spec],
        out_shape=[jax.ShapeDtypeStruct((T, F), MXU_DTYPE), jax.ShapeDtypeStruct((T, F), MXU_DTYPE)],
        compiler_params=_params("parallel"),
    )(half, dy, w_down, G, U)


ATTN_ROWS = 256


def _head_variants(kv128):
    lane = lax.broadcasted_iota(jnp.int32, kv128.shape, 1)
    low = lane < HEAD_DIM
    rolled = pltpu.roll(kv128, HEAD_DIM, 1)
    out = {}
    for j in (0, 1):
        for pos in (0, 1):
            src = kv128 if j == pos else rolled
            out[j, pos] = jnp.where(low if pos == 0 else jnp.logical_not(low), src, 0.0).astype(MXU_DTYPE)
    return out


def _fold_variant(d128, j, pos):
    lane = lax.broadcasted_iota(jnp.int32, d128.shape, 1)
    low = lane < HEAD_DIM
    kept = jnp.where(low if pos == 0 else jnp.logical_not(low), d128, 0.0)
    return kept if j == pos else pltpu.roll(kept, HEAD_DIM, 1)


def _attn_mask(block, rows, keys):
    r = lax.broadcasted_iota(jnp.int32, (rows, keys), 0)
    c = lax.broadcasted_iota(jnp.int32, (rows, keys), 1)
    q_chunk = r // CHUNK
    k_chunk = c // CHUNK - HALO // CHUNK
    first_chunk = jnp.where(block > 0, -(HALO // CHUNK), 0)
    valid = (k_chunk <= q_chunk) & (k_chunk >= q_chunk - HALO // CHUNK) & (k_chunk >= first_chunk)
    dist = jnp.abs(r + HALO - c).astype(F32)
    return valid, dist


def _attn_probs(q128, k_var, head, sink, valid, dist):
    s = _dot(q128, k_var, 1, 1) * (HEAD_DIM ** -0.5) - ALIBI_SLOPES[head] * dist
    s = jnp.where(valid, s, NEG_BIG)
    mx = jnp.maximum(jnp.max(s, axis=-1, keepdims=True), sink)
    p = jnp.exp(s - mx)
    p_sink = jnp.exp(sink - mx)
    inv = 1.0 / (jnp.sum(p, axis=-1, keepdims=True) + p_sink)
    return p * inv, p_sink * inv


def _attn_specs(T, tq, order):
    ratio = tq // HALO
    q_spec = pl.BlockSpec((tq, ATTN_WIDTH), lambda i: (order(i), 0))
    cur = lambda col: pl.BlockSpec((tq, KV_WIDTH), lambda i: (order(i), col))
    prev = lambda col: pl.BlockSpec((HALO, KV_WIDTH), lambda i: (jnp.maximum(order(i) * ratio - 1, 0), col))
    k_col, v_col = ATTN_WIDTH // KV_WIDTH, ATTN_WIDTH // KV_WIDTH + 1
    return [pl.BlockSpec(memory_space=pltpu.SMEM), q_spec, prev(k_col), cur(k_col), prev(v_col), cur(v_col)]


def attn_fwd(qkv, sinks):
    T = qkv.shape[0]
    tq = _tile(T, ATTN_ROWS)
    keys = tq + HALO

    def body(sink_ref, q_ref, kp_ref, kc_ref, vp_ref, vc_ref, o_ref):
        block = pl.program_id(0)
        k_var = _head_variants(jnp.concatenate([kp_ref[...], kc_ref[...]], axis=0))
        v_var = _head_variants(jnp.concatenate([vp_ref[...], vc_ref[...]], axis=0))
        valid, dist = _attn_mask(block, tq, keys)
        for pair in range(N_Q_HEADS // 2):
            q128 = q_ref[:, pair * LANES:(pair + 1) * LANES].astype(MXU_DTYPE)
            acc = None
            for pos in (0, 1):
                head = 2 * pair + pos
                j = head // 4
                p, _ = _attn_probs(q128, k_var[j, pos], head, sink_ref[head], valid, dist)
                d = _dot(p.astype(MXU_DTYPE), v_var[j, pos], 1, 0)
                acc = d if acc is None else acc + d
            o_ref[:, pair * LANES:(pair + 1) * LANES] = acc.astype(o_ref.dtype)

    return pl.pallas_call(
        body, name="attn_fwd", grid=(T // tq,),
        in_specs=_attn_specs(T, tq, lambda i: i),
        out_specs=pl.BlockSpec((tq, ATTN_WIDTH), lambda i: (i, 0)),
        out_shape=jax.ShapeDtypeStruct((T, ATTN_WIDTH), MXU_DTYPE),
        compiler_params=_params("parallel"),
    )(sinks, qkv, qkv, qkv, qkv, qkv)


def attn_bwd(qkv, sinks, d_out):
    T = qkv.shape[0]
    tq = _tile(T, ATTN_ROWS)
    keys = tq + HALO
    n_blocks = T // tq
    order = lambda i: n_blocks - 1 - i

    def body(sink_ref, q_ref, kp_ref, kc_ref, vp_ref, vc_ref, do_ref, dqkv_ref, dsink_ref, carry_ref):
        step = pl.program_id(0)
        block = order(step)
        k_var = _head_variants(jnp.concatenate([kp_ref[...], kc_ref[...]], axis=0))
        v_var = _head_variants(jnp.concatenate([vp_ref[...], vc_ref[...]], axis=0))
        valid, dist = _attn_mask(block, tq, keys)

        @pl.when(step == 0)
        def _():
            carry_ref[...] = jnp.zeros_like(carry_ref)
            dsink_ref[...] = jnp.zeros_like(dsink_ref)

        dk = jnp.zeros((keys, KV_WIDTH), F32)
        dv = jnp.zeros((keys, KV_WIDTH), F32)
        for pair in range(N_Q_HEADS // 2):
            q128 = q_ref[:, pair * LANES:(pair + 1) * LANES].astype(MXU_DTYPE)
            do128 = do_ref[:, pair * LANES:(pair + 1) * LANES].astype(MXU_DTYPE)
            dq128 = None
            for pos in (0, 1):
                head = 2 * pair + pos
                j = head // 4
                p, p_sink = _attn_probs(q128, k_var[j, pos], head, sink_ref[head], valid, dist)
                dp = _dot(do128, v_var[j, pos], 1, 1)
                delta = jnp.sum(p * dp, axis=-1, keepdims=True)
                ds = (p * (dp - delta) * (HEAD_DIM ** -0.5)).astype(MXU_DTYPE)
                d = _dot(ds, k_var[j, pos], 1, 0)
                dq128 = d if dq128 is None else dq128 + d
                dk = dk + _fold_variant(_dot(ds, q128, 0, 0), j, pos)
                dv = dv + _fold_variant(_dot(p.astype(MXU_DTYPE), do128, 0, 0), j, pos)
                dsink_ref[pl.ds(head, 1), :] += jnp.broadcast_to(
                    -jnp.sum(p_sink * delta, axis=0, keepdims=True), (1, LANES))
            dqkv_ref[:, pair * LANES:(pair + 1) * LANES] = dq128.astype(dqkv_ref.dtype)
        carried = carry_ref[...]
        if tq > HALO:
            carried = jnp.concatenate([jnp.zeros((tq - HALO, 2 * KV_WIDTH), F32), carried], axis=0)
        dkv = jnp.concatenate([dk, dv], axis=1)
        dqkv_ref[:, ATTN_WIDTH:] = (dkv[HALO:] + carried).astype(dqkv_ref.dtype)
        carry_ref[...] = dkv[:HALO]

    out_rows = pl.BlockSpec((tq, ATTN_WIDTH + 2 * KV_WIDTH), lambda i: (order(i), 0))
    dqkv, dsink = pl.pallas_call(
        body, name="attn_bwd", grid=(n_blocks,),
        in_specs=_attn_specs(T, tq, order) + [pl.BlockSpec((tq, ATTN_WIDTH), lambda i: (order(i), 0))],
        out_specs=[out_rows, pl.BlockSpec((N_Q_HEADS, LANES), lambda i: (0, 0))],
        out_shape=[jax.ShapeDtypeStruct((T, ATTN_WIDTH + 2 * KV_WIDTH), MXU_DTYPE),
                   jax.ShapeDtypeStruct((N_Q_HEADS, LANES), F32)],
        scratch_shapes=[pltpu.VMEM((HALO, 2 * KV_WIDTH), F32)],
        compiler_params=_params("arbitrary"),
    )(sinks, qkv, qkv, qkv, qkv, qkv, d_out)
    return dqkv, dsink[:, 0]


def _mem_probs(q, k):
    s = _dot(q, k, 1, 1) * (MEM_HEAD_DIM ** -0.5)
    p = jnp.exp(s - jnp.max(s, axis=-1, keepdims=True))
    return p / jnp.sum(p, axis=-1, keepdims=True)


def mem_attn_fwd(cq, kv):
    T = cq.shape[0]
    M = kv.shape[0]
    tq = _tile(T, 512)

    def body(q_ref, kv_ref, o_ref):
        for h in range(MEM_HEADS):
            lo, hi = h * MEM_HEAD_DIM, (h + 1) * MEM_HEAD_DIM
            p = _mem_probs(q_ref[:, lo:hi].astype(MXU_DTYPE), kv_ref[:, lo:hi].astype(MXU_DTYPE))
            v = kv_ref[:, MEM_WIDTH + lo:MEM_WIDTH + hi].astype(MXU_DTYPE)
            o_ref[:, lo:hi] = _dot(p.astype(MXU_DTYPE), v, 1, 0).astype(o_ref.dtype)

    return pl.pallas_call(
        body, name="mem_attn_fwd", grid=(T // tq,),
        in_specs=[pl.BlockSpec((tq, MEM_WIDTH), lambda i: (i, 0)), pl.BlockSpec((M, 2 * MEM_WIDTH), lambda i: (0, 0))],
        out_specs=pl.BlockSpec((tq, MEM_WIDTH), lambda i: (i, 0)),
        out_shape=jax.ShapeDtypeStruct((T, MEM_WIDTH), MXU_DTYPE),
        compiler_params=_params("parallel"),
    )(cq, kv)


def mem_attn_bwd(cq, kv, d_out):
    T = cq.shape[0]
    M = kv.shape[0]
    tq = _tile(T, 512)

    def body(q_ref, kv_ref, do_ref, dq_ref, dkv_ref):
        @pl.when(pl.program_id(0) == 0)
        def _():
            dkv_ref[...] = jnp.zeros_like(dkv_ref)

        for h in range(MEM_HEADS):
            lo, hi = h * MEM_HEAD_DIM, (h + 1) * MEM_HEAD_DIM
            q = q_ref[:, lo:hi].astype(MXU_DTYPE)
            k = kv_ref[:, lo:hi].astype(MXU_DTYPE)
            v = kv_ref[:, MEM_WIDTH + lo:MEM_WIDTH + hi].astype(MXU_DTYPE)
            do = do_ref[:, lo:hi].astype(MXU_DTYPE)
            p = _mem_probs(q, k)
            dp = _dot(do, v, 1, 1)
            ds = (p * (dp - jnp.sum(p * dp, axis=-1, keepdims=True)) * (MEM_HEAD_DIM ** -0.5)).astype(MXU_DTYPE)
            dq_ref[:, lo:hi] = _dot(ds, k, 1, 0).astype(dq_ref.dtype)
            dkv_ref[:, lo:hi] += _dot(ds, q, 0, 0)
            dkv_ref[:, MEM_WIDTH + lo:MEM_WIDTH + hi] += _dot(p.astype(MXU_DTYPE), do, 0, 0)

    rows = pl.BlockSpec((tq, MEM_WIDTH), lambda i: (i, 0))
    whole = pl.BlockSpec((M, 2 * MEM_WIDTH), lambda i: (0, 0))
    return pl.pallas_call(
        body, name="mem_attn_bwd", grid=(T // tq,),
        in_specs=[rows, whole, rows], out_specs=[rows, whole],
        out_shape=[jax.ShapeDtypeStruct((T, MEM_WIDTH), MXU_DTYPE), jax.ShapeDtypeStruct((M, 2 * MEM_WIDTH), F32)],
        compiler_params=_params("arbitrary"),
    )(cq, kv, d_out)


LRU_ROWS = 256


def _shift_down(ext, k, rows):
    return pltpu.roll(ext, k, 0)[SUBLANES:SUBLANES + rows] if k else ext[SUBLANES:SUBLANES + rows]


def _shift_up(ext, k, rows):
    return pltpu.roll(ext, rows + SUBLANES - k, 0)[:rows] if k else ext[:rows]


def _conv(x_ext, conv_w_ref, conv_b_ref, rows):
    xc = conv_b_ref[...] + conv_w_ref[CONV_WIDTH - 1:CONV_WIDTH, :] * _shift_down(x_ext, 0, rows)
    for j in range(CONV_WIDTH - 1):
        xc = xc + conv_w_ref[j:j + 1, :] * _shift_down(x_ext, CONV_WIDTH - 1 - j, rows)
    return xc


def _block_matmul(x, w_ref, cb):
    return jnp.concatenate(
        [_dot(x[:, c * LANES:(c + 1) * LANES].astype(MXU_DTYPE), w_ref[c], 1, cb)
         for c in range(LRU_WIDTH // LRU_TILE)], axis=1)


def _lru_gates(block, xc, wa_ref, ba_ref, wx_ref, bx_ref, sp_ref):
    rows = xc.shape[0]
    ra = _sigmoid(_block_matmul(xc, wa_ref, 0) + ba_ref[...])
    gi = _sigmoid(_block_matmul(xc, wx_ref, 0) + bx_ref[...])
    log_a = -LRU_C * ra * sp_ref[...]
    a = jnp.exp(log_a)
    one_minus = 1.0 - jnp.exp(2.0 * log_a)
    mult = jnp.sqrt(jnp.maximum(one_minus, 0.0))
    row = lax.broadcasted_iota(jnp.int32, (rows, 1), 0)
    first = row == jnp.where(block == 0, 0, -1)
    mult = jnp.where(first, 1.0, mult)
    return ra, gi, a, one_minus, mult, first


def _gelu(x):
    inner = np.sqrt(2.0 / np.pi).astype(np.float32) * (x + 0.044715 * (x * x * x))
    return 0.5 * x * (1.0 + jnp.tanh(inner))


def _gelu_and_grad(x):
    c = np.sqrt(2.0 / np.pi).astype(np.float32)
    x2 = x * x
    t = jnp.tanh(c * (x + 0.044715 * (x2 * x)))
    half = 0.5 * (1.0 + t)
    return x * half, half + 0.5 * x * (1.0 - t * t) * c * (1.0 + 3.0 * 0.044715 * x2)


def _x_ext(block, prev_ref, cur_ref):
    prev = jnp.where(block > 0, prev_ref[...], 0.0)
    return jnp.concatenate([prev, cur_ref[...]], axis=0)


def _lru_in_specs(tb, order, n_rows):
    ratio = tb // SUBLANES
    rows = pl.BlockSpec((tb, LRU_WIDTH), lambda i: (order(i), 0))
    prev8 = pl.BlockSpec((SUBLANES, LRU_WIDTH), lambda i: (jnp.maximum(order(i) * ratio - 1, 0), 0))
    vec = pl.BlockSpec((1, LRU_WIDTH), lambda i: (0, 0))
    conv_w = pl.BlockSpec((CONV_WIDTH, LRU_WIDTH), lambda i: (0, 0))
    tiles = pl.BlockSpec((LRU_WIDTH // LRU_TILE, LRU_TILE, LRU_TILE), lambda i: (0, 0, 0))
    return rows, prev8, vec, conv_w, tiles


def _linear_scan(a, b, carry, *, reverse):
    rows, width = a.shape
    groups = rows // SUBLANES
    a3 = a.reshape(groups, SUBLANES, width)
    b3 = b.reshape(groups, SUBLANES, width)
    sub = lax.broadcasted_iota(jnp.int32, (1, SUBLANES, 1), 1)
    shift = 1
    while shift < SUBLANES:
        keep = (sub < SUBLANES - shift) if reverse else (sub >= shift)
        amount = SUBLANES - shift if reverse else shift
        b3 = b3 + a3 * jnp.where(keep, pltpu.roll(b3, amount, 1), 0.0)
        a3 = a3 * jnp.where(keep, pltpu.roll(a3, amount, 1), 1.0)
        shift *= 2
    out = [None] * groups
    edge = 0 if reverse else SUBLANES - 1
    for g in (reversed(range(groups)) if reverse else range(groups)):
        out[g] = b3[g] + a3[g] * carry
        carry = out[g][edge:edge + 1, :]
    return jnp.concatenate(out, axis=0), carry


def lru_fwd(xr, yr, conv_w, conv_b, wa_t, ba, wx_t, bx, sp):
    T = xr.shape[0]
    tb = _tile(T, LRU_ROWS)

    def body(xp_ref, x_ref, y_ref, cw_ref, cb_ref, wa_ref, ba_ref, wx_ref, bx_ref, sp_ref, h_ref, r_ref, carry_ref):
        block = pl.program_id(0)
        xc = _conv(_x_ext(block, xp_ref, x_ref), cw_ref, cb_ref, tb)
        _, gi, a, _, mult, _ = _lru_gates(block, xc, wa_ref, ba_ref, wx_ref, bx_ref, sp_ref)
        b = mult * gi * xc

        @pl.when(block == 0)
        def _():
            carry_ref[...] = jnp.zeros_like(carry_ref)

        h, carry_ref[...] = _linear_scan(a, b, carry_ref[...], reverse=False)
        h_ref[...] = h
        r_ref[...] = (h * _gelu(y_ref[...])).astype(r_ref.dtype)

    rows, prev8, vec, cw, tiles = _lru_in_specs(tb, lambda i: i, T)
    return pl.pallas_call(
        body, name="lru_fwd", grid=(T // tb,),
        in_specs=[prev8, rows, rows, cw, vec, tiles, vec, tiles, vec, vec],
        out_specs=[rows, rows],
        out_shape=[jax.ShapeDtypeStruct((T, LRU_WIDTH), F32), jax.ShapeDtypeStruct((T, LRU_WIDTH), MXU_DTYPE)],
        scratch_shapes=[pltpu.VMEM((1, LRU_WIDTH), F32)],
        compiler_params=_params("arbitrary"),
    )(xr, xr, yr, conv_w, conv_b, wa_t, ba, wx_t, bx, sp)


def lru_bwd(xr, yr, h, d_r, conv_w, conv_b, wa_t, ba, wx_t, bx, sp):
    T = xr.shape[0]
    tb = _tile(T, LRU_ROWS)
    n_blocks = T // tb
    order = lambda i: n_blocks - 1 - i
    n_tiles = LRU_WIDTH // LRU_TILE

    def body(xp_ref, x_ref, y_ref, hp_ref, h_ref, dr_ref, cw_ref, cb_ref, wa_ref, ba_ref, wx_ref, bx_ref, sp_ref,
             dx_ref, dy_ref, dwa_ref, dwx_ref, dba_ref, dbx_ref, dsp_ref, dcb_ref, dcw_ref, lam_ref, a_next_ref, dxc_next_ref):
        step = pl.program_id(0)
        block = order(step)

        @pl.when(step == 0)
        def _():
            for ref in (lam_ref, a_next_ref, dxc_next_ref, dwa_ref, dwx_ref, dba_ref, dbx_ref, dsp_ref, dcb_ref, dcw_ref):
                ref[...] = jnp.zeros_like(ref)

        x_ext = _x_ext(block, xp_ref, x_ref)
        xc = _conv(x_ext, cw_ref, cb_ref, tb)
        ra, gi, a, one_minus, mult, first = _lru_gates(block, xc, wa_ref, ba_ref, wx_ref, bx_ref, sp_ref)
        yv = y_ref[...]
        hv = h_ref[...]
        drv = dr_ref[...].astype(F32)
        gelu_y, gelu_grad_y = _gelu_and_grad(yv)
        dy_ref[...] = (drv * hv * gelu_grad_y).astype(dy_ref.dtype)

        row = lax.broadcasted_iota(jnp.int32, (tb, 1), 0)
        coef = jnp.where(row == tb - 1, a_next_ref[...], pltpu.roll(a, tb - 1, 0))
        lam, lam_ref[...] = _linear_scan(coef, drv * gelu_y, lam_ref[...], reverse=True)
        a_next_ref[...] = a[0:1, :]

        h_prev = _shift_down(jnp.concatenate([jnp.where(block > 0, hp_ref[...], 0.0), hv], axis=0), 1, tb)
        d_a = lam * h_prev
        d_mult = jnp.where(first, 0.0, lam * gi * xc)
        d_gi = lam * mult * xc
        d_xc = lam * mult * gi
        safe = one_minus > 0.0
        d_log_a = d_a * a + d_mult * jnp.where(safe, -(1.0 - one_minus) * lax.rsqrt(jnp.where(safe, one_minus, 1.0)), 0.0)
        d_za = d_log_a * (-LRU_C * sp_ref[...]) * ra * (1.0 - ra)
        d_zx = d_gi * gi * (1.0 - gi)
        dsp_ref[...] += jnp.sum(d_log_a * (-LRU_C * ra), axis=0, keepdims=True)
        dba_ref[...] += jnp.sum(d_za, axis=0, keepdims=True)
        dbx_ref[...] += jnp.sum(d_zx, axis=0, keepdims=True)
        d_xc = d_xc + _block_matmul(d_za, wa_ref, 1) + _block_matmul(d_zx, wx_ref, 1)
        dcb_ref[...] += jnp.sum(d_xc, axis=0, keepdims=True)
        d_ext = jnp.concatenate([d_xc, dxc_next_ref[...]], axis=0)
        dxc_next_ref[...] = d_xc[:SUBLANES]
        dx = None
        for j in range(CONV_WIDTH):
            k = CONV_WIDTH - 1 - j
            term = cw_ref[j:j + 1, :] * _shift_up(d_ext, k, tb)
            dx = term if dx is None else dx + term
            dcw_ref[j:j + 1, :] += jnp.sum(d_xc * _shift_down(x_ext, k, tb), axis=0, keepdims=True)
        dx_ref[...] = dx.astype(dx_ref.dtype)
        xc_m = xc.astype(MXU_DTYPE)
        for c in range(n_tiles):
            lanes = slice(c * LANES, (c + 1) * LANES)
            dwa_ref[c] += _dot(xc_m[:, lanes], d_za[:, lanes].astype(MXU_DTYPE), 0, 0)
            dwx_ref[c] += _dot(xc_m[:, lanes], d_zx[:, lanes].astype(MXU_DTYPE), 0, 0)

    rows, prev8, vec, cw, tiles = _lru_in_specs(tb, order, T)
    return pl.pallas_call(
        body, name="lru_bwd", grid=(n_blocks,),
        in_specs=[prev8, rows, rows, prev8, rows, rows, cw, vec, tiles, vec, tiles, vec, vec],
        out_specs=[rows, rows, tiles, tiles, vec, vec, vec, vec, cw],
        out_shape=[jax.ShapeDtypeStruct((T, LRU_WIDTH), MXU_DTYPE), jax.ShapeDtypeStruct((T, LRU_WIDTH), MXU_DTYPE),
                   jax.ShapeDtypeStruct((n_tiles, LRU_TILE, LRU_TILE), F32),
                   jax.ShapeDtypeStruct((n_tiles, LRU_TILE, LRU_TILE), F32)]
        + [jax.ShapeDtypeStruct((1, LRU_WIDTH), F32)] * 4 + [jax.ShapeDtypeStruct((CONV_WIDTH, LRU_WIDTH), F32)],
        scratch_shapes=[pltpu.VMEM((1, LRU_WIDTH), F32), pltpu.VMEM((1, LRU_WIDTH), F32),
                        pltpu.VMEM((SUBLANES, LRU_WIDTH), F32)],
        compiler_params=_params("arbitrary"),
    )(xr, xr, yr, h, h, d_r, conv_w, conv_b, wa_t, ba, wx_t, bx, sp)


def _whole(w):
    return pl.BlockSpec(w.shape, lambda i: (0, 0), pipeline_mode=pl.Buffered(1))


def merge_fwd(mixers, weights, gates, bias):
    T = gates.shape[0]
    tm = _tile(T, 512)
    D = D_MODEL
    contract = (1, 0, 1)

    def body(a_ref, r_ref, c_ref, wa_ref, wr_ref, wc_ref, g_ref, b_ref, m_ref, ba_ref, br_ref, bc_ref):
        acc = None
        for k, (x_ref, w_ref, out_ref) in enumerate(((a_ref, wa_ref, ba_ref), (r_ref, wr_ref, br_ref), (c_ref, wc_ref, bc_ref))):
            cols = slice(k * D, (k + 1) * D)
            branch = _dot(x_ref[...], w_ref[...], 1, contract[k])
            out_ref[...] = branch.astype(out_ref.dtype)
            term = _sigmoid(g_ref[:, cols] + b_ref[:, cols]) * branch
            acc = term if acc is None else acc + term
        m_ref[...] = acc.astype(m_ref.dtype)

    rows = pl.BlockSpec((tm, D), lambda i: (i, 0))
    act = jax.ShapeDtypeStruct((T, D), MXU_DTYPE)
    return pl.pallas_call(
        body, name="merge_fwd", grid=(T // tm,),
        in_specs=[pl.BlockSpec((tm, m.shape[1]), lambda i: (i, 0)) for m in mixers] + [_whole(w) for w in weights]
        + [pl.BlockSpec((tm, 3 * D), lambda i: (i, 0)), pl.BlockSpec((1, 3 * D), lambda i: (0, 0))],
        out_specs=[rows] * 4, out_shape=[act] * 4, compiler_params=_params("parallel"),
    )(*mixers, *weights, gates, bias)


def merge_bwd(dy, w_out, weights, gates, bias, branches):
    T = gates.shape[0]
    tm = _tile(T, 256)
    D = D_MODEL
    contract = (0, 1, 0)

    def body(dy_ref, wo_ref, wa_ref, wr_ref, wc_ref, g_ref, b_ref, ba_ref, br_ref, bc_ref,
             dba_ref, dbr_ref, dbc_ref, da_ref, dr_ref, dc_ref, dg_ref, db_ref):
        @pl.when(pl.program_id(0) == 0)
        def _():
            db_ref[...] = jnp.zeros_like(db_ref)

        dm = _dot(dy_ref[...].astype(MXU_DTYPE), wo_ref[...], 1, 1)
        for k, (w_ref, branch_ref, dbranch_ref, dmixer_ref) in enumerate(
                ((wa_ref, ba_ref, dba_ref, da_ref), (wr_ref, br_ref, dbr_ref, dr_ref), (wc_ref, bc_ref, dbc_ref, dc_ref))):
            cols = slice(k * D, (k + 1) * D)
            s = _sigmoid(g_ref[:, cols] + b_ref[:, cols])
            d_branch = (dm * s).astype(MXU_DTYPE)
            dbranch_ref[...] = d_branch
            dmixer_ref[...] = _dot(d_branch, w_ref[...], 1, contract[k]).astype(dmixer_ref.dtype)
            d_pre = dm * branch_ref[...].astype(F32) * s * (1.0 - s)
            dg_ref[:, cols] = d_pre.astype(dg_ref.dtype)
            db_ref[:, cols] += jnp.sum(d_pre, axis=0, keepdims=True)

    rows = pl.BlockSpec((tm, D), lambda i: (i, 0))
    half = pl.BlockSpec((tm, ATTN_WIDTH), lambda i: (i, 0))
    wide = pl.BlockSpec((tm, 3 * D), lambda i: (i, 0))
    vec = pl.BlockSpec((1, 3 * D), lambda i: (0, 0))
    act = jax.ShapeDtypeStruct((T, D), MXU_DTYPE)
    return pl.pallas_call(
        body, name="merge_bwd", grid=(T // tm,),
        in_specs=[rows, _whole(w_out)] + [_whole(w) for w in weights] + [wide, vec, rows, rows, rows],
        out_specs=[rows, rows, rows, half, rows, half, wide, vec],
        out_shape=[act, act, act, jax.ShapeDtypeStruct((T, ATTN_WIDTH), MXU_DTYPE), jax.ShapeDtypeStruct((T, D), F32),
                   jax.ShapeDtypeStruct((T, MEM_WIDTH), MXU_DTYPE), jax.ShapeDtypeStruct((T, 3 * D), MXU_DTYPE),
                   jax.ShapeDtypeStruct((1, 3 * D), F32)],
        compiler_params=_params("arbitrary"),
    )(dy, w_out, *weights, gates, bias, *branches)


def loss_head(x, g, target):
    T, D = x.shape
    tm = _tile(T, 512)

    def body(x_ref, g_ref, t_ref, loss_ref, dx_ref, dg_ref):
        @pl.when(pl.program_id(0) == 0)
        def _():
            loss_ref[...] = jnp.zeros_like(loss_ref)
            dg_ref[...] = jnp.zeros_like(dg_ref)

        xv = x_ref[...]
        gv = g_ref[...]
        r = lax.rsqrt(jnp.mean(xv * xv, axis=-1, keepdims=True) + EPS)
        xh = xv * r
        err = xh * gv - t_ref[...]
        per_row = jnp.mean(err * err, axis=-1, keepdims=True)
        loss_ref[...] += jnp.broadcast_to(0.5 * jnp.sum(per_row, axis=0, keepdims=True), loss_ref.shape)
        dyv = err * (1.0 / D)
        dg_ref[...] += jnp.sum(dyv * xh, axis=0, keepdims=True)
        gd = dyv * gv
        dx_ref[...] = r * (gd - xh * jnp.mean(gd * xh, axis=-1, keepdims=True))

    rows = pl.BlockSpec((tm, D), lambda i: (i, 0))
    vec = pl.BlockSpec((1, D), lambda i: (0, 0))
    return pl.pallas_call(
        body, name="loss_head", grid=(T // tm,),
        in_specs=[rows, vec, rows], out_specs=[pl.BlockSpec((1, LANES), lambda i: (0, 0)), rows, vec],
        out_shape=[jax.ShapeDtypeStruct((1, LANES), F32), jax.ShapeDtypeStruct((T, D), F32),
                   jax.ShapeDtypeStruct((1, D), F32)],
        compiler_params=_params("arbitrary"),
    )(x, g, target)


def _adamw_math(w, g, m, v):
    m = ADAM_B1 * m + (1.0 - ADAM_B1) * g
    v = ADAM_B2 * v + (1.0 - ADAM_B2) * (g * g)
    m_hat = m / (1.0 - ADAM_B1 ** ADAM_STEP)
    v_hat = v / (1.0 - ADAM_B2 ** ADAM_STEP)
    delta = -ADAM_LR * (m_hat / (jnp.sqrt(v_hat) + ADAM_EPS) + ADAM_WD * w)
    return delta, m, v


def _as_rows(a):
    return a.reshape(-1, a.shape[-1])


def sum_parts(layers):
    n, R, C = layers[0].shape
    depth = len(layers)
    tr = _tile(R, 512 if C <= 1024 else 128)
    blocks = R // tr

    def body(*refs):
        o_ref = refs[-1]
        for l in range(depth):
            @pl.when(pl.program_id(0) == l)
            def _(p_ref=refs[l]):
                acc = p_ref[0].astype(F32)
                for k in range(1, n):
                    acc = acc + p_ref[k].astype(F32)
                o_ref[...] = acc

    in_specs = [pl.BlockSpec((n, tr, C), lambda L, i, l=l: (0, jnp.where(L == l, i, 0), 0)) for l in range(depth)]
    return pl.pallas_call(
        body, name="sum_parts", grid=(depth, blocks), in_specs=in_specs,
        out_specs=pl.BlockSpec((tr, C), lambda L, i: (L * blocks + i, 0)),
        out_shape=jax.ShapeDtypeStruct((depth * R, C), F32), compiler_params=_params("arbitrary", "arbitrary"),
    )(*layers)


def adamw_sum(w, parts, m, v):
    stacked = not isinstance(parts, (list, tuple))
    n = parts.shape[0] if stacked else len(parts)
    R, C = w.shape
    tr = _tile(R, 512 if n <= 2 and C <= 1024 else 256)

    def body(*refs):
        w_ref, m_ref, v_ref = refs[:3]
        g_ref, d_ref, nm_ref, nv_ref = refs[-4:]
        terms = [refs[3][k] for k in range(n)] if stacked else [p_ref[...] for p_ref in refs[3:3 + n]]
        g = terms[0]
        for term in terms[1:]:
            g = g + term
        delta, nm, nv = _adamw_math(w_ref[...], g, m_ref[...], v_ref[...])
        g_ref[...] = g
        d_ref[...] = delta
        nm_ref[...] = nm
        nv_ref[...] = nv

    rows = pl.BlockSpec((tr, C), lambda i: (i, 0))
    part_specs = [pl.BlockSpec((n, tr, C), lambda i: (0, i, 0))] if stacked else [rows] * n
    shape = jax.ShapeDtypeStruct((R, C), F32)
    return pl.pallas_call(
        body, name="adamw_sum", grid=(R // tr,), in_specs=[rows] * 3 + part_specs, out_specs=[rows] * 4,
        out_shape=[shape] * 4, compiler_params=_params("parallel"),
    )(w, m, v, *([parts] if stacked else parts))


def _place():
    return lax.axis_index("x"), lax.axis_index("y"), lax.axis_index("c")


HBM_SPEC = pl.BlockSpec(memory_space=pltpu.HBM)
SEM_SPEC = pl.BlockSpec(memory_space=pltpu.SEMAPHORE)
DATAFLOW = pltpu.SideEffectType.DATAFLOW_SIDE_EFFECTING


PATTERNS = {"gather": ([(1, 0, 0), (0, 1, 0), (1, 1, 0)], N_CHIPS), "scatter": ([(1, 0, 0), (0, 1, 0), (1, 1, 0)], None),
            "sibling": ([(0, 0, 1)], None),
            "all": ([(fx, fy, fc) for fx in (0, 1) for fy in (0, 1) for fc in (0, 1)][1:], N_DEVICES)}


def _exchange_copies(pattern, srcs, lands, send_sems, recv_sems, local_sems):
    flips, _ = PATTERNS[pattern]
    x, y, c = _place()

    def slot(px, py, pc):
        return 4 * px + 2 * py + pc if pattern == "all" else 2 * px + py

    me = slot(x, y, c)
    outgoing, arriving, local = [], [], []
    for k, (src, land) in enumerate(zip(srcs, lands)):
        for d, (fx, fy, fc) in enumerate(flips):
            peer = (x ^ fx, y ^ fy, c ^ fc)
            pair = dict(send_sem=send_sems.at[len(flips) * k + d], recv_sem=recv_sems.at[len(flips) * k + d],
                        device_id=peer, device_id_type=MESH)
            if pattern == "sibling":
                outgoing.append(pltpu.make_async_remote_copy(src_ref=src, dst_ref=land, **pair))
                arriving.append(outgoing[-1])
                continue
            scatter = pattern == "scatter"
            outgoing.append(pltpu.make_async_remote_copy(
                src_ref=src.at[slot(*peer)] if scatter else src, dst_ref=land.at[me], **pair))
            arriving.append(pltpu.make_async_remote_copy(
                src_ref=src.at[me] if scatter else src, dst_ref=land.at[slot(*peer)], **pair))
        if pattern != "sibling":
            local.append(pltpu.make_async_copy(src.at[me] if pattern == "scatter" else src, land.at[me], local_sems.at[k]))
    return outgoing, arriving, local


def _semaphore_shapes(pattern, n):
    flips, _ = PATTERNS[pattern]
    local = [] if pattern == "sibling" else [pltpu.SemaphoreType.DMA((n,))]
    return [pltpu.SemaphoreType.DMA((len(flips) * n,)), pltpu.SemaphoreType.DMA((len(flips) * n,))] + local


def exchange_start(arrays, *, pattern, name):
    n = len(arrays)
    lead = PATTERNS[pattern][1]
    lands = [lax.empty(a.shape if lead is None else (lead,) + a.shape, a.dtype) for a in arrays]
    sem_shapes = _semaphore_shapes(pattern, n)
    n_sems = len(sem_shapes)

    def body(*refs):
        srcs, zones = refs[:n], refs[n:2 * n]
        sems = list(refs[2 * n:2 * n + n_sems]) + [None]
        token = refs[-1]
        outgoing, _, local = _exchange_copies(pattern, srcs, zones, *sems[:3])
        for cp in outgoing + local:
            cp.start()
        token[...] = jnp.zeros_like(token)

    hbm = lambda a: pltpu.HBM(a.shape, a.dtype)
    outs = pl.pallas_call(
        body, name=name,
        out_shape=(*sem_shapes, *[hbm(a) for a in arrays], *[hbm(a) for a in lands],
                   jax.ShapeDtypeStruct((SUBLANES, LANES), F32)),
        in_specs=[HBM_SPEC] * (2 * n),
        out_specs=(*[SEM_SPEC] * n_sems, *[HBM_SPEC] * (2 * n), pl.BlockSpec(memory_space=pltpu.VMEM)),
        input_output_aliases={i: n_sems + i for i in range(2 * n)},
        compiler_params=pltpu.CompilerParams(has_side_effects=DATAFLOW),
    )(*[pltpu.with_memory_space_constraint(a, pltpu.HBM) for a in list(arrays) + lands])
    return outs[:n_sems], outs[n_sems:n_sems + n], outs[n_sems + n:n_sems + 2 * n], outs[-1]


def exchange_wait(started, after, *, pattern, name, with_sources=False):
    sems, srcs, lands, _ = started
    n = len(srcs)
    n_sems = len(sems)

    def body(*refs):
        src_refs, zones = refs[:n], refs[n:2 * n]
        sem_refs = list(refs[2 * n:2 * n + n_sems]) + [None]
        outgoing, arriving, local = _exchange_copies(pattern, src_refs, zones, *sem_refs[:3])
        for sent, landed in zip(outgoing, arriving):
            sent.wait_send()
            landed.wait_recv()
        for cp in local:
            cp.wait()

    hbm = lambda a: pltpu.HBM(a.shape, a.dtype)
    outs = pl.pallas_call(
        body, name=name, out_shape=[hbm(a) for a in list(srcs) + list(lands)],
        in_specs=[HBM_SPEC] * (2 * n) + [SEM_SPEC] * n_sems + [pl.BlockSpec(memory_space=pl.ANY)],
        out_specs=[HBM_SPEC] * (2 * n), input_output_aliases={i: i for i in range(2 * n)},
        compiler_params=pltpu.CompilerParams(has_side_effects=DATAFLOW),
    )(*srcs, *lands, *sems, after)
    return (outs[:n], outs[n:]) if with_sources else outs[n:]


BIG = ("ffn1_w_gate", "ffn1_w_up", "ffn1_w_down", "w_in", "w_attn_out", "w_lru_out", "w_mem_kv", "w_mem_out",
       "w_out", "ffn2_w_gate", "ffn2_w_up", "ffn2_w_down")
TRANSPOSED = ("ffn1_w_gate", "ffn1_w_up", "w_in", "w_attn_out", "w_mem_out", "ffn2_w_gate", "ffn2_w_up")
SMALL = ("ffn1_norm", "mix_norm", "gate_bias", "attn_sinks", "conv_w", "conv_b", "lru_wa", "lru_ba", "lru_wx", "lru_bx",
         "lru_lambda", "mem_norm", "ffn2_norm", "final_norm")
WEIGHTS = ("ffn1_norm", "ffn1_w_gate", "ffn1_w_up", "ffn1_w_down", "mix_norm", "w_in", "gate_bias", "attn_sinks",
           "w_attn_out", "conv_w", "conv_b", "lru_wa", "lru_ba", "lru_wx", "lru_bx", "lru_lambda", "w_lru_out", "mem_norm",
           "w_mem_kv", "w_mem_out", "w_out", "ffn2_norm", "ffn2_w_gate", "ffn2_w_up", "ffn2_w_down", "final_norm")
SEGMENTS = (("qkv", 0, 768), ("xr", 768, 1792), ("yr", 1792, 2816), ("cq", 2816, 3328), ("gates", 3328, 6400))
PACK_ROW = 1024


STAGES = ("ffn1", "mix", "ffn2")
STAGE_BIG = {"ffn1": ("ffn1_w_gate", "ffn1_w_up", "ffn1_w_down"),
             "mix": ("w_in", "w_attn_out", "w_lru_out", "w_mem_kv", "w_mem_out", "w_out"),
             "ffn2": ("ffn2_w_gate", "ffn2_w_up", "ffn2_w_down")}


def _row_sharded(state, name):
    return jnp.swapaxes(state[name], 1, 2) if name in TRANSPOSED else state[name]


def _join_shards(gathered, name):
    if name == "conv_w":
        return jnp.concatenate([gathered[s] for s in range(N_CHIPS)], axis=-1)
    return gathered.reshape(-1, gathered.shape[-1])


def _split_shards(full):
    return full.reshape(N_CHIPS, -1, full.shape[-1])


def _lane_tiles(w):
    z = jnp.zeros((LRU_WIDTH // LRU_TILE, HEAD_DIM, HEAD_DIM), w.dtype)
    top = jnp.concatenate([w[0::2], z], axis=2)
    bottom = jnp.concatenate([z, w[1::2]], axis=2)
    return jnp.concatenate([top, bottom], axis=1)


def _from_lane_tiles(t):
    pairs = jnp.stack([t[:, :HEAD_DIM, :HEAD_DIM], t[:, HEAD_DIM:, HEAD_DIM:]], axis=1)
    return pairs.reshape(2 * t.shape[0], HEAD_DIM, HEAD_DIM)


def _pack_rows(size):
    return -(-size // (SUBLANES * PACK_ROW)) * SUBLANES


def _pack(arrays):
    blocks = []
    for a in arrays:
        flat = a.reshape(-1).astype(F32)
        rows = _pack_rows(flat.shape[0])
        blocks.append(jnp.pad(flat, (0, rows * PACK_ROW - flat.shape[0])).reshape(rows, PACK_ROW))
    return jnp.concatenate(blocks, axis=0)


def _unpack(packed, shapes):
    out, at = [], 0
    for shape in shapes:
        size = int(np.prod(shape))
        rows = _pack_rows(size)
        out.append(packed[at:at + rows].reshape(-1)[:size].reshape(shape))
        at += rows
    return out


def _ffn_forward(x, h, w_gate_t, w_up_t, w_down_after, next_norm):
    G, U, A = ffn_up(h, w_gate_t, w_up_t)
    out = mm([A], [w_down_after(G)], nt=False, out_dtype=F32, res=x, scale=0.5, norm=next_norm, name="ffn_down")
    y, h_next = out if next_norm is not None else (out, None)
    return y, h_next, (x, h, G, U, A)


def _ffn_backward(half, dy, dy_lo, saved, norm, w_gate_t, w_up_t, w_down):
    x, h, G, U, A = saved
    dG, dU = ffn_bwd_act(half, dy_lo, w_down, G, U)
    d_down = mm_tn(A, dy_lo, out_dtype=MXU_DTYPE, b_scale=0.5, name="ffn_dw_down")
    d_gate_t = mm_tn(dG, h, out_dtype=MXU_DTYPE, name="ffn_dw_up")
    d_up_t = mm_tn(dU, h, out_dtype=MXU_DTYPE, name="ffn_dw_up")
    dx, dx_lo, d_norm = mm_rms_bwd([dG, dU], [w_gate_t, w_up_t], x, norm, dy, nt=False, name="ffn_dx")
    return dx, dx_lo, d_norm, d_gate_t, d_up_t, d_down


def kernel(x, mem, ffn1_norm, ffn1_w_gate, ffn1_w_up, ffn1_w_down, mix_norm, w_in, gate_bias, attn_sinks, w_attn_out, conv_w, conv_b, lru_wa, lru_ba, lru_wx, lru_bx, lru_lambda, w_lru_out, mem_norm, w_mem_kv, w_mem_out, w_out, ffn2_norm, ffn2_w_gate, ffn2_w_up, ffn2_w_down, final_norm, loss_target, m_ffn1_norm, m_ffn1_w_gate, m_ffn1_w_up, m_ffn1_w_down, m_mix_norm, m_w_in, m_gate_bias, m_attn_sinks, m_w_attn_out, m_conv_w, m_conv_b, m_lru_wa, m_lru_ba, m_lru_wx, m_lru_bx, m_lru_lambda, m_w_lru_out, m_mem_norm, m_w_mem_kv, m_w_mem_out, m_w_out, m_ffn2_norm, m_ffn2_w_gate, m_ffn2_w_up, m_ffn2_w_down, m_final_norm, v_ffn1_norm, v_ffn1_w_gate, v_ffn1_w_up, v_ffn1_w_down, v_mix_norm, v_w_in, v_gate_bias, v_attn_sinks, v_w_attn_out, v_conv_w, v_conv_b, v_lru_wa, v_lru_ba, v_lru_wx, v_lru_bx, v_lru_lambda, v_w_lru_out, v_mem_norm, v_w_mem_kv, v_w_mem_out, v_w_out, v_ffn2_norm, v_ffn2_w_gate, v_ffn2_w_up, v_ffn2_w_down, v_final_norm):
    args = dict(locals())
    W = {n: args[n] for n in WEIGHTS}
    M = {n: args["m_" + n] for n in WEIGHTS}
    V = {n: args["v_" + n] for n in WEIGHTS}
    chip = 2 * lax.axis_index("x") + lax.axis_index("y")
    x0 = x[0]
    mem0 = mem[0]
    target = loss_target[0]

    def row(v):
        return v.reshape(1, -1)

    def stage_names(stage):
        return STAGE_BIG[stage] + (("conv_w",) if stage == "mix" else ())

    RW, RM, RV = ({n: _row_sharded(state, n) for n in BIG} for state in (W, M, V))
    gathers = {}
    started_all = jnp.zeros((), F32)
    groups = {}
    for l in range(DEPTH):
        for stage in STAGES:
            groups[l, stage] = stage_names(stage)
            if (l, stage) == (0, "ffn1"):
                groups[l, stage], groups[l, "ffn1_down"] = groups[l, stage][:2], groups[l, stage][2:]
    for (l, stage), names in groups.items():
        shards = [W[n][l] if n == "conv_w" else RW[n][l].astype(MXU_DTYPE) for n in names]
        gathers[l, stage] = exchange_start(shards, pattern="gather", name=f"gather_start_{stage}_{l}")
        started_all = started_all + gathers[l, stage][3][0, 0]

    def weights_of(l, stage, after):
        lands = exchange_wait(gathers[l, stage], after, pattern="gather", name=f"gather_wait_{stage}_{l}")
        return {n: _join_shards(g, n) for n, g in zip(groups[l, stage], lands)}

    saved = []
    full = []
    xs = x0
    h = rms_fwd(x0, row(W["ffn1_norm"][0]) + started_all)
    for l in range(DEPTH):
        P = weights_of(l, "ffn1", xs)

        def ffn1_w_down(after, P=P, l=l):
            if (l, "ffn1_down") in groups:
                P.update(weights_of(l, "ffn1_down", after))
            return P["ffn1_w_down"]

        xs, h, ffn1 = _ffn_forward(xs, h, P["ffn1_w_gate"], P["ffn1_w_up"], ffn1_w_down, row(W["mix_norm"][l]))
        x_mix = xs
        P.update(weights_of(l, "mix", xs))
        seg = dict(zip([name for name, _, _ in SEGMENTS], proj_fwd(h, P["w_in"], SEGMENTS)))
        attn = attn_fwd(seg["qkv"], W["attn_sinks"][l])
        wa_t = _lane_tiles(W["lru_wa"][l]).astype(MXU_DTYPE)
        wx_t = _lane_tiles(W["lru_wx"][l]).astype(MXU_DTYPE)
        sp = row(jax.nn.softplus(-W["lru_lambda"][l]))
        lru_consts = (P["conv_w"], row(W["conv_b"][l]), wa_t, row(W["lru_ba"][l]), wx_t, row(W["lru_bx"][l]), sp)
        h_lru, rec = lru_fwd(seg["xr"], seg["yr"], *lru_consts)
        mem_n = rms_fwd(mem0, row(W["mem_norm"][l]))
        kv = mm([mem_n], [P["w_mem_kv"]], nt=False, out_dtype=F32, name="mem_kv")
        cross = mem_attn_fwd(seg["cq"], kv)
        bias = row(W["gate_bias"][l])
        branch_weights = (P["w_attn_out"], P["w_lru_out"], P["w_mem_out"])
        merged, *branches = merge_fwd((attn, rec, cross), branch_weights, seg["gates"], bias)
        h_mix = h
        xs, h = mm([merged], [P["w_out"]], nt=False, out_dtype=F32, res=x_mix, norm=row(W["ffn2_norm"][l]), name="mix_out")
        mix = (x_mix, h_mix, seg, attn, lru_consts, h_lru, rec, mem_n, kv, cross, branches, bias, merged)
        P.update(weights_of(l, "ffn2", xs))
        next_norm = row(W["ffn1_norm"][l + 1]) if l + 1 < DEPTH else None
        xs, h, ffn2 = _ffn_forward(xs, h, P["ffn2_w_gate"], P["ffn2_w_up"], lambda after, P=P: P["ffn2_w_down"], next_norm)
        saved.append((ffn1, mix, ffn2))
        full.append(P)

    loss_lanes, dx, d_final = loss_head(xs, row(W["final_norm"]), target)
    loss = lax.psum(loss_lanes[0, 0], ("x", "y", "c"))
    dx_lo = dx.astype(MXU_DTYPE)

    big_grads = {}
    small_grads = {n: [None] * DEPTH for n in SMALL if n != "final_norm"}
    scatters = {}
    token = jnp.zeros((), F32)

    def send_grads(l, stage):
        chunks = [_split_shards(big_grads[n, l]) for n in STAGE_BIG[stage]]
        scatters[l, stage] = exchange_start(chunks, pattern="scatter", name=f"scatter_start_{stage}_{l}")
        return scatters[l, stage][3][0, 0]

    for l in reversed(range(DEPTH)):
        P = full[l]
        ffn1, mix, ffn2 = saved[l]
        dx, dx_lo, g_norm, g_gate, g_up, g_down = _ffn_backward(
            (0.5 + token).reshape(1), dx, dx_lo, ffn2, row(W["ffn2_norm"][l]), P["ffn2_w_gate"], P["ffn2_w_up"], P["ffn2_w_down"])
        small_grads["ffn2_norm"][l] = g_norm
        big_grads["ffn2_w_gate", l], big_grads["ffn2_w_up", l], big_grads["ffn2_w_down", l] = g_gate, g_up, g_down
        token = send_grads(l, "ffn2")

        x_mix, h, seg, attn, lru_consts, h_lru, rec, mem_n, kv, cross, branches, bias, merged = mix
        bias = bias + token
        big_grads["w_out", l] = mm_tn(merged, dx_lo, out_dtype=MXU_DTYPE, name="dw_out")
        d_attn_br, d_lru_br, d_mem_br, d_attn, d_rec, d_cross, d_gates, d_bias = merge_bwd(
            dx_lo, P["w_out"], (P["w_attn_out"], P["w_lru_out"], P["w_mem_out"]), seg["gates"], bias, branches)
        small_grads["gate_bias"][l] = d_bias
        big_grads["w_attn_out", l] = mm_tn(d_attn_br, attn, out_dtype=MXU_DTYPE, name="dw_attn_out")
        big_grads["w_lru_out", l] = mm_tn(rec, d_lru_br, out_dtype=MXU_DTYPE, name="dw_lru_out")
        big_grads["w_mem_out", l] = mm_tn(d_mem_br, cross, out_dtype=MXU_DTYPE, name="dw_mem_out")

        d_qkv, d_sinks = attn_bwd(seg["qkv"], W["attn_sinks"][l], d_attn)
        small_grads["attn_sinks"][l] = d_sinks

        d_xr, d_yr, d_wa_t, d_wx_t, d_ba, d_bx, d_sp, d_cb, d_cw = lru_bwd(seg["xr"], seg["yr"], h_lru, d_rec, *lru_consts)
        small_grads["conv_w"][l] = d_cw
        small_grads["conv_b"][l] = d_cb
        small_grads["lru_wa"][l] = _from_lane_tiles(d_wa_t)
        small_grads["lru_wx"][l] = _from_lane_tiles(d_wx_t)
        small_grads["lru_ba"][l] = d_ba
        small_grads["lru_bx"][l] = d_bx
        small_grads["lru_lambda"][l] = -d_sp * _sigmoid(-row(W["lru_lambda"][l]))

        d_cq, d_kv = mem_attn_bwd(seg["cq"], kv, d_cross)
        big_grads["w_mem_kv", l] = mm_tn(mem_n, d_kv, out_dtype=MXU_DTYPE, name="dw_mem_kv")
        _, _, g_mem_norm = mm_rms_bwd([d_kv], [P["w_mem_kv"]], mem0, row(W["mem_norm"][l]), None, nt=True, name="mem_dnorm")
        small_grads["mem_norm"][l] = g_mem_norm

        d_seg = {"qkv": d_qkv, "xr": d_xr, "yr": d_yr, "cq": d_cq, "gates": d_gates}
        big_grads["w_in", l] = jnp.concatenate(
            [mm_tn(d_seg[name], h, out_dtype=MXU_DTYPE, name="dw_in_" + name) for name, _, _ in SEGMENTS], axis=0)
        dx, dx_lo, g_norm = mm_rms_bwd([d_seg[name] for name, _, _ in SEGMENTS], [P["w_in"]], x_mix, row(W["mix_norm"][l]), dx,
                                nt=False, b_rows=[(lo, hi) for _, lo, hi in SEGMENTS], name="mix_dx")
        small_grads["mix_norm"][l] = g_norm
        token = send_grads(l, "mix")

        dx, dx_lo, g_norm, g_gate, g_up, g_down = _ffn_backward(
            (0.5 + token).reshape(1), dx, dx_lo, ffn1, row(W["ffn1_norm"][l]), P["ffn1_w_gate"], P["ffn1_w_up"], P["ffn1_w_down"])
        small_grads["ffn1_norm"][l] = g_norm
        big_grads["ffn1_w_gate", l], big_grads["ffn1_w_up", l], big_grads["ffn1_w_down", l] = g_gate, g_up, g_down
        token = send_grads(l, "ffn1")

    grad_x = dx[None]

    small_list = [jnp.stack(small_grads[n]) for n in SMALL if n != "final_norm"] + [d_final]
    small_shapes = [(DEPTH,) + W[n].shape[1:] if n != "conv_w" else (DEPTH, CONV_WIDTH, LRU_WIDTH)
                    for n in SMALL if n != "final_norm"] + [W["final_norm"].shape]
    small_exchange = exchange_start([_pack(small_list)], pattern="all", name="small_start")

    arrived = {}
    backward_done = dx[:1, :1] + token + small_exchange[3][0, 0]
    for l in range(DEPTH):
        for stage in STAGES:
            lands = exchange_wait(scatters[l, stage], backward_done, pattern="scatter", name=f"scatter_wait_{stage}_{l}")
            for n, a in zip(STAGE_BIG[stage], lands):
                arrived[n, l] = a.reshape(N_CHIPS, -1, a.shape[-1])
    siblings = {}
    after = jnp.zeros((1, 1), F32)
    for stage in STAGES:
        partial = [sum_parts([arrived[n, l] for l in range(DEPTH)]) for n in STAGE_BIG[stage]]
        siblings[stage] = exchange_start(partial, pattern="sibling", name=f"sibling_start_{stage}")
        after = after + siblings[stage][3][0, 0]
    results = {}
    for stage in STAGES:
        own, theirs = exchange_wait(siblings[stage], after, pattern="sibling", name=f"sibling_wait_{stage}", with_sources=True)
        for n, mine, other in zip(STAGE_BIG[stage], own, theirs):
            outs = adamw_sum(_as_rows(RW[n]), [mine, other], _as_rows(RM[n]), _as_rows(RV[n]))
            after = outs[0]
            outs = [o.reshape(RW[n].shape) for o in outs]
            results[n] = [jnp.swapaxes(o, 1, 2) for o in outs] if n in TRANSPOSED else outs
    everyone = exchange_wait(small_exchange, after, pattern="all", name="small_wait")[0]

    def own_columns(full_conv):
        return lax.dynamic_slice_in_dim(full_conv, chip * (LRU_WIDTH // N_CHIPS), LRU_WIDTH // N_CHIPS, axis=2)

    def packed_state(state):
        arrays = []
        for n in SMALL:
            a = state[n]
            if n == "conv_w":
                a = lax.dynamic_update_slice_in_dim(jnp.zeros((DEPTH, CONV_WIDTH, LRU_WIDTH), F32), a,
                                                    chip * (LRU_WIDTH // N_CHIPS), axis=2)
            arrays.append(a)
        return _pack(arrays)

    small_outs = adamw_sum(packed_state(W), everyone, packed_state(M), packed_state(V))
    for kind, packed in enumerate(small_outs):
        for n, a in zip(SMALL, _unpack(packed, small_shapes)):
            results.setdefault(n, [None] * 4)[kind] = own_columns(a) if n == "conv_w" else a.reshape(W[n].shape)

    return (loss, grad_x, *[results[n][0] for n in WEIGHTS], *[results[n][1] for n in WEIGHTS],
            *[results[n][2] for n in WEIGHTS], *[results[n][3] for n in WEIGHTS])
```
